```python
import jax, jax.numpy as jnp
from jax import lax
import numpy as np

D_MODEL = 1024
BATCH = 8
SEQ = 2048
DEPTH = 1
DEC_BATCH = 128
DEC_SEQ = 1
PAST_LEN = 16384
PAGE_SIZE = 128

MIX_A = D_MODEL // 2
MIX_B = D_MODEL - MIX_A
HA = 4
DK_A = MIX_A // HA
DV_A = MIX_A // HA
HS_B = 64
HB = MIX_B // HS_B
CONV_W = 4
CHUNK = 64
LORA_W = 64
LORA_A = 64
RMS_EPS = 1e-6
GN_EPS = 64e-5
A_QKV = 3 * MIX_A
A_W = A_QKV + MIX_A + 2 * HA
B_W = 4 * MIX_B + LORA_W + LORA_A
P_W = A_W + B_W

kernel_name = "hymba_gdn_rwkv7_step"

F32 = jnp.float32


def rmsnorm(x, w):
    x32 = x.astype(F32)
    return x32 * lax.rsqrt(jnp.mean(x32 * x32, -1, keepdims=True) + RMS_EPS) * w.astype(F32)


def l2norm(x):
    return x * lax.rsqrt(jnp.sum(x * x, -1, keepdims=True) + 1e-12)


def short_conv(buf, u, w):
    t = u.shape[1]
    full = jnp.concatenate([buf, u], axis=1)
    out = sum(full[:, i:i + t] * w[i] for i in range(CONV_W))
    return jax.nn.silu(out), full[:, full.shape[1] - (CONV_W - 1):]


def gated_delta_chunked(q, k, v, g, beta, s0):
    bsz, t, h, dk = q.shape
    t_pad = -(-t // CHUNK) * CHUNK
    pad = t_pad - t
    padt = lambda z: jnp.pad(z, [(0, 0), (0, pad)] + [(0, 0)] * (z.ndim - 2))
    q, k, v, g, beta = (padt(z) for z in (q, k, v, g, beta))
    n = t_pad // CHUNK

    def blocks(z):
        z = z.reshape((bsz, n, CHUNK) + z.shape[2:])
        return jnp.swapaxes(jnp.moveaxis(z, 3, 2), 0, 1)

    qb, kb, vb, gb, bb = (blocks(z) for z in (q, k, v, g, beta))
    G = jnp.cumsum(gb, axis=-1)
    idx = jnp.arange(CHUNK)
    strict = idx[:, None] > idx[None, :]
    incl = idx[:, None] >= idx[None, :]
    diff = G[..., :, None] - G[..., None, :]
    decay = jnp.where(incl, jnp.exp(jnp.where(incl, diff, 0.0)), 0.0)
    kk = jnp.einsum('nbhck,nbhdk->nbhcd', kb, kb)
    amat = jnp.where(strict, bb[..., :, None] * decay * kk, 0.0) + jnp.eye(CHUNK, dtype=F32)
    rhs = jnp.concatenate([(bb * jnp.exp(G))[..., None] * kb, bb[..., None] * vb], axis=-1)
    sol = lax.linalg.triangular_solve(amat, rhs, left_side=True, lower=True, unit_diagonal=True)
    wmat, umat = sol[..., :dk], sol[..., dk:]
    qk = decay * jnp.einsum('nbhck,nbhdk->nbhcd', qb, kb)
    qg = qb * jnp.exp(G)[..., None]
    kdec = kb * jnp.exp(G[..., -1:] - G)[..., None]
    glast = jnp.exp(G[..., -1])

    def step(s, xs):
        w_c, u_c, qg_c, qk_c, kd_c, gl_c = xs
        u = u_c - jnp.einsum('bhck,bhkv->bhcv', w_c, s)
        o = jnp.einsum('bhck,bhkv->bhcv', qg_c, s) + jnp.einsum('bhcd,bhdv->bhcv', qk_c, u)
        s = gl_c[..., None, None] * s + jnp.einsum('bhck,bhcv->bhkv', kd_c, u)
        return s, o

    s, o = lax.scan(step, s0, (wmat, umat, qg, qk, kdec, glast))
    o = jnp.transpose(o, (1, 0, 3, 2, 4)).reshape(bsz, t_pad, h, -1)[:, :t]
    return o, s


def rwkv7_scan(r, w, k, v, kk, a, s0):
    def step(s, xs):
        r_t, w_t, k_t, v_t, kk_t, a_t = xs
        sa = jnp.einsum('bhvk,bhk->bhv', s, -kk_t)
        s = (s * w_t[:, :, None, :] + sa[..., None] * (kk_t * a_t)[:, :, None, :]
             + v_t[..., None] * k_t[:, :, None, :])
        return s, jnp.einsum('bhvk,bhk->bhv', s, r_t)

    xs = tuple(jnp.swapaxes(z, 0, 1) for z in (r, w, k, v, kk, a))
    s, y = lax.scan(step, s0, xs)
    return jnp.swapaxes(y, 0, 1), s


def mixer_layer(x, s_a, s_conv, s_b, s_shift, norm_w, w_in, conv_w, a_log, dt_bias, a_norm_w,
                mu, w0, w2, a0, a2, k_k, k_a, r_k, ln_w, ln_b, w_out):
    bsz, t, _ = x.shape
    xn = rmsnorm(x, norm_w).astype(x.dtype)
    proj = jnp.einsum('btd,dp->btp', xn, w_in).astype(F32)
    pa, pb = proj[..., :A_W], proj[..., A_W:]

    qkv, conv_new = short_conv(s_conv.astype(F32), pa[..., :A_QKV], conv_w.astype(F32))
    q = l2norm(qkv[..., :MIX_A].reshape(bsz, t, HA, DK_A)) * (DK_A ** -0.5)
    k = l2norm(qkv[..., MIX_A:2 * MIX_A].reshape(bsz, t, HA, DK_A))
    v = qkv[..., 2 * MIX_A:].reshape(bsz, t, HA, DV_A)
    gate_a = pa[..., A_QKV:A_QKV + MIX_A]
    beta = jax.nn.sigmoid(pa[..., A_QKV + MIX_A:A_QKV + MIX_A + HA])
    g = -jnp.exp(a_log.astype(F32)) * jax.nn.softplus(pa[..., A_QKV + MIX_A + HA:] + dt_bias.astype(F32))
    o_a, s_a_new = gated_delta_chunked(q, k, v, g, beta, s_a.astype(F32))
    o_a = rmsnorm(o_a, a_norm_w).reshape(bsz, t, MIX_A) * jax.nn.silu(gate_a)

    prev = jnp.concatenate([s_shift.astype(F32)[:, None], pb[:, :-1]], axis=1)
    xb = pb + (prev - pb) * mu.astype(F32)
    shift_new = pb[:, -1]
    hd = lambda z: z.reshape(bsz, t, HB, HS_B)
    r = xb[..., :MIX_B]
    kr = xb[..., MIX_B:2 * MIX_B]
    vr = xb[..., 2 * MIX_B:3 * MIX_B]
    gate_b = xb[..., 3 * MIX_B:4 * MIX_B]
    wd = xb[..., 4 * MIX_B:4 * MIX_B + LORA_W]
    ad = xb[..., 4 * MIX_B + LORA_W:]
    w_ll = -jax.nn.softplus(-(w0.astype(F32) + jnp.tanh(wd) @ w2.astype(F32))) - 0.5
    wdec = jnp.exp(-jnp.exp(w_ll))
    a = jax.nn.sigmoid(a0.astype(F32) + ad @ a2.astype(F32))
    kk = l2norm(hd(kr * k_k.astype(F32)))
    kmod = kr * (1.0 + (a - 1.0) * k_a.astype(F32))
    r_h, k_h, v_h = hd(r), hd(kmod), hd(vr)
    y_b, s_b_new = rwkv7_scan(r_h, hd(wdec), k_h, v_h, kk, hd(a), s_b.astype(F32))
    mean = jnp.mean(y_b, -1, keepdims=True)
    var = jnp.mean(jnp.square(y_b - mean), -1, keepdims=True)
    gn = ((y_b - mean) * lax.rsqrt(var + GN_EPS)).reshape(bsz, t, MIX_B) * ln_w.astype(F32) + ln_b.astype(F32)
    bonus = jnp.sum(r_h * k_h * r_k.astype(F32), -1, keepdims=True) * v_h
    o_b = (gn + bonus.reshape(bsz, t, MIX_B)) * jax.nn.silu(gate_b)

    mix = jnp.concatenate([o_a, o_b], axis=-1).astype(x.dtype)
    h = x + jnp.einsum('btm,md->btd', mix, w_out)
    dt = x.dtype
    return h, (s_a_new.astype(dt), conv_new.astype(dt), s_b_new.astype(dt), shift_new.astype(dt))


def setup_inputs(seed: int = 0) -> dict:
    key = jax.random.key(seed)
    ks = jax.random.split(key, 32)
    nrm = lambda i, shape, s=1.0: jax.random.normal(ks[i], shape, F32) * s
    return {
        "x_prompt": nrm(0, (BATCH, SEQ, D_MODEL)),
        "x_sample": nrm(1, (DEC_BATCH, DEC_SEQ, D_MODEL)),
        "state_a_mat": nrm(2, (DEPTH, DEC_BATCH, HA, DK_A, DV_A), 0.5),
        "state_a_conv": nrm(3, (DEPTH, DEC_BATCH, CONV_W - 1, A_QKV)),
        "state_b_mat": nrm(4, (DEPTH, DEC_BATCH, HB, HS_B, HS_B), 0.1),
        "state_b_shift": nrm(5, (DEPTH, DEC_BATCH, B_W)),
        "norm_w": 1.0 + nrm(6, (DEPTH, D_MODEL), 0.01),
        "w_in": nrm(7, (DEPTH, D_MODEL, P_W), D_MODEL ** -0.5),
        "conv_w": nrm(8, (DEPTH, CONV_W, A_QKV), CONV_W ** -0.5),
        "a_log": jnp.log(jax.random.uniform(ks[9], (DEPTH, HA), F32, 1.0, 16.0)),
        "dt_bias": nrm(10, (DEPTH, HA), 0.1),
        "a_norm_w": 1.0 + nrm(11, (DEPTH, DV_A), 0.01),
        "mu": jax.random.uniform(ks[12], (DEPTH, B_W), F32, 0.0, 1.0),
        "w0": nrm(13, (DEPTH, MIX_B), 0.5),
        "w2": nrm(14, (DEPTH, LORA_W, MIX_B), 0.5 * LORA_W ** -0.5),
        "a0": nrm(15, (DEPTH, MIX_B), 0.1),
        "a2": nrm(16, (DEPTH, LORA_A, MIX_B), LORA_A ** -0.5),
        "k_k": 0.85 + nrm(17, (DEPTH, MIX_B), 0.02),
        "k_a": 1.0 + nrm(18, (DEPTH, MIX_B), 0.02),
        "r_k": nrm(19, (DEPTH, HB, HS_B), 0.1),
        "ln_w": 1.0 + nrm(20, (DEPTH, MIX_B), 0.01),
        "ln_b": nrm(21, (DEPTH, MIX_B), 0.01),
        "w_out": nrm(22, (DEPTH, D_MODEL, D_MODEL), D_MODEL ** -0.5),
        "final_norm_w": 1.0 + nrm(23, (D_MODEL,), 0.01),
    }


def reference(x_prompt, x_sample, state_a_mat, state_a_conv, state_b_mat, state_b_shift,
              norm_w, w_in, conv_w, a_log, dt_bias, a_norm_w, mu, w0, w2, a0, a2, k_k, k_a,
              r_k, ln_w, ln_b, w_out, final_norm_w):
    hp, hs = x_prompt, x_sample
    dt = x_prompt.dtype
    p_amat, p_aconv, p_bmat, p_bshift = [], [], [], []
    s_amat, s_aconv, s_bmat, s_bshift = [], [], [], []
    for l in range(DEPTH):
        wts = (norm_w[l], w_in[l], conv_w[l], a_log[l], dt_bias[l], a_norm_w[l], mu[l], w0[l], w2[l],
               a0[l], a2[l], k_k[l], k_a[l], r_k[l], ln_w[l], ln_b[l], w_out[l])
        z_a = jnp.zeros((BATCH, HA, DK_A, DV_A), dt)
        z_c = jnp.zeros((BATCH, CONV_W - 1, A_QKV), dt)
        z_b = jnp.zeros((BATCH, HB, HS_B, HS_B), dt)
        z_s = jnp.zeros((BATCH, B_W), dt)
        hp, (pa, pc, pbm, pbs) = mixer_layer(hp, z_a, z_c, z_b, z_s, *wts)
        hs, (sa, sc, sbm, sbs) = mixer_layer(hs, state_a_mat[l], state_a_conv[l], state_b_mat[l],
                                             state_b_shift[l], *wts)
        p_amat.append(pa); p_aconv.append(pc); p_bmat.append(pbm); p_bshift.append(pbs)
        s_amat.append(sa); s_aconv.append(sc); s_bmat.append(sbm); s_bshift.append(sbs)
    y_prompt = rmsnorm(hp, final_norm_w).astype(dt)
    y_sample = rmsnorm(hs, final_norm_w).astype(dt)
    return (y_prompt, y_sample,
            jnp.stack(p_amat), jnp.stack(p_aconv), jnp.stack(p_bmat), jnp.stack(p_bshift),
            jnp.stack(s_amat), jnp.stack(s_aconv), jnp.stack(s_bmat), jnp.stack(s_bshift))
```

```python
import functools

import jax
import jax.numpy as jnp
from jax import lax
from jax.experimental import pallas as pl
from jax.experimental.pallas import tpu as pltpu

F32 = jnp.float32
BF16 = jnp.bfloat16

D_MODEL = 1024
MIX_A = 512
MIX_B = 512
HA = 4
DK_A = 128
HB = 8
HS_B = 64
CONV_W = 4
LORA = 64
A_QKV = 3 * MIX_A
A_MAIN = A_QKV + MIX_A
B_W = 4 * MIX_B + 2 * LORA
BD_W = 128
P_CAT = A_MAIN + B_W + BD_W
RMS_EPS = 1e-6
GN_EPS = 64e-5
CHUNK = 64
TC = 256
SB = 8
VMEM_LIMIT = 56 * 1024 * 1024

HIGHEST = lax.Precision.HIGHEST
NN = (((1,), (0,)), ((), ()))
NT = (((1,), (1,)), ((), ()))
TN = (((0,), (0,)), ((), ()))


def _dot(a, b, dn=NN, prec=None):
    return lax.dot_general(a, b, dn, precision=prec, preferred_element_type=F32)


def _sigmoid(x):
    return 1.0 / (1.0 + jnp.exp(-x))


def _silu(x):
    return x * _sigmoid(x)


def _softplus(x):
    return jnp.maximum(x, 0.0) + jnp.log(1.0 + jnp.exp(-jnp.abs(x)))


def _l2norm(x):
    return x * lax.rsqrt(jnp.sum(x * x, axis=-1, keepdims=True) + 1e-12)


def _iota2(shape, dim):
    return lax.broadcasted_iota(jnp.int32, shape, dim)


def _chunk_tril(n):
    r = _iota2((n, n), 0)
    c = _iota2((n, n), 1)
    same = (r // CHUNK) == (c // CHUNK)
    return jnp.where(same & (c <= r), 1.0, 0.0).astype(F32)


def _tri_inv(low):
    n = low.shape[0]
    r = _iota2((n, n), 0)
    c = _iota2((n, n), 1)
    eye = jnp.where(r == c, 1.0, 0.0).astype(F32)
    l0 = jnp.where((r // 8) == (c // 8), low, 0.0)
    l2 = _dot(l0, l0)
    l4 = _dot(l2, l2)
    d = _dot(_dot(eye - l0, eye + l2), eye + l4)
    s = 8
    while s < n:
        off = jnp.where(((r // (2 * s)) == (c // (2 * s))) & ((r // s) != (c // s)), low, 0.0)
        d = d - _dot(d, _dot(off, d))
        s *= 2
    return d


def _inproj_kernel(x_ref, nw_ref, w_ref, oa_ref, ob_ref, obd_ref):
    x = x_ref[...]
    xn = x * lax.rsqrt(jnp.mean(x * x, axis=-1, keepdims=True) + RMS_EPS) * nw_ref[...]
    p = _dot(xn.astype(BF16), w_ref[...])
    oa_ref[...] = p[:, :A_MAIN]
    ob_ref[...] = p[:, A_MAIN:A_MAIN + B_W]
    obd_ref[...] = p[:, A_MAIN + B_W:]


def _inproj(x2d, norm_w, w_cat):
    n = x2d.shape[0]
    tm = min(256, n)
    row = lambda i: (i, 0)
    fixed = lambda i: (0, 0)
    return pl.pallas_call(
        _inproj_kernel,
        grid=(n // tm,),
        in_specs=[pl.BlockSpec((tm, D_MODEL), row),
                  pl.BlockSpec((1, D_MODEL), fixed),
                  pl.BlockSpec((D_MODEL, P_CAT), fixed)],
        out_specs=[pl.BlockSpec((tm, A_MAIN), row),
                   pl.BlockSpec((tm, B_W), row),
                   pl.BlockSpec((tm, BD_W), row)],
        out_shape=[jax.ShapeDtypeStruct((n, A_MAIN), F32),
                   jax.ShapeDtypeStruct((n, B_W), F32),
                   jax.ShapeDtypeStruct((n, BD_W), F32)],
        compiler_params=pltpu.CompilerParams(dimension_semantics=("arbitrary",),
                                             vmem_limit_bytes=VMEM_LIMIT),
        name="inproj",
    )(x2d, norm_w, w_cat)


def _outproj_kernel(x_ref, ma_ref, mb_ref, w_ref, fw_ref, y_ref):
    h = (x_ref[...]
         + _dot(ma_ref[...].astype(BF16), w_ref[0:MIX_A, :])
         + _dot(mb_ref[...].astype(BF16), w_ref[MIX_A:, :]))
    y_ref[...] = h * lax.rsqrt(jnp.mean(h * h, axis=-1, keepdims=True) + RMS_EPS) * fw_ref[...]


def _outproj(x2d, mix_a, mix_b, w_out, final_w):
    n = x2d.shape[0]
    tm = min(512, n)
    row = lambda i: (i, 0)
    fixed = lambda i: (0, 0)
    return pl.pallas_call(
        _outproj_kernel,
        grid=(n // tm,),
        in_specs=[pl.BlockSpec((tm, D_MODEL), row),
                  pl.BlockSpec((tm, MIX_A), row),
                  pl.BlockSpec((tm, MIX_B), row),
                  pl.BlockSpec((D_MODEL, D_MODEL), fixed),
                  pl.BlockSpec((1, D_MODEL), fixed)],
        out_specs=pl.BlockSpec((tm, D_MODEL), row),
        out_shape=jax.ShapeDtypeStruct((n, D_MODEL), F32),
        compiler_params=pltpu.CompilerParams(dimension_semantics=("arbitrary",),
                                             vmem_limit_bytes=VMEM_LIMIT),
        name="outproj",
    )(x2d, mix_a, mix_b, w_out, final_w)


def _gdn_gates(bd, hp):
    beta = _sigmoid(bd)
    g = -jnp.exp(hp[0:1, :]) * _softplus(bd + hp[1:2, :])
    return beta, g


def _gdn_prompt_kernel(pa_ref, bd_ref, cw_ref, hp_ref, anw_ref,
                       o_ref, s_out_ref, c_out_ref, ext_ref, s_ref):
    c = pl.program_id(1)

    @pl.when(c == 0)
    def _():
        ext_ref[0:8, :] = jnp.zeros((8, A_QKV), F32)
        s_ref[...] = jnp.zeros_like(s_ref)

    ext_ref[8:8 + TC, :] = pa_ref[:, 0:A_QKV]
    cw = cw_ref[...]
    conv = (ext_ref[5:5 + TC, :] * cw[0:1, :] + ext_ref[6:6 + TC, :] * cw[1:2, :]
            + ext_ref[7:7 + TC, :] * cw[2:3, :] + ext_ref[8:8 + TC, :] * cw[3:4, :])
    qkv = _silu(conv)
    c_out_ref[...] = ext_ref[TC + 5:TC + 8, :]
    ext_ref[0:8, :] = ext_ref[TC:TC + 8, :]

    beta_all, g_all = _gdn_gates(bd_ref[...], hp_ref[...])
    gcum = _dot(_chunk_tril(TC), g_all, prec=HIGHEST)
    gcum_t = gcum.T
    anw = anw_ref[...]

    r = _iota2((CHUNK, CHUNK), 0)
    cc = _iota2((CHUNK, CHUNK), 1)
    incl = cc <= r
    strict = cc < r

    for j in range(TC // CHUNK):
        rows = slice(j * CHUNK, (j + 1) * CHUNK)
        for h in range(HA):
            lanes = slice(h * DK_A, (h + 1) * DK_A)
            q = _l2norm(qkv[rows, lanes]) * (DK_A ** -0.5)
            k = _l2norm(qkv[rows, MIX_A + h * DK_A:MIX_A + (h + 1) * DK_A])
            v = qkv[rows, 2 * MIX_A + h * DK_A:2 * MIX_A + (h + 1) * DK_A]
            beta = beta_all[rows, h:h + 1]
            gc = gcum[rows, HA + h:HA + h + 1]
            gr = gcum_t[HA + h:HA + h + 1, rows]
            glast = gcum[(j + 1) * CHUNK - 1:(j + 1) * CHUNK, HA + h:HA + h + 1]
            decay = jnp.where(incl, jnp.exp(jnp.where(incl, gc - gr, 0.0)), 0.0)
            kk = _dot(k, k, NT)
            low = jnp.where(strict, beta * decay * kk, 0.0)
            tinv = _tri_inv(low)
            eg = jnp.exp(gc)
            w = _dot(tinv, (beta * eg) * k)
            u = _dot(tinv, beta * v)
            qk = decay * _dot(q, k, NT)
            qg = q * eg
            kdec = k * jnp.exp(glast - gc)
            s = s_ref[h]
            u2 = u - _dot(w, s)
            o = _dot(qg, s) + _dot(qk, u2)
            s_ref[h] = jnp.exp(glast) * s + _dot(kdec, u2, TN)
            on = o * lax.rsqrt(jnp.mean(o * o, axis=-1, keepdims=True) + RMS_EPS) * anw
            gate = pa_ref[rows, A_QKV + h * DK_A:A_QKV + (h + 1) * DK_A]
            o_ref[rows, lanes] = on * _silu(gate)

    s_out_ref[...] = s_ref[...]


def _gdn_prompt(pa, bd, conv_w, hp, anw):
    b, t, _ = pa.shape
    blk = lambda i, j: (i, j, 0)
    fixed = lambda i, j: (0, 0)
    return pl.pallas_call(
        _gdn_prompt_kernel,
        grid=(b, t // TC),
        in_specs=[pl.BlockSpec((None, TC, A_MAIN), blk),
                  pl.BlockSpec((None, TC, BD_W), blk),
                  pl.BlockSpec((CONV_W, A_QKV), fixed),
                  pl.BlockSpec((8, BD_W), fixed),
                  pl.BlockSpec((1, DK_A), fixed)],
        out_specs=[pl.BlockSpec((None, TC, MIX_A), blk),
                   pl.BlockSpec((None, HA, DK_A, DK_A), lambda i, j: (i, 0, 0, 0)),
                   pl.BlockSpec((None, CONV_W - 1, A_QKV), lambda i, j: (i, 0, 0))],
        out_shape=[jax.ShapeDtypeStruct((b, t, MIX_A), F32),
                   jax.ShapeDtypeStruct((b, HA, DK_A, DK_A), F32),
                   jax.ShapeDtypeStruct((b, CONV_W - 1, A_QKV), F32)],
        scratch_shapes=[pltpu.VMEM((TC + 8, A_QKV), F32),
                        pltpu.VMEM((HA, DK_A, DK_A), F32)],
        compiler_params=pltpu.CompilerParams(dimension_semantics=("arbitrary", "arbitrary"),
                                             vmem_limit_bytes=VMEM_LIMIT),
        name="gdn_prompt",
    )(pa, bd, conv_w, hp, anw)


def _rwkv_front(xb, w0, w2, a0, a2, k_k, k_a):
    r = xb[:, 0:MIX_B]
    kr = xb[:, MIX_B:2 * MIX_B]
    vr = xb[:, 2 * MIX_B:3 * MIX_B]
    gate = xb[:, 3 * MIX_B:4 * MIX_B]
    wd = xb[:, 4 * MIX_B:4 * MIX_B + LORA]
    ad = xb[:, 4 * MIX_B + LORA:]
    w_ll = -_softplus(-(w0 + _dot(jnp.tanh(wd), w2))) - 0.5
    logw = -jnp.exp(w_ll)
    a = _sigmoid(a0 + _dot(ad, a2))
    kraw = kr * k_k
    kk = jnp.concatenate(
        [_l2norm(kraw[:, h * HS_B:(h + 1) * HS_B]) for h in range(HB)], axis=-1)
    kmod = kr * (1.0 + (a - 1.0) * k_a)
    return r, kmod, vr, gate, logw, a, kk


def _rwkv_back(y, r, kmod, v, gate, rk, lnw, lnb):
    mean = jnp.mean(y, axis=-1, keepdims=True)
    yc = y - mean
    var = jnp.mean(yc * yc, axis=-1, keepdims=True)
    gn = yc * lax.rsqrt(var + GN_EPS) * lnw + lnb
    bonus = jnp.sum(r * kmod * rk, axis=-1, keepdims=True) * v
    return (gn + bonus) * _silu(gate)


def _rwkv_prompt_kernel(pb_ref, mu_ref, w0_ref, w2_ref, a0_ref, a2_ref, kk_ref, ka_ref,
                        rk_ref, lnw_ref, lnb_ref,
                        o_ref, s_out_ref, sh_out_ref, carry_ref, s_ref):
    c = pl.program_id(1)

    @pl.when(c == 0)
    def _():
        carry_ref[...] = jnp.zeros_like(carry_ref)
        s_ref[...] = jnp.zeros_like(s_ref)

    pb = pb_ref[...]
    rowi = _iota2((TC, 1), 0)
    prev = jnp.where(rowi == 0, carry_ref[0:1, :], pltpu.roll(pb, 1, 0))
    carry_ref[0:1, :] = pb[TC - 1:TC, :]
    sh_out_ref[...] = pb[TC - 1:TC, :]
    xb = pb + (prev - pb) * mu_ref[...]
    r, kmod, v, gate, logw, a, kk = _rwkv_front(
        xb, w0_ref[...], w2_ref[...], a0_ref[...], a2_ref[...], kk_ref[...], ka_ref[...])

    g = _dot(_chunk_tril(TC), logw, prec=HIGHEST)
    eg = jnp.exp(g)
    egn = jnp.exp(-g)
    ag = -kk * jnp.exp(g - logw)
    kka = kk * a
    bg = kka * egn
    kg = kmod * egn
    rg = r * eg
    rk = rk_ref[...]
    lnw = lnw_ref[...]
    lnb = lnb_ref[...]

    ri = _iota2((CHUNK, CHUNK), 0)
    ci = _iota2((CHUNK, CHUNK), 1)
    incl = ci <= ri
    strict = ci < ri

    for j in range(TC // CHUNK):
        rows = slice(j * CHUNK, (j + 1) * CHUNK)
        glast = g[(j + 1) * CHUNK - 1:(j + 1) * CHUNK, :]
        edec = jnp.exp(glast - g[rows, :])
        bdec = kka[rows, :] * edec
        kdec = kmod[rows, :] * edec
        eglast = jnp.exp(glast)
        for h in range(HB):
            ln = slice(h * HS_B, (h + 1) * HS_B)
            ag_h = ag[rows, ln]
            rg_h = rg[rows, ln]
            bg_h = bg[rows, ln]
            kg_h = kg[rows, ln]
            v_h = v[rows, ln]
            a_ab = jnp.where(strict, _dot(ag_h, bg_h, NT), 0.0)
            a_ak = jnp.where(strict, _dot(ag_h, kg_h, NT), 0.0)
            a_rb = jnp.where(incl, _dot(rg_h, bg_h, NT), 0.0)
            a_rk = jnp.where(incl, _dot(rg_h, kg_h, NT), 0.0)
            tinv = _tri_inv(-a_ab)
            s = s_ref[h]
            u = _dot(tinv, _dot(ag_h, s, NT) + _dot(a_ak, v_h))
            y = _dot(rg_h, s, NT) + _dot(a_rb, u) + _dot(a_rk, v_h)
            s_ref[h] = s * eglast[:, ln] + _dot(u, bdec[:, ln], TN) + _dot(v_h, kdec[:, ln], TN)
            o_ref[rows, ln] = _rwkv_back(y, r[rows, ln], kmod[rows, ln], v_h, gate[rows, ln],
                                         rk[:, ln], lnw[:, ln], lnb[:, ln])

    s_out_ref[...] = s_ref[...]


def _rwkv_prompt(pb, mu, w0, w2, a0, a2, k_k, k_a, rk, lnw, lnb):
    b, t, _ = pb.shape
    blk = lambda i, j: (i, j, 0)
    fixed = lambda i, j: (0, 0)
    vec = pl.BlockSpec((1, MIX_B), fixed)
    return pl.pallas_call(
        _rwkv_prompt_kernel,
        grid=(b, t // TC),
        in_specs=[pl.BlockSpec((None, TC, B_W), blk),
                  pl.BlockSpec((1, B_W), fixed),
                  vec, pl.BlockSpec((LORA, MIX_B), fixed),
                  vec, pl.BlockSpec((LORA, MIX_B), fixed),
                  vec, vec, vec, vec, vec],
        out_specs=[pl.BlockSpec((None, TC, MIX_B), blk),
                   pl.BlockSpec((None, HB, HS_B, HS_B), lambda i, j: (i, 0, 0, 0)),
                   pl.BlockSpec((None, 1, B_W), lambda i, j: (i, 0, 0))],
        out_shape=[jax.ShapeDtypeStruct((b, t, MIX_B), F32),
                   jax.ShapeDtypeStruct((b, HB, HS_B, HS_B), F32),
                   jax.ShapeDtypeStruct((b, 1, B_W), F32)],
        scratch_shapes=[pltpu.VMEM((8, B_W), F32),
                        pltpu.VMEM((HB, HS_B, HS_B), F32)],
        compiler_params=pltpu.CompilerParams(dimension_semantics=("arbitrary", "arbitrary"),
                                             vmem_limit_bytes=VMEM_LIMIT),
        name="rwkv_prompt",
    )(pb, mu, w0, w2, a0, a2, k_k, k_a, rk, lnw, lnb)


def _pad_rows(rows, width):
    return jnp.concatenate(rows + [jnp.zeros((8 - len(rows), width), F32)], axis=0)


def _gdn_step_kernel(pa_ref, bd_ref, sc_ref, s_ref, cw_ref, hp_ref, anw_ref,
                     o_ref, s_out_ref, c_out_ref,
                     q_scr, k_scr, v_scr, beta_scr, eg_scr, o_scr):
    u = pa_ref[:, 0:A_QKV]
    cw = cw_ref[...]
    conv = (sc_ref[0] * cw[0:1, :] + sc_ref[1] * cw[1:2, :] + sc_ref[2] * cw[2:3, :] + u * cw[3:4, :])
    qkv = _silu(conv)
    c_out_ref[0] = sc_ref[1]
    c_out_ref[1] = sc_ref[2]
    c_out_ref[2] = u

    beta_all, g_all = _gdn_gates(bd_ref[...], hp_ref[...])
    eg_all = jnp.exp(g_all)
    for h in range(HA):
        q_scr[h] = _l2norm(qkv[:, h * DK_A:(h + 1) * DK_A]) * (DK_A ** -0.5)
        k_scr[h] = _l2norm(qkv[:, MIX_A + h * DK_A:MIX_A + (h + 1) * DK_A])
        v_scr[h] = qkv[:, 2 * MIX_A + h * DK_A:2 * MIX_A + (h + 1) * DK_A]
        beta_scr[h] = jnp.broadcast_to(beta_all[:, h:h + 1], (SB, DK_A))
        eg_scr[h] = jnp.broadcast_to(eg_all[:, HA + h:HA + h + 1], (SB, DK_A))

    def body(b, carry):
        for h in range(HA):
            row = pl.ds(b, 1)
            q = q_scr[h, row, :]
            k = k_scr[h, row, :]
            v = v_scr[h, row, :]
            beta = beta_scr[h, row, :]
            eg = eg_scr[h, row, :]
            s0 = s_ref[b, h]
            ks_qs = _dot(_pad_rows([k, q], DK_A), s0)
            u2 = beta * (v - eg * ks_qs[0:1, :])
            qk = jnp.sum(q * k, axis=-1, keepdims=True)
            o_scr[h, row, :] = eg * ks_qs[1:2, :] + qk * u2
            s_out_ref[b, h] = eg[:, 0:1] * s0 + _dot(_pad_rows([k], DK_A), _pad_rows([u2], DK_A), TN)
        return carry

    lax.fori_loop(0, SB, body, 0)

    anw = anw_ref[...]
    for h in range(HA):
        o = o_scr[h]
        on = o * lax.rsqrt(jnp.mean(o * o, axis=-1, keepdims=True) + RMS_EPS) * anw
        gate = pa_ref[:, A_QKV + h * DK_A:A_QKV + (h + 1) * DK_A]
        o_ref[:, h * DK_A:(h + 1) * DK_A] = on * _silu(gate)


def _gdn_step(pa, bd, sconv_t, s_a, conv_w, hp, anw):
    n = pa.shape[0]
    row = lambda i: (i, 0)
    fixed = lambda i: (0, 0)
    hv = pltpu.VMEM((HA, SB, DK_A), F32)
    return pl.pallas_call(
        _gdn_step_kernel,
        grid=(n // SB,),
        in_specs=[pl.BlockSpec((SB, A_MAIN), row),
                  pl.BlockSpec((SB, BD_W), row),
                  pl.BlockSpec((CONV_W - 1, SB, A_QKV), lambda i: (0, i, 0)),
                  pl.BlockSpec((SB, HA, DK_A, DK_A), lambda i: (i, 0, 0, 0)),
                  pl.BlockSpec((CONV_W, A_QKV), fixed),
                  pl.BlockSpec((8, BD_W), fixed),
                  pl.BlockSpec((1, DK_A), fixed)],
        out_specs=[pl.BlockSpec((SB, MIX_A), row),
                   pl.BlockSpec((SB, HA, DK_A, DK_A), lambda i: (i, 0, 0, 0)),
                   pl.BlockSpec((CONV_W - 1, SB, A_QKV), lambda i: (0, i, 0))],
        out_shape=[jax.ShapeDtypeStruct((n, MIX_A), F32),
                   jax.ShapeDtypeStruct((n, HA, DK_A, DK_A), F32),
                   jax.ShapeDtypeStruct((CONV_W - 1, n, A_QKV), F32)],
        scratch_shapes=[hv, hv, hv, hv, hv, hv],
        compiler_params=pltpu.CompilerParams(dimension_semantics=("arbitrary",),
                                             vmem_limit_bytes=VMEM_LIMIT),
        name="gdn_step",
    )(pa, bd, sconv_t, s_a, conv_w, hp, anw)


def _rwkv_step_kernel(pb_ref, sh_ref, s_ref, mu_ref, w0_ref, w2_ref, a0_ref, a2_ref, kk_ref, ka_ref,
                      rk_ref, lnw_ref, lnb_ref,
                      o_ref, s_out_ref,
                      nkk_scr, b_scr, k_scr, v_scr, r_scr, w_scr, y_scr):
    pb = pb_ref[...]
    xb = pb + (sh_ref[...] - pb) * mu_ref[...]
    r, kmod, v, gate, logw, a, kk = _rwkv_front(
        xb, w0_ref[...], w2_ref[...], a0_ref[...], a2_ref[...], kk_ref[...], ka_ref[...])
    wdec = jnp.exp(logw)
    for h in range(HB):
        ln = slice(h * HS_B, (h + 1) * HS_B)
        nkk_scr[h] = -kk[:, ln]
        b_scr[h] = (kk * a)[:, ln]
        k_scr[h] = kmod[:, ln]
        v_scr[h] = v[:, ln]
        r_scr[h] = r[:, ln]
        w_scr[h] = wdec[:, ln]

    def body(b, carry):
        row = pl.ds(b, 1)
        for h in range(HB):
            s0 = s_ref[b, h]
            sa = _dot(_pad_rows([nkk_scr[h, row, :]], HS_B), s0, NT)[0:1, :]
            upd = _dot(_pad_rows([sa, v_scr[h, row, :]], HS_B),
                       _pad_rows([b_scr[h, row, :], k_scr[h, row, :]], HS_B), TN)
            s1 = s0 * w_scr[h, row, :] + upd
            s_out_ref[b, h] = s1
            y_scr[h, row, :] = _dot(_pad_rows([r_scr[h, row, :]], HS_B), s1, NT)[0:1, :]
        return carry

    lax.fori_loop(0, SB, body, 0)

    rk = rk_ref[...]
    lnw = lnw_ref[...]
    lnb = lnb_ref[...]
    for h in range(HB):
        ln = slice(h * HS_B, (h + 1) * HS_B)
        o_ref[:, ln] = _rwkv_back(y_scr[h], r[:, ln], kmod[:, ln], v[:, ln], gate[:, ln],
                                  rk[:, ln], lnw[:, ln], lnb[:, ln])


def _rwkv_step(pb, shift, s_b, mu, w0, w2, a0, a2, k_k, k_a, rk, lnw, lnb):
    n = pb.shape[0]
    row = lambda i: (i, 0)
    fixed = lambda i: (0, 0)
    vec = pl.BlockSpec((1, MIX_B), fixed)
    hv = pltpu.VMEM((HB, SB, HS_B), F32)
    return pl.pallas_call(
        _rwkv_step_kernel,
        grid=(n // SB,),
        in_specs=[pl.BlockSpec((SB, B_W), row),
                  pl.BlockSpec((SB, B_W), row),
                  pl.BlockSpec((SB, HB, HS_B, HS_B), lambda i: (i, 0, 0, 0)),
                  pl.BlockSpec((1, B_W), fixed),
                  vec, pl.BlockSpec((LORA, MIX_B), fixed),
                  vec, pl.BlockSpec((LORA, MIX_B), fixed),
                  vec, vec, vec, vec, vec],
        out_specs=[pl.BlockSpec((SB, MIX_B), row),
                   pl.BlockSpec((SB, HB, HS_B, HS_B), lambda i: (i, 0, 0, 0))],
        out_shape=[jax.ShapeDtypeStruct((n, MIX_B), F32),
                   jax.ShapeDtypeStruct((n, HB, HS_B, HS_B), F32)],
        scratch_shapes=[hv, hv, hv, hv, hv, hv, hv],
        compiler_params=pltpu.CompilerParams(dimension_semantics=("arbitrary",),
                                             vmem_limit_bytes=VMEM_LIMIT),
        name="rwkv_step",
    )(pb, shift, s_b, mu, w0, w2, a0, a2, k_k, k_a, rk, lnw, lnb)


def kernel(x_prompt, x_sample, state_a_mat, state_a_conv, state_b_mat, state_b_shift, norm_w, w_in,
           conv_w, a_log, dt_bias, a_norm_w, mu, w0, w2, a0, a2, k_k, k_a, r_k, ln_w, ln_b, w_out,
           final_norm_w):
    bsz, seq, _ = x_prompt.shape
    nsmp = x_sample.shape[0]
    a_w = A_MAIN + 2 * HA

    w = w_in[0]
    w_cat = jnp.concatenate(
        [w[:, :A_MAIN], w[:, a_w:], w[:, A_MAIN:a_w], jnp.zeros((D_MODEL, BD_W - 2 * HA), F32)],
        axis=1).astype(BF16)
    w_o = w_out[0].astype(BF16)
    hp = jnp.zeros((8, BD_W), F32)
    hp = hp.at[0, HA:2 * HA].set(a_log[0]).at[1, HA:2 * HA].set(dt_bias[0])
    nw = norm_w[0].reshape(1, D_MODEL)
    fw = final_norm_w.reshape(1, D_MODEL)
    anw = a_norm_w[0].reshape(1, DK_A)
    row = lambda z: z[0].reshape(1, -1)
    b_params = (row(mu), row(w0), w2[0], row(a0), a2[0], row(k_k), row(k_a), row(r_k), row(ln_w), row(ln_b))

    xp = x_prompt.reshape(bsz * seq, D_MODEL)
    pa, pb, bd = _inproj(xp, nw, w_cat)
    oa, p_amat, p_aconv = _gdn_prompt(pa.reshape(bsz, seq, A_MAIN), bd.reshape(bsz, seq, BD_W),
                                      conv_w[0], hp, anw)
    ob, p_bmat, p_bshift = _rwkv_prompt(pb.reshape(bsz, seq, B_W), *b_params)
    y_prompt = _outproj(xp, oa.reshape(bsz * seq, MIX_A), ob.reshape(bsz * seq, MIX_B), w_o, fw)

    xs = x_sample.reshape(nsmp, D_MODEL)
    sa_, sb_, sbd = _inproj(xs, nw, w_cat)
    sconv_t = jnp.swapaxes(state_a_conv[0], 0, 1)
    soa, s_amat, s_aconv_t = _gdn_step(sa_, sbd, sconv_t, state_a_mat[0], conv_w[0], hp, anw)
    sob, s_bmat = _rwkv_step(sb_, state_b_shift[0], state_b_mat[0], *b_params)
    y_sample = _outproj(xs, soa, sob, w_o, fw)

    return (y_prompt.reshape(bsz, seq, D_MODEL), y_sample.reshape(nsmp, 1, D_MODEL),
            p_amat[None], p_aconv[None], p_bmat[None], p_bshift.reshape(1, bsz, B_W),
            s_amat[None], jnp.swapaxes(s_aconv_t, 0, 1)[None], s_bmat[None], sb_[None])
```

```python
import functools

import jax
import jax.numpy as jnp
from jax import lax
from jax.experimental import pallas as pl
from jax.experimental.pallas import tpu as pltpu

F32 = jnp.float32
BF16 = jnp.bfloat16

D_MODEL = 1024
MIX_A = 512
MIX_B = 512
HA = 4
DK_A = 128
HB = 8
HS_B = 64
CONV_W = 4
LORA = 64
A_QKV = 3 * MIX_A
A_MAIN = A_QKV + MIX_A
B_W = 4 * MIX_B + 2 * LORA
BD_W = 128
P_CAT = A_MAIN + B_W + BD_W
RMS_EPS = 1e-6
GN_EPS = 64e-5
CHUNK = 64
TC = 256
SB = 8
VMEM_LIMIT = 56 * 1024 * 1024

HIGHEST = lax.Precision.HIGHEST
NN = (((1,), (0,)), ((), ()))
NT = (((1,), (1,)), ((), ()))
TN = (((0,), (0,)), ((), ()))


def _dot(a, b, dn=NN, prec=None):
    return lax.dot_general(a, b, dn, precision=prec, preferred_element_type=F32)


def _sigmoid(x):
    return 1.0 / (1.0 + jnp.exp(-x))


def _silu(x):
    return x * _sigmoid(x)


def _softplus(x):
    return jnp.maximum(x, 0.0) + jnp.log(1.0 + jnp.exp(-jnp.abs(x)))


def _l2norm(x):
    return x * lax.rsqrt(jnp.sum(x * x, axis=-1, keepdims=True) + 1e-12)


def _iota2(shape, dim):
    return lax.broadcasted_iota(jnp.int32, shape, dim)


def _chunk_tril(n):
    r = _iota2((n, n), 0)
    c = _iota2((n, n), 1)
    same = (r // CHUNK) == (c // CHUNK)
    return jnp.where(same & (c <= r), 1.0, 0.0).astype(F32)


def _tri_inv_many(lows):
    n = lows[0].shape[0]
    r = _iota2((n, n), 0)
    c = _iota2((n, n), 1)
    eye = jnp.where(r == c, 1.0, 0.0).astype(F32)
    base = (r // 8) == (c // 8)
    l0 = [jnp.where(base, low, 0.0) for low in lows]
    l2 = [_dot(a, a) for a in l0]
    l4 = [_dot(a, a) for a in l2]
    d = [_dot(eye - a, eye + b) for a, b in zip(l0, l2)]
    d = [_dot(a, eye + b) for a, b in zip(d, l4)]
    s = 8
    while s < n:
        sel = ((r // (2 * s)) == (c // (2 * s))) & ((r // s) != (c // s))
        t = [_dot(jnp.where(sel, low, 0.0), a) for low, a in zip(lows, d)]
        d = [a - _dot(a, b) for a, b in zip(d, t)]
        s *= 2
    return d


def _inproj_kernel(x_ref, nw_ref, w_ref, oa_ref, ob_ref, obd_ref):
    x = x_ref[...]
    xn = x * lax.rsqrt(jnp.mean(x * x, axis=-1, keepdims=True) + RMS_EPS) * nw_ref[...]
    p = _dot(xn.astype(BF16), w_ref[...])
    oa_ref[...] = p[:, :A_MAIN]
    ob_ref[...] = p[:, A_MAIN:A_MAIN + B_W]
    obd_ref[...] = p[:, A_MAIN + B_W:]


def _inproj(x2d, norm_w, w_cat):
    n = x2d.shape[0]
    tm = min(256, n)
    row = lambda i: (i, 0)
    fixed = lambda i: (0, 0)
    return pl.pallas_call(
        _inproj_kernel,
        grid=(n // tm,),
        in_specs=[pl.BlockSpec((tm, D_MODEL), row),
                  pl.BlockSpec((1, D_MODEL), fixed),
                  pl.BlockSpec((D_MODEL, P_CAT), fixed)],
        out_specs=[pl.BlockSpec((tm, A_MAIN), row),
                   pl.BlockSpec((tm, B_W), row),
                   pl.BlockSpec((tm, BD_W), row)],
        out_shape=[jax.ShapeDtypeStruct((n, A_MAIN), F32),
                   jax.ShapeDtypeStruct((n, B_W), F32),
                   jax.ShapeDtypeStruct((n, BD_W), F32)],
        compiler_params=pltpu.CompilerParams(dimension_semantics=("arbitrary",),
                                             vmem_limit_bytes=VMEM_LIMIT),
        name="inproj",
    )(x2d, norm_w, w_cat)


def _outproj_kernel(x_ref, ma_ref, mb_ref, w_ref, fw_ref, y_ref):
    h = (x_ref[...]
         + _dot(ma_ref[...].astype(BF16), w_ref[0:MIX_A, :])
         + _dot(mb_ref[...].astype(BF16), w_ref[MIX_A:, :]))
    y_ref[...] = h * lax.rsqrt(jnp.mean(h * h, axis=-1, keepdims=True) + RMS_EPS) * fw_ref[...]


def _outproj(x2d, mix_a, mix_b, w_out, final_w):
    n = x2d.shape[0]
    tm = min(512, n)
    row = lambda i: (i, 0)
    fixed = lambda i: (0, 0)
    return pl.pallas_call(
        _outproj_kernel,
        grid=(n // tm,),
        in_specs=[pl.BlockSpec((tm, D_MODEL), row),
                  pl.BlockSpec((tm, MIX_A), row),
                  pl.BlockSpec((tm, MIX_B), row),
                  pl.BlockSpec((D_MODEL, D_MODEL), fixed),
                  pl.BlockSpec((1, D_MODEL), fixed)],
        out_specs=pl.BlockSpec((tm, D_MODEL), row),
        out_shape=jax.ShapeDtypeStruct((n, D_MODEL), F32),
        compiler_params=pltpu.CompilerParams(dimension_semantics=("arbitrary",),
                                             vmem_limit_bytes=VMEM_LIMIT),
        name="outproj",
    )(x2d, mix_a, mix_b, w_out, final_w)


def _gdn_gates(bd, hp):
    beta = _sigmoid(bd)
    g = -jnp.exp(hp[0:1, :]) * _softplus(bd + hp[1:2, :])
    return beta, g


def _gdn_prompt_kernel(pa_ref, bd_ref, cw_ref, hp_ref, anw_ref,
                       o_ref, s_out_ref, c_out_ref, ext_ref, s_ref):
    c = pl.program_id(1)

    @pl.when(c == 0)
    def _():
        ext_ref[0:8, :] = jnp.zeros((8, A_QKV), F32)
        s_ref[...] = jnp.zeros_like(s_ref)

    ext_ref[8:8 + TC, :] = pa_ref[:, 0:A_QKV]
    cw = cw_ref[...]
    conv = (ext_ref[5:5 + TC, :] * cw[0:1, :] + ext_ref[6:6 + TC, :] * cw[1:2, :]
            + ext_ref[7:7 + TC, :] * cw[2:3, :] + ext_ref[8:8 + TC, :] * cw[3:4, :])
    qkv = _silu(conv)
    c_out_ref[...] = ext_ref[TC + 5:TC + 8, :]
    ext_ref[0:8, :] = ext_ref[TC:TC + 8, :]

    beta_all, g_all = _gdn_gates(bd_ref[...], hp_ref[...])
    gcum = _dot(_chunk_tril(TC), g_all, prec=HIGHEST)
    gcum_t = gcum.T
    anw = anw_ref[...]

    r = _iota2((CHUNK, CHUNK), 0)
    cc = _iota2((CHUNK, CHUNK), 1)
    incl = cc <= r
    strict = cc < r

    nj = TC // CHUNK
    jh = [(j, h) for j in range(nj) for h in range(HA)]
    rows_of = lambda j: slice(j * CHUNK, (j + 1) * CHUNK)
    q = [_l2norm(qkv[rows_of(j), h * DK_A:(h + 1) * DK_A]) * (DK_A ** -0.5) for j, h in jh]
    k = [_l2norm(qkv[rows_of(j), MIX_A + h * DK_A:MIX_A + (h + 1) * DK_A]) for j, h in jh]
    v = [qkv[rows_of(j), 2 * MIX_A + h * DK_A:2 * MIX_A + (h + 1) * DK_A] for j, h in jh]
    beta = [beta_all[rows_of(j), h:h + 1] for j, h in jh]
    gc = [gcum[rows_of(j), HA + h:HA + h + 1] for j, h in jh]
    gr = [gcum_t[HA + h:HA + h + 1, rows_of(j)] for j, h in jh]
    glast = [gcum[(j + 1) * CHUNK - 1:(j + 1) * CHUNK, HA + h:HA + h + 1] for j, h in jh]
    decay = [jnp.where(incl, jnp.exp(jnp.where(incl, a - b, 0.0)), 0.0) for a, b in zip(gc, gr)]
    kk = [_dot(a, a, NT) for a in k]
    qk = [_dot(a, b, NT) for a, b in zip(q, k)]
    low = [jnp.where(strict, b * d * m, 0.0) for b, d, m in zip(beta, decay, kk)]
    tinv = _tri_inv_many(low)
    eg = [jnp.exp(a) for a in gc]
    w = [_dot(t, (b * e) * a) for t, b, e, a in zip(tinv, beta, eg, k)]
    u = [_dot(t, b * a) for t, b, a in zip(tinv, beta, v)]
    qk = [d * m for d, m in zip(decay, qk)]
    qg = [a * e for a, e in zip(q, eg)]
    kdec_t = [(a * jnp.exp(gl - g)).T for a, gl, g in zip(k, glast, gc)]
    eglast = [jnp.exp(gl) for gl in glast]

    for j in range(nj):
        idx = [j * HA + h for h in range(HA)]
        s = [s_ref[h] for h in range(HA)]
        u2 = [u[i] - _dot(w[i], s[h]) for h, i in enumerate(idx)]
        for h, i in enumerate(idx):
            s_ref[h] = eglast[i] * s[h] + _dot(kdec_t[i], u2[h])
        o = [_dot(qg[i], s[h]) + _dot(qk[i], u2[h]) for h, i in enumerate(idx)]
        for h in range(HA):
            on = o[h] * lax.rsqrt(jnp.mean(o[h] * o[h], axis=-1, keepdims=True) + RMS_EPS) * anw
            gate = pa_ref[rows_of(j), A_QKV + h * DK_A:A_QKV + (h + 1) * DK_A]
            o_ref[rows_of(j), h * DK_A:(h + 1) * DK_A] = on * _silu(gate)

    s_out_ref[...] = s_ref[...]


def _gdn_prompt(pa, bd, conv_w, hp, anw):
    b, t, _ = pa.shape
    blk = lambda i, j: (i, j, 0)
    fixed = lambda i, j: (0, 0)
    return pl.pallas_call(
        _gdn_prompt_kernel,
        grid=(b, t // TC),
        in_specs=[pl.BlockSpec((None, TC, A_MAIN), blk),
                  pl.BlockSpec((None, TC, BD_W), blk),
                  pl.BlockSpec((CONV_W, A_QKV), fixed),
                  pl.BlockSpec((8, BD_W), fixed),
                  pl.BlockSpec((1, DK_A), fixed)],
        out_specs=[pl.BlockSpec((None, TC, MIX_A), blk),
                   pl.BlockSpec((None, HA, DK_A, DK_A), lambda i, j: (i, 0, 0, 0)),
                   pl.BlockSpec((None, CONV_W - 1, A_QKV), lambda i, j: (i, 0, 0))],
        out_shape=[jax.ShapeDtypeStruct((b, t, MIX_A), F32),
                   jax.ShapeDtypeStruct((b, HA, DK_A, DK_A), F32),
                   jax.ShapeDtypeStruct((b, CONV_W - 1, A_QKV), F32)],
        scratch_shapes=[pltpu.VMEM((TC + 8, A_QKV), F32),
                        pltpu.VMEM((HA, DK_A, DK_A), F32)],
        compiler_params=pltpu.CompilerParams(dimension_semantics=("arbitrary", "arbitrary"),
                                             vmem_limit_bytes=VMEM_LIMIT),
        name="gdn_prompt",
    )(pa, bd, conv_w, hp, anw)


def _rwkv_front(xb, w0, w2, a0, a2, k_k, k_a):
    r = xb[:, 0:MIX_B]
    kr = xb[:, MIX_B:2 * MIX_B]
    vr = xb[:, 2 * MIX_B:3 * MIX_B]
    gate = xb[:, 3 * MIX_B:4 * MIX_B]
    wd = xb[:, 4 * MIX_B:4 * MIX_B + LORA]
    ad = xb[:, 4 * MIX_B + LORA:]
    w_ll = -_softplus(-(w0 + _dot(jnp.tanh(wd), w2))) - 0.5
    logw = -jnp.exp(w_ll)
    a = _sigmoid(a0 + _dot(ad, a2))
    kraw = kr * k_k
    kk = jnp.concatenate(
        [_l2norm(kraw[:, h * HS_B:(h + 1) * HS_B]) for h in range(HB)], axis=-1)
    kmod = kr * (1.0 + (a - 1.0) * k_a)
    return r, kmod, vr, gate, logw, a, kk


def _rwkv_back(y, r, kmod, v, gate, rk, lnw, lnb):
    mean = jnp.mean(y, axis=-1, keepdims=True)
    yc = y - mean
    var = jnp.mean(yc * yc, axis=-1, keepdims=True)
    gn = yc * lax.rsqrt(var + GN_EPS) * lnw + lnb
    bonus = jnp.sum(r * kmod * rk, axis=-1, keepdims=True) * v
    return (gn + bonus) * _silu(gate)


def _rwkv_prompt_kernel(pb_ref, mu_ref, w0_ref, w2_ref, a0_ref, a2_ref, kk_ref, ka_ref,
                        rk_ref, lnw_ref, lnb_ref,
                        o_ref, s_out_ref, sh_out_ref, carry_ref, s_ref):
    c = pl.program_id(1)

    @pl.when(c == 0)
    def _():
        carry_ref[...] = jnp.zeros_like(carry_ref)
        s_ref[...] = jnp.zeros_like(s_ref)

    pb = pb_ref[...]
    rowi = _iota2((TC, 1), 0)
    prev = jnp.where(rowi == 0, carry_ref[0:1, :], pltpu.roll(pb, 1, 0))
    carry_ref[0:1, :] = pb[TC - 1:TC, :]
    sh_out_ref[...] = pb[TC - 1:TC, :]
    xb = pb + (prev - pb) * mu_ref[...]
    r, kmod, v, gate, logw, a, kk = _rwkv_front(
        xb, w0_ref[...], w2_ref[...], a0_ref[...], a2_ref[...], kk_ref[...], ka_ref[...])

    g = _dot(_chunk_tril(TC), logw, prec=HIGHEST)
    eg = jnp.exp(g)
    egn = jnp.exp(-g)
    ag = -kk * jnp.exp(g - logw)
    kka = kk * a
    bg = kka * egn
    kg = kmod * egn
    rg = r * eg
    rk = rk_ref[...]
    lnw = lnw_ref[...]
    lnb = lnb_ref[...]

    ri = _iota2((CHUNK, CHUNK), 0)
    ci = _iota2((CHUNK, CHUNK), 1)
    incl = ci <= ri
    strict = ci < ri

    nj = TC // CHUNK
    jh = [(j, h) for j in range(nj) for h in range(HB)]
    rows_of = lambda j: slice(j * CHUNK, (j + 1) * CHUNK)
    lanes_of = lambda h: slice(h * HS_B, (h + 1) * HS_B)
    glast = [g[(j + 1) * CHUNK - 1:(j + 1) * CHUNK, :] for j in range(nj)]
    edec = [jnp.exp(glast[j] - g[rows_of(j), :]) for j in range(nj)]
    bdec = [kka[rows_of(j), :] * edec[j] for j in range(nj)]
    kdec = [kmod[rows_of(j), :] * edec[j] for j in range(nj)]
    eglast = [jnp.exp(glast[j]) for j in range(nj)]

    cut = lambda z: [z[rows_of(j), lanes_of(h)] for j, h in jh]
    ag_h, rg_h, bg_h, kg_h, v_h = cut(ag), cut(rg), cut(bg), cut(kg), cut(v)
    a_ab = [jnp.where(strict, _dot(a, b, NT), 0.0) for a, b in zip(ag_h, bg_h)]
    a_ak = [jnp.where(strict, _dot(a, b, NT), 0.0) for a, b in zip(ag_h, kg_h)]
    a_rb = [jnp.where(incl, _dot(a, b, NT), 0.0) for a, b in zip(rg_h, bg_h)]
    a_rk = [jnp.where(incl, _dot(a, b, NT), 0.0) for a, b in zip(rg_h, kg_h)]
    tinv = _tri_inv_many([-a for a in a_ab])
    akv = [_dot(a, b) for a, b in zip(a_ak, v_h)]
    wt_t = [_dot(t, a).T for t, a in zip(tinv, ag_h)]
    ut_t = [_dot(t, a).T for t, a in zip(tinv, akv)]
    vk = [_dot(a, kdec[j][:, lanes_of(h)], TN) for a, (j, h) in zip(v_h, jh)]
    yv = [_dot(a, b) for a, b in zip(a_rk, v_h)]

    for j in range(nj):
        idx = [j * HB + h for h in range(HB)]
        s = [s_ref[h] for h in range(HB)]
        u_t = [_dot(s[h], wt_t[i]) + ut_t[i] for h, i in enumerate(idx)]
        for h, i in enumerate(idx):
            s_ref[h] = (s[h] * eglast[j][:, lanes_of(h)]
                        + _dot(u_t[h], bdec[j][:, lanes_of(h)]) + vk[i])
        y = [_dot(rg_h[i], s[h], NT) + _dot(a_rb[i], u_t[h].T) + yv[i] for h, i in enumerate(idx)]
        for h in range(HB):
            ln = lanes_of(h)
            o_ref[rows_of(j), ln] = _rwkv_back(
                y[h], r[rows_of(j), ln], kmod[rows_of(j), ln], v_h[idx[h]], gate[rows_of(j), ln],
                rk[:, ln], lnw[:, ln], lnb[:, ln])

    s_out_ref[...] = s_ref[...]


def _rwkv_prompt(pb, mu, w0, w2, a0, a2, k_k, k_a, rk, lnw, lnb):
    b, t, _ = pb.shape
    blk = lambda i, j: (i, j, 0)
    fixed = lambda i, j: (0, 0)
    vec = pl.BlockSpec((1, MIX_B), fixed)
    return pl.pallas_call(
        _rwkv_prompt_kernel,
        grid=(b, t // TC),
        in_specs=[pl.BlockSpec((None, TC, B_W), blk),
                  pl.BlockSpec((1, B_W), fixed),
                  vec, pl.BlockSpec((LORA, MIX_B), fixed),
                  vec, pl.BlockSpec((LORA, MIX_B), fixed),
                  vec, vec, vec, vec, vec],
        out_specs=[pl.BlockSpec((None, TC, MIX_B), blk),
                   pl.BlockSpec((None, HB, HS_B, HS_B), lambda i, j: (i, 0, 0, 0)),
                   pl.BlockSpec((None, 1, B_W), lambda i, j: (i, 0, 0))],
        out_shape=[jax.ShapeDtypeStruct((b, t, MIX_B), F32),
                   jax.ShapeDtypeStruct((b, HB, HS_B, HS_B), F32),
                   jax.ShapeDtypeStruct((b, 1, B_W), F32)],
        scratch_shapes=[pltpu.VMEM((8, B_W), F32),
                        pltpu.VMEM((HB, HS_B, HS_B), F32)],
        compiler_params=pltpu.CompilerParams(dimension_semantics=("arbitrary", "arbitrary"),
                                             vmem_limit_bytes=VMEM_LIMIT),
        name="rwkv_prompt",
    )(pb, mu, w0, w2, a0, a2, k_k, k_a, rk, lnw, lnb)


def _pad_rows(rows, width):
    return jnp.concatenate(rows + [jnp.zeros((8 - len(rows), width), F32)], axis=0)


def _gdn_step_kernel(pa_ref, bd_ref, sc_ref, s_ref, cw_ref, hp_ref, anw_ref,
                     o_ref, s_out_ref, c_out_ref,
                     q_scr, k_scr, v_scr, beta_scr, eg_scr, o_scr):
    u = pa_ref[:, 0:A_QKV]
    cw = cw_ref[...]
    conv = (sc_ref[0] * cw[0:1, :] + sc_ref[1] * cw[1:2, :] + sc_ref[2] * cw[2:3, :] + u * cw[3:4, :])
    qkv = _silu(conv)
    c_out_ref[0] = sc_ref[1]
    c_out_ref[1] = sc_ref[2]
    c_out_ref[2] = u

    beta_all, g_all = _gdn_gates(bd_ref[...], hp_ref[...])
    eg_all = jnp.exp(g_all)
    for h in range(HA):
        q_scr[h] = _l2norm(qkv[:, h * DK_A:(h + 1) * DK_A]) * (DK_A ** -0.5)
        k_scr[h] = _l2norm(qkv[:, MIX_A + h * DK_A:MIX_A + (h + 1) * DK_A])
        v_scr[h] = qkv[:, 2 * MIX_A + h * DK_A:2 * MIX_A + (h + 1) * DK_A]
        beta_scr[h] = jnp.broadcast_to(beta_all[:, h:h + 1], (SB, DK_A))
        eg_scr[h] = jnp.broadcast_to(eg_all[:, HA + h:HA + h + 1], (SB, DK_A))

    def body(b, carry):
        for h in range(HA):
            row = pl.ds(b, 1)
            q = q_scr[h, row, :]
            k = k_scr[h, row, :]
            v = v_scr[h, row, :]
            beta = beta_scr[h, row, :]
            eg = eg_scr[h, row, :]
            s0 = s_ref[b, h]
            ks_qs = _dot(_pad_rows([k, q], DK_A), s0)
            u2 = beta * (v - eg * ks_qs[0:1, :])
            qk = jnp.sum(q * k, axis=-1, keepdims=True)
            o_scr[h, row, :] = eg * ks_qs[1:2, :] + qk * u2
            s_out_ref[b, h] = eg[:, 0:1] * s0 + _dot(_pad_rows([k], DK_A), _pad_rows([u2], DK_A), TN)
        return carry

    lax.fori_loop(0, SB, body, 0)

    anw = anw_ref[...]
    for h in range(HA):
        o = o_scr[h]
        on = o * lax.rsqrt(jnp.mean(o * o, axis=-1, keepdims=True) + RMS_EPS) * anw
        gate = pa_ref[:, A_QKV + h * DK_A:A_QKV + (h + 1) * DK_A]
        o_ref[:, h * DK_A:(h + 1) * DK_A] = on * _silu(gate)


def _gdn_step(pa, bd, sconv_t, s_a, conv_w, hp, anw):
    n = pa.shape[0]
    row = lambda i: (i, 0)
    fixed = lambda i: (0, 0)
    hv = pltpu.VMEM((HA, SB, DK_A), F32)
    return pl.pallas_call(
        _gdn_step_kernel,
        grid=(n // SB,),
        in_specs=[pl.BlockSpec((SB, A_MAIN), row),
                  pl.BlockSpec((SB, BD_W), row),
                  pl.BlockSpec((CONV_W - 1, SB, A_QKV), lambda i: (0, i, 0)),
                  pl.BlockSpec((SB, HA, DK_A, DK_A), lambda i: (i, 0, 0, 0)),
                  pl.BlockSpec((CONV_W, A_QKV), fixed),
                  pl.BlockSpec((8, BD_W), fixed),
                  pl.BlockSpec((1, DK_A), fixed)],
        out_specs=[pl.BlockSpec((SB, MIX_A), row),
                   pl.BlockSpec((SB, HA, DK_A, DK_A), lambda i: (i, 0, 0, 0)),
                   pl.BlockSpec((CONV_W - 1, SB, A_QKV), lambda i: (0, i, 0))],
        out_shape=[jax.ShapeDtypeStruct((n, MIX_A), F32),
                   jax.ShapeDtypeStruct((n, HA, DK_A, DK_A), F32),
                   jax.ShapeDtypeStruct((CONV_W - 1, n, A_QKV), F32)],
        scratch_shapes=[hv, hv, hv, hv, hv, hv],
        compiler_params=pltpu.CompilerParams(dimension_semantics=("arbitrary",),
                                             vmem_limit_bytes=VMEM_LIMIT),
        name="gdn_step",
    )(pa, bd, sconv_t, s_a, conv_w, hp, anw)


def _rwkv_step_kernel(pb_ref, sh_ref, s_ref, mu_ref, w0_ref, w2_ref, a0_ref, a2_ref, kk_ref, ka_ref,
                      rk_ref, lnw_ref, lnb_ref,
                      o_ref, s_out_ref,
                      nkk_scr, b_scr, k_scr, v_scr, r_scr, w_scr, y_scr):
    pb = pb_ref[...]
    xb = pb + (sh_ref[...] - pb) * mu_ref[...]
    r, kmod, v, gate, logw, a, kk = _rwkv_front(
        xb, w0_ref[...], w2_ref[...], a0_ref[...], a2_ref[...], kk_ref[...], ka_ref[...])
    wdec = jnp.exp(logw)
    for h in range(HB):
        ln = slice(h * HS_B, (h + 1) * HS_B)
        nkk_scr[h] = -kk[:, ln]
        b_scr[h] = (kk * a)[:, ln]
        k_scr[h] = kmod[:, ln]
        v_scr[h] = v[:, ln]
        r_scr[h] = r[:, ln]
        w_scr[h] = wdec[:, ln]

    def body(b, carry):
        row = pl.ds(b, 1)
        for h in range(HB):
            s0 = s_ref[b, h]
            sa = _dot(_pad_rows([nkk_scr[h, row, :]], HS_B), s0, NT)[0:1, :]
            upd = _dot(_pad_rows([sa, v_scr[h, row, :]], HS_B),
                       _pad_rows([b_scr[h, row, :], k_scr[h, row, :]], HS_B), TN)
            s1 = s0 * w_scr[h, row, :] + upd
            s_out_ref[b, h] = s1
            y_scr[h, row, :] = _dot(_pad_rows([r_scr[h, row, :]], HS_B), s1, NT)[0:1, :]
        return carry

    lax.fori_loop(0, SB, body, 0)

    rk = rk_ref[...]
    lnw = lnw_ref[...]
    lnb = lnb_ref[...]
    for h in range(HB):
        ln = slice(h * HS_B, (h + 1) * HS_B)
        o_ref[:, ln] = _rwkv_back(y_scr[h], r[:, ln], kmod[:, ln], v[:, ln], gate[:, ln],
                                  rk[:, ln], lnw[:, ln], lnb[:, ln])


def _rwkv_step(pb, shift, s_b, mu, w0, w2, a0, a2, k_k, k_a, rk, lnw, lnb):
    n = pb.shape[0]
    row = lambda i: (i, 0)
    fixed = lambda i: (0, 0)
    vec = pl.BlockSpec((1, MIX_B), fixed)
    hv = pltpu.VMEM((HB, SB, HS_B), F32)
    return pl.pallas_call(
        _rwkv_step_kernel,
        grid=(n // SB,),
        in_specs=[pl.BlockSpec((SB, B_W), row),
                  pl.BlockSpec((SB, B_W), row),
                  pl.BlockSpec((SB, HB, HS_B, HS_B), lambda i: (i, 0, 0, 0)),
                  pl.BlockSpec((1, B_W), fixed),
                  vec, pl.BlockSpec((LORA, MIX_B), fixed),
                  vec, pl.BlockSpec((LORA, MIX_B), fixed),
                  vec, vec, vec, vec, vec],
        out_specs=[pl.BlockSpec((SB, MIX_B), row),
                   pl.BlockSpec((SB, HB, HS_B, HS_B), lambda i: (i, 0, 0, 0))],
        out_shape=[jax.ShapeDtypeStruct((n, MIX_B), F32),
                   jax.ShapeDtypeStruct((n, HB, HS_B, HS_B), F32)],
        scratch_shapes=[hv, hv, hv, hv, hv, hv, hv],
        compiler_params=pltpu.CompilerParams(dimension_semantics=("arbitrary",),
                                             vmem_limit_bytes=VMEM_LIMIT),
        name="rwkv_step",
    )(pb, shift, s_b, mu, w0, w2, a0, a2, k_k, k_a, rk, lnw, lnb)


def kernel(x_prompt, x_sample, state_a_mat, state_a_conv, state_b_mat, state_b_shift, norm_w, w_in,
           conv_w, a_log, dt_bias, a_norm_w, mu, w0, w2, a0, a2, k_k, k_a, r_k, ln_w, ln_b, w_out,
           final_norm_w):
    bsz, seq, _ = x_prompt.shape
    nsmp = x_sample.shape[0]
    a_w = A_MAIN + 2 * HA

    w = w_in[0]
    w_cat = jnp.concatenate(
        [w[:, :A_MAIN], w[:, a_w:], w[:, A_MAIN:a_w], jnp.zeros((D_MODEL, BD_W - 2 * HA), F32)],
        axis=1).astype(BF16)
    w_o = w_out[0].astype(BF16)
    hp = jnp.zeros((8, BD_W), F32)
    hp = hp.at[0, HA:2 * HA].set(a_log[0]).at[1, HA:2 * HA].set(dt_bias[0])
    nw = norm_w[0].reshape(1, D_MODEL)
    fw = final_norm_w.reshape(1, D_MODEL)
    anw = a_norm_w[0].reshape(1, DK_A)
    row = lambda z: z[0].reshape(1, -1)
    b_params = (row(mu), row(w0), w2[0], row(a0), a2[0], row(k_k), row(k_a), row(r_k), row(ln_w), row(ln_b))

    xp = x_prompt.reshape(bsz * seq, D_MODEL)
    pa, pb, bd = _inproj(xp, nw, w_cat)
    oa, p_amat, p_aconv = _gdn_prompt(pa.reshape(bsz, seq, A_MAIN), bd.reshape(bsz, seq, BD_W),
                                      conv_w[0], hp, anw)
    ob, p_bmat, p_bshift = _rwkv_prompt(pb.reshape(bsz, seq, B_W), *b_params)
    y_prompt = _outproj(xp, oa.reshape(bsz * seq, MIX_A), ob.reshape(bsz * seq, MIX_B), w_o, fw)

    xs = x_sample.reshape(nsmp, D_MODEL)
    sa_, sb_, sbd = _inproj(xs, nw, w_cat)
    sconv_t = jnp.swapaxes(state_a_conv[0], 0, 1)
    soa, s_amat, s_aconv_t = _gdn_step(sa_, sbd, sconv_t, state_a_mat[0], conv_w[0], hp, anw)
    sob, s_bmat = _rwkv_step(sb_, state_b_shift[0], state_b_mat[0], *b_params)
    y_sample = _outproj(xs, soa, sob, w_o, fw)

    return (y_prompt.reshape(bsz, seq, D_MODEL), y_sample.reshape(nsmp, 1, D_MODEL),
            p_amat[None], p_aconv[None], p_bmat[None], p_bshift.reshape(1, bsz, B_W),
            s_amat[None], jnp.swapaxes(s_aconv_t, 0, 1)[None], s_bmat[None], sb_[None])
```

```python
import functools

import jax
import jax.numpy as jnp
from jax import lax
from jax.experimental import pallas as pl
from jax.experimental.pallas import tpu as pltpu

F32 = jnp.float32
BF16 = jnp.bfloat16

D_MODEL = 1024
MIX_A = 512
MIX_B = 512
HA = 4
DK_A = 128
HB = 8
HS_B = 64
CONV_W = 4
LORA = 64
A_QKV = 3 * MIX_A
A_MAIN = A_QKV + MIX_A
B_W = 4 * MIX_B + 2 * LORA
BD_W = 128
P_CAT = A_MAIN + B_W + BD_W
RMS_EPS = 1e-6
GN_EPS = 64e-5
CHUNK = 64
TC = 256
SB = 8
VMEM_LIMIT = 56 * 1024 * 1024

HIGHEST = lax.Precision.HIGHEST
NN = (((1,), (0,)), ((), ()))
NT = (((1,), (1,)), ((), ()))
TN = (((0,), (0,)), ((), ()))


def _dot(a, b, dn=NN, prec=None):
    return lax.dot_general(a, b, dn, precision=prec, preferred_element_type=F32)


def _sigmoid(x):
    return 1.0 / (1.0 + jnp.exp(-x))


def _silu(x):
    return x * _sigmoid(x)


def _softplus(x):
    return jnp.maximum(x, 0.0) + jnp.log(1.0 + jnp.exp(-jnp.abs(x)))


def _l2norm(x):
    return x * lax.rsqrt(jnp.sum(x * x, axis=-1, keepdims=True) + 1e-12)


def _iota2(shape, dim):
    return lax.broadcasted_iota(jnp.int32, shape, dim)


def _chunk_tril(n):
    r = _iota2((n, n), 0)
    c = _iota2((n, n), 1)
    same = (r // CHUNK) == (c // CHUNK)
    return jnp.where(same & (c <= r), 1.0, 0.0).astype(F32)


def _tri_inv_many(lows):
    n = lows[0].shape[0]
    r = _iota2((n, n), 0)
    c = _iota2((n, n), 1)
    eye = jnp.where(r == c, 1.0, 0.0).astype(F32)
    base = (r // 8) == (c // 8)
    l0 = [jnp.where(base, low, 0.0) for low in lows]
    l2 = [_dot(a, a) for a in l0]
    l4 = [_dot(a, a) for a in l2]
    d = [_dot(eye - a, eye + b) for a, b in zip(l0, l2)]
    d = [_dot(a, eye + b) for a, b in zip(d, l4)]
    s = 8
    while s < n:
        sel = ((r // (2 * s)) == (c // (2 * s))) & ((r // s) != (c // s))
        t = [_dot(jnp.where(sel, low, 0.0), a) for low, a in zip(lows, d)]
        d = [a - _dot(a, b) for a, b in zip(d, t)]
        s *= 2
    return d


def _inproj_kernel(x_ref, nw_ref, w_ref, oa_ref, ob_ref, obd_ref):
    x = x_ref[...]
    xn = x * lax.rsqrt(jnp.mean(x * x, axis=-1, keepdims=True) + RMS_EPS) * nw_ref[...]
    p = _dot(xn.astype(BF16), w_ref[...])
    oa_ref[...] = p[:, :A_MAIN]
    ob_ref[...] = p[:, A_MAIN:A_MAIN + B_W]
    obd_ref[...] = p[:, A_MAIN + B_W:]


def _inproj(x2d, norm_w, w_cat):
    n = x2d.shape[0]
    tm = min(256, n)
    row = lambda i: (i, 0)
    fixed = lambda i: (0, 0)
    return pl.pallas_call(
        _inproj_kernel,
        grid=(n // tm,),
        in_specs=[pl.BlockSpec((tm, D_MODEL), row),
                  pl.BlockSpec((1, D_MODEL), fixed),
                  pl.BlockSpec((D_MODEL, P_CAT), fixed)],
        out_specs=[pl.BlockSpec((tm, A_MAIN), row),
                   pl.BlockSpec((tm, B_W), row),
                   pl.BlockSpec((tm, BD_W), row)],
        out_shape=[jax.ShapeDtypeStruct((n, A_MAIN), F32),
                   jax.ShapeDtypeStruct((n, B_W), F32),
                   jax.ShapeDtypeStruct((n, BD_W), F32)],
        compiler_params=pltpu.CompilerParams(dimension_semantics=("arbitrary",),
                                             vmem_limit_bytes=VMEM_LIMIT),
        name="inproj",
    )(x2d, norm_w, w_cat)


def _outproj_kernel(x_ref, ma_ref, mb_ref, w_ref, fw_ref, y_ref):
    h = (x_ref[...]
         + _dot(ma_ref[...].astype(BF16), w_ref[0:MIX_A, :])
         + _dot(mb_ref[...].astype(BF16), w_ref[MIX_A:, :]))
    y_ref[...] = h * lax.rsqrt(jnp.mean(h * h, axis=-1, keepdims=True) + RMS_EPS) * fw_ref[...]


def _outproj(x2d, mix_a, mix_b, w_out, final_w):
    n = x2d.shape[0]
    tm = min(512, n)
    row = lambda i: (i, 0)
    fixed = lambda i: (0, 0)
    return pl.pallas_call(
        _outproj_kernel,
        grid=(n // tm,),
        in_specs=[pl.BlockSpec((tm, D_MODEL), row),
                  pl.BlockSpec((tm, MIX_A), row),
                  pl.BlockSpec((tm, MIX_B), row),
                  pl.BlockSpec((D_MODEL, D_MODEL), fixed),
                  pl.BlockSpec((1, D_MODEL), fixed)],
        out_specs=pl.BlockSpec((tm, D_MODEL), row),
        out_shape=jax.ShapeDtypeStruct((n, D_MODEL), F32),
        compiler_params=pltpu.CompilerParams(dimension_semantics=("arbitrary",),
                                             vmem_limit_bytes=VMEM_LIMIT),
        name="outproj",
    )(x2d, mix_a, mix_b, w_out, final_w)


def _gdn_gates(bd, hp):
    beta = _sigmoid(bd)
    g = -jnp.exp(hp[0:1, :]) * _softplus(bd + hp[1:2, :])
    return beta, g


def _gdn_prompt_kernel(pa_ref, bd_ref, cw_ref, hp_ref, anw_ref,
                       o_ref, s_out_ref, c_out_ref, ext_ref, s_ref):
    c = pl.program_id(1)

    @pl.when(c == 0)
    def _():
        ext_ref[0:8, :] = jnp.zeros((8, A_QKV), F32)
        s_ref[...] = jnp.zeros_like(s_ref)

    ext_ref[8:8 + TC, :] = pa_ref[:, 0:A_QKV]
    cw = cw_ref[...]
    conv = (ext_ref[5:5 + TC, :] * cw[0:1, :] + ext_ref[6:6 + TC, :] * cw[1:2, :]
            + ext_ref[7:7 + TC, :] * cw[2:3, :] + ext_ref[8:8 + TC, :] * cw[3:4, :])
    qkv = _silu(conv)
    c_out_ref[...] = ext_ref[TC + 5:TC + 8, :]
    ext_ref[0:8, :] = ext_ref[TC:TC + 8, :]

    beta_all, g_all = _gdn_gates(bd_ref[...], hp_ref[...])
    gcum = _dot(_chunk_tril(TC), g_all, prec=HIGHEST)
    gcum_t = gcum.T
    anw = anw_ref[...]

    r = _iota2((CHUNK, CHUNK), 0)
    cc = _iota2((CHUNK, CHUNK), 1)
    incl = cc <= r
    strict = cc < r

    nj = TC // CHUNK
    jh = [(j, h) for j in range(nj) for h in range(HA)]
    rows_of = lambda j: slice(j * CHUNK, (j + 1) * CHUNK)
    q = [_l2norm(qkv[rows_of(j), h * DK_A:(h + 1) * DK_A]) * (DK_A ** -0.5) for j, h in jh]
    k = [_l2norm(qkv[rows_of(j), MIX_A + h * DK_A:MIX_A + (h + 1) * DK_A]) for j, h in jh]
    v = [qkv[rows_of(j), 2 * MIX_A + h * DK_A:2 * MIX_A + (h + 1) * DK_A] for j, h in jh]
    beta = [beta_all[rows_of(j), h:h + 1] for j, h in jh]
    gc = [gcum[rows_of(j), HA + h:HA + h + 1] for j, h in jh]
    gr = [gcum_t[HA + h:HA + h + 1, rows_of(j)] for j, h in jh]
    glast = [gcum[(j + 1) * CHUNK - 1:(j + 1) * CHUNK, HA + h:HA + h + 1] for j, h in jh]
    decay = [jnp.where(incl, jnp.exp(jnp.where(incl, a - b, 0.0)), 0.0) for a, b in zip(gc, gr)]
    kk = [_dot(a, a, NT) for a in k]
    qk = [_dot(a, b, NT) for a, b in zip(q, k)]
    low = [jnp.where(strict, b * d * m, 0.0) for b, d, m in zip(beta, decay, kk)]
    tinv = _tri_inv_many(low)
    eg = [jnp.exp(a) for a in gc]
    w = [_dot(t, (b * e) * a) for t, b, e, a in zip(tinv, beta, eg, k)]
    u = [_dot(t, b * a) for t, b, a in zip(tinv, beta, v)]
    qk = [d * m for d, m in zip(decay, qk)]
    qg = [a * e for a, e in zip(q, eg)]
    kdec_t = [(a * jnp.exp(gl - g)).T for a, gl, g in zip(k, glast, gc)]
    eglast = [jnp.exp(gl) for gl in glast]

    for j in range(nj):
        idx = [j * HA + h for h in range(HA)]
        s = [s_ref[h] for h in range(HA)]
        u2 = [u[i] - _dot(w[i], s[h]) for h, i in enumerate(idx)]
        for h, i in enumerate(idx):
            s_ref[h] = eglast[i] * s[h] + _dot(kdec_t[i], u2[h])
        o = [_dot(qg[i], s[h]) + _dot(qk[i], u2[h]) for h, i in enumerate(idx)]
        for h in range(HA):
            on = o[h] * lax.rsqrt(jnp.mean(o[h] * o[h], axis=-1, keepdims=True) + RMS_EPS) * anw
            gate = pa_ref[rows_of(j), A_QKV + h * DK_A:A_QKV + (h + 1) * DK_A]
            o_ref[rows_of(j), h * DK_A:(h + 1) * DK_A] = on * _silu(gate)

    s_out_ref[...] = s_ref[...]


def _gdn_prompt(pa, bd, conv_w, hp, anw):
    b, t, _ = pa.shape
    blk = lambda i, j: (i, j, 0)
    fixed = lambda i, j: (0, 0)
    return pl.pallas_call(
        _gdn_prompt_kernel,
        grid=(b, t // TC),
        in_specs=[pl.BlockSpec((None, TC, A_MAIN), blk),
                  pl.BlockSpec((None, TC, BD_W), blk),
                  pl.BlockSpec((CONV_W, A_QKV), fixed),
                  pl.BlockSpec((8, BD_W), fixed),
                  pl.BlockSpec((1, DK_A), fixed)],
        out_specs=[pl.BlockSpec((None, TC, MIX_A), blk),
                   pl.BlockSpec((None, HA, DK_A, DK_A), lambda i, j: (i, 0, 0, 0)),
                   pl.BlockSpec((None, CONV_W - 1, A_QKV), lambda i, j: (i, 0, 0))],
        out_shape=[jax.ShapeDtypeStruct((b, t, MIX_A), F32),
                   jax.ShapeDtypeStruct((b, HA, DK_A, DK_A), F32),
                   jax.ShapeDtypeStruct((b, CONV_W - 1, A_QKV), F32)],
        scratch_shapes=[pltpu.VMEM((TC + 8, A_QKV), F32),
                        pltpu.VMEM((HA, DK_A, DK_A), F32)],
        compiler_params=pltpu.CompilerParams(dimension_semantics=("arbitrary", "arbitrary"),
                                             vmem_limit_bytes=VMEM_LIMIT),
        name="gdn_prompt",
    )(pa, bd, conv_w, hp, anw)


def _rwkv_front(xb, w0, w2, a0, a2, k_k, k_a):
    r = xb[:, 0:MIX_B]
    kr = xb[:, MIX_B:2 * MIX_B]
    vr = xb[:, 2 * MIX_B:3 * MIX_B]
    gate = xb[:, 3 * MIX_B:4 * MIX_B]
    wd = xb[:, 4 * MIX_B:4 * MIX_B + LORA]
    ad = xb[:, 4 * MIX_B + LORA:]
    w_ll = -_softplus(-(w0 + _dot(jnp.tanh(wd), w2))) - 0.5
    logw = -jnp.exp(w_ll)
    a = _sigmoid(a0 + _dot(ad, a2))
    kraw = kr * k_k
    kk = jnp.concatenate(
        [_l2norm(kraw[:, h * HS_B:(h + 1) * HS_B]) for h in range(HB)], axis=-1)
    kmod = kr * (1.0 + (a - 1.0) * k_a)
    return r, kmod, vr, gate, logw, a, kk


def _rwkv_back(y, r, kmod, v, gate, rk, lnw, lnb):
    mean = jnp.mean(y, axis=-1, keepdims=True)
    yc = y - mean
    var = jnp.mean(yc * yc, axis=-1, keepdims=True)
    gn = yc * lax.rsqrt(var + GN_EPS) * lnw + lnb
    bonus = jnp.sum(r * kmod * rk, axis=-1, keepdims=True) * v
    return (gn + bonus) * _silu(gate)


def _rwkv_prompt_kernel(pb_ref, mu_ref, w0_ref, w2_ref, a0_ref, a2_ref, kk_ref, ka_ref,
                        rk_ref, lnw_ref, lnb_ref,
                        o_ref, s_out_ref, sh_out_ref, carry_ref, s_ref):
    c = pl.program_id(1)

    @pl.when(c == 0)
    def _():
        carry_ref[...] = jnp.zeros_like(carry_ref)
        s_ref[...] = jnp.zeros_like(s_ref)

    pb = pb_ref[...]
    rowi = _iota2((TC, 1), 0)
    prev = jnp.where(rowi == 0, carry_ref[0:1, :], pltpu.roll(pb, 1, 0))
    carry_ref[0:1, :] = pb[TC - 1:TC, :]
    sh_out_ref[...] = pb[TC - 1:TC, :]
    xb = pb + (prev - pb) * mu_ref[...]
    r, kmod, v, gate, logw, a, kk = _rwkv_front(
        xb, w0_ref[...], w2_ref[...], a0_ref[...], a2_ref[...], kk_ref[...], ka_ref[...])

    g = _dot(_chunk_tril(TC), logw, prec=HIGHEST)
    eg = jnp.exp(g)
    egn = jnp.exp(-g)
    ag = -kk * jnp.exp(g - logw)
    kka = kk * a
    bg = kka * egn
    kg = kmod * egn
    rg = r * eg
    rk = rk_ref[...]
    lnw = lnw_ref[...]
    lnb = lnb_ref[...]

    ri = _iota2((CHUNK, CHUNK), 0)
    ci = _iota2((CHUNK, CHUNK), 1)
    incl = ci <= ri
    strict = ci < ri

    nj = TC // CHUNK
    jh = [(j, h) for j in range(nj) for h in range(HB)]
    rows_of = lambda j: slice(j * CHUNK, (j + 1) * CHUNK)
    lanes_of = lambda h: slice(h * HS_B, (h + 1) * HS_B)
    glast = [g[(j + 1) * CHUNK - 1:(j + 1) * CHUNK, :] for j in range(nj)]
    edec = [jnp.exp(glast[j] - g[rows_of(j), :]) for j in range(nj)]
    bdec = [kka[rows_of(j), :] * edec[j] for j in range(nj)]
    kdec = [kmod[rows_of(j), :] * edec[j] for j in range(nj)]
    eglast = [jnp.exp(glast[j]) for j in range(nj)]

    cut = lambda z: [z[rows_of(j), lanes_of(h)] for j, h in jh]
    ag_h, rg_h, bg_h, kg_h, v_h = cut(ag), cut(rg), cut(bg), cut(kg), cut(v)
    a_ab = [jnp.where(strict, _dot(a, b, NT), 0.0) for a, b in zip(ag_h, bg_h)]
    a_ak = [jnp.where(strict, _dot(a, b, NT), 0.0) for a, b in zip(ag_h, kg_h)]
    a_rb = [jnp.where(incl, _dot(a, b, NT), 0.0) for a, b in zip(rg_h, bg_h)]
    a_rk = [jnp.where(incl, _dot(a, b, NT), 0.0) for a, b in zip(rg_h, kg_h)]
    tinv = _tri_inv_many([-a for a in a_ab])
    akv = [_dot(a, b) for a, b in zip(a_ak, v_h)]
    wt_t = [_dot(t, a).T for t, a in zip(tinv, ag_h)]
    ut_t = [_dot(t, a).T for t, a in zip(tinv, akv)]
    vk = [_dot(a, kdec[j][:, lanes_of(h)], TN) for a, (j, h) in zip(v_h, jh)]
    yv = [_dot(a, b) for a, b in zip(a_rk, v_h)]

    for j in range(nj):
        idx = [j * HB + h for h in range(HB)]
        s = [s_ref[h] for h in range(HB)]
        u_t = [_dot(s[h], wt_t[i]) + ut_t[i] for h, i in enumerate(idx)]
        for h, i in enumerate(idx):
            s_ref[h] = (s[h] * eglast[j][:, lanes_of(h)]
                        + _dot(u_t[h], bdec[j][:, lanes_of(h)]) + vk[i])
        y = [_dot(rg_h[i], s[h], NT) + _dot(a_rb[i], u_t[h].T) + yv[i] for h, i in enumerate(idx)]
        for h in range(HB):
            ln = lanes_of(h)
            o_ref[rows_of(j), ln] = _rwkv_back(
                y[h], r[rows_of(j), ln], kmod[rows_of(j), ln], v_h[idx[h]], gate[rows_of(j), ln],
                rk[:, ln], lnw[:, ln], lnb[:, ln])

    s_out_ref[...] = s_ref[...]


def _rwkv_prompt(pb, mu, w0, w2, a0, a2, k_k, k_a, rk, lnw, lnb):
    b, t, _ = pb.shape
    blk = lambda i, j: (i, j, 0)
    fixed = lambda i, j: (0, 0)
    vec = pl.BlockSpec((1, MIX_B), fixed)
    return pl.pallas_call(
        _rwkv_prompt_kernel,
        grid=(b, t // TC),
        in_specs=[pl.BlockSpec((None, TC, B_W), blk),
                  pl.BlockSpec((1, B_W), fixed),
                  vec, pl.BlockSpec((LORA, MIX_B), fixed),
                  vec, pl.BlockSpec((LORA, MIX_B), fixed),
                  vec, vec, vec, vec, vec],
        out_specs=[pl.BlockSpec((None, TC, MIX_B), blk),
                   pl.BlockSpec((None, HB, HS_B, HS_B), lambda i, j: (i, 0, 0, 0)),
                   pl.BlockSpec((None, 1, B_W), lambda i, j: (i, 0, 0))],
        out_shape=[jax.ShapeDtypeStruct((b, t, MIX_B), F32),
                   jax.ShapeDtypeStruct((b, HB, HS_B, HS_B), F32),
                   jax.ShapeDtypeStruct((b, 1, B_W), F32)],
        scratch_shapes=[pltpu.VMEM((8, B_W), F32),
                        pltpu.VMEM((HB, HS_B, HS_B), F32)],
        compiler_params=pltpu.CompilerParams(dimension_semantics=("arbitrary", "arbitrary"),
                                             vmem_limit_bytes=VMEM_LIMIT),
        name="rwkv_prompt",
    )(pb, mu, w0, w2, a0, a2, k_k, k_a, rk, lnw, lnb)


def _pad_rows(rows, width):
    return jnp.concatenate(rows + [jnp.zeros((8 - len(rows), width), F32)], axis=0)


def _gdn_step_kernel(pa_ref, bd_ref, sc_ref, s_ref, cw_ref, hp_ref, anw_ref,
                     o_ref, s_out_ref, c_out_ref):
    u = pa_ref[:, 0:A_QKV]
    cw = cw_ref[...]
    conv = (sc_ref[0] * cw[0:1, :] + sc_ref[1] * cw[1:2, :] + sc_ref[2] * cw[2:3, :] + u * cw[3:4, :])
    qkv = _silu(conv)
    c_out_ref[0] = sc_ref[1]
    c_out_ref[1] = sc_ref[2]
    c_out_ref[2] = u

    beta_all, g_all = _gdn_gates(bd_ref[...], hp_ref[...])
    eg_all = jnp.exp(g_all)
    q = [_l2norm(qkv[:, h * DK_A:(h + 1) * DK_A]) * (DK_A ** -0.5) for h in range(HA)]
    k = [_l2norm(qkv[:, MIX_A + h * DK_A:MIX_A + (h + 1) * DK_A]) for h in range(HA)]
    v = [qkv[:, 2 * MIX_A + h * DK_A:2 * MIX_A + (h + 1) * DK_A] for h in range(HA)]
    beta = [beta_all[:, h:h + 1] for h in range(HA)]
    eg = [eg_all[:, HA + h:HA + h + 1] for h in range(HA)]

    bh = [(b, h) for b in range(SB) for h in range(HA)]
    s0 = [s_ref[b, h] for b, h in bh]
    ks_qs = [_dot(_pad_rows([k[h][b:b + 1, :], q[h][b:b + 1, :]], DK_A), s) for (b, h), s in zip(bh, s0)]
    anw = anw_ref[...]
    for h in range(HA):
        ks = jnp.concatenate([ks_qs[b * HA + h][0:1, :] for b in range(SB)], axis=0)
        qs = jnp.concatenate([ks_qs[b * HA + h][1:2, :] for b in range(SB)], axis=0)
        u2 = beta[h] * (v[h] - eg[h] * ks)
        for b in range(SB):
            s_out_ref[b, h] = (eg[h][b:b + 1, :] * s0[b * HA + h]
                               + _dot(_pad_rows([k[h][b:b + 1, :]], DK_A),
                                      _pad_rows([u2[b:b + 1, :]], DK_A), TN))
        o = eg[h] * qs + jnp.sum(q[h] * k[h], axis=-1, keepdims=True) * u2
        on = o * lax.rsqrt(jnp.mean(o * o, axis=-1, keepdims=True) + RMS_EPS) * anw
        gate = pa_ref[:, A_QKV + h * DK_A:A_QKV + (h + 1) * DK_A]
        o_ref[:, h * DK_A:(h + 1) * DK_A] = on * _silu(gate)


def _gdn_step(pa, bd, sconv_t, s_a, conv_w, hp, anw):
    n = pa.shape[0]
    row = lambda i: (i, 0)
    fixed = lambda i: (0, 0)
    return pl.pallas_call(
        _gdn_step_kernel,
        grid=(n // SB,),
        in_specs=[pl.BlockSpec((SB, A_MAIN), row),
                  pl.BlockSpec((SB, BD_W), row),
                  pl.BlockSpec((CONV_W - 1, SB, A_QKV), lambda i: (0, i, 0)),
                  pl.BlockSpec((SB, HA, DK_A, DK_A), lambda i: (i, 0, 0, 0)),
                  pl.BlockSpec((CONV_W, A_QKV), fixed),
                  pl.BlockSpec((8, BD_W), fixed),
                  pl.BlockSpec((1, DK_A), fixed)],
        out_specs=[pl.BlockSpec((SB, MIX_A), row),
                   pl.BlockSpec((SB, HA, DK_A, DK_A), lambda i: (i, 0, 0, 0)),
                   pl.BlockSpec((CONV_W - 1, SB, A_QKV), lambda i: (0, i, 0))],
        out_shape=[jax.ShapeDtypeStruct((n, MIX_A), F32),
                   jax.ShapeDtypeStruct((n, HA, DK_A, DK_A), F32),
                   jax.ShapeDtypeStruct((CONV_W - 1, n, A_QKV), F32)],
        compiler_params=pltpu.CompilerParams(dimension_semantics=("arbitrary",),
                                             vmem_limit_bytes=VMEM_LIMIT),
        name="gdn_step",
    )(pa, bd, sconv_t, s_a, conv_w, hp, anw)


def _rwkv_step_kernel(pb_ref, sh_ref, s_ref, mu_ref, w0_ref, w2_ref, a0_ref, a2_ref, kk_ref, ka_ref,
                      rk_ref, lnw_ref, lnb_ref,
                      o_ref, s_out_ref):
    pb = pb_ref[...]
    xb = pb + (sh_ref[...] - pb) * mu_ref[...]
    r, kmod, v, gate, logw, a, kk = _rwkv_front(
        xb, w0_ref[...], w2_ref[...], a0_ref[...], a2_ref[...], kk_ref[...], ka_ref[...])
    wdec = jnp.exp(logw)
    nkk = -kk
    kka = kk * a

    bh = [(b, h) for b in range(SB) for h in range(HB)]
    cut = lambda z: [z[b:b + 1, h * HS_B:(h + 1) * HS_B] for b, h in bh]
    nkk_r, kka_r, k_r, v_r, r_r, w_r = cut(nkk), cut(kka), cut(kmod), cut(v), cut(r), cut(wdec)
    s0 = [s_ref[b, h] for b, h in bh]
    sa = [_dot(_pad_rows([x], HS_B), s, NT)[0:1, :] for x, s in zip(nkk_r, s0)]
    upd = [_dot(_pad_rows([x, y], HS_B), _pad_rows([z, t], HS_B), TN)
           for x, y, z, t in zip(sa, v_r, kka_r, k_r)]
    s1 = [s * w + d for s, w, d in zip(s0, w_r, upd)]
    for (b, h), s in zip(bh, s1):
        s_out_ref[b, h] = s
    y_r = [_dot(_pad_rows([x], HS_B), s, NT)[0:1, :] for x, s in zip(r_r, s1)]

    rk = rk_ref[...]
    lnw = lnw_ref[...]
    lnb = lnb_ref[...]
    for h in range(HB):
        ln = slice(h * HS_B, (h + 1) * HS_B)
        y = jnp.concatenate([y_r[b * HB + h] for b in range(SB)], axis=0)
        o_ref[:, ln] = _rwkv_back(y, r[:, ln], kmod[:, ln], v[:, ln], gate[:, ln],
                                  rk[:, ln], lnw[:, ln], lnb[:, ln])


def _rwkv_step(pb, shift, s_b, mu, w0, w2, a0, a2, k_k, k_a, rk, lnw, lnb):
    n = pb.shape[0]
    row = lambda i: (i, 0)
    fixed = lambda i: (0, 0)
    vec = pl.BlockSpec((1, MIX_B), fixed)
    return pl.pallas_call(
        _rwkv_step_kernel,
        grid=(n // SB,),
        in_specs=[pl.BlockSpec((SB, B_W), row),
                  pl.BlockSpec((SB, B_W), row),
                  pl.BlockSpec((SB, HB, HS_B, HS_B), lambda i: (i, 0, 0, 0)),
                  pl.BlockSpec((1, B_W), fixed),
                  vec, pl.BlockSpec((LORA, MIX_B), fixed),
                  vec, pl.BlockSpec((LORA, MIX_B), fixed),
                  vec, vec, vec, vec, vec],
        out_specs=[pl.BlockSpec((SB, MIX_B), row),
                   pl.BlockSpec((SB, HB, HS_B, HS_B), lambda i: (i, 0, 0, 0))],
        out_shape=[jax.ShapeDtypeStruct((n, MIX_B), F32),
                   jax.ShapeDtypeStruct((n, HB, HS_B, HS_B), F32)],
        compiler_params=pltpu.CompilerParams(dimension_semantics=("arbitrary",),
                                             vmem_limit_bytes=VMEM_LIMIT),
        name="rwkv_step",
    )(pb, shift, s_b, mu, w0, w2, a0, a2, k_k, k_a, rk, lnw, lnb)


def kernel(x_prompt, x_sample, state_a_mat, state_a_conv, state_b_mat, state_b_shift, norm_w, w_in,
           conv_w, a_log, dt_bias, a_norm_w, mu, w0, w2, a0, a2, k_k, k_a, r_k, ln_w, ln_b, w_out,
           final_norm_w):
    bsz, seq, _ = x_prompt.shape
    nsmp = x_sample.shape[0]
    a_w = A_MAIN + 2 * HA

    w = w_in[0]
    w_cat = jnp.concatenate(
        [w[:, :A_MAIN], w[:, a_w:], w[:, A_MAIN:a_w], jnp.zeros((D_MODEL, BD_W - 2 * HA), F32)],
        axis=1).astype(BF16)
    w_o = w_out[0].astype(BF16)
    hp = jnp.zeros((8, BD_W), F32)
    hp = hp.at[0, HA:2 * HA].set(a_log[0]).at[1, HA:2 * HA].set(dt_bias[0])
    nw = norm_w[0].reshape(1, D_MODEL)
    fw = final_norm_w.reshape(1, D_MODEL)
    anw = a_norm_w[0].reshape(1, DK_A)
    row = lambda z: z[0].reshape(1, -1)
    b_params = (row(mu), row(w0), w2[0], row(a0), a2[0], row(k_k), row(k_a), row(r_k), row(ln_w), row(ln_b))

    xp = x_prompt.reshape(bsz * seq, D_MODEL)
    pa, pb, bd = _inproj(xp, nw, w_cat)
    oa, p_amat, p_aconv = _gdn_prompt(pa.reshape(bsz, seq, A_MAIN), bd.reshape(bsz, seq, BD_W),
                                      conv_w[0], hp, anw)
    ob, p_bmat, p_bshift = _rwkv_prompt(pb.reshape(bsz, seq, B_W), *b_params)
    y_prompt = _outproj(xp, oa.reshape(bsz * seq, MIX_A), ob.reshape(bsz * seq, MIX_B), w_o, fw)

    xs = x_sample.reshape(nsmp, D_MODEL)
    sa_, sb_, sbd = _inproj(xs, nw, w_cat)
    sconv_t = jnp.swapaxes(state_a_conv[0], 0, 1)
    soa, s_amat, s_aconv_t = _gdn_step(sa_, sbd, sconv_t, state_a_mat[0], conv_w[0], hp, anw)
    sob, s_bmat = _rwkv_step(sb_, state_b_shift[0], state_b_mat[0], *b_params)
    y_sample = _outproj(xs, soa, sob, w_o, fw)

    return (y_prompt.reshape(bsz, seq, D_MODEL), y_sample.reshape(nsmp, 1, D_MODEL),
            p_amat[None], p_aconv[None], p_bmat[None], p_bshift.reshape(1, bsz, B_W),
            s_amat[None], jnp.swapaxes(s_aconv_t, 0, 1)[None], s_bmat[None], sb_[None])
```

```python
import jax
import jax.numpy as jnp
from jax import lax
from jax.experimental import pallas as pl
from jax.experimental.pallas import tpu as pltpu

F32 = jnp.float32
BF16 = jnp.bfloat16

D_MODEL = 1024
MIX_A = 512
MIX_B = 512
HA = 4
DK_A = 128
HB = 8
HS_B = 64
CONV_W = 4
LORA = 64
A_QKV = 3 * MIX_A
A_MAIN = A_QKV + MIX_A
B_W = 4 * MIX_B + 2 * LORA
BD_W = 128
P_CAT = A_MAIN + B_W + BD_W
RMS_EPS = 1e-6
GN_EPS = 64e-5
CHUNK = 64
TC = 256
SB = 8
VMEM_LIMIT = 56 * 1024 * 1024

HIGHEST = lax.Precision.HIGHEST
NN = (((1,), (0,)), ((), ()))
NT = (((1,), (1,)), ((), ()))
TN = (((0,), (0,)), ((), ()))


def _dot(a, b, dn=NN, prec=None):
    return lax.dot_general(a, b, dn, precision=prec, preferred_element_type=F32)


def _sigmoid(x):
    return 1.0 / (1.0 + jnp.exp(-x))


def _silu(x):
    return x * _sigmoid(x)


def _softplus(x):
    return jnp.maximum(x, 0.0) + jnp.log(1.0 + jnp.exp(-jnp.abs(x)))


def _l2norm(x):
    return x * lax.rsqrt(jnp.sum(x * x, axis=-1, keepdims=True) + 1e-12)


def _group_sum(x, ones_blk):
    return _dot(x.astype(BF16), ones_blk)


def _iota2(shape, dim):
    return lax.broadcasted_iota(jnp.int32, shape, dim)


def _chunk_tril(n):
    r = _iota2((n, n), 0)
    c = _iota2((n, n), 1)
    same = (r // CHUNK) == (c // CHUNK)
    return jnp.where(same & (c <= r), 1.0, 0.0).astype(F32)


def _tri_inv_many(lows):
    n = lows[0].shape[0]
    r = _iota2((n, n), 0)
    c = _iota2((n, n), 1)
    eye = jnp.where(r == c, 1.0, 0.0).astype(F32)
    base = (r // 8) == (c // 8)
    l0 = [jnp.where(base, low, 0.0) for low in lows]
    l2 = [_dot(a, a) for a in l0]
    l4 = [_dot(a, a) for a in l2]
    d = [_dot(eye - a, eye + b) for a, b in zip(l0, l2)]
    d = [_dot(a, eye + b) for a, b in zip(d, l4)]
    s = 8
    while s < n:
        sel = ((r // (2 * s)) == (c // (2 * s))) & ((r // s) != (c // s))
        t = [_dot(jnp.where(sel, low, 0.0), a) for low, a in zip(lows, d)]
        d = [a - _dot(a, b) for a, b in zip(d, t)]
        s *= 2
    return d


def _inproj_kernel(x_ref, nw_ref, w_ref, oa_ref, ob_ref, obd_ref):
    x = x_ref[...]
    xn = x * lax.rsqrt(jnp.mean(x * x, axis=-1, keepdims=True) + RMS_EPS) * nw_ref[...]
    p = _dot(xn.astype(BF16), w_ref[...])
    oa_ref[...] = p[:, :A_MAIN]
    ob_ref[...] = p[:, A_MAIN:A_MAIN + B_W]
    obd_ref[...] = p[:, A_MAIN + B_W:]


def _inproj(x2d, norm_w, w_cat):
    n = x2d.shape[0]
    tm = min(256, n)
    row = lambda i: (i, 0)
    fixed = lambda i: (0, 0)
    return pl.pallas_call(
        _inproj_kernel,
        grid=(n // tm,),
        in_specs=[pl.BlockSpec((tm, D_MODEL), row),
                  pl.BlockSpec((1, D_MODEL), fixed),
                  pl.BlockSpec((D_MODEL, P_CAT), fixed)],
        out_specs=[pl.BlockSpec((tm, A_MAIN), row),
                   pl.BlockSpec((tm, B_W), row),
                   pl.BlockSpec((tm, BD_W), row)],
        out_shape=[jax.ShapeDtypeStruct((n, A_MAIN), F32),
                   jax.ShapeDtypeStruct((n, B_W), F32),
                   jax.ShapeDtypeStruct((n, BD_W), F32)],
        compiler_params=pltpu.CompilerParams(dimension_semantics=("arbitrary",),
                                             vmem_limit_bytes=VMEM_LIMIT),
        name="inproj",
    )(x2d, norm_w, w_cat)


def _outproj_kernel(x_ref, ma_ref, mb_ref, w_ref, fw_ref, y_ref):
    h = (x_ref[...]
         + _dot(ma_ref[...].astype(BF16), w_ref[0:MIX_A, :])
         + _dot(mb_ref[...].astype(BF16), w_ref[MIX_A:, :]))
    y_ref[...] = h * lax.rsqrt(jnp.mean(h * h, axis=-1, keepdims=True) + RMS_EPS) * fw_ref[...]


def _outproj(x2d, mix_a, mix_b, w_out, final_w):
    n = x2d.shape[0]
    tm = min(512, n)
    row = lambda i: (i, 0)
    fixed = lambda i: (0, 0)
    return pl.pallas_call(
        _outproj_kernel,
        grid=(n // tm,),
        in_specs=[pl.BlockSpec((tm, D_MODEL), row),
                  pl.BlockSpec((tm, MIX_A), row),
                  pl.BlockSpec((tm, MIX_B), row),
                  pl.BlockSpec((D_MODEL, D_MODEL), fixed),
                  pl.BlockSpec((1, D_MODEL), fixed)],
        out_specs=pl.BlockSpec((tm, D_MODEL), row),
        out_shape=jax.ShapeDtypeStruct((n, D_MODEL), F32),
        compiler_params=pltpu.CompilerParams(dimension_semantics=("arbitrary",),
                                             vmem_limit_bytes=VMEM_LIMIT),
        name="outproj",
    )(x2d, mix_a, mix_b, w_out, final_w)


def _gdn_gates(bd, hp):
    beta = _sigmoid(bd)
    g = -jnp.exp(hp[0:1, :]) * _softplus(bd + hp[1:2, :])
    return beta, g


def _gdn_qk(qkv):
    heads = lambda z: [z[:, h * DK_A:(h + 1) * DK_A] for h in range(HA)]
    qn = jnp.concatenate([_l2norm(x) * (DK_A ** -0.5) for x in heads(qkv[:, 0:MIX_A])], axis=-1)
    kn = jnp.concatenate([_l2norm(x) for x in heads(qkv[:, MIX_A:2 * MIX_A])], axis=-1)
    return qn, kn


def _gdn_out(o, gate, anw):
    on = jnp.concatenate(
        [x * lax.rsqrt(jnp.mean(x * x, axis=-1, keepdims=True) + RMS_EPS)
         for x in [o[:, h * DK_A:(h + 1) * DK_A] for h in range(HA)]], axis=-1)
    return on * anw * _silu(gate)


def _gdn_prompt_kernel(pa_ref, bd_ref, cw_ref, hp_ref, anw_ref,
                       o_ref, s_out_ref, c_out_ref, ext_ref, s_ref):
    c = pl.program_id(1)

    @pl.when(c == 0)
    def _():
        ext_ref[0:8, :] = jnp.zeros((8, A_QKV), F32)
        s_ref[...] = jnp.zeros_like(s_ref)

    ext_ref[8:8 + TC, :] = pa_ref[:, 0:A_QKV]
    cw = cw_ref[...]
    conv = (ext_ref[5:5 + TC, :] * cw[0:1, :] + ext_ref[6:6 + TC, :] * cw[1:2, :]
            + ext_ref[7:7 + TC, :] * cw[2:3, :] + ext_ref[8:8 + TC, :] * cw[3:4, :])
    qkv = _silu(conv)
    c_out_ref[...] = ext_ref[TC + 5:TC + 8, :]
    ext_ref[0:8, :] = ext_ref[TC:TC + 8, :]

    qn, kn = _gdn_qk(qkv)
    beta_all, g_all = _gdn_gates(bd_ref[...], hp_ref[...])
    gcum = _dot(_chunk_tril(TC), g_all, prec=HIGHEST)
    gcum_t = gcum.T

    r = _iota2((CHUNK, CHUNK), 0)
    cc = _iota2((CHUNK, CHUNK), 1)
    incl = cc <= r
    strict = cc < r

    nj = TC // CHUNK
    jh = [(j, h) for j in range(nj) for h in range(HA)]
    rows_of = lambda j: slice(j * CHUNK, (j + 1) * CHUNK)
    lanes_of = lambda h: slice(h * DK_A, (h + 1) * DK_A)
    q = [qn[rows_of(j), lanes_of(h)] for j, h in jh]
    k = [kn[rows_of(j), lanes_of(h)] for j, h in jh]
    v = [qkv[rows_of(j), 2 * MIX_A + h * DK_A:2 * MIX_A + (h + 1) * DK_A] for j, h in jh]
    beta = [beta_all[rows_of(j), h:h + 1] for j, h in jh]
    gc = [gcum[rows_of(j), HA + h:HA + h + 1] for j, h in jh]
    gr = [gcum_t[HA + h:HA + h + 1, rows_of(j)] for j, h in jh]
    glast = [gcum[(j + 1) * CHUNK - 1:(j + 1) * CHUNK, HA + h:HA + h + 1] for j, h in jh]
    decay = [jnp.where(incl, jnp.exp(jnp.where(incl, a - b, 0.0)), 0.0) for a, b in zip(gc, gr)]
    kk = [_dot(a, a, NT) for a in k]
    qk = [_dot(a, b, NT) for a, b in zip(q, k)]
    low = [jnp.where(strict, b * d * m, 0.0) for b, d, m in zip(beta, decay, kk)]
    tinv = _tri_inv_many(low)
    eg = [jnp.exp(a) for a in gc]
    w = [_dot(t, (b * e) * a) for t, b, e, a in zip(tinv, beta, eg, k)]
    u = [_dot(t, b * a) for t, b, a in zip(tinv, beta, v)]
    qk = [d * m for d, m in zip(decay, qk)]
    qg = [a * e for a, e in zip(q, eg)]
    kdec_t = [(a * jnp.exp(gl - g)).T for a, gl, g in zip(k, glast, gc)]
    eglast = [jnp.exp(gl) for gl in glast]

    o_rows = []
    for j in range(nj):
        idx = [j * HA + h for h in range(HA)]
        s = [s_ref[h] for h in range(HA)]
        u2 = [u[i] - _dot(w[i], s[h]) for h, i in enumerate(idx)]
        for h, i in enumerate(idx):
            s_ref[h] = eglast[i] * s[h] + _dot(kdec_t[i], u2[h])
        o_rows.append(jnp.concatenate(
            [_dot(qg[i], s[h]) + _dot(qk[i], u2[h]) for h, i in enumerate(idx)], axis=-1))

    o_ref[...] = _gdn_out(jnp.concatenate(o_rows, axis=0), pa_ref[:, A_QKV:A_MAIN], anw_ref[...])
    s_out_ref[...] = s_ref[...]


def _gdn_prompt(pa, bd, conv_w, hp, anw):
    b, t, _ = pa.shape
    blk = lambda i, j: (i, j, 0)
    fixed = lambda i, j: (0, 0)
    return pl.pallas_call(
        _gdn_prompt_kernel,
        grid=(b, t // TC),
        in_specs=[pl.BlockSpec((None, TC, A_MAIN), blk),
                  pl.BlockSpec((None, TC, BD_W), blk),
                  pl.BlockSpec((CONV_W, A_QKV), fixed),
                  pl.BlockSpec((8, BD_W), fixed),
                  pl.BlockSpec((1, MIX_A), fixed)],
        out_specs=[pl.BlockSpec((None, TC, MIX_A), blk),
                   pl.BlockSpec((None, HA, DK_A, DK_A), lambda i, j: (i, 0, 0, 0)),
                   pl.BlockSpec((None, CONV_W - 1, A_QKV), lambda i, j: (i, 0, 0))],
        out_shape=[jax.ShapeDtypeStruct((b, t, MIX_A), F32),
                   jax.ShapeDtypeStruct((b, HA, DK_A, DK_A), F32),
                   jax.ShapeDtypeStruct((b, CONV_W - 1, A_QKV), F32)],
        scratch_shapes=[pltpu.VMEM((TC + 8, A_QKV), F32),
                        pltpu.VMEM((HA, DK_A, DK_A), F32)],
        compiler_params=pltpu.CompilerParams(dimension_semantics=("arbitrary", "arbitrary"),
                                             vmem_limit_bytes=VMEM_LIMIT),
        name="gdn_prompt",
    )(pa, bd, conv_w, hp, anw)


def _rwkv_front(xb, w0, w2, a0, a2, k_k, k_a, ones_blk):
    r = xb[:, 0:MIX_B]
    kr = xb[:, MIX_B:2 * MIX_B]
    vr = xb[:, 2 * MIX_B:3 * MIX_B]
    gate = xb[:, 3 * MIX_B:4 * MIX_B]
    wd = xb[:, 4 * MIX_B:4 * MIX_B + LORA]
    ad = xb[:, 4 * MIX_B + LORA:]
    w_ll = -_softplus(-(w0 + _dot(jnp.tanh(wd), w2))) - 0.5
    logw = -jnp.exp(w_ll)
    a = _sigmoid(a0 + _dot(ad, a2))
    kraw = kr * k_k
    kk = kraw * lax.rsqrt(_group_sum(kraw * kraw, ones_blk) + 1e-12)
    kmod = kr * (1.0 + (a - 1.0) * k_a)
    return r, kmod, vr, gate, logw, a, kk


def _rwkv_back(y, r, kmod, v, gate, rk, lnw, lnb, ones_blk):
    inv = 1.0 / HS_B
    yc = y - _group_sum(y, ones_blk) * inv
    var = _group_sum(yc * yc, ones_blk) * inv
    gn = yc * lax.rsqrt(var + GN_EPS) * lnw + lnb
    bonus = _group_sum(r * kmod * rk, ones_blk) * v
    return (gn + bonus) * _silu(gate)


def _rwkv_prompt_kernel(pb_ref, mu_ref, w0_ref, w2_ref, a0_ref, a2_ref, kk_ref, ka_ref,
                        rk_ref, lnw_ref, lnb_ref, ones_ref,
                        o_ref, s_out_ref, sh_out_ref, carry_ref, s_ref):
    c = pl.program_id(1)

    @pl.when(c == 0)
    def _():
        carry_ref[...] = jnp.zeros_like(carry_ref)
        s_ref[...] = jnp.zeros_like(s_ref)

    ones_blk = ones_ref[...]
    pb = pb_ref[...]
    rowi = _iota2((TC, 1), 0)
    prev = jnp.where(rowi == 0, carry_ref[0:1, :], pltpu.roll(pb, 1, 0))
    carry_ref[0:1, :] = pb[TC - 1:TC, :]
    sh_out_ref[...] = pb[TC - 1:TC, :]
    xb = pb + (prev - pb) * mu_ref[...]
    r, kmod, v, gate, logw, a, kk = _rwkv_front(
        xb, w0_ref[...], w2_ref[...], a0_ref[...], a2_ref[...], kk_ref[...], ka_ref[...], ones_blk)

    g = _dot(_chunk_tril(TC), logw, prec=HIGHEST)
    eg = jnp.exp(g)
    egn = jnp.exp(-g)
    ag = -kk * jnp.exp(g - logw)
    kka = kk * a
    bg = kka * egn
    kg = kmod * egn
    rg = r * eg

    ri = _iota2((CHUNK, CHUNK), 0)
    ci = _iota2((CHUNK, CHUNK), 1)
    incl = ci <= ri
    strict = ci < ri

    nj = TC // CHUNK
    jh = [(j, h) for j in range(nj) for h in range(HB)]
    rows_of = lambda j: slice(j * CHUNK, (j + 1) * CHUNK)
    lanes_of = lambda h: slice(h * HS_B, (h + 1) * HS_B)
    glast = [g[(j + 1) * CHUNK - 1:(j + 1) * CHUNK, :] for j in range(nj)]
    edec = [jnp.exp(glast[j] - g[rows_of(j), :]) for j in range(nj)]
    bdec = [kka[rows_of(j), :] * edec[j] for j in range(nj)]
    kdec = [kmod[rows_of(j), :] * edec[j] for j in range(nj)]
    eglast = [jnp.exp(glast[j]) for j in range(nj)]

    cut = lambda z: [z[rows_of(j), lanes_of(h)] for j, h in jh]
    ag_h, rg_h, bg_h, kg_h, v_h = cut(ag), cut(rg), cut(bg), cut(kg), cut(v)
    a_ab = [jnp.where(strict, _dot(a, b, NT), 0.0) for a, b in zip(ag_h, bg_h)]
    a_ak = [jnp.where(strict, _dot(a, b, NT), 0.0) for a, b in zip(ag_h, kg_h)]
    a_rb = [jnp.where(incl, _dot(a, b, NT), 0.0) for a, b in zip(rg_h, bg_h)]
    a_rk = [jnp.where(incl, _dot(a, b, NT), 0.0) for a, b in zip(rg_h, kg_h)]
    tinv = _tri_inv_many([-a for a in a_ab])
    akv = [_dot(a, b) for a, b in zip(a_ak, v_h)]
    wt_t = [_dot(t, a).T for t, a in zip(tinv, ag_h)]
    ut_t = [_dot(t, a).T for t, a in zip(tinv, akv)]
    vk = [_dot(a, kdec[j][:, lanes_of(h)], TN) for a, (j, h) in zip(v_h, jh)]
    yv = [_dot(a, b) for a, b in zip(a_rk, v_h)]

    y_rows = []
    for j in range(nj):
        idx = [j * HB + h for h in range(HB)]
        s = [s_ref[h] for h in range(HB)]
        u_t = [_dot(s[h], wt_t[i]) + ut_t[i] for h, i in enumerate(idx)]
        for h, i in enumerate(idx):
            s_ref[h] = (s[h] * eglast[j][:, lanes_of(h)]
                        + _dot(u_t[h], bdec[j][:, lanes_of(h)]) + vk[i])
        y_rows.append(jnp.concatenate(
            [_dot(rg_h[i], s[h], NT) + _dot(a_rb[i], u_t[h].T) + yv[i] for h, i in enumerate(idx)],
            axis=-1))

    o_ref[...] = _rwkv_back(jnp.concatenate(y_rows, axis=0), r, kmod, v, gate,
                            rk_ref[...], lnw_ref[...], lnb_ref[...], ones_blk)
    s_out_ref[...] = s_ref[...]


def _rwkv_prompt(pb, mu, w0, w2, a0, a2, k_k, k_a, rk, lnw, lnb, ones_blk):
    b, t, _ = pb.shape
    blk = lambda i, j: (i, j, 0)
    fixed = lambda i, j: (0, 0)
    vec = pl.BlockSpec((1, MIX_B), fixed)
    return pl.pallas_call(
        _rwkv_prompt_kernel,
        grid=(b, t // TC),
        in_specs=[pl.BlockSpec((None, TC, B_W), blk),
                  pl.BlockSpec((1, B_W), fixed),
                  vec, pl.BlockSpec((LORA, MIX_B), fixed),
                  vec, pl.BlockSpec((LORA, MIX_B), fixed),
                  vec, vec, vec, vec, vec,
                  pl.BlockSpec((MIX_B, MIX_B), fixed)],
        out_specs=[pl.BlockSpec((None, TC, MIX_B), blk),
                   pl.BlockSpec((None, HB, HS_B, HS_B), lambda i, j: (i, 0, 0, 0)),
                   pl.BlockSpec((None, 1, B_W), lambda i, j: (i, 0, 0))],
        out_shape=[jax.ShapeDtypeStruct((b, t, MIX_B), F32),
                   jax.ShapeDtypeStruct((b, HB, HS_B, HS_B), F32),
                   jax.ShapeDtypeStruct((b, 1, B_W), F32)],
        scratch_shapes=[pltpu.VMEM((8, B_W), F32),
                        pltpu.VMEM((HB, HS_B, HS_B), F32)],
        compiler_params=pltpu.CompilerParams(dimension_semantics=("arbitrary", "arbitrary"),
                                             vmem_limit_bytes=VMEM_LIMIT),
        name="rwkv_prompt",
    )(pb, mu, w0, w2, a0, a2, k_k, k_a, rk, lnw, lnb, ones_blk)


def _pad_rows(rows, width):
    return jnp.concatenate(rows + [jnp.zeros((8 - len(rows), width), F32)], axis=0)


def _gdn_step_kernel(pa_ref, bd_ref, sc_ref, s_ref, cw_ref, hp_ref, anw_ref,
                     o_ref, s_out_ref, c_out_ref):
    u = pa_ref[:, 0:A_QKV]
    cw = cw_ref[...]
    conv = (sc_ref[0] * cw[0:1, :] + sc_ref[1] * cw[1:2, :] + sc_ref[2] * cw[2:3, :] + u * cw[3:4, :])
    qkv = _silu(conv)
    c_out_ref[0] = sc_ref[1]
    c_out_ref[1] = sc_ref[2]
    c_out_ref[2] = u

    qn, kn = _gdn_qk(qkv)
    beta_all, g_all = _gdn_gates(bd_ref[...], hp_ref[...])
    eg_all = jnp.exp(g_all)
    lanes_of = lambda h: slice(h * DK_A, (h + 1) * DK_A)
    q = [qn[:, lanes_of(h)] for h in range(HA)]
    k = [kn[:, lanes_of(h)] for h in range(HA)]
    v = [qkv[:, 2 * MIX_A + h * DK_A:2 * MIX_A + (h + 1) * DK_A] for h in range(HA)]
    beta = [beta_all[:, h:h + 1] for h in range(HA)]
    eg = [eg_all[:, HA + h:HA + h + 1] for h in range(HA)]

    bh = [(b, h) for b in range(SB) for h in range(HA)]
    s0 = [s_ref[b, h] for b, h in bh]
    ks_qs = [_dot(_pad_rows([k[h][b:b + 1, :], q[h][b:b + 1, :]], DK_A), s) for (b, h), s in zip(bh, s0)]
    o_heads = []
    for h in range(HA):
        ks = jnp.concatenate([ks_qs[b * HA + h][0:1, :] for b in range(SB)], axis=0)
        qs = jnp.concatenate([ks_qs[b * HA + h][1:2, :] for b in range(SB)], axis=0)
        u2 = beta[h] * (v[h] - eg[h] * ks)
        for b in range(SB):
            s_out_ref[b, h] = (eg[h][b:b + 1, :] * s0[b * HA + h]
                               + _dot(_pad_rows([k[h][b:b + 1, :]], DK_A),
                                      _pad_rows([u2[b:b + 1, :]], DK_A), TN))
        o_heads.append(eg[h] * qs + jnp.sum(q[h] * k[h], axis=-1, keepdims=True) * u2)

    o_ref[...] = _gdn_out(jnp.concatenate(o_heads, axis=-1), pa_ref[:, A_QKV:A_MAIN], anw_ref[...])


def _gdn_step(pa, bd, sconv_t, s_a, conv_w, hp, anw):
    n = pa.shape[0]
    row = lambda i: (i, 0)
    fixed = lambda i: (0, 0)
    return pl.pallas_call(
        _gdn_step_kernel,
        grid=(n // SB,),
        in_specs=[pl.BlockSpec((SB, A_MAIN), row),
                  pl.BlockSpec((SB, BD_W), row),
                  pl.BlockSpec((CONV_W - 1, SB, A_QKV), lambda i: (0, i, 0)),
                  pl.BlockSpec((SB, HA, DK_A, DK_A), lambda i: (i, 0, 0, 0)),
                  pl.BlockSpec((CONV_W, A_QKV), fixed),
                  pl.BlockSpec((8, BD_W), fixed),
                  pl.BlockSpec((1, MIX_A), fixed)],
        out_specs=[pl.BlockSpec((SB, MIX_A), row),
                   pl.BlockSpec((SB, HA, DK_A, DK_A), lambda i: (i, 0, 0, 0)),
                   pl.BlockSpec((CONV_W - 1, SB, A_QKV), lambda i: (0, i, 0))],
        out_shape=[jax.ShapeDtypeStruct((n, MIX_A), F32),
                   jax.ShapeDtypeStruct((n, HA, DK_A, DK_A), F32),
                   jax.ShapeDtypeStruct((CONV_W - 1, n, A_QKV), F32)],
        compiler_params=pltpu.CompilerParams(dimension_semantics=("arbitrary",),
                                             vmem_limit_bytes=VMEM_LIMIT),
        name="gdn_step",
    )(pa, bd, sconv_t, s_a, conv_w, hp, anw)


def _rwkv_step_kernel(pb_ref, sh_ref, s_ref, mu_ref, w0_ref, w2_ref, a0_ref, a2_ref, kk_ref, ka_ref,
                      rk_ref, lnw_ref, lnb_ref, ones_ref,
                      o_ref, s_out_ref):
    ones_blk = ones_ref[...]
    pb = pb_ref[...]
    xb = pb + (sh_ref[...] - pb) * mu_ref[...]
    r, kmod, v, gate, logw, a, kk = _rwkv_front(
        xb, w0_ref[...], w2_ref[...], a0_ref[...], a2_ref[...], kk_ref[...], ka_ref[...], ones_blk)
    wdec = jnp.exp(logw)
    nkk = -kk
    kka = kk * a

    bh = [(b, h) for b in range(SB) for h in range(HB)]
    cut = lambda z: [z[b:b + 1, h * HS_B:(h + 1) * HS_B] for b, h in bh]
    nkk_r, kka_r, k_r, v_r, r_r, w_r = cut(nkk), cut(kka), cut(kmod), cut(v), cut(r), cut(wdec)
    s0 = [s_ref[b, h] for b, h in bh]
    sa = [_dot(_pad_rows([x], HS_B), s, NT)[0:1, :] for x, s in zip(nkk_r, s0)]
    upd = [_dot(_pad_rows([x, y], HS_B), _pad_rows([z, t], HS_B), TN)
           for x, y, z, t in zip(sa, v_r, kka_r, k_r)]
    s1 = [s * w + d for s, w, d in zip(s0, w_r, upd)]
    for (b, h), s in zip(bh, s1):
        s_out_ref[b, h] = s
    y_r = [_dot(_pad_rows([x], HS_B), s, NT)[0:1, :] for x, s in zip(r_r, s1)]
    y = jnp.concatenate(
        [jnp.concatenate([y_r[b * HB + h] for b in range(SB)], axis=0) for h in range(HB)], axis=-1)

    o_ref[...] = _rwkv_back(y, r, kmod, v, gate, rk_ref[...], lnw_ref[...], lnb_ref[...], ones_blk)


def _rwkv_step(pb, shift, s_b, mu, w0, w2, a0, a2, k_k, k_a, rk, lnw, lnb, ones_blk):
    n = pb.shape[0]
    row = lambda i: (i, 0)
    fixed = lambda i: (0, 0)
    vec = pl.BlockSpec((1, MIX_B), fixed)
    return pl.pallas_call(
        _rwkv_step_kernel,
        grid=(n // SB,),
        in_specs=[pl.BlockSpec((SB, B_W), row),
                  pl.BlockSpec((SB, B_W), row),
                  pl.BlockSpec((SB, HB, HS_B, HS_B), lambda i: (i, 0, 0, 0)),
                  pl.BlockSpec((1, B_W), fixed),
                  vec, pl.BlockSpec((LORA, MIX_B), fixed),
                  vec, pl.BlockSpec((LORA, MIX_B), fixed),
                  vec, vec, vec, vec, vec,
                  pl.BlockSpec((MIX_B, MIX_B), fixed)],
        out_specs=[pl.BlockSpec((SB, MIX_B), row),
                   pl.BlockSpec((SB, HB, HS_B, HS_B), lambda i: (i, 0, 0, 0))],
        out_shape=[jax.ShapeDtypeStruct((n, MIX_B), F32),
                   jax.ShapeDtypeStruct((n, HB, HS_B, HS_B), F32)],
        compiler_params=pltpu.CompilerParams(dimension_semantics=("arbitrary",),
                                             vmem_limit_bytes=VMEM_LIMIT),
        name="rwkv_step",
    )(pb, shift, s_b, mu, w0, w2, a0, a2, k_k, k_a, rk, lnw, lnb, ones_blk)


def _block_ones(width, group):
    idx = jnp.arange(width) // group
    return (idx[:, None] == idx[None, :]).astype(BF16)


def kernel(x_prompt, x_sample, state_a_mat, state_a_conv, state_b_mat, state_b_shift, norm_w, w_in,
           conv_w, a_log, dt_bias, a_norm_w, mu, w0, w2, a0, a2, k_k, k_a, r_k, ln_w, ln_b, w_out,
           final_norm_w):
    bsz, seq, _ = x_prompt.shape
    nsmp = x_sample.shape[0]
    a_w = A_MAIN + 2 * HA

    w = w_in[0]
    w_cat = jnp.concatenate(
        [w[:, :A_MAIN], w[:, a_w:], w[:, A_MAIN:a_w], jnp.zeros((D_MODEL, BD_W - 2 * HA), F32)],
        axis=1).astype(BF16)
    w_o = w_out[0].astype(BF16)
    hp = jnp.zeros((8, BD_W), F32)
    hp = hp.at[0, HA:2 * HA].set(a_log[0]).at[1, HA:2 * HA].set(dt_bias[0])
    nw = norm_w[0].reshape(1, D_MODEL)
    fw = final_norm_w.reshape(1, D_MODEL)
    anw = jnp.tile(a_norm_w[0].reshape(1, DK_A), (1, HA))
    ones_b = _block_ones(MIX_B, HS_B)
    row = lambda z: z[0].reshape(1, -1)
    b_params = (row(mu), row(w0), w2[0], row(a0), a2[0], row(k_k), row(k_a), row(r_k), row(ln_w), row(ln_b),
                ones_b)

    xp = x_prompt.reshape(bsz * seq, D_MODEL)
    pa, pb, bd = _inproj(xp, nw, w_cat)
    oa, p_amat, p_aconv = _gdn_prompt(pa.reshape(bsz, seq, A_MAIN), bd.reshape(bsz, seq, BD_W),
                                      conv_w[0], hp, anw)
    ob, p_bmat, p_bshift = _rwkv_prompt(pb.reshape(bsz, seq, B_W), *b_params)
    y_prompt = _outproj(xp, oa.reshape(bsz * seq, MIX_A), ob.reshape(bsz * seq, MIX_B), w_o, fw)

    xs = x_sample.reshape(nsmp, D_MODEL)
    sa_, sb_, sbd = _inproj(xs, nw, w_cat)
    sconv_t = jnp.swapaxes(state_a_conv[0], 0, 1)
    soa, s_amat, s_aconv_t = _gdn_step(sa_, sbd, sconv_t, state_a_mat[0], conv_w[0], hp, anw)
    sob, s_bmat = _rwkv_step(sb_, state_b_shift[0], state_b_mat[0], *b_params)
    y_sample = _outproj(xs, soa, sob, w_o, fw)

    return (y_prompt.reshape(bsz, seq, D_MODEL), y_sample.reshape(nsmp, 1, D_MODEL),
            p_amat[None], p_aconv[None], p_bmat[None], p_bshift.reshape(1, bsz, B_W),
            s_amat[None], jnp.swapaxes(s_aconv_t, 0, 1)[None], s_bmat[None], sb_[None])
```

```python
import jax
import jax.numpy as jnp
from jax import lax
from jax.experimental import pallas as pl
from jax.experimental.pallas import tpu as pltpu

F32 = jnp.float32
BF16 = jnp.bfloat16

D_MODEL = 1024
MIX_A = 512
MIX_B = 512
HA = 4
DK_A = 128
HB = 8
HS_B = 64
CONV_W = 4
LORA = 64
A_QKV = 3 * MIX_A
A_MAIN = A_QKV + MIX_A
B_W = 4 * MIX_B + 2 * LORA
BD_W = 128
P_CAT = A_MAIN + B_W + BD_W
RMS_EPS = 1e-6
GN_EPS = 64e-5
CHUNK = 64
TC = 256
SB = 8
VMEM_LIMIT = 56 * 1024 * 1024

HIGHEST = lax.Precision.HIGHEST
NN = (((1,), (0,)), ((), ()))
NT = (((1,), (1,)), ((), ()))
TN = (((0,), (0,)), ((), ()))


def _dot(a, b, dn=NN, prec=None):
    return lax.dot_general(a, b, dn, precision=prec, preferred_element_type=F32)


def _sigmoid(x):
    return 1.0 / (1.0 + jnp.exp(-x))


def _silu(x):
    return x * _sigmoid(x)


def _softplus(x):
    return jnp.maximum(x, 0.0) + jnp.log(1.0 + jnp.exp(-jnp.abs(x)))


def _l2norm(x):
    return x * lax.rsqrt(jnp.sum(x * x, axis=-1, keepdims=True) + 1e-12)


def _group_sum(x, ones_blk):
    return _dot(x.astype(BF16), ones_blk)


def _iota2(shape, dim):
    return lax.broadcasted_iota(jnp.int32, shape, dim)


def _chunk_tril(n):
    r = _iota2((n, n), 0)
    c = _iota2((n, n), 1)
    same = (r // CHUNK) == (c // CHUNK)
    return jnp.where(same & (c <= r), 1.0, 0.0).astype(F32)


def _tri_inv_many(lows):
    n = lows[0].shape[0]
    r = _iota2((n, n), 0)
    c = _iota2((n, n), 1)
    eye = jnp.where(r == c, 1.0, 0.0).astype(F32)
    base = (r // 8) == (c // 8)
    l0 = [jnp.where(base, low, 0.0) for low in lows]
    l2 = [_dot(a, a) for a in l0]
    l4 = [_dot(a, a) for a in l2]
    d = [_dot(eye - a, eye + b) for a, b in zip(l0, l2)]
    d = [_dot(a, eye + b) for a, b in zip(d, l4)]
    s = 8
    while s < n:
        sel = ((r // (2 * s)) == (c // (2 * s))) & ((r // s) != (c // s))
        t = [_dot(jnp.where(sel, low, 0.0), a) for low, a in zip(lows, d)]
        d = [a - _dot(a, b) for a, b in zip(d, t)]
        s *= 2
    return d


def _inproj_kernel(x_ref, nw_ref, w_ref, oa_ref, ob_ref, obd_ref):
    x = x_ref[...]
    xn = x * lax.rsqrt(jnp.mean(x * x, axis=-1, keepdims=True) + RMS_EPS) * nw_ref[...]
    p = _dot(xn.astype(BF16), w_ref[...])
    oa_ref[...] = p[:, :A_MAIN]
    ob_ref[...] = p[:, A_MAIN:A_MAIN + B_W]
    obd_ref[...] = p[:, A_MAIN + B_W:]


def _inproj(x2d, norm_w, w_cat):
    n = x2d.shape[0]
    tm = min(256, n)
    row = lambda i: (i, 0)
    fixed = lambda i: (0, 0)
    return pl.pallas_call(
        _inproj_kernel,
        grid=(n // tm,),
        in_specs=[pl.BlockSpec((tm, D_MODEL), row),
                  pl.BlockSpec((1, D_MODEL), fixed),
                  pl.BlockSpec((D_MODEL, P_CAT), fixed)],
        out_specs=[pl.BlockSpec((tm, A_MAIN), row),
                   pl.BlockSpec((tm, B_W), row),
                   pl.BlockSpec((tm, BD_W), row)],
        out_shape=[jax.ShapeDtypeStruct((n, A_MAIN), F32),
                   jax.ShapeDtypeStruct((n, B_W), F32),
                   jax.ShapeDtypeStruct((n, BD_W), F32)],
        compiler_params=pltpu.CompilerParams(dimension_semantics=("arbitrary",),
                                             vmem_limit_bytes=VMEM_LIMIT),
        name="inproj",
    )(x2d, norm_w, w_cat)


def _outproj_kernel(x_ref, ma_ref, mb_ref, w_ref, fw_ref, y_ref):
    h = (x_ref[...]
         + _dot(ma_ref[...].astype(BF16), w_ref[0:MIX_A, :])
         + _dot(mb_ref[...].astype(BF16), w_ref[MIX_A:, :]))
    y_ref[...] = h * lax.rsqrt(jnp.mean(h * h, axis=-1, keepdims=True) + RMS_EPS) * fw_ref[...]


def _outproj(x2d, mix_a, mix_b, w_out, final_w):
    n = x2d.shape[0]
    tm = min(512, n)
    row = lambda i: (i, 0)
    fixed = lambda i: (0, 0)
    return pl.pallas_call(
        _outproj_kernel,
        grid=(n // tm,),
        in_specs=[pl.BlockSpec((tm, D_MODEL), row),
                  pl.BlockSpec((tm, MIX_A), row),
                  pl.BlockSpec((tm, MIX_B), row),
                  pl.BlockSpec((D_MODEL, D_MODEL), fixed),
                  pl.BlockSpec((1, D_MODEL), fixed)],
        out_specs=pl.BlockSpec((tm, D_MODEL), row),
        out_shape=jax.ShapeDtypeStruct((n, D_MODEL), F32),
        compiler_params=pltpu.CompilerParams(dimension_semantics=("arbitrary",),
                                             vmem_limit_bytes=VMEM_LIMIT),
        name="outproj",
    )(x2d, mix_a, mix_b, w_out, final_w)


def _gdn_gates(bd, hp):
    beta = _sigmoid(bd)
    g = -jnp.exp(hp[0:1, :]) * _softplus(bd + hp[1:2, :])
    return beta, g


def _gdn_qk(qkv):
    heads = lambda z: [z[:, h * DK_A:(h + 1) * DK_A] for h in range(HA)]
    qn = jnp.concatenate([_l2norm(x) * (DK_A ** -0.5) for x in heads(qkv[:, 0:MIX_A])], axis=-1)
    kn = jnp.concatenate([_l2norm(x) for x in heads(qkv[:, MIX_A:2 * MIX_A])], axis=-1)
    return qn, kn


def _gdn_out(o, gate, anw):
    on = jnp.concatenate(
        [x * lax.rsqrt(jnp.mean(x * x, axis=-1, keepdims=True) + RMS_EPS)
         for x in [o[:, h * DK_A:(h + 1) * DK_A] for h in range(HA)]], axis=-1)
    return on * anw * _silu(gate)


def _gdn_prompt_kernel(pa_ref, bd_ref, cw_ref, hp_ref, anw_ref,
                       o_ref, s_out_ref, c_out_ref, ext_ref, s_ref):
    c = pl.program_id(1)

    @pl.when(c == 0)
    def _():
        ext_ref[0:8, :] = jnp.zeros((8, A_QKV), F32)
        s_ref[...] = jnp.zeros_like(s_ref)

    ext_ref[8:8 + TC, :] = pa_ref[:, 0:A_QKV]
    cw = cw_ref[...]
    conv = (ext_ref[5:5 + TC, :] * cw[0:1, :] + ext_ref[6:6 + TC, :] * cw[1:2, :]
            + ext_ref[7:7 + TC, :] * cw[2:3, :] + ext_ref[8:8 + TC, :] * cw[3:4, :])
    qkv = _silu(conv)
    c_out_ref[...] = ext_ref[TC + 5:TC + 8, :]
    ext_ref[0:8, :] = ext_ref[TC:TC + 8, :]

    qn, kn = _gdn_qk(qkv)
    beta_all, g_all = _gdn_gates(bd_ref[...], hp_ref[...])
    gcum = _dot(_chunk_tril(TC), g_all, prec=HIGHEST)
    gcum_t = gcum.T

    r = _iota2((CHUNK, CHUNK), 0)
    cc = _iota2((CHUNK, CHUNK), 1)
    incl = cc <= r
    strict = cc < r

    nj = TC // CHUNK
    jh = [(j, h) for j in range(nj) for h in range(HA)]
    rows_of = lambda j: slice(j * CHUNK, (j + 1) * CHUNK)
    lanes_of = lambda h: slice(h * DK_A, (h + 1) * DK_A)
    q = [qn[rows_of(j), lanes_of(h)] for j, h in jh]
    k = [kn[rows_of(j), lanes_of(h)] for j, h in jh]
    v = [qkv[rows_of(j), 2 * MIX_A + h * DK_A:2 * MIX_A + (h + 1) * DK_A] for j, h in jh]
    beta = [beta_all[rows_of(j), h:h + 1] for j, h in jh]
    gc = [gcum[rows_of(j), HA + h:HA + h + 1] for j, h in jh]
    gr = [gcum_t[HA + h:HA + h + 1, rows_of(j)] for j, h in jh]
    glast = [gcum[(j + 1) * CHUNK - 1:(j + 1) * CHUNK, HA + h:HA + h + 1] for j, h in jh]
    decay = [jnp.where(incl, jnp.exp(jnp.where(incl, a - b, 0.0)), 0.0) for a, b in zip(gc, gr)]
    kq = [_dot(jnp.concatenate([a, b], axis=0), a, NT) for a, b in zip(k, q)]
    low = [jnp.where(strict, b * d * m[0:CHUNK, :], 0.0) for b, d, m in zip(beta, decay, kq)]
    tinv = _tri_inv_many(low)
    eg = [jnp.exp(a) for a in gc]
    wu = [_dot(t, jnp.concatenate([(b * e) * a, b * c_], axis=1))
          for t, b, e, a, c_ in zip(tinv, beta, eg, k, v)]
    kwu = [_dot(a * jnp.exp(gl - g), x, TN) for a, gl, g, x in zip(k, glast, gc, wu)]
    qwu = [_dot(d * m[CHUNK:, :], x) for d, m, x in zip(decay, kq, wu)]
    rd = _iota2((DK_A, DK_A), 0)
    cd = _iota2((DK_A, DK_A), 1)
    lhs = [jnp.concatenate([jnp.where(rd == cd, jnp.exp(gl), 0.0) - kx[:, 0:DK_A],
                            a * e - qx[:, 0:DK_A]], axis=0)
           for gl, kx, a, e, qx in zip(glast, kwu, q, eg, qwu)]

    o_rows = []
    for j in range(nj):
        idx = [j * HA + h for h in range(HA)]
        res = [_dot(lhs[i], s_ref[h]) for h, i in enumerate(idx)]
        for h, i in enumerate(idx):
            s_ref[h] = res[h][0:DK_A, :] + kwu[i][:, DK_A:]
        o_rows.append(jnp.concatenate(
            [res[h][DK_A:, :] + qwu[i][:, DK_A:] for h, i in enumerate(idx)], axis=-1))

    o_ref[...] = _gdn_out(jnp.concatenate(o_rows, axis=0), pa_ref[:, A_QKV:A_MAIN], anw_ref[...])
    s_out_ref[...] = s_ref[...]


def _gdn_prompt(pa, bd, conv_w, hp, anw):
    b, t, _ = pa.shape
    blk = lambda i, j: (i, j, 0)
    fixed = lambda i, j: (0, 0)
    return pl.pallas_call(
        _gdn_prompt_kernel,
        grid=(b, t // TC),
        in_specs=[pl.BlockSpec((None, TC, A_MAIN), blk),
                  pl.BlockSpec((None, TC, BD_W), blk),
                  pl.BlockSpec((CONV_W, A_QKV), fixed),
                  pl.BlockSpec((8, BD_W), fixed),
                  pl.BlockSpec((1, MIX_A), fixed)],
        out_specs=[pl.BlockSpec((None, TC, MIX_A), blk),
                   pl.BlockSpec((None, HA, DK_A, DK_A), lambda i, j: (i, 0, 0, 0)),
                   pl.BlockSpec((None, CONV_W - 1, A_QKV), lambda i, j: (i, 0, 0))],
        out_shape=[jax.ShapeDtypeStruct((b, t, MIX_A), F32),
                   jax.ShapeDtypeStruct((b, HA, DK_A, DK_A), F32),
                   jax.ShapeDtypeStruct((b, CONV_W - 1, A_QKV), F32)],
        scratch_shapes=[pltpu.VMEM((TC + 8, A_QKV), F32),
                        pltpu.VMEM((HA, DK_A, DK_A), F32)],
        compiler_params=pltpu.CompilerParams(dimension_semantics=("arbitrary", "arbitrary"),
                                             vmem_limit_bytes=VMEM_LIMIT),
        name="gdn_prompt",
    )(pa, bd, conv_w, hp, anw)


def _rwkv_front(xb, w0, w2, a0, a2, k_k, k_a, ones_blk):
    r = xb[:, 0:MIX_B]
    kr = xb[:, MIX_B:2 * MIX_B]
    vr = xb[:, 2 * MIX_B:3 * MIX_B]
    gate = xb[:, 3 * MIX_B:4 * MIX_B]
    wd = xb[:, 4 * MIX_B:4 * MIX_B + LORA]
    ad = xb[:, 4 * MIX_B + LORA:]
    w_ll = -_softplus(-(w0 + _dot(jnp.tanh(wd), w2))) - 0.5
    logw = -jnp.exp(w_ll)
    a = _sigmoid(a0 + _dot(ad, a2))
    kraw = kr * k_k
    kk = kraw * lax.rsqrt(_group_sum(kraw * kraw, ones_blk) + 1e-12)
    kmod = kr * (1.0 + (a - 1.0) * k_a)
    return r, kmod, vr, gate, logw, a, kk


def _rwkv_back(y, r, kmod, v, gate, rk, lnw, lnb, ones_blk):
    inv = 1.0 / HS_B
    yc = y - _group_sum(y, ones_blk) * inv
    var = _group_sum(yc * yc, ones_blk) * inv
    gn = yc * lax.rsqrt(var + GN_EPS) * lnw + lnb
    bonus = _group_sum(r * kmod * rk, ones_blk) * v
    return (gn + bonus) * _silu(gate)


def _rwkv_prompt_kernel(pb_ref, mu_ref, w0_ref, w2_ref, a0_ref, a2_ref, kk_ref, ka_ref,
                        rk_ref, lnw_ref, lnb_ref, ones_ref,
                        o_ref, s_out_ref, sh_out_ref, carry_ref, s_ref):
    c = pl.program_id(1)

    @pl.when(c == 0)
    def _():
        carry_ref[...] = jnp.zeros_like(carry_ref)
        s_ref[...] = jnp.zeros_like(s_ref)

    ones_blk = ones_ref[...]
    pb = pb_ref[...]
    rowi = _iota2((TC, 1), 0)
    prev = jnp.where(rowi == 0, carry_ref[0:1, :], pltpu.roll(pb, 1, 0))
    carry_ref[0:1, :] = pb[TC - 1:TC, :]
    sh_out_ref[...] = pb[TC - 1:TC, :]
    xb = pb + (prev - pb) * mu_ref[...]
    r, kmod, v, gate, logw, a, kk = _rwkv_front(
        xb, w0_ref[...], w2_ref[...], a0_ref[...], a2_ref[...], kk_ref[...], ka_ref[...], ones_blk)

    g = _dot(_chunk_tril(TC), logw, prec=HIGHEST)
    eg = jnp.exp(g)
    egn = jnp.exp(-g)
    ag = -kk * jnp.exp(g - logw)
    kka = kk * a
    bg = kka * egn
    kg = kmod * egn
    rg = r * eg

    ri = _iota2((CHUNK, CHUNK), 0)
    ci = _iota2((CHUNK, CHUNK), 1)
    incl = ci <= ri
    strict = ci < ri

    nj = TC // CHUNK
    jh = [(j, h) for j in range(nj) for h in range(HB)]
    rows_of = lambda j: slice(j * CHUNK, (j + 1) * CHUNK)
    lanes_of = lambda h: slice(h * HS_B, (h + 1) * HS_B)
    glast = [g[(j + 1) * CHUNK - 1:(j + 1) * CHUNK, :] for j in range(nj)]
    edec = [jnp.exp(glast[j] - g[rows_of(j), :]) for j in range(nj)]
    bdec = [kka[rows_of(j), :] * edec[j] for j in range(nj)]
    kdec = [kmod[rows_of(j), :] * edec[j] for j in range(nj)]
    eglast = [jnp.exp(glast[j]) for j in range(nj)]

    cut = lambda z: [z[rows_of(j), lanes_of(h)] for j, h in jh]
    ag_h, rg_h, bg_h, kg_h, v_h = cut(ag), cut(rg), cut(bg), cut(kg), cut(v)
    bdec_h = [bdec[j][:, lanes_of(h)] for j, h in jh]
    kdec_h = [kdec[j][:, lanes_of(h)] for j, h in jh]
    eglast_h = [eglast[j][:, lanes_of(h)] for j, h in jh]
    amat = [_dot(jnp.concatenate([a, b], axis=0), jnp.concatenate([c_, d], axis=0), NT)
            for a, b, c_, d in zip(ag_h, rg_h, bg_h, kg_h)]
    a_ab = [jnp.where(strict, m[0:CHUNK, 0:CHUNK], 0.0) for m in amat]
    a_ak = [jnp.where(strict, m[0:CHUNK, CHUNK:], 0.0) for m in amat]
    a_rb = [jnp.where(incl, m[CHUNK:, 0:CHUNK], 0.0) for m in amat]
    a_rk = [jnp.where(incl, m[CHUNK:, CHUNK:], 0.0) for m in amat]
    tinv = _tri_inv_many([-a for a in a_ab])
    kv = [_dot(jnp.concatenate([a, b], axis=0), c_) for a, b, c_ in zip(a_ak, a_rk, v_h)]
    wu = [_dot(t, jnp.concatenate([a, x[0:CHUNK, :]], axis=1)) for t, a, x in zip(tinv, ag_h, kv)]
    rwu = [_dot(a, x) for a, x in zip(a_rb, wu)]
    zero = jnp.zeros((CHUNK, HS_B), F32)
    mn = [_dot(jnp.concatenate([a, b], axis=0),
               jnp.concatenate([x, jnp.concatenate([zero, c_], axis=1)], axis=0), TN)
          for a, b, x, c_ in zip(bdec_h, kdec_h, wu, v_h)]
    lhs = [jnp.concatenate([jnp.where(ri == ci, e, 0.0) + m[:, 0:HS_B], a + x[:, 0:HS_B]], axis=0)
           for e, m, a, x in zip(eglast_h, mn, rg_h, rwu)]

    y_rows = []
    for j in range(nj):
        idx = [j * HB + h for h in range(HB)]
        res = [_dot(lhs[i], s_ref[h]) for h, i in enumerate(idx)]
        for h, i in enumerate(idx):
            s_ref[h] = res[h][0:HS_B, :] + mn[i][:, HS_B:]
        y_rows.append(jnp.concatenate(
            [res[h][HS_B:, :] + rwu[i][:, HS_B:] + kv[i][CHUNK:, :] for h, i in enumerate(idx)],
            axis=-1))

    o_ref[...] = _rwkv_back(jnp.concatenate(y_rows, axis=0), r, kmod, v, gate,
                            rk_ref[...], lnw_ref[...], lnb_ref[...], ones_blk)
    for h in range(HB):
        s_out_ref[h] = s_ref[h].T


def _rwkv_prompt(pb, mu, w0, w2, a0, a2, k_k, k_a, rk, lnw, lnb, ones_blk):
    b, t, _ = pb.shape
    blk = lambda i, j: (i, j, 0)
    fixed = lambda i, j: (0, 0)
    vec = pl.BlockSpec((1, MIX_B), fixed)
    return pl.pallas_call(
        _rwkv_prompt_kernel,
        grid=(b, t // TC),
        in_specs=[pl.BlockSpec((None, TC, B_W), blk),
                  pl.BlockSpec((1, B_W), fixed),
                  vec, pl.BlockSpec((LORA, MIX_B), fixed),
                  vec, pl.BlockSpec((LORA, MIX_B), fixed),
                  vec, vec, vec, vec, vec,
                  pl.BlockSpec((MIX_B, MIX_B), fixed)],
        out_specs=[pl.BlockSpec((None, TC, MIX_B), blk),
                   pl.BlockSpec((None, HB, HS_B, HS_B), lambda i, j: (i, 0, 0, 0)),
                   pl.BlockSpec((None, 1, B_W), lambda i, j: (i, 0, 0))],
        out_shape=[jax.ShapeDtypeStruct((b, t, MIX_B), F32),
                   jax.ShapeDtypeStruct((b, HB, HS_B, HS_B), F32),
                   jax.ShapeDtypeStruct((b, 1, B_W), F32)],
        scratch_shapes=[pltpu.VMEM((8, B_W), F32),
                        pltpu.VMEM((HB, HS_B, HS_B), F32)],
        compiler_params=pltpu.CompilerParams(dimension_semantics=("arbitrary", "arbitrary"),
                                             vmem_limit_bytes=VMEM_LIMIT),
        name="rwkv_prompt",
    )(pb, mu, w0, w2, a0, a2, k_k, k_a, rk, lnw, lnb, ones_blk)


def _pad_rows(rows, width):
    return jnp.concatenate(rows + [jnp.zeros((8 - len(rows), width), F32)], axis=0)


def _gdn_step_kernel(pa_ref, bd_ref, sc_ref, s_ref, cw_ref, hp_ref, anw_ref,
                     o_ref, s_out_ref, c_out_ref):
    u = pa_ref[:, 0:A_QKV]
    cw = cw_ref[...]
    conv = (sc_ref[0] * cw[0:1, :] + sc_ref[1] * cw[1:2, :] + sc_ref[2] * cw[2:3, :] + u * cw[3:4, :])
    qkv = _silu(conv)
    c_out_ref[0] = sc_ref[1]
    c_out_ref[1] = sc_ref[2]
    c_out_ref[2] = u

    qn, kn = _gdn_qk(qkv)
    beta_all, g_all = _gdn_gates(bd_ref[...], hp_ref[...])
    eg_all = jnp.exp(g_all)
    lanes_of = lambda h: slice(h * DK_A, (h + 1) * DK_A)
    q = [qn[:, lanes_of(h)] for h in range(HA)]
    k = [kn[:, lanes_of(h)] for h in range(HA)]
    v = [qkv[:, 2 * MIX_A + h * DK_A:2 * MIX_A + (h + 1) * DK_A] for h in range(HA)]
    beta = [beta_all[:, h:h + 1] for h in range(HA)]
    eg = [eg_all[:, HA + h:HA + h + 1] for h in range(HA)]

    bh = [(b, h) for b in range(SB) for h in range(HA)]
    s0 = [s_ref[b, h] for b, h in bh]
    ks_qs = [_dot(_pad_rows([k[h][b:b + 1, :], q[h][b:b + 1, :]], DK_A), s) for (b, h), s in zip(bh, s0)]
    o_heads = []
    for h in range(HA):
        ks = jnp.concatenate([ks_qs[b * HA + h][0:1, :] for b in range(SB)], axis=0)
        qs = jnp.concatenate([ks_qs[b * HA + h][1:2, :] for b in range(SB)], axis=0)
        u2 = beta[h] * (v[h] - eg[h] * ks)
        for b in range(SB):
            s_out_ref[b, h] = (eg[h][b:b + 1, :] * s0[b * HA + h]
                               + _dot(_pad_rows([k[h][b:b + 1, :]], DK_A),
                                      _pad_rows([u2[b:b + 1, :]], DK_A), TN))
        o_heads.append(eg[h] * qs + jnp.sum(q[h] * k[h], axis=-1, keepdims=True) * u2)

    o_ref[...] = _gdn_out(jnp.concatenate(o_heads, axis=-1), pa_ref[:, A_QKV:A_MAIN], anw_ref[...])


def _gdn_step(pa, bd, sconv_t, s_a, conv_w, hp, anw):
    n = pa.shape[0]
    row = lambda i: (i, 0)
    fixed = lambda i: (0, 0)
    return pl.pallas_call(
        _gdn_step_kernel,
        grid=(n // SB,),
        in_specs=[pl.BlockSpec((SB, A_MAIN), row),
                  pl.BlockSpec((SB, BD_W), row),
                  pl.BlockSpec((CONV_W - 1, SB, A_QKV), lambda i: (0, i, 0)),
                  pl.BlockSpec((SB, HA, DK_A, DK_A), lambda i: (i, 0, 0, 0)),
                  pl.BlockSpec((CONV_W, A_QKV), fixed),
                  pl.BlockSpec((8, BD_W), fixed),
                  pl.BlockSpec((1, MIX_A), fixed)],
        out_specs=[pl.BlockSpec((SB, MIX_A), row),
                   pl.BlockSpec((SB, HA, DK_A, DK_A), lambda i: (i, 0, 0, 0)),
                   pl.BlockSpec((CONV_W - 1, SB, A_QKV), lambda i: (0, i, 0))],
        out_shape=[jax.ShapeDtypeStruct((n, MIX_A), F32),
                   jax.ShapeDtypeStruct((n, HA, DK_A, DK_A), F32),
                   jax.ShapeDtypeStruct((CONV_W - 1, n, A_QKV), F32)],
        compiler_params=pltpu.CompilerParams(dimension_semantics=("arbitrary",),
                                             vmem_limit_bytes=VMEM_LIMIT),
        name="gdn_step",
    )(pa, bd, sconv_t, s_a, conv_w, hp, anw)


def _rwkv_step_kernel(pb_ref, sh_ref, s_ref, mu_ref, w0_ref, w2_ref, a0_ref, a2_ref, kk_ref, ka_ref,
                      rk_ref, lnw_ref, lnb_ref, ones_ref,
                      o_ref, s_out_ref):
    ones_blk = ones_ref[...]
    pb = pb_ref[...]
    xb = pb + (sh_ref[...] - pb) * mu_ref[...]
    r, kmod, v, gate, logw, a, kk = _rwkv_front(
        xb, w0_ref[...], w2_ref[...], a0_ref[...], a2_ref[...], kk_ref[...], ka_ref[...], ones_blk)
    wdec = jnp.exp(logw)
    nkk = -kk
    kka = kk * a

    bh = [(b, h) for b in range(SB) for h in range(HB)]
    cut = lambda z: [z[b:b + 1, h * HS_B:(h + 1) * HS_B] for b, h in bh]
    nkk_r, kka_r, k_r, v_r, r_r, w_r = cut(nkk), cut(kka), cut(kmod), cut(v), cut(r), cut(wdec)
    s0 = [s_ref[b, h] for b, h in bh]
    sa = [_dot(_pad_rows([x], HS_B), s, NT)[0:1, :] for x, s in zip(nkk_r, s0)]
    upd = [_dot(_pad_rows([x, y], HS_B), _pad_rows([z, t], HS_B), TN)
           for x, y, z, t in zip(sa, v_r, kka_r, k_r)]
    s1 = [s * w + d for s, w, d in zip(s0, w_r, upd)]
    for (b, h), s in zip(bh, s1):
        s_out_ref[b, h] = s
    y_r = [_dot(_pad_rows([x], HS_B), s, NT)[0:1, :] for x, s in zip(r_r, s1)]
    y = jnp.concatenate(
        [jnp.concatenate([y_r[b * HB + h] for b in range(SB)], axis=0) for h in range(HB)], axis=-1)

    o_ref[...] = _rwkv_back(y, r, kmod, v, gate, rk_ref[...], lnw_ref[...], lnb_ref[...], ones_blk)


def _rwkv_step(pb, shift, s_b, mu, w0, w2, a0, a2, k_k, k_a, rk, lnw, lnb, ones_blk):
    n = pb.shape[0]
    row = lambda i: (i, 0)
    fixed = lambda i: (0, 0)
    vec = pl.BlockSpec((1, MIX_B), fixed)
    return pl.pallas_call(
        _rwkv_step_kernel,
        grid=(n // SB,),
        in_specs=[pl.BlockSpec((SB, B_W), row),
                  pl.BlockSpec((SB, B_W), row),
                  pl.BlockSpec((SB, HB, HS_B, HS_B), lambda i: (i, 0, 0, 0)),
                  pl.BlockSpec((1, B_W), fixed),
                  vec, pl.BlockSpec((LORA, MIX_B), fixed),
                  vec, pl.BlockSpec((LORA, MIX_B), fixed),
                  vec, vec, vec, vec, vec,
                  pl.BlockSpec((MIX_B, MIX_B), fixed)],
        out_specs=[pl.BlockSpec((SB, MIX_B), row),
                   pl.BlockSpec((SB, HB, HS_B, HS_B), lambda i: (i, 0, 0, 0))],
        out_shape=[jax.ShapeDtypeStruct((n, MIX_B), F32),
                   jax.ShapeDtypeStruct((n, HB, HS_B, HS_B), F32)],
        compiler_params=pltpu.CompilerParams(dimension_semantics=("arbitrary",),
                                             vmem_limit_bytes=VMEM_LIMIT),
        name="rwkv_step",
    )(pb, shift, s_b, mu, w0, w2, a0, a2, k_k, k_a, rk, lnw, lnb, ones_blk)


def _block_ones(width, group):
    idx = jnp.arange(width) // group
    return (idx[:, None] == idx[None, :]).astype(BF16)


def kernel(x_prompt, x_sample, state_a_mat, state_a_conv, state_b_mat, state_b_shift, norm_w, w_in,
           conv_w, a_log, dt_bias, a_norm_w, mu, w0, w2, a0, a2, k_k, k_a, r_k, ln_w, ln_b, w_out,
           final_norm_w):
    bsz, seq, _ = x_prompt.shape
    nsmp = x_sample.shape[0]
    a_w = A_MAIN + 2 * HA

    w = w_in[0]
    w_cat = jnp.concatenate(
        [w[:, :A_MAIN], w[:, a_w:], w[:, A_MAIN:a_w], jnp.zeros((D_MODEL, BD_W - 2 * HA), F32)],
        axis=1).astype(BF16)
    w_o = w_out[0].astype(BF16)
    hp = jnp.zeros((8, BD_W), F32)
    hp = hp.at[0, HA:2 * HA].set(a_log[0]).at[1, HA:2 * HA].set(dt_bias[0])
    nw = norm_w[0].reshape(1, D_MODEL)
    fw = final_norm_w.reshape(1, D_MODEL)
    anw = jnp.tile(a_norm_w[0].reshape(1, DK_A), (1, HA))
    ones_b = _block_ones(MIX_B, HS_B)
    row = lambda z: z[0].reshape(1, -1)
    b_params = (row(mu), row(w0), w2[0], row(a0), a2[0], row(k_k), row(k_a), row(r_k), row(ln_w), row(ln_b),
                ones_b)

    xp = x_prompt.reshape(bsz * seq, D_MODEL)
    pa, pb, bd = _inproj(xp, nw, w_cat)
    oa, p_amat, p_aconv = _gdn_prompt(pa.reshape(bsz, seq, A_MAIN), bd.reshape(bsz, seq, BD_W),
                                      conv_w[0], hp, anw)
    ob, p_bmat, p_bshift = _rwkv_prompt(pb.reshape(bsz, seq, B_W), *b_params)
    y_prompt = _outproj(xp, oa.reshape(bsz * seq, MIX_A), ob.reshape(bsz * seq, MIX_B), w_o, fw)

    xs = x_sample.reshape(nsmp, D_MODEL)
    sa_, sb_, sbd = _inproj(xs, nw, w_cat)
    sconv_t = jnp.swapaxes(state_a_conv[0], 0, 1)
    soa, s_amat, s_aconv_t = _gdn_step(sa_, sbd, sconv_t, state_a_mat[0], conv_w[0], hp, anw)
    sob, s_bmat = _rwkv_step(sb_, state_b_shift[0], state_b_mat[0], *b_params)
    y_sample = _outproj(xs, soa, sob, w_o, fw)

    return (y_prompt.reshape(bsz, seq, D_MODEL), y_sample.reshape(nsmp, 1, D_MODEL),
            p_amat[None], p_aconv[None], p_bmat[None], p_bshift.reshape(1, bsz, B_W),
            s_amat[None], jnp.swapaxes(s_aconv_t, 0, 1)[None], s_bmat[None], sb_[None])
```

```python
import jax
import jax.numpy as jnp
from jax import lax
from jax.experimental import pallas as pl
from jax.experimental.pallas import tpu as pltpu

F32 = jnp.float32
BF16 = jnp.bfloat16

D_MODEL = 1024
MIX_A = 512
MIX_B = 512
HA = 4
DK_A = 128
HB = 8
HS_B = 64
CONV_W = 4
LORA = 64
A_QKV = 3 * MIX_A
A_MAIN = A_QKV + MIX_A
B_W = 4 * MIX_B + 2 * LORA
BD_W = 128
RMS_EPS = 1e-6
GN_EPS = 64e-5
CHUNK = 64
TC = 256
SB = 8
VMEM_LIMIT = 56 * 1024 * 1024

HIGHEST = lax.Precision.HIGHEST
NN = (((1,), (0,)), ((), ()))
NT = (((1,), (1,)), ((), ()))
TN = (((0,), (0,)), ((), ()))


def _dot(a, b, dn=NN, prec=None):
    return lax.dot_general(a, b, dn, precision=prec, preferred_element_type=F32)


def _sigmoid(x):
    return 1.0 / (1.0 + jnp.exp(-x))


def _silu(x):
    return x * _sigmoid(x)


def _softplus(x):
    return jnp.maximum(x, 0.0) + jnp.log(1.0 + jnp.exp(-jnp.abs(x)))


def _l2norm(x):
    return x * lax.rsqrt(jnp.sum(x * x, axis=-1, keepdims=True) + 1e-12)


def _group_sum(x, ones_blk):
    return _dot(x.astype(BF16), ones_blk)


def _iota2(shape, dim):
    return lax.broadcasted_iota(jnp.int32, shape, dim)


def _chunk_tril(n):
    r = _iota2((n, n), 0)
    c = _iota2((n, n), 1)
    same = (r // CHUNK) == (c // CHUNK)
    return jnp.where(same & (c <= r), 1.0, 0.0).astype(F32)


def _tri_inv_many(lows):
    n = lows[0].shape[0]
    r = _iota2((n, n), 0)
    c = _iota2((n, n), 1)
    eye = jnp.where(r == c, 1.0, 0.0).astype(F32)
    base = (r // 8) == (c // 8)
    l0 = [jnp.where(base, low, 0.0) for low in lows]
    l2 = [_dot(a, a) for a in l0]
    l4 = [_dot(a, a) for a in l2]
    d = [_dot(eye - a, eye + b) for a, b in zip(l0, l2)]
    d = [_dot(a, eye + b) for a, b in zip(d, l4)]
    s = 8
    while s < n:
        sel = ((r // (2 * s)) == (c // (2 * s))) & ((r // s) != (c // s))
        t = [_dot(jnp.where(sel, low, 0.0), a) for low, a in zip(lows, d)]
        d = [a - _dot(a, b) for a, b in zip(d, t)]
        s *= 2
    return d


def _inproj_kernel(x_ref, nw_ref, wa_ref, wb_ref, oa_ref, ob_ref, obd_ref):
    x = x_ref[...]
    xn = (x * lax.rsqrt(jnp.mean(x * x, axis=-1, keepdims=True) + RMS_EPS) * nw_ref[...]).astype(BF16)
    oa_ref[...] = _dot(xn, wa_ref[...])
    pb = _dot(xn, wb_ref[...])
    ob_ref[...] = pb[:, :B_W]
    obd_ref[...] = pb[:, B_W:]


def _inproj(x2d, norm_w, w_a, w_b):
    n = x2d.shape[0]
    tm = min(512, n)
    row = lambda i: (i, 0)
    fixed = lambda i: (0, 0)
    return pl.pallas_call(
        _inproj_kernel,
        grid=(n // tm,),
        in_specs=[pl.BlockSpec((tm, D_MODEL), row),
                  pl.BlockSpec((1, D_MODEL), fixed),
                  pl.BlockSpec((D_MODEL, A_MAIN), fixed),
                  pl.BlockSpec((D_MODEL, B_W + BD_W), fixed)],
        out_specs=[pl.BlockSpec((tm, A_MAIN), row),
                   pl.BlockSpec((tm, B_W), row),
                   pl.BlockSpec((tm, BD_W), row)],
        out_shape=[jax.ShapeDtypeStruct((n, A_MAIN), F32),
                   jax.ShapeDtypeStruct((n, B_W), F32),
                   jax.ShapeDtypeStruct((n, BD_W), F32)],
        compiler_params=pltpu.CompilerParams(dimension_semantics=("arbitrary",),
                                             vmem_limit_bytes=VMEM_LIMIT),
        name="inproj",
    )(x2d, norm_w, w_a, w_b)


def _outproj_kernel(x_ref, ma_ref, mb_ref, w_ref, fw_ref, y_ref):
    h = (x_ref[...]
         + _dot(ma_ref[...].astype(BF16), w_ref[0:MIX_A, :])
         + _dot(mb_ref[...].astype(BF16), w_ref[MIX_A:, :]))
    y_ref[...] = h * lax.rsqrt(jnp.mean(h * h, axis=-1, keepdims=True) + RMS_EPS) * fw_ref[...]


def _outproj(x2d, mix_a, mix_b, w_out, final_w):
    n = x2d.shape[0]
    tm = min(512, n)
    row = lambda i: (i, 0)
    fixed = lambda i: (0, 0)
    return pl.pallas_call(
        _outproj_kernel,
        grid=(n // tm,),
        in_specs=[pl.BlockSpec((tm, D_MODEL), row),
                  pl.BlockSpec((tm, MIX_A), row),
                  pl.BlockSpec((tm, MIX_B), row),
                  pl.BlockSpec((D_MODEL, D_MODEL), fixed),
                  pl.BlockSpec((1, D_MODEL), fixed)],
        out_specs=pl.BlockSpec((tm, D_MODEL), row),
        out_shape=jax.ShapeDtypeStruct((n, D_MODEL), F32),
        compiler_params=pltpu.CompilerParams(dimension_semantics=("arbitrary",),
                                             vmem_limit_bytes=VMEM_LIMIT),
        name="outproj",
    )(x2d, mix_a, mix_b, w_out, final_w)


def _gdn_gates(bd, hp):
    beta = _sigmoid(bd)
    g = -jnp.exp(hp[0:1, :]) * _softplus(bd + hp[1:2, :])
    return beta, g


def _gdn_qk(qkv):
    heads = lambda z: [z[:, h * DK_A:(h + 1) * DK_A] for h in range(HA)]
    qn = jnp.concatenate([_l2norm(x) * (DK_A ** -0.5) for x in heads(qkv[:, 0:MIX_A])], axis=-1)
    kn = jnp.concatenate([_l2norm(x) for x in heads(qkv[:, MIX_A:2 * MIX_A])], axis=-1)
    return qn, kn


def _gdn_out(o, gate, anw):
    on = jnp.concatenate(
        [x * lax.rsqrt(jnp.mean(x * x, axis=-1, keepdims=True) + RMS_EPS)
         for x in [o[:, h * DK_A:(h + 1) * DK_A] for h in range(HA)]], axis=-1)
    return on * anw * _silu(gate)


def _gdn_prompt_kernel(pa_ref, bd_ref, cw_ref, hp_ref, anw_ref,
                       o_ref, s_out_ref, c_out_ref, ext_ref, s_ref):
    c = pl.program_id(1)

    @pl.when(c == 0)
    def _():
        ext_ref[0:8, :] = jnp.zeros((8, A_QKV), F32)
        s_ref[...] = jnp.zeros_like(s_ref)

    ext_ref[8:8 + TC, :] = pa_ref[:, 0:A_QKV]
    cw = cw_ref[...]
    conv = (ext_ref[5:5 + TC, :] * cw[0:1, :] + ext_ref[6:6 + TC, :] * cw[1:2, :]
            + ext_ref[7:7 + TC, :] * cw[2:3, :] + ext_ref[8:8 + TC, :] * cw[3:4, :])
    qkv = _silu(conv)
    c_out_ref[...] = ext_ref[TC + 5:TC + 8, :]
    ext_ref[0:8, :] = ext_ref[TC:TC + 8, :]

    qn, kn = _gdn_qk(qkv)
    beta_all, g_all = _gdn_gates(bd_ref[...], hp_ref[...])
    gcum = _dot(_chunk_tril(TC), g_all, prec=HIGHEST)
    gcum_t = gcum.T

    r = _iota2((CHUNK, CHUNK), 0)
    cc = _iota2((CHUNK, CHUNK), 1)
    incl = cc <= r
    strict = cc < r

    nj = TC // CHUNK
    jh = [(j, h) for j in range(nj) for h in range(HA)]
    rows_of = lambda j: slice(j * CHUNK, (j + 1) * CHUNK)
    lanes_of = lambda h: slice(h * DK_A, (h + 1) * DK_A)
    q = [qn[rows_of(j), lanes_of(h)] for j, h in jh]
    k = [kn[rows_of(j), lanes_of(h)] for j, h in jh]
    v = [qkv[rows_of(j), 2 * MIX_A + h * DK_A:2 * MIX_A + (h + 1) * DK_A] for j, h in jh]
    beta = [beta_all[rows_of(j), h:h + 1] for j, h in jh]
    gc = [gcum[rows_of(j), HA + h:HA + h + 1] for j, h in jh]
    gr = [gcum_t[HA + h:HA + h + 1, rows_of(j)] for j, h in jh]
    glast = [gcum[(j + 1) * CHUNK - 1:(j + 1) * CHUNK, HA + h:HA + h + 1] for j, h in jh]
    decay = [jnp.where(incl, jnp.exp(jnp.where(incl, a - b, 0.0)), 0.0) for a, b in zip(gc, gr)]
    kq = [_dot(jnp.concatenate([a, b], axis=0), a, NT) for a, b in zip(k, q)]
    low = [jnp.where(strict, b * d * m[0:CHUNK, :], 0.0) for b, d, m in zip(beta, decay, kq)]
    tinv = _tri_inv_many(low)
    eg = [jnp.exp(a) for a in gc]
    wu = [_dot(t, jnp.concatenate([(b * e) * a, b * c_], axis=1))
          for t, b, e, a, c_ in zip(tinv, beta, eg, k, v)]
    kwu = [_dot(a * jnp.exp(gl - g), x, TN) for a, gl, g, x in zip(k, glast, gc, wu)]
    qwu = [_dot(d * m[CHUNK:, :], x) for d, m, x in zip(decay, kq, wu)]
    rd = _iota2((DK_A, DK_A), 0)
    cd = _iota2((DK_A, DK_A), 1)
    lhs = [jnp.concatenate([jnp.where(rd == cd, jnp.exp(gl), 0.0) - kx[:, 0:DK_A],
                            a * e - qx[:, 0:DK_A]], axis=0)
           for gl, kx, a, e, qx in zip(glast, kwu, q, eg, qwu)]

    o_rows = []
    for j in range(nj):
        idx = [j * HA + h for h in range(HA)]
        res = [_dot(lhs[i], s_ref[h]) for h, i in enumerate(idx)]
        for h, i in enumerate(idx):
            s_ref[h] = res[h][0:DK_A, :] + kwu[i][:, DK_A:]
        o_rows.append(jnp.concatenate(
            [res[h][DK_A:, :] + qwu[i][:, DK_A:] for h, i in enumerate(idx)], axis=-1))

    o_ref[...] = _gdn_out(jnp.concatenate(o_rows, axis=0), pa_ref[:, A_QKV:A_MAIN], anw_ref[...])
    s_out_ref[...] = s_ref[...]


def _gdn_prompt(pa, bd, conv_w, hp, anw):
    b, t, _ = pa.shape
    blk = lambda i, j: (i, j, 0)
    fixed = lambda i, j: (0, 0)
    return pl.pallas_call(
        _gdn_prompt_kernel,
        grid=(b, t // TC),
        in_specs=[pl.BlockSpec((None, TC, A_MAIN), blk),
                  pl.BlockSpec((None, TC, BD_W), blk),
                  pl.BlockSpec((CONV_W, A_QKV), fixed),
                  pl.BlockSpec((8, BD_W), fixed),
                  pl.BlockSpec((1, MIX_A), fixed)],
        out_specs=[pl.BlockSpec((None, TC, MIX_A), blk),
                   pl.BlockSpec((None, HA, DK_A, DK_A), lambda i, j: (i, 0, 0, 0)),
                   pl.BlockSpec((None, CONV_W - 1, A_QKV), lambda i, j: (i, 0, 0))],
        out_shape=[jax.ShapeDtypeStruct((b, t, MIX_A), F32),
                   jax.ShapeDtypeStruct((b, HA, DK_A, DK_A), F32),
                   jax.ShapeDtypeStruct((b, CONV_W - 1, A_QKV), F32)],
        scratch_shapes=[pltpu.VMEM((TC + 8, A_QKV), F32),
                        pltpu.VMEM((HA, DK_A, DK_A), F32)],
        compiler_params=pltpu.CompilerParams(dimension_semantics=("arbitrary", "arbitrary"),
                                             vmem_limit_bytes=VMEM_LIMIT),
        name="gdn_prompt",
    )(pa, bd, conv_w, hp, anw)


def _rwkv_front(xb, w0, w2, a0, a2, k_k, k_a, ones_blk):
    r = xb[:, 0:MIX_B]
    kr = xb[:, MIX_B:2 * MIX_B]
    vr = xb[:, 2 * MIX_B:3 * MIX_B]
    gate = xb[:, 3 * MIX_B:4 * MIX_B]
    wd = xb[:, 4 * MIX_B:4 * MIX_B + LORA]
    ad = xb[:, 4 * MIX_B + LORA:]
    w_ll = -_softplus(-(w0 + _dot(jnp.tanh(wd), w2))) - 0.5
    logw = -jnp.exp(w_ll)
    a = _sigmoid(a0 + _dot(ad, a2))
    kraw = kr * k_k
    kk = kraw * lax.rsqrt(_group_sum(kraw * kraw, ones_blk) + 1e-12)
    kmod = kr * (1.0 + (a - 1.0) * k_a)
    return r, kmod, vr, gate, logw, a, kk


def _rwkv_back(y, r, kmod, v, gate, rk, lnw, lnb, ones_blk):
    inv = 1.0 / HS_B
    yc = y - _group_sum(y, ones_blk) * inv
    var = _group_sum(yc * yc, ones_blk) * inv
    gn = yc * lax.rsqrt(var + GN_EPS) * lnw + lnb
    bonus = _group_sum(r * kmod * rk, ones_blk) * v
    return (gn + bonus) * _silu(gate)


def _rwkv_prompt_kernel(pb_ref, mu_ref, w0_ref, w2_ref, a0_ref, a2_ref, kk_ref, ka_ref,
                        rk_ref, lnw_ref, lnb_ref, ones_ref,
                        o_ref, s_out_ref, sh_out_ref, carry_ref, s_ref):
    c = pl.program_id(1)

    @pl.when(c == 0)
    def _():
        carry_ref[...] = jnp.zeros_like(carry_ref)
        s_ref[...] = jnp.zeros_like(s_ref)

    ones_blk = ones_ref[...]
    pb = pb_ref[...]
    rowi = _iota2((TC, 1), 0)
    prev = jnp.where(rowi == 0, carry_ref[0:1, :], pltpu.roll(pb, 1, 0))
    carry_ref[0:1, :] = pb[TC - 1:TC, :]
    sh_out_ref[...] = pb[TC - 1:TC, :]
    xb = pb + (prev - pb) * mu_ref[...]
    r, kmod, v, gate, logw, a, kk = _rwkv_front(
        xb, w0_ref[...], w2_ref[...], a0_ref[...], a2_ref[...], kk_ref[...], ka_ref[...], ones_blk)

    g = _dot(_chunk_tril(TC), logw, prec=HIGHEST)
    eg = jnp.exp(g)
    egn = jnp.exp(-g)
    ag = -kk * jnp.exp(g - logw)
    kka = kk * a
    bg = kka * egn
    kg = kmod * egn
    rg = r * eg

    ri = _iota2((CHUNK, CHUNK), 0)
    ci = _iota2((CHUNK, CHUNK), 1)
    incl = ci <= ri
    strict = ci < ri

    nj = TC // CHUNK
    jh = [(j, h) for j in range(nj) for h in range(HB)]
    rows_of = lambda j: slice(j * CHUNK, (j + 1) * CHUNK)
    lanes_of = lambda h: slice(h * HS_B, (h + 1) * HS_B)
    glast = [g[(j + 1) * CHUNK - 1:(j + 1) * CHUNK, :] for j in range(nj)]
    edec = [jnp.exp(glast[j] - g[rows_of(j), :]) for j in range(nj)]
    bdec = [kka[rows_of(j), :] * edec[j] for j in range(nj)]
    kdec = [kmod[rows_of(j), :] * edec[j] for j in range(nj)]
    eglast = [jnp.exp(glast[j]) for j in range(nj)]

    cut = lambda z: [z[rows_of(j), lanes_of(h)] for j, h in jh]
    ag_h, rg_h, bg_h, kg_h, v_h = cut(ag), cut(rg), cut(bg), cut(kg), cut(v)
    bdec_h = [bdec[j][:, lanes_of(h)] for j, h in jh]
    kdec_h = [kdec[j][:, lanes_of(h)] for j, h in jh]
    eglast_h = [eglast[j][:, lanes_of(h)] for j, h in jh]
    amat = [_dot(jnp.concatenate([a, b], axis=0), jnp.concatenate([c_, d], axis=0), NT)
            for a, b, c_, d in zip(ag_h, rg_h, bg_h, kg_h)]
    a_ab = [jnp.where(strict, m[0:CHUNK, 0:CHUNK], 0.0) for m in amat]
    a_ak = [jnp.where(strict, m[0:CHUNK, CHUNK:], 0.0) for m in amat]
    a_rb = [jnp.where(incl, m[CHUNK:, 0:CHUNK], 0.0) for m in amat]
    a_rk = [jnp.where(incl, m[CHUNK:, CHUNK:], 0.0) for m in amat]
    tinv = _tri_inv_many([-a for a in a_ab])
    kv = [_dot(jnp.concatenate([a, b], axis=0), c_) for a, b, c_ in zip(a_ak, a_rk, v_h)]
    wu = [_dot(t, jnp.concatenate([a, x[0:CHUNK, :]], axis=1)) for t, a, x in zip(tinv, ag_h, kv)]
    rwu = [_dot(a, x) for a, x in zip(a_rb, wu)]
    zero = jnp.zeros((CHUNK, HS_B), F32)
    mn = [_dot(jnp.concatenate([a, b], axis=0),
               jnp.concatenate([x, jnp.concatenate([zero, c_], axis=1)], axis=0), TN)
          for a, b, x, c_ in zip(bdec_h, kdec_h, wu, v_h)]
    lhs = [jnp.concatenate([jnp.where(ri == ci, e, 0.0) + m[:, 0:HS_B], a + x[:, 0:HS_B]], axis=0)
           for e, m, a, x in zip(eglast_h, mn, rg_h, rwu)]

    y_rows = []
    for j in range(nj):
        idx = [j * HB + h for h in range(HB)]
        res = [_dot(lhs[i], s_ref[h]) for h, i in enumerate(idx)]
        for h, i in enumerate(idx):
            s_ref[h] = res[h][0:HS_B, :] + mn[i][:, HS_B:]
        y_rows.append(jnp.concatenate(
            [res[h][HS_B:, :] + rwu[i][:, HS_B:] + kv[i][CHUNK:, :] for h, i in enumerate(idx)],
            axis=-1))

    o_ref[...] = _rwkv_back(jnp.concatenate(y_rows, axis=0), r, kmod, v, gate,
                            rk_ref[...], lnw_ref[...], lnb_ref[...], ones_blk)
    for h in range(HB):
        s_out_ref[h] = s_ref[h].T


def _rwkv_prompt(pb, mu, w0, w2, a0, a2, k_k, k_a, rk, lnw, lnb, ones_blk):
    b, t, _ = pb.shape
    blk = lambda i, j: (i, j, 0)
    fixed = lambda i, j: (0, 0)
    vec = pl.BlockSpec((1, MIX_B), fixed)
    return pl.pallas_call(
        _rwkv_prompt_kernel,
        grid=(b, t // TC),
        in_specs=[pl.BlockSpec((None, TC, B_W), blk),
                  pl.BlockSpec((1, B_W), fixed),
                  vec, pl.BlockSpec((LORA, MIX_B), fixed),
                  vec, pl.BlockSpec((LORA, MIX_B), fixed),
                  vec, vec, vec, vec, vec,
                  pl.BlockSpec((MIX_B, MIX_B), fixed)],
        out_specs=[pl.BlockSpec((None, TC, MIX_B), blk),
                   pl.BlockSpec((None, HB, HS_B, HS_B), lambda i, j: (i, 0, 0, 0)),
                   pl.BlockSpec((None, 1, B_W), lambda i, j: (i, 0, 0))],
        out_shape=[jax.ShapeDtypeStruct((b, t, MIX_B), F32),
                   jax.ShapeDtypeStruct((b, HB, HS_B, HS_B), F32),
                   jax.ShapeDtypeStruct((b, 1, B_W), F32)],
        scratch_shapes=[pltpu.VMEM((8, B_W), F32),
                        pltpu.VMEM((HB, HS_B, HS_B), F32)],
        compiler_params=pltpu.CompilerParams(dimension_semantics=("arbitrary", "arbitrary"),
                                             vmem_limit_bytes=VMEM_LIMIT),
        name="rwkv_prompt",
    )(pb, mu, w0, w2, a0, a2, k_k, k_a, rk, lnw, lnb, ones_blk)


def _pad_rows(rows, width):
    return jnp.concatenate(rows + [jnp.zeros((8 - len(rows), width), F32)], axis=0)


def _gdn_step_kernel(pa_ref, bd_ref, sc_ref, s_ref, cw_ref, hp_ref, anw_ref,
                     o_ref, s_out_ref, c_out_ref):
    u = pa_ref[:, 0:A_QKV]
    cw = cw_ref[...]
    conv = (sc_ref[0] * cw[0:1, :] + sc_ref[1] * cw[1:2, :] + sc_ref[2] * cw[2:3, :] + u * cw[3:4, :])
    qkv = _silu(conv)
    c_out_ref[0] = sc_ref[1]
    c_out_ref[1] = sc_ref[2]
    c_out_ref[2] = u

    qn, kn = _gdn_qk(qkv)
    beta_all, g_all = _gdn_gates(bd_ref[...], hp_ref[...])
    eg_all = jnp.exp(g_all)
    lanes_of = lambda h: slice(h * DK_A, (h + 1) * DK_A)
    q = [qn[:, lanes_of(h)] for h in range(HA)]
    k = [kn[:, lanes_of(h)] for h in range(HA)]
    v = [qkv[:, 2 * MIX_A + h * DK_A:2 * MIX_A + (h + 1) * DK_A] for h in range(HA)]
    beta = [beta_all[:, h:h + 1] for h in range(HA)]
    eg = [eg_all[:, HA + h:HA + h + 1] for h in range(HA)]

    bh = [(b, h) for b in range(SB) for h in range(HA)]
    s0 = [s_ref[b, h] for b, h in bh]
    ks_qs = [_dot(_pad_rows([k[h][b:b + 1, :], q[h][b:b + 1, :]], DK_A), s) for (b, h), s in zip(bh, s0)]
    o_heads = []
    for h in range(HA):
        ks = jnp.concatenate([ks_qs[b * HA + h][0:1, :] for b in range(SB)], axis=0)
        qs = jnp.concatenate([ks_qs[b * HA + h][1:2, :] for b in range(SB)], axis=0)
        u2 = beta[h] * (v[h] - eg[h] * ks)
        for b in range(SB):
            s_out_ref[b, h] = (eg[h][b:b + 1, :] * s0[b * HA + h]
                               + _dot(_pad_rows([k[h][b:b + 1, :]], DK_A),
                                      _pad_rows([u2[b:b + 1, :]], DK_A), TN))
        o_heads.append(eg[h] * qs + jnp.sum(q[h] * k[h], axis=-1, keepdims=True) * u2)

    o_ref[...] = _gdn_out(jnp.concatenate(o_heads, axis=-1), pa_ref[:, A_QKV:A_MAIN], anw_ref[...])


def _gdn_step(pa, bd, sconv_t, s_a, conv_w, hp, anw):
    n = pa.shape[0]
    row = lambda i: (i, 0)
    fixed = lambda i: (0, 0)
    return pl.pallas_call(
        _gdn_step_kernel,
        grid=(n // SB,),
        in_specs=[pl.BlockSpec((SB, A_MAIN), row),
                  pl.BlockSpec((SB, BD_W), row),
                  pl.BlockSpec((CONV_W - 1, SB, A_QKV), lambda i: (0, i, 0)),
                  pl.BlockSpec((SB, HA, DK_A, DK_A), lambda i: (i, 0, 0, 0)),
                  pl.BlockSpec((CONV_W, A_QKV), fixed),
                  pl.BlockSpec((8, BD_W), fixed),
                  pl.BlockSpec((1, MIX_A), fixed)],
        out_specs=[pl.BlockSpec((SB, MIX_A), row),
                   pl.BlockSpec((SB, HA, DK_A, DK_A), lambda i: (i, 0, 0, 0)),
                   pl.BlockSpec((CONV_W - 1, SB, A_QKV), lambda i: (0, i, 0))],
        out_shape=[jax.ShapeDtypeStruct((n, MIX_A), F32),
                   jax.ShapeDtypeStruct((n, HA, DK_A, DK_A), F32),
                   jax.ShapeDtypeStruct((CONV_W - 1, n, A_QKV), F32)],
        compiler_params=pltpu.CompilerParams(dimension_semantics=("arbitrary",),
                                             vmem_limit_bytes=VMEM_LIMIT),
        name="gdn_step",
    )(pa, bd, sconv_t, s_a, conv_w, hp, anw)


def _rwkv_step_lanes_kernel(pb_ref, sh_ref, s_ref, mu_ref, w0_ref, w2_ref, a0_ref, a2_ref, kk_ref, ka_ref,
                            rk_ref, lnw_ref, lnb_ref, ones_ref,
                            o_ref, s_out_ref, vec_scr, row_scr, y_scr):
    h = pl.program_id(0)
    ones_blk = ones_ref[...]

    @pl.when(h == 0)
    def _():
        pb = pb_ref[...]
        xb = pb + (sh_ref[...] - pb) * mu_ref[...]
        r, kmod, v, gate, logw, a, kk = _rwkv_front(
            xb, w0_ref[...], w2_ref[...], a0_ref[...], a2_ref[...], kk_ref[...], ka_ref[...], ones_blk)
        for slot, z in enumerate((-kk, kk * a, kmod, v, r, jnp.exp(logw))):
            vec_scr[slot] = z.T
        for slot, z in enumerate((r, kmod, v, gate)):
            row_scr[slot] = z

    base = pl.multiple_of(h * HS_B, HS_B)
    chan = pl.ds(base, HS_B)
    nkk = vec_scr[0, chan, :]
    kka = vec_scr[1, chan, :]
    kmod_t = vec_scr[2, chan, :]
    r_t = vec_scr[4, chan, :]
    wdec = vec_scr[5, chan, :]

    def body(vi, carry):
        s0 = s_ref[0, vi]
        sa = jnp.sum(s0 * nkk, axis=0, keepdims=True)
        s1 = s0 * wdec + sa * kka + vec_scr[3, pl.ds(base + vi, 1), :] * kmod_t
        s_out_ref[0, vi] = s1
        y_scr[pl.ds(base + vi, 1), :] = jnp.sum(s1 * r_t, axis=0, keepdims=True)
        return carry

    lax.fori_loop(0, HS_B, body, 0, unroll=4)

    @pl.when(h == HB - 1)
    def _():
        o_ref[...] = _rwkv_back(y_scr[...].T, row_scr[0], row_scr[1], row_scr[2], row_scr[3],
                                rk_ref[...], lnw_ref[...], lnb_ref[...], ones_blk)


def _rwkv_step_lanes(pb, shift, s_hvkb, mu, w0, w2, a0, a2, k_k, k_a, rk, lnw, lnb, ones_blk):
    n = pb.shape[0]
    fixed = lambda i: (0, 0)
    vec = pl.BlockSpec((1, MIX_B), fixed)
    state = pl.BlockSpec((1, HS_B, HS_B, n), lambda i: (i, 0, 0, 0))
    return pl.pallas_call(
        _rwkv_step_lanes_kernel,
        grid=(HB,),
        in_specs=[pl.BlockSpec((n, B_W), fixed),
                  pl.BlockSpec((n, B_W), fixed),
                  state,
                  pl.BlockSpec((1, B_W), fixed),
                  vec, pl.BlockSpec((LORA, MIX_B), fixed),
                  vec, pl.BlockSpec((LORA, MIX_B), fixed),
                  vec, vec, vec, vec, vec,
                  pl.BlockSpec((MIX_B, MIX_B), fixed)],
        out_specs=[pl.BlockSpec((n, MIX_B), fixed), state],
        out_shape=[jax.ShapeDtypeStruct((n, MIX_B), F32),
                   jax.ShapeDtypeStruct((HB, HS_B, HS_B, n), F32)],
        scratch_shapes=[pltpu.VMEM((6, MIX_B, n), F32),
                        pltpu.VMEM((4, n, MIX_B), F32),
                        pltpu.VMEM((MIX_B, n), F32)],
        compiler_params=pltpu.CompilerParams(dimension_semantics=("arbitrary",),
                                             vmem_limit_bytes=VMEM_LIMIT),
        name="rwkv_step",
    )(pb, shift, s_hvkb, mu, w0, w2, a0, a2, k_k, k_a, rk, lnw, lnb, ones_blk)


def _block_ones(width, group):
    idx = jnp.arange(width) // group
    return (idx[:, None] == idx[None, :]).astype(BF16)


def kernel(x_prompt, x_sample, state_a_mat, state_a_conv, state_b_mat, state_b_shift, norm_w, w_in,
           conv_w, a_log, dt_bias, a_norm_w, mu, w0, w2, a0, a2, k_k, k_a, r_k, ln_w, ln_b, w_out,
           final_norm_w):
    bsz, seq, _ = x_prompt.shape
    nsmp = x_sample.shape[0]
    a_w = A_MAIN + 2 * HA

    w = w_in[0]
    w_parts = (w[:, :A_MAIN].astype(BF16),
               jnp.concatenate([w[:, a_w:], w[:, A_MAIN:a_w],
                                jnp.zeros((D_MODEL, BD_W - 2 * HA), F32)], axis=1).astype(BF16))
    w_o = w_out[0].astype(BF16)
    hp = jnp.zeros((8, BD_W), F32)
    hp = hp.at[0, HA:2 * HA].set(a_log[0]).at[1, HA:2 * HA].set(dt_bias[0])
    nw = norm_w[0].reshape(1, D_MODEL)
    fw = final_norm_w.reshape(1, D_MODEL)
    anw = jnp.tile(a_norm_w[0].reshape(1, DK_A), (1, HA))
    ones_b = _block_ones(MIX_B, HS_B)
    row = lambda z: z[0].reshape(1, -1)
    b_params = (row(mu), row(w0), w2[0], row(a0), a2[0], row(k_k), row(k_a), row(r_k), row(ln_w), row(ln_b),
                ones_b)

    xp = x_prompt.reshape(bsz * seq, D_MODEL)
    pa, pb, bd = _inproj(xp, nw, *w_parts)
    oa, p_amat, p_aconv = _gdn_prompt(pa.reshape(bsz, seq, A_MAIN), bd.reshape(bsz, seq, BD_W),
                                      conv_w[0], hp, anw)
    ob, p_bmat, p_bshift = _rwkv_prompt(pb.reshape(bsz, seq, B_W), *b_params)
    y_prompt = _outproj(xp, oa.reshape(bsz * seq, MIX_A), ob.reshape(bsz * seq, MIX_B), w_o, fw)

    xs = x_sample.reshape(nsmp, D_MODEL)
    sa_, sb_, sbd = _inproj(xs, nw, *w_parts)
    sconv_t = jnp.swapaxes(state_a_conv[0], 0, 1)
    soa, s_amat, s_aconv_t = _gdn_step(sa_, sbd, sconv_t, state_a_mat[0], conv_w[0], hp, anw)
    sob, s_bmat_t = _rwkv_step_lanes(sb_, state_b_shift[0], jnp.transpose(state_b_mat[0], (1, 2, 3, 0)),
                                     *b_params)
    s_bmat = jnp.transpose(s_bmat_t, (3, 0, 1, 2))
    y_sample = _outproj(xs, soa, sob, w_o, fw)

    return (y_prompt.reshape(bsz, seq, D_MODEL), y_sample.reshape(nsmp, 1, D_MODEL),
            p_amat[None], p_aconv[None], p_bmat[None], p_bshift.reshape(1, bsz, B_W),
            s_amat[None], jnp.swapaxes(s_aconv_t, 0, 1)[None], s_bmat[None], sb_[None])
```

```python
import jax
import jax.numpy as jnp
from jax import lax
from jax.experimental import pallas as pl
from jax.experimental.pallas import tpu as pltpu

F32 = jnp.float32
BF16 = jnp.bfloat16

D_MODEL = 1024
MIX_A = 512
MIX_B = 512
HA = 4
DK_A = 128
HB = 8
HS_B = 64
PAIR = 2 * HS_B
CONV_W = 4
LORA = 64
A_QKV = 3 * MIX_A
A_MAIN = A_QKV + MIX_A
B_W = 4 * MIX_B + 2 * LORA
BD_W = 128
RMS_EPS = 1e-6
GN_EPS = 64e-5
CHUNK = 64
TC = 256
SB = 8
VMEM_LIMIT = 56 * 1024 * 1024

HIGHEST = lax.Precision.HIGHEST
NN = (((1,), (0,)), ((), ()))
NT = (((1,), (1,)), ((), ()))
TN = (((0,), (0,)), ((), ()))


def _dot(a, b, dn=NN, prec=None):
    return lax.dot_general(a, b, dn, precision=prec, preferred_element_type=F32)


def _sigmoid(x):
    return 1.0 / (1.0 + jnp.exp(-x))


def _silu(x):
    return x * _sigmoid(x)


def _softplus(x):
    return jnp.maximum(x, 0.0) + jnp.log(1.0 + jnp.exp(-jnp.abs(x)))


def _l2norm(x):
    return x * lax.rsqrt(jnp.sum(x * x, axis=-1, keepdims=True) + 1e-12)


def _group_sum(x, ones_blk):
    return _dot(x.astype(BF16), ones_blk)


def _iota2(shape, dim):
    return lax.broadcasted_iota(jnp.int32, shape, dim)


def _chunk_tril(n):
    r = _iota2((n, n), 0)
    c = _iota2((n, n), 1)
    same = (r // CHUNK) == (c // CHUNK)
    return jnp.where(same & (c <= r), 1.0, 0.0).astype(F32)


def _tri_inv_many(lows):
    n = lows[0].shape[0]
    r = _iota2((n, n), 0)
    c = _iota2((n, n), 1)
    eye = jnp.where(r == c, 1.0, 0.0).astype(F32)
    base = (r // 8) == (c // 8)
    l0 = [jnp.where(base, low, 0.0) for low in lows]
    l2 = [_dot(a, a) for a in l0]
    l4 = [_dot(a, a) for a in l2]
    d = [_dot(eye - a, eye + b) for a, b in zip(l0, l2)]
    d = [_dot(a, eye + b) for a, b in zip(d, l4)]
    s = 8
    while s < n:
        sel = ((r // (2 * s)) == (c // (2 * s))) & ((r // s) != (c // s))
        t = [_dot(jnp.where(sel, low, 0.0), a) for low, a in zip(lows, d)]
        d = [a - _dot(a, b) for a, b in zip(d, t)]
        s *= 2
    return d


def _pair_diag(x):
    n = x.shape[0]
    left = _iota2(x.shape, 1) < n
    return jnp.concatenate([jnp.where(left, x, 0.0), jnp.where(left, 0.0, x)], axis=0)


def _tri_inv_pairs(lows):
    n = lows[0].shape[0]
    r = _iota2((n, 2 * n), 0)
    lane = _iota2((n, 2 * n), 1)
    c = jnp.where(lane >= n, lane - n, lane)
    mul = lambda a, b: _dot(a, _pair_diag(b))
    eye = jnp.where(r == c, 1.0, 0.0).astype(F32)
    base = (r // 8) == (c // 8)
    l0 = [jnp.where(base, low, 0.0) for low in lows]
    l2 = [mul(a, a) for a in l0]
    l4 = [mul(a, a) for a in l2]
    d = [mul(eye - a, eye + b) for a, b in zip(l0, l2)]
    d = [mul(a, eye + b) for a, b in zip(d, l4)]
    s = 8
    while s < n:
        sel = ((r // (2 * s)) == (c // (2 * s))) & ((r // s) != (c // s))
        t = [mul(jnp.where(sel, low, 0.0), a) for low, a in zip(lows, d)]
        d = [a - mul(a, b) for a, b in zip(d, t)]
        s *= 2
    return d


def _inproj_kernel(x_ref, nw_ref, wa_ref, wb_ref, oa_ref, ob_ref, obd_ref):
    x = x_ref[...]
    xn = (x * lax.rsqrt(jnp.mean(x * x, axis=-1, keepdims=True) + RMS_EPS) * nw_ref[...]).astype(BF16)
    oa_ref[...] = _dot(xn, wa_ref[...])
    pb = _dot(xn, wb_ref[...])
    ob_ref[...] = pb[:, :B_W]
    obd_ref[...] = pb[:, B_W:]


def _inproj(x2d, norm_w, w_a, w_b):
    n = x2d.shape[0]
    tm = min(512, n)
    row = lambda i: (i, 0)
    fixed = lambda i: (0, 0)
    return pl.pallas_call(
        _inproj_kernel,
        grid=(n // tm,),
        in_specs=[pl.BlockSpec((tm, D_MODEL), row),
                  pl.BlockSpec((1, D_MODEL), fixed),
                  pl.BlockSpec((D_MODEL, A_MAIN), fixed),
                  pl.BlockSpec((D_MODEL, B_W + BD_W), fixed)],
        out_specs=[pl.BlockSpec((tm, A_MAIN), row),
                   pl.BlockSpec((tm, B_W), row),
                   pl.BlockSpec((tm, BD_W), row)],
        out_shape=[jax.ShapeDtypeStruct((n, A_MAIN), F32),
                   jax.ShapeDtypeStruct((n, B_W), F32),
                   jax.ShapeDtypeStruct((n, BD_W), F32)],
        compiler_params=pltpu.CompilerParams(dimension_semantics=("arbitrary",),
                                             vmem_limit_bytes=VMEM_LIMIT),
        name="inproj",
    )(x2d, norm_w, w_a, w_b)


def _outproj_kernel(x_ref, ma_ref, mb_ref, w_ref, fw_ref, y_ref):
    h = (x_ref[...]
         + _dot(ma_ref[...].astype(BF16), w_ref[0:MIX_A, :])
         + _dot(mb_ref[...].astype(BF16), w_ref[MIX_A:, :]))
    y_ref[...] = h * lax.rsqrt(jnp.mean(h * h, axis=-1, keepdims=True) + RMS_EPS) * fw_ref[...]


def _outproj(x2d, mix_a, mix_b, w_out, final_w):
    n = x2d.shape[0]
    tm = min(512, n)
    row = lambda i: (i, 0)
    fixed = lambda i: (0, 0)
    return pl.pallas_call(
        _outproj_kernel,
        grid=(n // tm,),
        in_specs=[pl.BlockSpec((tm, D_MODEL), row),
                  pl.BlockSpec((tm, MIX_A), row),
                  pl.BlockSpec((tm, MIX_B), row),
                  pl.BlockSpec((D_MODEL, D_MODEL), fixed),
                  pl.BlockSpec((1, D_MODEL), fixed)],
        out_specs=pl.BlockSpec((tm, D_MODEL), row),
        out_shape=jax.ShapeDtypeStruct((n, D_MODEL), F32),
        compiler_params=pltpu.CompilerParams(dimension_semantics=("arbitrary",),
                                             vmem_limit_bytes=VMEM_LIMIT),
        name="outproj",
    )(x2d, mix_a, mix_b, w_out, final_w)


def _gdn_gates(bd, hp):
    beta = _sigmoid(bd)
    g = -jnp.exp(hp[0:1, :]) * _softplus(bd + hp[1:2, :])
    return beta, g


def _gdn_qk(qkv):
    heads = lambda z: [z[:, h * DK_A:(h + 1) * DK_A] for h in range(HA)]
    qn = jnp.concatenate([_l2norm(x) * (DK_A ** -0.5) for x in heads(qkv[:, 0:MIX_A])], axis=-1)
    kn = jnp.concatenate([_l2norm(x) for x in heads(qkv[:, MIX_A:2 * MIX_A])], axis=-1)
    return qn, kn


def _gdn_out(o, gate, anw):
    on = jnp.concatenate(
        [x * lax.rsqrt(jnp.mean(x * x, axis=-1, keepdims=True) + RMS_EPS)
         for x in [o[:, h * DK_A:(h + 1) * DK_A] for h in range(HA)]], axis=-1)
    return on * anw * _silu(gate)


def _gdn_prompt_kernel(pa_ref, bd_ref, cw_ref, hp_ref, anw_ref,
                       o_ref, s_out_ref, c_out_ref, ext_ref, s_ref):
    c = pl.program_id(1)

    @pl.when(c == 0)
    def _():
        ext_ref[0:8, :] = jnp.zeros((8, A_QKV), F32)
        s_ref[...] = jnp.zeros_like(s_ref)

    ext_ref[8:8 + TC, :] = pa_ref[:, 0:A_QKV]
    cw = cw_ref[...]
    conv = (ext_ref[5:5 + TC, :] * cw[0:1, :] + ext_ref[6:6 + TC, :] * cw[1:2, :]
            + ext_ref[7:7 + TC, :] * cw[2:3, :] + ext_ref[8:8 + TC, :] * cw[3:4, :])
    qkv = _silu(conv)
    c_out_ref[...] = ext_ref[TC + 5:TC + 8, :]
    ext_ref[0:8, :] = ext_ref[TC:TC + 8, :]

    qn, kn = _gdn_qk(qkv)
    beta_all, g_all = _gdn_gates(bd_ref[...], hp_ref[...])
    gcum = _dot(_chunk_tril(TC), g_all, prec=HIGHEST)
    gcum_t = gcum.T

    r = _iota2((CHUNK, CHUNK), 0)
    cc = _iota2((CHUNK, CHUNK), 1)
    incl = cc <= r
    strict = cc < r

    nj = TC // CHUNK
    jh = [(j, h) for j in range(nj) for h in range(HA)]
    rows_of = lambda j: slice(j * CHUNK, (j + 1) * CHUNK)
    lanes_of = lambda h: slice(h * DK_A, (h + 1) * DK_A)
    q = [qn[rows_of(j), lanes_of(h)] for j, h in jh]
    k = [kn[rows_of(j), lanes_of(h)] for j, h in jh]
    v = [qkv[rows_of(j), 2 * MIX_A + h * DK_A:2 * MIX_A + (h + 1) * DK_A] for j, h in jh]
    beta = [beta_all[rows_of(j), h:h + 1] for j, h in jh]
    gc = [gcum[rows_of(j), HA + h:HA + h + 1] for j, h in jh]
    gr = [gcum_t[HA + h:HA + h + 1, rows_of(j)] for j, h in jh]
    glast = [gcum[(j + 1) * CHUNK - 1:(j + 1) * CHUNK, HA + h:HA + h + 1] for j, h in jh]
    decay = [jnp.where(incl, jnp.exp(jnp.where(incl, a - b, 0.0)), 0.0) for a, b in zip(gc, gr)]
    kq = [_dot(jnp.concatenate([a, b], axis=0), a, NT) for a, b in zip(k, q)]
    low = [jnp.where(strict, b * d * m[0:CHUNK, :], 0.0) for b, d, m in zip(beta, decay, kq)]
    tinv = _tri_inv_many(low)
    eg = [jnp.exp(a) for a in gc]
    wu = [_dot(t, jnp.concatenate([(b * e) * a, b * c_], axis=1))
          for t, b, e, a, c_ in zip(tinv, beta, eg, k, v)]
    kwu = [_dot(a * jnp.exp(gl - g), x, TN) for a, gl, g, x in zip(k, glast, gc, wu)]
    qwu = [_dot(d * m[CHUNK:, :], x) for d, m, x in zip(decay, kq, wu)]
    rd = _iota2((DK_A, DK_A), 0)
    cd = _iota2((DK_A, DK_A), 1)
    lhs = [jnp.concatenate([jnp.where(rd == cd, jnp.exp(gl), 0.0) - kx[:, 0:DK_A],
                            a * e - qx[:, 0:DK_A]], axis=0)
           for gl, kx, a, e, qx in zip(glast, kwu, q, eg, qwu)]

    o_rows = []
    for j in range(nj):
        idx = [j * HA + h for h in range(HA)]
        res = [_dot(lhs[i], s_ref[h]) for h, i in enumerate(idx)]
        for h, i in enumerate(idx):
            s_ref[h] = res[h][0:DK_A, :] + kwu[i][:, DK_A:]
        o_rows.append(jnp.concatenate(
            [res[h][DK_A:, :] + qwu[i][:, DK_A:] for h, i in enumerate(idx)], axis=-1))

    o_ref[...] = _gdn_out(jnp.concatenate(o_rows, axis=0), pa_ref[:, A_QKV:A_MAIN],
                          anw_ref[...]).astype(o_ref.dtype)
    s_out_ref[...] = s_ref[...]


def _gdn_prompt(pa, bd, conv_w, hp, anw):
    b, t, _ = pa.shape
    blk = lambda i, j: (i, j, 0)
    fixed = lambda i, j: (0, 0)
    return pl.pallas_call(
        _gdn_prompt_kernel,
        grid=(b, t // TC),
        in_specs=[pl.BlockSpec((None, TC, A_MAIN), blk),
                  pl.BlockSpec((None, TC, BD_W), blk),
                  pl.BlockSpec((CONV_W, A_QKV), fixed),
                  pl.BlockSpec((8, BD_W), fixed),
                  pl.BlockSpec((1, MIX_A), fixed)],
        out_specs=[pl.BlockSpec((None, TC, MIX_A), blk),
                   pl.BlockSpec((None, HA, DK_A, DK_A), lambda i, j: (i, 0, 0, 0)),
                   pl.BlockSpec((None, CONV_W - 1, A_QKV), lambda i, j: (i, 0, 0))],
        out_shape=[jax.ShapeDtypeStruct((b, t, MIX_A), BF16),
                   jax.ShapeDtypeStruct((b, HA, DK_A, DK_A), F32),
                   jax.ShapeDtypeStruct((b, CONV_W - 1, A_QKV), F32)],
        scratch_shapes=[pltpu.VMEM((TC + 8, A_QKV), F32),
                        pltpu.VMEM((HA, DK_A, DK_A), F32)],
        compiler_params=pltpu.CompilerParams(dimension_semantics=("arbitrary", "arbitrary"),
                                             vmem_limit_bytes=VMEM_LIMIT),
        name="gdn_prompt",
    )(pa, bd, conv_w, hp, anw)


def _rwkv_front(xb, w0, w2, a0, a2, k_k, k_a, ones_blk):
    r = xb[:, 0:MIX_B]
    kr = xb[:, MIX_B:2 * MIX_B]
    vr = xb[:, 2 * MIX_B:3 * MIX_B]
    gate = xb[:, 3 * MIX_B:4 * MIX_B]
    wd = xb[:, 4 * MIX_B:4 * MIX_B + LORA]
    ad = xb[:, 4 * MIX_B + LORA:]
    w_ll = -_softplus(-(w0 + _dot(jnp.tanh(wd), w2))) - 0.5
    logw = -jnp.exp(w_ll)
    a = _sigmoid(a0 + _dot(ad, a2))
    kraw = kr * k_k
    kk = kraw * lax.rsqrt(_group_sum(kraw * kraw, ones_blk) + 1e-12)
    kmod = kr * (1.0 + (a - 1.0) * k_a)
    return r, kmod, vr, gate, logw, a, kk


def _rwkv_back(y, r, kmod, v, gate, rk, lnw, lnb, ones_blk):
    inv = 1.0 / HS_B
    yc = y - _group_sum(y, ones_blk) * inv
    var = _group_sum(yc * yc, ones_blk) * inv
    gn = yc * lax.rsqrt(var + GN_EPS) * lnw + lnb
    bonus = _group_sum(r * kmod * rk, ones_blk) * v
    return (gn + bonus) * _silu(gate)


def _rwkv_prompt_kernel(pb_ref, mu_ref, w0_ref, w2_ref, a0_ref, a2_ref, kk_ref, ka_ref,
                        rk_ref, lnw_ref, lnb_ref, ones_ref,
                        o_ref, s_out_ref, sh_out_ref, carry_ref, s_ref):
    c = pl.program_id(1)

    @pl.when(c == 0)
    def _():
        carry_ref[...] = jnp.zeros_like(carry_ref)
        s_ref[...] = jnp.zeros_like(s_ref)

    ones_blk = ones_ref[...]
    pb = pb_ref[...]
    rowi = _iota2((TC, 1), 0)
    prev = jnp.where(rowi == 0, carry_ref[0:1, :], pltpu.roll(pb, 1, 0))
    carry_ref[0:1, :] = pb[TC - 1:TC, :]
    sh_out_ref[...] = pb[TC - 1:TC, :]
    xb = pb + (prev - pb) * mu_ref[...]
    r, kmod, v, gate, logw, a, kk = _rwkv_front(
        xb, w0_ref[...], w2_ref[...], a0_ref[...], a2_ref[...], kk_ref[...], ka_ref[...], ones_blk)

    g = _dot(_chunk_tril(TC), logw, prec=HIGHEST)
    eg = jnp.exp(g)
    egn = jnp.exp(-g)
    ag = -kk * jnp.exp(g - logw)
    kka = kk * a
    bg = kka * egn
    kg = kmod * egn
    rg = r * eg

    rowp = _iota2((CHUNK, PAIR), 0)
    lane = _iota2((CHUNK, PAIR), 1)
    colp = jnp.where(lane >= HS_B, lane - HS_B, lane)
    left = lane < HS_B
    incl = colp <= rowp
    strict = colp < rowp
    eye = colp == rowp
    zero = jnp.zeros((CHUNK, PAIR), F32)
    pack = lambda f: jnp.where(left, f[0:HS_B, :], f[HS_B:, :])

    nj = TC // CHUNK
    npair = HB // 2
    jp = [(j, p) for j in range(nj) for p in range(npair)]
    rows_of = lambda j: slice(j * CHUNK, (j + 1) * CHUNK)
    lanes_of = lambda p: slice(p * PAIR, (p + 1) * PAIR)
    glast = [g[(j + 1) * CHUNK - 1:(j + 1) * CHUNK, :] for j in range(nj)]
    edec = [jnp.exp(glast[j] - g[rows_of(j), :]) for j in range(nj)]
    bdec = [kka[rows_of(j), :] * edec[j] for j in range(nj)]
    kdec = [kmod[rows_of(j), :] * edec[j] for j in range(nj)]
    eglast = [jnp.exp(glast[j]) for j in range(nj)]

    cut = lambda z: [z[rows_of(j), lanes_of(p)] for j, p in jp]
    ag_p, rg_p, bg_p, kg_p, v_p = cut(ag), cut(rg), cut(bg), cut(kg), cut(v)
    bdec_p = [bdec[j][:, lanes_of(p)] for j, p in jp]
    kdec_p = [kdec[j][:, lanes_of(p)] for j, p in jp]
    eglast_p = [eglast[j][:, lanes_of(p)] for j, p in jp]
    amat = [_dot(jnp.concatenate([a, b], axis=0),
                 jnp.concatenate([_pair_diag(c_), _pair_diag(d)], axis=0), NT)
            for a, b, c_, d in zip(ag_p, rg_p, bg_p, kg_p)]
    a_ab = [jnp.where(strict, m[0:CHUNK, 0:PAIR], 0.0) for m in amat]
    a_ak = [jnp.where(strict, m[0:CHUNK, PAIR:], 0.0) for m in amat]
    a_rb = [jnp.where(incl, m[CHUNK:, 0:PAIR], 0.0) for m in amat]
    a_rk = [jnp.where(incl, m[CHUNK:, PAIR:], 0.0) for m in amat]
    tinv = _tri_inv_pairs([-a for a in a_ab])
    kv = [_dot(jnp.concatenate([a, b], axis=0), _pair_diag(c_)) for a, b, c_ in zip(a_ak, a_rk, v_p)]
    wu = [_dot(t, jnp.concatenate([_pair_diag(a), _pair_diag(x[0:CHUNK, :])], axis=1))
          for t, a, x in zip(tinv, ag_p, kv)]
    rwu = [_dot(a, jnp.concatenate([_pair_diag(x[:, 0:PAIR]), _pair_diag(x[:, PAIR:])], axis=1))
           for a, x in zip(a_rb, wu)]
    mn = [_dot(jnp.concatenate([a, b], axis=0),
               jnp.concatenate([x, jnp.concatenate([zero, c_], axis=1)], axis=0), TN)
          for a, b, x, c_ in zip(bdec_p, kdec_p, wu, v_p)]
    lhs = [jnp.concatenate([jnp.where(eye, e, 0.0) + pack(m[:, 0:PAIR]), a + x[:, 0:PAIR]], axis=0)
           for e, m, a, x in zip(eglast_p, mn, rg_p, rwu)]

    y_rows = []
    for j in range(nj):
        idx = [j * npair + p for p in range(npair)]
        res = [_dot(lhs[i], _pair_diag(s_ref[p])) for p, i in enumerate(idx)]
        for p, i in enumerate(idx):
            s_ref[p] = res[p][0:HS_B, :] + pack(mn[i][:, PAIR:])
        y_rows.append(jnp.concatenate(
            [res[p][HS_B:, :] + rwu[i][:, PAIR:] + kv[i][CHUNK:, :] for p, i in enumerate(idx)],
            axis=-1))

    o_ref[...] = _rwkv_back(jnp.concatenate(y_rows, axis=0), r, kmod, v, gate,
                            rk_ref[...], lnw_ref[...], lnb_ref[...], ones_blk).astype(o_ref.dtype)
    for p in range(npair):
        st = s_ref[p].T
        s_out_ref[2 * p] = st[0:HS_B, :]
        s_out_ref[2 * p + 1] = st[HS_B:, :]


def _rwkv_prompt(pb, mu, w0, w2, a0, a2, k_k, k_a, rk, lnw, lnb, ones_blk):
    b, t, _ = pb.shape
    blk = lambda i, j: (i, j, 0)
    fixed = lambda i, j: (0, 0)
    vec = pl.BlockSpec((1, MIX_B), fixed)
    return pl.pallas_call(
        _rwkv_prompt_kernel,
        grid=(b, t // TC),
        in_specs=[pl.BlockSpec((None, TC, B_W), blk),
                  pl.BlockSpec((1, B_W), fixed),
                  vec, pl.BlockSpec((LORA, MIX_B), fixed),
                  vec, pl.BlockSpec((LORA, MIX_B), fixed),
                  vec, vec, vec, vec, vec,
                  pl.BlockSpec((MIX_B, MIX_B), fixed)],
        out_specs=[pl.BlockSpec((None, TC, MIX_B), blk),
                   pl.BlockSpec((None, HB, HS_B, HS_B), lambda i, j: (i, 0, 0, 0)),
                   pl.BlockSpec((None, 1, B_W), lambda i, j: (i, 0, 0))],
        out_shape=[jax.ShapeDtypeStruct((b, t, MIX_B), BF16),
                   jax.ShapeDtypeStruct((b, HB, HS_B, HS_B), F32),
                   jax.ShapeDtypeStruct((b, 1, B_W), F32)],
        scratch_shapes=[pltpu.VMEM((8, B_W), F32),
                        pltpu.VMEM((HB // 2, HS_B, PAIR), F32)],
        compiler_params=pltpu.CompilerParams(dimension_semantics=("arbitrary", "arbitrary"),
                                             vmem_limit_bytes=VMEM_LIMIT),
        name="rwkv_prompt",
    )(pb, mu, w0, w2, a0, a2, k_k, k_a, rk, lnw, lnb, ones_blk)


def _pad_rows(rows, width):
    return jnp.concatenate(rows + [jnp.zeros((8 - len(rows), width), F32)], axis=0)


def _gdn_step_kernel(pa_ref, bd_ref, sc_ref, s_ref, cw_ref, hp_ref, anw_ref,
                     o_ref, s_out_ref, c_out_ref):
    u = pa_ref[:, 0:A_QKV]
    cw = cw_ref[...]
    conv = (sc_ref[0] * cw[0:1, :] + sc_ref[1] * cw[1:2, :] + sc_ref[2] * cw[2:3, :] + u * cw[3:4, :])
    qkv = _silu(conv)
    c_out_ref[0] = sc_ref[1]
    c_out_ref[1] = sc_ref[2]
    c_out_ref[2] = u

    qn, kn = _gdn_qk(qkv)
    beta_all, g_all = _gdn_gates(bd_ref[...], hp_ref[...])
    eg_all = jnp.exp(g_all)
    lanes_of = lambda h: slice(h * DK_A, (h + 1) * DK_A)
    q = [qn[:, lanes_of(h)] for h in range(HA)]
    k = [kn[:, lanes_of(h)] for h in range(HA)]
    v = [qkv[:, 2 * MIX_A + h * DK_A:2 * MIX_A + (h + 1) * DK_A] for h in range(HA)]
    beta = [beta_all[:, h:h + 1] for h in range(HA)]
    eg = [eg_all[:, HA + h:HA + h + 1] for h in range(HA)]

    bh = [(b, h) for b in range(SB) for h in range(HA)]
    s0 = [s_ref[b, h] for b, h in bh]
    ks_qs = [_dot(_pad_rows([k[h][b:b + 1, :], q[h][b:b + 1, :]], DK_A), s) for (b, h), s in zip(bh, s0)]
    o_heads = []
    for h in range(HA):
        ks = jnp.concatenate([ks_qs[b * HA + h][0:1, :] for b in range(SB)], axis=0)
        qs = jnp.concatenate([ks_qs[b * HA + h][1:2, :] for b in range(SB)], axis=0)
        u2 = beta[h] * (v[h] - eg[h] * ks)
        for b in range(SB):
            s_out_ref[b, h] = (eg[h][b:b + 1, :] * s0[b * HA + h]
                               + _dot(_pad_rows([k[h][b:b + 1, :]], DK_A),
                                      _pad_rows([u2[b:b + 1, :]], DK_A), TN))
        o_heads.append(eg[h] * qs + jnp.sum(q[h] * k[h], axis=-1, keepdims=True) * u2)

    o_ref[...] = _gdn_out(jnp.concatenate(o_heads, axis=-1), pa_ref[:, A_QKV:A_MAIN], anw_ref[...])


def _gdn_step(pa, bd, sconv_t, s_a, conv_w, hp, anw):
    n = pa.shape[0]
    row = lambda i: (i, 0)
    fixed = lambda i: (0, 0)
    return pl.pallas_call(
        _gdn_step_kernel,
        grid=(n // SB,),
        in_specs=[pl.BlockSpec((SB, A_MAIN), row),
                  pl.BlockSpec((SB, BD_W), row),
                  pl.BlockSpec((CONV_W - 1, SB, A_QKV), lambda i: (0, i, 0)),
                  pl.BlockSpec((SB, HA, DK_A, DK_A), lambda i: (i, 0, 0, 0)),
                  pl.BlockSpec((CONV_W, A_QKV), fixed),
                  pl.BlockSpec((8, BD_W), fixed),
                  pl.BlockSpec((1, MIX_A), fixed)],
        out_specs=[pl.BlockSpec((SB, MIX_A), row),
                   pl.BlockSpec((SB, HA, DK_A, DK_A), lambda i: (i, 0, 0, 0)),
                   pl.BlockSpec((CONV_W - 1, SB, A_QKV), lambda i: (0, i, 0))],
        out_shape=[jax.ShapeDtypeStruct((n, MIX_A), F32),
                   jax.ShapeDtypeStruct((n, HA, DK_A, DK_A), F32),
                   jax.ShapeDtypeStruct((CONV_W - 1, n, A_QKV), F32)],
        compiler_params=pltpu.CompilerParams(dimension_semantics=("arbitrary",),
                                             vmem_limit_bytes=VMEM_LIMIT),
        name="gdn_step",
    )(pa, bd, sconv_t, s_a, conv_w, hp, anw)


def _rwkv_step_lanes_kernel(pb_ref, sh_ref, s_ref, mu_ref, w0_ref, w2_ref, a0_ref, a2_ref, kk_ref, ka_ref,
                            rk_ref, lnw_ref, lnb_ref, ones_ref,
                            o_ref, s_out_ref, vec_scr, row_scr, y_scr):
    h = pl.program_id(0)
    ones_blk = ones_ref[...]

    @pl.when(h == 0)
    def _():
        pb = pb_ref[...]
        xb = pb + (sh_ref[...] - pb) * mu_ref[...]
        r, kmod, v, gate, logw, a, kk = _rwkv_front(
            xb, w0_ref[...], w2_ref[...], a0_ref[...], a2_ref[...], kk_ref[...], ka_ref[...], ones_blk)
        for slot, z in enumerate((-kk, kk * a, kmod, v, r, jnp.exp(logw))):
            vec_scr[slot] = z.T
        for slot, z in enumerate((r, kmod, v, gate)):
            row_scr[slot] = z

    base = pl.multiple_of(h * HS_B, HS_B)
    chan = pl.ds(base, HS_B)
    nkk = vec_scr[0, chan, :]
    kka = vec_scr[1, chan, :]
    kmod_t = vec_scr[2, chan, :]
    r_t = vec_scr[4, chan, :]
    wdec = vec_scr[5, chan, :]

    def body(vi, carry):
        s0 = s_ref[0, vi]
        sa = jnp.sum(s0 * nkk, axis=0, keepdims=True)
        s1 = s0 * wdec + sa * kka + vec_scr[3, pl.ds(base + vi, 1), :] * kmod_t
        s_out_ref[0, vi] = s1
        y_scr[pl.ds(base + vi, 1), :] = jnp.sum(s1 * r_t, axis=0, keepdims=True)
        return carry

    lax.fori_loop(0, HS_B, body, 0, unroll=4)

    @pl.when(h == HB - 1)
    def _():
        o_ref[...] = _rwkv_back(y_scr[...].T, row_scr[0], row_scr[1], row_scr[2], row_scr[3],
                                rk_ref[...], lnw_ref[...], lnb_ref[...], ones_blk)


def _rwkv_step_lanes(pb, shift, s_hvkb, mu, w0, w2, a0, a2, k_k, k_a, rk, lnw, lnb, ones_blk):
    n = pb.shape[0]
    fixed = lambda i: (0, 0)
    vec = pl.BlockSpec((1, MIX_B), fixed)
    state = pl.BlockSpec((1, HS_B, HS_B, n), lambda i: (i, 0, 0, 0))
    return pl.pallas_call(
        _rwkv_step_lanes_kernel,
        grid=(HB,),
        in_specs=[pl.BlockSpec((n, B_W), fixed),
                  pl.BlockSpec((n, B_W), fixed),
                  state,
                  pl.BlockSpec((1, B_W), fixed),
                  vec, pl.BlockSpec((LORA, MIX_B), fixed),
                  vec, pl.BlockSpec((LORA, MIX_B), fixed),
                  vec, vec, vec, vec, vec,
                  pl.BlockSpec((MIX_B, MIX_B), fixed)],
        out_specs=[pl.BlockSpec((n, MIX_B), fixed), state],
        out_shape=[jax.ShapeDtypeStruct((n, MIX_B), F32),
                   jax.ShapeDtypeStruct((HB, HS_B, HS_B, n), F32)],
        scratch_shapes=[pltpu.VMEM((6, MIX_B, n), F32),
                        pltpu.VMEM((4, n, MIX_B), F32),
                        pltpu.VMEM((MIX_B, n), F32)],
        compiler_params=pltpu.CompilerParams(dimension_semantics=("arbitrary",),
                                             vmem_limit_bytes=VMEM_LIMIT),
        name="rwkv_step",
    )(pb, shift, s_hvkb, mu, w0, w2, a0, a2, k_k, k_a, rk, lnw, lnb, ones_blk)


def _block_ones(width, group):
    idx = jnp.arange(width) // group
    return (idx[:, None] == idx[None, :]).astype(BF16)


def kernel(x_prompt, x_sample, state_a_mat, state_a_conv, state_b_mat, state_b_shift, norm_w, w_in,
           conv_w, a_log, dt_bias, a_norm_w, mu, w0, w2, a0, a2, k_k, k_a, r_k, ln_w, ln_b, w_out,
           final_norm_w):
    bsz, seq, _ = x_prompt.shape
    nsmp = x_sample.shape[0]
    a_w = A_MAIN + 2 * HA

    w = w_in[0]
    w_parts = (w[:, :A_MAIN].astype(BF16),
               jnp.concatenate([w[:, a_w:], w[:, A_MAIN:a_w],
                                jnp.zeros((D_MODEL, BD_W - 2 * HA), F32)], axis=1).astype(BF16))
    w_o = w_out[0].astype(BF16)
    hp = jnp.zeros((8, BD_W), F32)
    hp = hp.at[0, HA:2 * HA].set(a_log[0]).at[1, HA:2 * HA].set(dt_bias[0])
    nw = norm_w[0].reshape(1, D_MODEL)
    fw = final_norm_w.reshape(1, D_MODEL)
    anw = jnp.tile(a_norm_w[0].reshape(1, DK_A), (1, HA))
    ones_b = _block_ones(MIX_B, HS_B)
    row = lambda z: z[0].reshape(1, -1)
    b_params = (row(mu), row(w0), w2[0], row(a0), a2[0], row(k_k), row(k_a), row(r_k), row(ln_w), row(ln_b),
                ones_b)

    xp = x_prompt.reshape(bsz * seq, D_MODEL)
    pa, pb, bd = _inproj(xp, nw, *w_parts)
    oa, p_amat, p_aconv = _gdn_prompt(pa.reshape(bsz, seq, A_MAIN), bd.reshape(bsz, seq, BD_W),
                                      conv_w[0], hp, anw)
    ob, p_bmat, p_bshift = _rwkv_prompt(pb.reshape(bsz, seq, B_W), *b_params)
    y_prompt = _outproj(xp, oa.reshape(bsz * seq, MIX_A), ob.reshape(bsz * seq, MIX_B), w_o, fw)

    xs = x_sample.reshape(nsmp, D_MODEL)
    sa_, sb_, sbd = _inproj(xs, nw, *w_parts)
    sconv_t = jnp.swapaxes(state_a_conv[0], 0, 1)
    soa, s_amat, s_aconv_t = _gdn_step(sa_, sbd, sconv_t, state_a_mat[0], conv_w[0], hp, anw)
    sob, s_bmat_t = _rwkv_step_lanes(sb_, state_b_shift[0], jnp.transpose(state_b_mat[0], (1, 2, 3, 0)),
                                     *b_params)
    s_bmat = jnp.transpose(s_bmat_t, (3, 0, 1, 2))
    y_sample = _outproj(xs, soa, sob, w_o, fw)

    return (y_prompt.reshape(bsz, seq, D_MODEL), y_sample.reshape(nsmp, 1, D_MODEL),
            p_amat[None], p_aconv[None], p_bmat[None], p_bshift.reshape(1, bsz, B_W),
            s_amat[None], jnp.swapaxes(s_aconv_t, 0, 1)[None], s_bmat[None], sb_[None])
```

```python
import jax
import jax.numpy as jnp
from jax import lax
from jax.experimental import pallas as pl
from jax.experimental.pallas import tpu as pltpu

F32 = jnp.float32
BF16 = jnp.bfloat16

D_MODEL = 1024
MIX_A = 512
MIX_B = 512
HA = 4
DK_A = 128
HB = 8
HS_B = 64
PAIR = 2 * HS_B
CONV_W = 4
LORA = 64
A_QKV = 3 * MIX_A
A_MAIN = A_QKV + MIX_A
B_W = 4 * MIX_B + 2 * LORA
BD_W = 128
RMS_EPS = 1e-6
GN_EPS = 64e-5
CHUNK = 64
TC = 256
TC_A = 256
SB = 16
VMEM_LIMIT = 56 * 1024 * 1024

HIGHEST = lax.Precision.HIGHEST
NN = (((1,), (0,)), ((), ()))
NT = (((1,), (1,)), ((), ()))
TN = (((0,), (0,)), ((), ()))


def _dot(a, b, dn=NN, prec=None):
    return lax.dot_general(a, b, dn, precision=prec, preferred_element_type=F32)


def _sigmoid(x):
    return 0.5 * jnp.tanh(0.5 * x) + 0.5


def _silu(x):
    h = 0.5 * x
    return h * jnp.tanh(h) + h


def _softplus(x):
    return jnp.maximum(x, 0.0) + jnp.log(1.0 + jnp.exp(-jnp.abs(x)))


def _l2norm(x):
    return x * lax.rsqrt(jnp.sum(x * x, axis=-1, keepdims=True) + 1e-12)


def _group_sum(x, ones_blk):
    return _dot(x.astype(BF16), ones_blk)


def _iota2(shape, dim):
    return lax.broadcasted_iota(jnp.int32, shape, dim)


def _chunk_tril(n):
    r = _iota2((n, n), 0)
    c = _iota2((n, n), 1)
    same = (r // CHUNK) == (c // CHUNK)
    return jnp.where(same & (c <= r), 1.0, 0.0).astype(F32)


def _tri_inv_many(lows):
    n = lows[0].shape[0]
    r = _iota2((n, n), 0)
    c = _iota2((n, n), 1)
    eye = jnp.where(r == c, 1.0, 0.0).astype(F32)
    base = (r // 8) == (c // 8)
    l0 = [jnp.where(base, low, 0.0) for low in lows]
    l2 = [_dot(a, a) for a in l0]
    l4 = [_dot(a, a) for a in l2]
    d = [_dot(eye - a, eye + b) for a, b in zip(l0, l2)]
    d = [_dot(a, eye + b) for a, b in zip(d, l4)]
    s = 8
    while s < n:
        sel = ((r // (2 * s)) == (c // (2 * s))) & ((r // s) != (c // s))
        t = [_dot(jnp.where(sel, low, 0.0), a) for low, a in zip(lows, d)]
        d = [a - _dot(a, b) for a, b in zip(d, t)]
        s *= 2
    return d


def _pair_diag(x):
    left = _iota2(x.shape, 1) < x.shape[1] // 2
    return jnp.concatenate([jnp.where(left, x, 0.0), jnp.where(left, 0.0, x)], axis=0)


def _tri_inv_pairs(lows):
    n = lows[0].shape[0]
    r = _iota2((n, 2 * n), 0)
    lane = _iota2((n, 2 * n), 1)
    c = jnp.where(lane >= n, lane - n, lane)
    mul = lambda a, b: _dot(a, _pair_diag(b))
    eye = jnp.where(r == c, 1.0, 0.0).astype(F32)
    base = (r // 8) == (c // 8)
    l0 = [jnp.where(base, low, 0.0) for low in lows]
    l2 = [mul(a, a) for a in l0]
    l4 = [mul(a, a) for a in l2]
    d = [mul(eye - a, eye + b) for a, b in zip(l0, l2)]
    d = [mul(a, eye + b) for a, b in zip(d, l4)]
    s = 8
    while s < n:
        sel = ((r // (2 * s)) == (c // (2 * s))) & ((r // s) != (c // s))
        t = [mul(jnp.where(sel, low, 0.0), a) for low, a in zip(lows, d)]
        d = [a - mul(a, b) for a, b in zip(d, t)]
        s *= 2
    return d


def _inproj_kernel(x_ref, nw_ref, wa_ref, wb_ref, oa_ref, ob_ref, obd_ref):
    x = x_ref[...]
    xn = (x * lax.rsqrt(jnp.mean(x * x, axis=-1, keepdims=True) + RMS_EPS) * nw_ref[...]).astype(BF16)
    oa_ref[...] = _dot(xn, wa_ref[...])
    pb = _dot(xn, wb_ref[...])
    ob_ref[...] = pb[:, :B_W]
    obd_ref[...] = pb[:, B_W:]


def _inproj(x2d, norm_w, w_a, w_b):
    n = x2d.shape[0]
    tm = min(512, n)
    row = lambda i: (i, 0)
    fixed = lambda i: (0, 0)
    return pl.pallas_call(
        _inproj_kernel,
        grid=(n // tm,),
        in_specs=[pl.BlockSpec((tm, D_MODEL), row),
                  pl.BlockSpec((1, D_MODEL), fixed),
                  pl.BlockSpec((D_MODEL, A_MAIN), fixed),
                  pl.BlockSpec((D_MODEL, B_W + BD_W), fixed)],
        out_specs=[pl.BlockSpec((tm, A_MAIN), row),
                   pl.BlockSpec((tm, B_W), row),
                   pl.BlockSpec((tm, BD_W), row)],
        out_shape=[jax.ShapeDtypeStruct((n, A_MAIN), F32),
                   jax.ShapeDtypeStruct((n, B_W), F32),
                   jax.ShapeDtypeStruct((n, BD_W), F32)],
        compiler_params=pltpu.CompilerParams(dimension_semantics=("arbitrary",),
                                             vmem_limit_bytes=VMEM_LIMIT),
        name="inproj",
    )(x2d, norm_w, w_a, w_b)


def _outproj_kernel(x_ref, ma_ref, mb_ref, w_ref, fw_ref, y_ref):
    h = (x_ref[...]
         + _dot(ma_ref[...].astype(BF16), w_ref[0:MIX_A, :])
         + _dot(mb_ref[...].astype(BF16), w_ref[MIX_A:, :]))
    y_ref[...] = h * lax.rsqrt(jnp.mean(h * h, axis=-1, keepdims=True) + RMS_EPS) * fw_ref[...]


def _outproj(x2d, mix_a, mix_b, w_out, final_w):
    n = x2d.shape[0]
    tm = min(1024, n)
    row = lambda i: (i, 0)
    fixed = lambda i: (0, 0)
    return pl.pallas_call(
        _outproj_kernel,
        grid=(n // tm,),
        in_specs=[pl.BlockSpec((tm, D_MODEL), row),
                  pl.BlockSpec((tm, MIX_A), row),
                  pl.BlockSpec((tm, MIX_B), row),
                  pl.BlockSpec((D_MODEL, D_MODEL), fixed),
                  pl.BlockSpec((1, D_MODEL), fixed)],
        out_specs=pl.BlockSpec((tm, D_MODEL), row),
        out_shape=jax.ShapeDtypeStruct((n, D_MODEL), F32),
        compiler_params=pltpu.CompilerParams(dimension_semantics=("arbitrary",),
                                             vmem_limit_bytes=VMEM_LIMIT),
        name="outproj",
    )(x2d, mix_a, mix_b, w_out, final_w)


def _gdn_gates(bd, hp):
    beta = _sigmoid(bd)
    g = -jnp.exp(hp[0:1, :]) * _softplus(bd + hp[1:2, :])
    return beta, g


def _gdn_qk(qkv):
    heads = lambda z: [z[:, h * DK_A:(h + 1) * DK_A] for h in range(HA)]
    qn = jnp.concatenate([_l2norm(x) * (DK_A ** -0.5) for x in heads(qkv[:, 0:MIX_A])], axis=-1)
    kn = jnp.concatenate([_l2norm(x) for x in heads(qkv[:, MIX_A:2 * MIX_A])], axis=-1)
    return qn, kn


def _gdn_out(o, gate, anw):
    on = jnp.concatenate(
        [x * lax.rsqrt(jnp.mean(x * x, axis=-1, keepdims=True) + RMS_EPS)
         for x in [o[:, h * DK_A:(h + 1) * DK_A] for h in range(HA)]], axis=-1)
    return on * anw * _silu(gate)


def _gdn_prompt_kernel(pa_ref, bd_ref, cw_ref, hp_ref, anw_ref,
                       o_ref, s_out_ref, c_out_ref, ext_ref, s_ref):
    c = pl.program_id(1)

    @pl.when(c == 0)
    def _():
        ext_ref[0:8, :] = jnp.zeros((8, A_QKV), F32)
        s_ref[...] = jnp.zeros_like(s_ref)

    tc = pa_ref.shape[0]
    ext_ref[8:8 + tc, :] = pa_ref[:, 0:A_QKV]
    cw = cw_ref[...]
    conv = (ext_ref[5:5 + tc, :] * cw[0:1, :] + ext_ref[6:6 + tc, :] * cw[1:2, :]
            + ext_ref[7:7 + tc, :] * cw[2:3, :] + ext_ref[8:8 + tc, :] * cw[3:4, :])
    qkv = _silu(conv)
    c_out_ref[...] = ext_ref[tc + 5:tc + 8, :]
    ext_ref[0:8, :] = ext_ref[tc:tc + 8, :]

    qn, kn = _gdn_qk(qkv)
    beta_all, g_all = _gdn_gates(bd_ref[...], hp_ref[...])
    gcum = _dot(_chunk_tril(tc), g_all, prec=HIGHEST)
    gcum_t = gcum.T

    r = _iota2((CHUNK, CHUNK), 0)
    cc = _iota2((CHUNK, CHUNK), 1)
    incl = cc <= r
    strict = cc < r

    nj = tc // CHUNK
    jh = [(j, h) for j in range(nj) for h in range(HA)]
    rows_of = lambda j: slice(j * CHUNK, (j + 1) * CHUNK)
    lanes_of = lambda h: slice(h * DK_A, (h + 1) * DK_A)
    q = [qn[rows_of(j), lanes_of(h)] for j, h in jh]
    k = [kn[rows_of(j), lanes_of(h)] for j, h in jh]
    v = [qkv[rows_of(j), 2 * MIX_A + h * DK_A:2 * MIX_A + (h + 1) * DK_A] for j, h in jh]
    beta = [beta_all[rows_of(j), h:h + 1] for j, h in jh]
    gc = [gcum[rows_of(j), HA + h:HA + h + 1] for j, h in jh]
    gr = [gcum_t[HA + h:HA + h + 1, rows_of(j)] for j, h in jh]
    glast = [gcum[(j + 1) * CHUNK - 1:(j + 1) * CHUNK, HA + h:HA + h + 1] for j, h in jh]
    decay = [jnp.where(incl, jnp.exp(jnp.where(incl, a - b, 0.0)), 0.0) for a, b in zip(gc, gr)]
    kq = [_dot(jnp.concatenate([a, b], axis=0), a, NT) for a, b in zip(k, q)]
    low = [jnp.where(strict, b * d * m[0:CHUNK, :], 0.0) for b, d, m in zip(beta, decay, kq)]
    tinv = _tri_inv_many(low)
    eg = [jnp.exp(a) for a in gc]
    wu = [_dot(t, jnp.concatenate([(b * e) * a, b * c_], axis=1))
          for t, b, e, a, c_ in zip(tinv, beta, eg, k, v)]
    kwu = [_dot(a * jnp.exp(gl - g), x, TN) for a, gl, g, x in zip(k, glast, gc, wu)]
    qwu = [_dot(d * m[CHUNK:, :], x) for d, m, x in zip(decay, kq, wu)]
    rd = _iota2((DK_A, DK_A), 0)
    cd = _iota2((DK_A, DK_A), 1)
    lhs = [jnp.concatenate([jnp.where(rd == cd, jnp.exp(gl), 0.0) - kx[:, 0:DK_A],
                            a * e - qx[:, 0:DK_A]], axis=0)
           for gl, kx, a, e, qx in zip(glast, kwu, q, eg, qwu)]

    o_rows = []
    for j in range(nj):
        idx = [j * HA + h for h in range(HA)]
        res = [_dot(lhs[i], s_ref[h]) for h, i in enumerate(idx)]
        for h, i in enumerate(idx):
            s_ref[h] = res[h][0:DK_A, :] + kwu[i][:, DK_A:]
        o_rows.append(jnp.concatenate(
            [res[h][DK_A:, :] + qwu[i][:, DK_A:] for h, i in enumerate(idx)], axis=-1))

    o_ref[...] = _gdn_out(jnp.concatenate(o_rows, axis=0), pa_ref[:, A_QKV:A_MAIN],
                          anw_ref[...]).astype(o_ref.dtype)
    s_out_ref[...] = s_ref[...]


def _gdn_prompt(pa, bd, conv_w, hp, anw):
    b, t, _ = pa.shape
    blk = lambda i, j: (i, j, 0)
    fixed = lambda i, j: (0, 0)
    return pl.pallas_call(
        _gdn_prompt_kernel,
        grid=(b, t // TC_A),
        in_specs=[pl.BlockSpec((None, TC_A, A_MAIN), blk),
                  pl.BlockSpec((None, TC_A, BD_W), blk),
                  pl.BlockSpec((CONV_W, A_QKV), fixed),
                  pl.BlockSpec((8, BD_W), fixed),
                  pl.BlockSpec((1, MIX_A), fixed)],
        out_specs=[pl.BlockSpec((None, TC_A, MIX_A), blk),
                   pl.BlockSpec((None, HA, DK_A, DK_A), lambda i, j: (i, 0, 0, 0)),
                   pl.BlockSpec((None, CONV_W - 1, A_QKV), lambda i, j: (i, 0, 0))],
        out_shape=[jax.ShapeDtypeStruct((b, t, MIX_A), BF16),
                   jax.ShapeDtypeStruct((b, HA, DK_A, DK_A), F32),
                   jax.ShapeDtypeStruct((b, CONV_W - 1, A_QKV), F32)],
        scratch_shapes=[pltpu.VMEM((TC_A + 8, A_QKV), F32),
                        pltpu.VMEM((HA, DK_A, DK_A), F32)],
        compiler_params=pltpu.CompilerParams(dimension_semantics=("arbitrary", "arbitrary"),
                                             vmem_limit_bytes=VMEM_LIMIT),
        name="gdn_prompt",
    )(pa, bd, conv_w, hp, anw)


def _rwkv_front(xb, w0, w2, a0, a2, k_k, k_a, ones_blk):
    r = xb[:, 0:MIX_B]
    kr = xb[:, MIX_B:2 * MIX_B]
    vr = xb[:, 2 * MIX_B:3 * MIX_B]
    gate = xb[:, 3 * MIX_B:4 * MIX_B]
    wd = xb[:, 4 * MIX_B:4 * MIX_B + LORA]
    ad = xb[:, 4 * MIX_B + LORA:]
    w_ll = -_softplus(-(w0 + _dot(jnp.tanh(wd), w2))) - 0.5
    logw = -jnp.exp(w_ll)
    a = _sigmoid(a0 + _dot(ad, a2))
    kraw = kr * k_k
    kk = kraw * lax.rsqrt(_group_sum(kraw * kraw, ones_blk) + 1e-12)
    kmod = kr * (1.0 + (a - 1.0) * k_a)
    return r, kmod, vr, gate, logw, a, kk


def _rwkv_back(y, r, kmod, v, gate, rk, lnw, lnb, ones_blk):
    inv = 1.0 / HS_B
    yc = y - _group_sum(y, ones_blk) * inv
    var = _group_sum(yc * yc, ones_blk) * inv
    gn = yc * lax.rsqrt(var + GN_EPS) * lnw + lnb
    bonus = _group_sum(r * kmod * rk, ones_blk) * v
    return (gn + bonus) * _silu(gate)


def _rwkv_prompt_kernel(pb_ref, mu_ref, w0_ref, w2_ref, a0_ref, a2_ref, kk_ref, ka_ref,
                        rk_ref, lnw_ref, lnb_ref, ones_ref,
                        o_ref, s_out_ref, sh_out_ref, carry_ref, s_ref):
    c = pl.program_id(1)

    @pl.when(c == 0)
    def _():
        carry_ref[...] = jnp.zeros_like(carry_ref)
        s_ref[...] = jnp.zeros_like(s_ref)

    ones_blk = ones_ref[...]
    pb = pb_ref[...]
    rowi = _iota2((TC, 1), 0)
    prev = jnp.where(rowi == 0, carry_ref[0:1, :], pltpu.roll(pb, 1, 0))
    carry_ref[0:1, :] = pb[TC - 1:TC, :]
    sh_out_ref[...] = pb[TC - 1:TC, :]
    xb = pb + (prev - pb) * mu_ref[...]
    r, kmod, v, gate, logw, a, kk = _rwkv_front(
        xb, w0_ref[...], w2_ref[...], a0_ref[...], a2_ref[...], kk_ref[...], ka_ref[...], ones_blk)

    g = _dot(_chunk_tril(TC), logw, prec=HIGHEST)
    eg = jnp.exp(g)
    egn = jnp.exp(-g)
    ag = -kk * jnp.exp(g - logw)
    kka = kk * a
    bg = kka * egn
    kg = kmod * egn
    rg = r * eg

    rowp = _iota2((CHUNK, PAIR), 0)
    lane = _iota2((CHUNK, PAIR), 1)
    colp = jnp.where(lane >= HS_B, lane - HS_B, lane)
    left = lane < HS_B
    incl = colp <= rowp
    strict = colp < rowp
    eye = colp == rowp
    zero = jnp.zeros((CHUNK, PAIR), F32)
    pack = lambda f: jnp.where(left, f[0:HS_B, :], f[HS_B:, :])

    nj = TC // CHUNK
    npair = HB // 2
    jp = [(j, p) for j in range(nj) for p in range(npair)]
    rows_of = lambda j: slice(j * CHUNK, (j + 1) * CHUNK)
    lanes_of = lambda p: slice(p * PAIR, (p + 1) * PAIR)
    glast = [g[(j + 1) * CHUNK - 1:(j + 1) * CHUNK, :] for j in range(nj)]
    edec = [jnp.exp(glast[j] - g[rows_of(j), :]) for j in range(nj)]
    bdec = [kka[rows_of(j), :] * edec[j] for j in range(nj)]
    kdec = [kmod[rows_of(j), :] * edec[j] for j in range(nj)]
    eglast = [jnp.exp(glast[j]) for j in range(nj)]

    cut = lambda z: [z[rows_of(j), lanes_of(p)] for j, p in jp]
    ag_p, rg_p, bg_p, kg_p, v_p = cut(ag), cut(rg), cut(bg), cut(kg), cut(v)
    bdec_p = [bdec[j][:, lanes_of(p)] for j, p in jp]
    kdec_p = [kdec[j][:, lanes_of(p)] for j, p in jp]
    eglast_p = [eglast[j][:, lanes_of(p)] for j, p in jp]
    amat = [_dot(jnp.concatenate([a, b], axis=0),
                 jnp.concatenate([_pair_diag(c_), _pair_diag(d)], axis=0), NT)
            for a, b, c_, d in zip(ag_p, rg_p, bg_p, kg_p)]
    a_ab = [jnp.where(strict, m[0:CHUNK, 0:PAIR], 0.0) for m in amat]
    a_ak = [jnp.where(strict, m[0:CHUNK, PAIR:], 0.0) for m in amat]
    a_rb = [jnp.where(incl, m[CHUNK:, 0:PAIR], 0.0) for m in amat]
    a_rk = [jnp.where(incl, m[CHUNK:, PAIR:], 0.0) for m in amat]
    tinv = _tri_inv_pairs([-a for a in a_ab])
    kv = [_dot(jnp.concatenate([a, b], axis=0), _pair_diag(c_)) for a, b, c_ in zip(a_ak, a_rk, v_p)]
    wu = [_dot(t, jnp.concatenate([_pair_diag(a), _pair_diag(x[0:CHUNK, :])], axis=1))
          for t, a, x in zip(tinv, ag_p, kv)]
    rwu = [_dot(a, jnp.concatenate([_pair_diag(x[:, 0:PAIR]), _pair_diag(x[:, PAIR:])], axis=1))
           for a, x in zip(a_rb, wu)]
    mn = [_dot(jnp.concatenate([a, b], axis=0),
               jnp.concatenate([x, jnp.concatenate([zero, c_], axis=1)], axis=0), TN)
          for a, b, x, c_ in zip(bdec_p, kdec_p, wu, v_p)]
    lhs = [jnp.concatenate([jnp.where(eye, e, 0.0) + pack(m[:, 0:PAIR]), a + x[:, 0:PAIR]], axis=0)
           for e, m, a, x in zip(eglast_p, mn, rg_p, rwu)]

    y_rows = []
    for j in range(nj):
        idx = [j * npair + p for p in range(npair)]
        res = [_dot(lhs[i], _pair_diag(s_ref[p])) for p, i in enumerate(idx)]
        for p, i in enumerate(idx):
            s_ref[p] = res[p][0:HS_B, :] + pack(mn[i][:, PAIR:])
        y_rows.append(jnp.concatenate(
            [res[p][HS_B:, :] + rwu[i][:, PAIR:] + kv[i][CHUNK:, :] for p, i in enumerate(idx)],
            axis=-1))

    o_ref[...] = _rwkv_back(jnp.concatenate(y_rows, axis=0), r, kmod, v, gate,
                            rk_ref[...], lnw_ref[...], lnb_ref[...], ones_blk).astype(o_ref.dtype)
    for p in range(npair):
        st = s_ref[p].T
        s_out_ref[2 * p] = st[0:HS_B, :]
        s_out_ref[2 * p + 1] = st[HS_B:, :]


def _rwkv_prompt(pb, mu, w0, w2, a0, a2, k_k, k_a, rk, lnw, lnb, ones_blk):
    b, t, _ = pb.shape
    blk = lambda i, j: (i, j, 0)
    fixed = lambda i, j: (0, 0)
    vec = pl.BlockSpec((1, MIX_B), fixed)
    return pl.pallas_call(
        _rwkv_prompt_kernel,
        grid=(b, t // TC),
        in_specs=[pl.BlockSpec((None, TC, B_W), blk),
                  pl.BlockSpec((1, B_W), fixed),
                  vec, pl.BlockSpec((LORA, MIX_B), fixed),
                  vec, pl.BlockSpec((LORA, MIX_B), fixed),
                  vec, vec, vec, vec, vec,
                  pl.BlockSpec((MIX_B, MIX_B), fixed)],
        out_specs=[pl.BlockSpec((None, TC, MIX_B), blk),
                   pl.BlockSpec((None, HB, HS_B, HS_B), lambda i, j: (i, 0, 0, 0)),
                   pl.BlockSpec((None, 1, B_W), lambda i, j: (i, 0, 0))],
        out_shape=[jax.ShapeDtypeStruct((b, t, MIX_B), BF16),
                   jax.ShapeDtypeStruct((b, HB, HS_B, HS_B), F32),
                   jax.ShapeDtypeStruct((b, 1, B_W), F32)],
        scratch_shapes=[pltpu.VMEM((8, B_W), F32),
                        pltpu.VMEM((HB // 2, HS_B, PAIR), F32)],
        compiler_params=pltpu.CompilerParams(dimension_semantics=("arbitrary", "arbitrary"),
                                             vmem_limit_bytes=VMEM_LIMIT),
        name="rwkv_prompt",
    )(pb, mu, w0, w2, a0, a2, k_k, k_a, rk, lnw, lnb, ones_blk)


def _pad_rows(rows, width):
    return jnp.concatenate(rows + [jnp.zeros((8 - len(rows), width), F32)], axis=0)


def _gdn_step_kernel(pa_ref, bd_ref, sc_ref, s_ref, cw_ref, hp_ref, anw_ref,
                     o_ref, s_out_ref, c_out_ref):
    u = pa_ref[:, 0:A_QKV]
    cw = cw_ref[...]
    conv = (sc_ref[0] * cw[0:1, :] + sc_ref[1] * cw[1:2, :] + sc_ref[2] * cw[2:3, :] + u * cw[3:4, :])
    qkv = _silu(conv)
    c_out_ref[0] = sc_ref[1]
    c_out_ref[1] = sc_ref[2]
    c_out_ref[2] = u

    qn, kn = _gdn_qk(qkv)
    beta_all, g_all = _gdn_gates(bd_ref[...], hp_ref[...])
    eg_all = jnp.exp(g_all)
    lanes_of = lambda h: slice(h * DK_A, (h + 1) * DK_A)
    q = [qn[:, lanes_of(h)] for h in range(HA)]
    k = [kn[:, lanes_of(h)] for h in range(HA)]
    v = [qkv[:, 2 * MIX_A + h * DK_A:2 * MIX_A + (h + 1) * DK_A] for h in range(HA)]
    beta = [beta_all[:, h:h + 1] for h in range(HA)]
    eg = [eg_all[:, HA + h:HA + h + 1] for h in range(HA)]

    bh = [(b, h) for b in range(SB) for h in range(HA)]
    s0 = [s_ref[b, h] for b, h in bh]
    ks_qs = [_dot(_pad_rows([k[h][b:b + 1, :], q[h][b:b + 1, :]], DK_A), s) for (b, h), s in zip(bh, s0)]
    o_heads = []
    for h in range(HA):
        ks = jnp.concatenate([ks_qs[b * HA + h][0:1, :] for b in range(SB)], axis=0)
        qs = jnp.concatenate([ks_qs[b * HA + h][1:2, :] for b in range(SB)], axis=0)
        u2 = beta[h] * (v[h] - eg[h] * ks)
        for b in range(SB):
            s_out_ref[b, h] = (eg[h][b:b + 1, :] * s0[b * HA + h]
                               + _dot(_pad_rows([k[h][b:b + 1, :]], DK_A),
                                      _pad_rows([u2[b:b + 1, :]], DK_A), TN))
        o_heads.append(eg[h] * qs + jnp.sum(q[h] * k[h], axis=-1, keepdims=True) * u2)

    o_ref[...] = _gdn_out(jnp.concatenate(o_heads, axis=-1), pa_ref[:, A_QKV:A_MAIN], anw_ref[...])


def _gdn_step(pa, bd, sconv_t, s_a, conv_w, hp, anw):
    n = pa.shape[0]
    row = lambda i: (i, 0)
    fixed = lambda i: (0, 0)
    return pl.pallas_call(
        _gdn_step_kernel,
        grid=(n // SB,),
        in_specs=[pl.BlockSpec((SB, A_MAIN), row),
                  pl.BlockSpec((SB, BD_W), row),
                  pl.BlockSpec((CONV_W - 1, SB, A_QKV), lambda i: (0, i, 0)),
                  pl.BlockSpec((SB, HA, DK_A, DK_A), lambda i: (i, 0, 0, 0)),
                  pl.BlockSpec((CONV_W, A_QKV), fixed),
                  pl.BlockSpec((8, BD_W), fixed),
                  pl.BlockSpec((1, MIX_A), fixed)],
        out_specs=[pl.BlockSpec((SB, MIX_A), row),
                   pl.BlockSpec((SB, HA, DK_A, DK_A), lambda i: (i, 0, 0, 0)),
                   pl.BlockSpec((CONV_W - 1, SB, A_QKV), lambda i: (0, i, 0))],
        out_shape=[jax.ShapeDtypeStruct((n, MIX_A), F32),
                   jax.ShapeDtypeStruct((n, HA, DK_A, DK_A), F32),
                   jax.ShapeDtypeStruct((CONV_W - 1, n, A_QKV), F32)],
        compiler_params=pltpu.CompilerParams(dimension_semantics=("arbitrary",),
                                             vmem_limit_bytes=VMEM_LIMIT),
        name="gdn_step",
    )(pa, bd, sconv_t, s_a, conv_w, hp, anw)


def _rwkv_step_lanes_kernel(pb_ref, sh_ref, s_ref, mu_ref, w0_ref, w2_ref, a0_ref, a2_ref, kk_ref, ka_ref,
                            rk_ref, lnw_ref, lnb_ref, ones_ref,
                            o_ref, s_out_ref, vec_scr, row_scr, y_scr):
    h = pl.program_id(0)
    ones_blk = ones_ref[...]

    @pl.when(h == 0)
    def _():
        pb = pb_ref[...]
        xb = pb + (sh_ref[...] - pb) * mu_ref[...]
        r, kmod, v, gate, logw, a, kk = _rwkv_front(
            xb, w0_ref[...], w2_ref[...], a0_ref[...], a2_ref[...], kk_ref[...], ka_ref[...], ones_blk)
        for slot, z in enumerate((-kk, kk * a, kmod, v, r, jnp.exp(logw))):
            vec_scr[slot] = z.T
        for slot, z in enumerate((r, kmod, v, gate)):
            row_scr[slot] = z

    base = pl.multiple_of(h * HS_B, HS_B)
    chan = pl.ds(base, HS_B)
    nkk = vec_scr[0, chan, :]
    kka = vec_scr[1, chan, :]
    kmod_t = vec_scr[2, chan, :]
    r_t = vec_scr[4, chan, :]
    wdec = vec_scr[5, chan, :]

    def body(vi, carry):
        s0 = s_ref[0, vi]
        sa = jnp.sum(s0 * nkk, axis=0, keepdims=True)
        s1 = s0 * wdec + sa * kka + vec_scr[3, pl.ds(base + vi, 1), :] * kmod_t
        s_out_ref[0, vi] = s1
        y_scr[pl.ds(base + vi, 1), :] = jnp.sum(s1 * r_t, axis=0, keepdims=True)
        return carry

    lax.fori_loop(0, HS_B, body, 0, unroll=4)

    @pl.when(h == HB - 1)
    def _():
        o_ref[...] = _rwkv_back(y_scr[...].T, row_scr[0], row_scr[1], row_scr[2], row_scr[3],
                                rk_ref[...], lnw_ref[...], lnb_ref[...], ones_blk)


def _rwkv_step_lanes(pb, shift, s_hvkb, mu, w0, w2, a0, a2, k_k, k_a, rk, lnw, lnb, ones_blk):
    n = pb.shape[0]
    fixed = lambda i: (0, 0)
    vec = pl.BlockSpec((1, MIX_B), fixed)
    state = pl.BlockSpec((1, HS_B, HS_B, n), lambda i: (i, 0, 0, 0))
    return pl.pallas_call(
        _rwkv_step_lanes_kernel,
        grid=(HB,),
        in_specs=[pl.BlockSpec((n, B_W), fixed),
                  pl.BlockSpec((n, B_W), fixed),
                  state,
                  pl.BlockSpec((1, B_W), fixed),
                  vec, pl.BlockSpec((LORA, MIX_B), fixed),
                  vec, pl.BlockSpec((LORA, MIX_B), fixed),
                  vec, vec, vec, vec, vec,
                  pl.BlockSpec((MIX_B, MIX_B), fixed)],
        out_specs=[pl.BlockSpec((n, MIX_B), fixed), state],
        out_shape=[jax.ShapeDtypeStruct((n, MIX_B), F32),
                   jax.ShapeDtypeStruct((HB, HS_B, HS_B, n), F32)],
        scratch_shapes=[pltpu.VMEM((6, MIX_B, n), F32),
                        pltpu.VMEM((4, n, MIX_B), F32),
                        pltpu.VMEM((MIX_B, n), F32)],
        compiler_params=pltpu.CompilerParams(dimension_semantics=("arbitrary",),
                                             vmem_limit_bytes=VMEM_LIMIT),
        name="rwkv_step",
    )(pb, shift, s_hvkb, mu, w0, w2, a0, a2, k_k, k_a, rk, lnw, lnb, ones_blk)


def _block_ones(width, group):
    idx = jnp.arange(width) // group
    return (idx[:, None] == idx[None, :]).astype(BF16)


def kernel(x_prompt, x_sample, state_a_mat, state_a_conv, state_b_mat, state_b_shift, norm_w, w_in,
           conv_w, a_log, dt_bias, a_norm_w, mu, w0, w2, a0, a2, k_k, k_a, r_k, ln_w, ln_b, w_out,
           final_norm_w):
    bsz, seq, _ = x_prompt.shape
    nsmp = x_sample.shape[0]
    a_w = A_MAIN + 2 * HA

    w = w_in[0]
    w_parts = (w[:, :A_MAIN].astype(BF16),
               jnp.concatenate([w[:, a_w:], w[:, A_MAIN:a_w],
                                jnp.zeros((D_MODEL, BD_W - 2 * HA), F32)], axis=1).astype(BF16))
    w_o = w_out[0].astype(BF16)
    hp = jnp.zeros((8, BD_W), F32)
    hp = hp.at[0, HA:2 * HA].set(a_log[0]).at[1, HA:2 * HA].set(dt_bias[0])
    nw = norm_w[0].reshape(1, D_MODEL)
    fw = final_norm_w.reshape(1, D_MODEL)
    anw = jnp.tile(a_norm_w[0].reshape(1, DK_A), (1, HA))
    ones_b = _block_ones(MIX_B, HS_B)
    row = lambda z: z[0].reshape(1, -1)
    b_params = (row(mu), row(w0), w2[0], row(a0), a2[0], row(k_k), row(k_a), row(r_k), row(ln_w), row(ln_b),
                ones_b)

    xp = x_prompt.reshape(bsz * seq, D_MODEL)
    pa, pb, bd = _inproj(xp, nw, *w_parts)
    oa, p_amat, p_aconv = _gdn_prompt(pa.reshape(bsz, seq, A_MAIN), bd.reshape(bsz, seq, BD_W),
                                      conv_w[0], hp, anw)
    ob, p_bmat, p_bshift = _rwkv_prompt(pb.reshape(bsz, seq, B_W), *b_params)
    y_prompt = _outproj(xp, oa.reshape(bsz * seq, MIX_A), ob.reshape(bsz * seq, MIX_B), w_o, fw)

    xs = x_sample.reshape(nsmp, D_MODEL)
    sa_, sb_, sbd = _inproj(xs, nw, *w_parts)
    sconv_t = jnp.swapaxes(state_a_conv[0], 0, 1)
    soa, s_amat, s_aconv_t = _gdn_step(sa_, sbd, sconv_t, state_a_mat[0], conv_w[0], hp, anw)
    sob, s_bmat_t = _rwkv_step_lanes(sb_, state_b_shift[0], jnp.transpose(state_b_mat[0], (1, 2, 3, 0)),
                                     *b_params)
    s_bmat = jnp.transpose(s_bmat_t, (3, 0, 1, 2))
    y_sample = _outproj(xs, soa, sob, w_o, fw)

    return (y_prompt.reshape(bsz, seq, D_MODEL), y_sample.reshape(nsmp, 1, D_MODEL),
            p_amat[None], p_aconv[None], p_bmat[None], p_bshift.reshape(1, bsz, B_W),
            s_amat[None], jnp.swapaxes(s_aconv_t, 0, 1)[None], s_bmat[None], sb_[None])
```

```python
import jax
import jax.numpy as jnp
from jax import lax
from jax.experimental import pallas as pl
from jax.experimental.pallas import tpu as pltpu

F32 = jnp.float32
BF16 = jnp.bfloat16

D_MODEL = 1024
MIX_A = 512
MIX_B = 512
HA = 4
DK_A = 128
HB = 8
HS_B = 64
PAIR = 2 * HS_B
CONV_W = 4
LORA = 64
A_QKV = 3 * MIX_A
A_MAIN = A_QKV + MIX_A
B_W = 4 * MIX_B + 2 * LORA
BD_W = 128
RMS_EPS = 1e-6
GN_EPS = 64e-5
CHUNK = 64
TC = 256
TC_A = 256
SB = 16
VMEM_LIMIT = 56 * 1024 * 1024

HIGHEST = lax.Precision.HIGHEST
NN = (((1,), (0,)), ((), ()))
NT = (((1,), (1,)), ((), ()))
TN = (((0,), (0,)), ((), ()))


def _dot(a, b, dn=NN, prec=None):
    return lax.dot_general(a, b, dn, precision=prec, preferred_element_type=F32)


def _sigmoid(x):
    return 0.5 * jnp.tanh(0.5 * x) + 0.5


def _silu(x):
    h = 0.5 * x
    return h * jnp.tanh(h) + h


def _softplus(x):
    return jnp.maximum(x, 0.0) + jnp.log(1.0 + jnp.exp(-jnp.abs(x)))


def _l2norm(x):
    return x * lax.rsqrt(jnp.sum(x * x, axis=-1, keepdims=True) + 1e-12)


def _group_sum(x, ones_blk):
    return _dot(x.astype(BF16), ones_blk)


def _iota2(shape, dim):
    return lax.broadcasted_iota(jnp.int32, shape, dim)


def _chunk_tril(n):
    r = _iota2((n, n), 0)
    c = _iota2((n, n), 1)
    same = (r // CHUNK) == (c // CHUNK)
    return jnp.where(same & (c <= r), 1.0, 0.0).astype(F32)


def _tri_inv_many(lows):
    n = lows[0].shape[0]
    r = _iota2((n, n), 0)
    c = _iota2((n, n), 1)
    eye = jnp.where(r == c, 1.0, 0.0).astype(F32)
    base = (r // 8) == (c // 8)
    l0 = [jnp.where(base, low, 0.0) for low in lows]
    l2 = [_dot(a, a) for a in l0]
    l4 = [_dot(a, a) for a in l2]
    d = [_dot(eye - a, eye + b) for a, b in zip(l0, l2)]
    d = [_dot(a, eye + b) for a, b in zip(d, l4)]
    s = 8
    while s < n:
        sel = ((r // (2 * s)) == (c // (2 * s))) & ((r // s) != (c // s))
        t = [_dot(jnp.where(sel, low, 0.0), a) for low, a in zip(lows, d)]
        d = [a - _dot(a, b) for a, b in zip(d, t)]
        s *= 2
    return d


def _pair_diag(x):
    left = _iota2(x.shape, 1) < x.shape[1] // 2
    return jnp.concatenate([jnp.where(left, x, 0.0), jnp.where(left, 0.0, x)], axis=0)


def _tri_inv_pairs(lows):
    n = lows[0].shape[0]
    r = _iota2((n, 2 * n), 0)
    lane = _iota2((n, 2 * n), 1)
    c = jnp.where(lane >= n, lane - n, lane)
    mul = lambda a, b: _dot(a, _pair_diag(b))
    eye = jnp.where(r == c, 1.0, 0.0).astype(F32)
    base = (r // 8) == (c // 8)
    l0 = [jnp.where(base, low, 0.0) for low in lows]
    l2 = [mul(a, a) for a in l0]
    l4 = [mul(a, a) for a in l2]
    d = [mul(eye - a, eye + b) for a, b in zip(l0, l2)]
    d = [mul(a, eye + b) for a, b in zip(d, l4)]
    s = 8
    while s < n:
        sel = ((r // (2 * s)) == (c // (2 * s))) & ((r // s) != (c // s))
        t = [mul(jnp.where(sel, low, 0.0), a) for low, a in zip(lows, d)]
        d = [a - mul(a, b) for a, b in zip(d, t)]
        s *= 2
    return d


def _inproj_kernel(x_ref, nw_ref, wa_ref, wb_ref, oa_ref, ob_ref, obd_ref):
    x = x_ref[...]
    xn = (x * lax.rsqrt(jnp.mean(x * x, axis=-1, keepdims=True) + RMS_EPS) * nw_ref[...]).astype(BF16)
    oa_ref[...] = _dot(xn, wa_ref[...])
    pb = _dot(xn, wb_ref[...])
    ob_ref[...] = pb[:, :B_W]
    obd_ref[...] = pb[:, B_W:]


def _inproj(x2d, norm_w, w_a, w_b):
    n = x2d.shape[0]
    tm = min(512, n)
    row = lambda i: (i, 0)
    fixed = lambda i: (0, 0)
    return pl.pallas_call(
        _inproj_kernel,
        grid=(n // tm,),
        in_specs=[pl.BlockSpec((tm, D_MODEL), row),
                  pl.BlockSpec((1, D_MODEL), fixed),
                  pl.BlockSpec((D_MODEL, A_MAIN), fixed),
                  pl.BlockSpec((D_MODEL, B_W + BD_W), fixed)],
        out_specs=[pl.BlockSpec((tm, A_MAIN), row),
                   pl.BlockSpec((tm, B_W), row),
                   pl.BlockSpec((tm, BD_W), row)],
        out_shape=[jax.ShapeDtypeStruct((n, A_MAIN), F32),
                   jax.ShapeDtypeStruct((n, B_W), F32),
                   jax.ShapeDtypeStruct((n, BD_W), F32)],
        compiler_params=pltpu.CompilerParams(dimension_semantics=("arbitrary",),
                                             vmem_limit_bytes=VMEM_LIMIT),
        name="inproj",
    )(x2d, norm_w, w_a, w_b)


def _outproj_kernel(x_ref, ma_ref, mb_ref, w_ref, fw_ref, y_ref):
    h = (x_ref[...]
         + _dot(ma_ref[...].astype(BF16), w_ref[0:MIX_A, :])
         + _dot(mb_ref[...].astype(BF16), w_ref[MIX_A:, :]))
    y_ref[...] = h * lax.rsqrt(jnp.mean(h * h, axis=-1, keepdims=True) + RMS_EPS) * fw_ref[...]


def _outproj(x2d, mix_a, mix_b, w_out, final_w):
    n = x2d.shape[0]
    tm = min(1024, n)
    row = lambda i: (i, 0)
    fixed = lambda i: (0, 0)
    return pl.pallas_call(
        _outproj_kernel,
        grid=(n // tm,),
        in_specs=[pl.BlockSpec((tm, D_MODEL), row),
                  pl.BlockSpec((tm, MIX_A), row),
                  pl.BlockSpec((tm, MIX_B), row),
                  pl.BlockSpec((D_MODEL, D_MODEL), fixed),
                  pl.BlockSpec((1, D_MODEL), fixed)],
        out_specs=pl.BlockSpec((tm, D_MODEL), row),
        out_shape=jax.ShapeDtypeStruct((n, D_MODEL), F32),
        compiler_params=pltpu.CompilerParams(dimension_semantics=("arbitrary",),
                                             vmem_limit_bytes=VMEM_LIMIT),
        name="outproj",
    )(x2d, mix_a, mix_b, w_out, final_w)


def _gdn_gates(bd, hp):
    beta = _sigmoid(bd)
    g = -jnp.exp(hp[0:1, :]) * _softplus(bd + hp[1:2, :])
    return beta, g


def _gdn_qk(qkv):
    heads = lambda z: [z[:, h * DK_A:(h + 1) * DK_A] for h in range(HA)]
    qn = jnp.concatenate([_l2norm(x) * (DK_A ** -0.5) for x in heads(qkv[:, 0:MIX_A])], axis=-1)
    kn = jnp.concatenate([_l2norm(x) for x in heads(qkv[:, MIX_A:2 * MIX_A])], axis=-1)
    return qn, kn


def _gdn_out(o, gate, anw):
    on = jnp.concatenate(
        [x * lax.rsqrt(jnp.mean(x * x, axis=-1, keepdims=True) + RMS_EPS)
         for x in [o[:, h * DK_A:(h + 1) * DK_A] for h in range(HA)]], axis=-1)
    return on * anw * _silu(gate)


def _gdn_prompt_kernel(pa_ref, bd_ref, cw_ref, hp_ref, anw_ref, x_ref, mb_ref, wo_ref, fw_ref,
                       y_ref, s_out_ref, c_out_ref, ext_ref, s_ref):
    c = pl.program_id(1)

    @pl.when(c == 0)
    def _():
        ext_ref[0:8, :] = jnp.zeros((8, A_QKV), F32)
        s_ref[...] = jnp.zeros_like(s_ref)


    tc = pa_ref.shape[0]
    ext_ref[8:8 + tc, :] = pa_ref[:, 0:A_QKV]
    cw = cw_ref[...]
    conv = (ext_ref[5:5 + tc, :] * cw[0:1, :] + ext_ref[6:6 + tc, :] * cw[1:2, :]
            + ext_ref[7:7 + tc, :] * cw[2:3, :] + ext_ref[8:8 + tc, :] * cw[3:4, :])
    qkv = _silu(conv)
    c_out_ref[...] = ext_ref[tc + 5:tc + 8, :]
    ext_ref[0:8, :] = ext_ref[tc:tc + 8, :]

    qn, kn = _gdn_qk(qkv)
    beta_all, g_all = _gdn_gates(bd_ref[...], hp_ref[...])
    gcum = _dot(_chunk_tril(tc), g_all, prec=HIGHEST)
    gcum_t = gcum.T

    r = _iota2((CHUNK, CHUNK), 0)
    cc = _iota2((CHUNK, CHUNK), 1)
    incl = cc <= r
    strict = cc < r

    nj = tc // CHUNK
    jh = [(j, h) for j in range(nj) for h in range(HA)]
    rows_of = lambda j: slice(j * CHUNK, (j + 1) * CHUNK)
    lanes_of = lambda h: slice(h * DK_A, (h + 1) * DK_A)
    q = [qn[rows_of(j), lanes_of(h)] for j, h in jh]
    k = [kn[rows_of(j), lanes_of(h)] for j, h in jh]
    v = [qkv[rows_of(j), 2 * MIX_A + h * DK_A:2 * MIX_A + (h + 1) * DK_A] for j, h in jh]
    beta = [beta_all[rows_of(j), h:h + 1] for j, h in jh]
    gc = [gcum[rows_of(j), HA + h:HA + h + 1] for j, h in jh]
    gr = [gcum_t[HA + h:HA + h + 1, rows_of(j)] for j, h in jh]
    glast = [gcum[(j + 1) * CHUNK - 1:(j + 1) * CHUNK, HA + h:HA + h + 1] for j, h in jh]
    decay = [jnp.where(incl, jnp.exp(jnp.where(incl, a - b, 0.0)), 0.0) for a, b in zip(gc, gr)]
    kq = [_dot(jnp.concatenate([a, b], axis=0), a, NT) for a, b in zip(k, q)]
    low = [jnp.where(strict, b * d * m[0:CHUNK, :], 0.0) for b, d, m in zip(beta, decay, kq)]
    tinv = _tri_inv_many(low)
    eg = [jnp.exp(a) for a in gc]
    wu = [_dot(t, jnp.concatenate([(b * e) * a, b * c_], axis=1))
          for t, b, e, a, c_ in zip(tinv, beta, eg, k, v)]
    kwu = [_dot(a * jnp.exp(gl - g), x, TN) for a, gl, g, x in zip(k, glast, gc, wu)]
    qwu = [_dot(d * m[CHUNK:, :], x) for d, m, x in zip(decay, kq, wu)]
    rd = _iota2((DK_A, DK_A), 0)
    cd = _iota2((DK_A, DK_A), 1)
    lhs = [jnp.concatenate([jnp.where(rd == cd, jnp.exp(gl), 0.0) - kx[:, 0:DK_A],
                            a * e - qx[:, 0:DK_A]], axis=0)
           for gl, kx, a, e, qx in zip(glast, kwu, q, eg, qwu)]

    o_rows = []
    for j in range(nj):
        idx = [j * HA + h for h in range(HA)]
        res = [_dot(lhs[i], s_ref[h]) for h, i in enumerate(idx)]
        for h, i in enumerate(idx):
            s_ref[h] = res[h][0:DK_A, :] + kwu[i][:, DK_A:]
        o_rows.append(jnp.concatenate(
            [res[h][DK_A:, :] + qwu[i][:, DK_A:] for h, i in enumerate(idx)], axis=-1))

    mix_a = _gdn_out(jnp.concatenate(o_rows, axis=0), pa_ref[:, A_QKV:A_MAIN], anw_ref[...])
    h = (x_ref[...] + _dot(mb_ref[...], wo_ref[MIX_A:, :])
         + _dot(mix_a.astype(BF16), wo_ref[0:MIX_A, :]))
    y_ref[...] = h * lax.rsqrt(jnp.mean(h * h, axis=-1, keepdims=True) + RMS_EPS) * fw_ref[...]
    s_out_ref[...] = s_ref[...]


def _gdn_prompt(pa, bd, conv_w, hp, anw, x, mix_b, w_out, final_w):
    b, t, _ = pa.shape
    blk = lambda i, j: (i, j, 0)
    fixed = lambda i, j: (0, 0)
    return pl.pallas_call(
        _gdn_prompt_kernel,
        grid=(b, t // TC_A),
        in_specs=[pl.BlockSpec((None, TC_A, A_MAIN), blk),
                  pl.BlockSpec((None, TC_A, BD_W), blk),
                  pl.BlockSpec((CONV_W, A_QKV), fixed),
                  pl.BlockSpec((8, BD_W), fixed),
                  pl.BlockSpec((1, MIX_A), fixed),
                  pl.BlockSpec((None, TC_A, D_MODEL), blk),
                  pl.BlockSpec((None, TC_A, MIX_B), blk),
                  pl.BlockSpec((D_MODEL, D_MODEL), fixed),
                  pl.BlockSpec((1, D_MODEL), fixed)],
        out_specs=[pl.BlockSpec((None, TC_A, D_MODEL), blk),
                   pl.BlockSpec((None, HA, DK_A, DK_A), lambda i, j: (i, 0, 0, 0)),
                   pl.BlockSpec((None, CONV_W - 1, A_QKV), lambda i, j: (i, 0, 0))],
        out_shape=[jax.ShapeDtypeStruct((b, t, D_MODEL), F32),
                   jax.ShapeDtypeStruct((b, HA, DK_A, DK_A), F32),
                   jax.ShapeDtypeStruct((b, CONV_W - 1, A_QKV), F32)],
        scratch_shapes=[pltpu.VMEM((TC_A + 8, A_QKV), F32),
                        pltpu.VMEM((HA, DK_A, DK_A), F32)],
        compiler_params=pltpu.CompilerParams(dimension_semantics=("arbitrary", "arbitrary"),
                                             vmem_limit_bytes=VMEM_LIMIT),
        name="gdn_prompt",
    )(pa, bd, conv_w, hp, anw, x, mix_b, w_out, final_w)


def _rwkv_front(xb, w0, w2, a0, a2, k_k, k_a, ones_blk):
    r = xb[:, 0:MIX_B]
    kr = xb[:, MIX_B:2 * MIX_B]
    vr = xb[:, 2 * MIX_B:3 * MIX_B]
    gate = xb[:, 3 * MIX_B:4 * MIX_B]
    wd = xb[:, 4 * MIX_B:4 * MIX_B + LORA]
    ad = xb[:, 4 * MIX_B + LORA:]
    w_ll = -_softplus(-(w0 + _dot(jnp.tanh(wd), w2))) - 0.5
    logw = -jnp.exp(w_ll)
    a = _sigmoid(a0 + _dot(ad, a2))
    kraw = kr * k_k
    kk = kraw * lax.rsqrt(_group_sum(kraw * kraw, ones_blk) + 1e-12)
    kmod = kr * (1.0 + (a - 1.0) * k_a)
    return r, kmod, vr, gate, logw, a, kk


def _rwkv_back(y, r, kmod, v, gate, rk, lnw, lnb, ones_blk):
    inv = 1.0 / HS_B
    yc = y - _group_sum(y, ones_blk) * inv
    var = _group_sum(yc * yc, ones_blk) * inv
    gn = yc * lax.rsqrt(var + GN_EPS) * lnw + lnb
    bonus = _group_sum(r * kmod * rk, ones_blk) * v
    return (gn + bonus) * _silu(gate)


def _rwkv_prompt_kernel(pb_ref, mu_ref, w0_ref, w2_ref, a0_ref, a2_ref, kk_ref, ka_ref,
                        rk_ref, lnw_ref, lnb_ref, ones_ref,
                        o_ref, s_out_ref, sh_out_ref, carry_ref, s_ref):
    c = pl.program_id(1)

    @pl.when(c == 0)
    def _():
        carry_ref[...] = jnp.zeros_like(carry_ref)
        s_ref[...] = jnp.zeros_like(s_ref)

    ones_blk = ones_ref[...]
    pb = pb_ref[...]
    rowi = _iota2((TC, 1), 0)
    prev = jnp.where(rowi == 0, carry_ref[0:1, :], pltpu.roll(pb, 1, 0))
    carry_ref[0:1, :] = pb[TC - 1:TC, :]
    sh_out_ref[...] = pb[TC - 1:TC, :]
    xb = pb + (prev - pb) * mu_ref[...]
    r, kmod, v, gate, logw, a, kk = _rwkv_front(
        xb, w0_ref[...], w2_ref[...], a0_ref[...], a2_ref[...], kk_ref[...], ka_ref[...], ones_blk)

    g = _dot(_chunk_tril(TC), logw, prec=HIGHEST)
    eg = jnp.exp(g)
    egn = jnp.exp(-g)
    ag = -kk * jnp.exp(g - logw)
    kka = kk * a
    bg = kka * egn
    kg = kmod * egn
    rg = r * eg

    rowp = _iota2((CHUNK, PAIR), 0)
    lane = _iota2((CHUNK, PAIR), 1)
    colp = jnp.where(lane >= HS_B, lane - HS_B, lane)
    left = lane < HS_B
    incl = colp <= rowp
    strict = colp < rowp
    eye = colp == rowp
    zero = jnp.zeros((CHUNK, PAIR), F32)
    pack = lambda f: jnp.where(left, f[0:HS_B, :], f[HS_B:, :])

    nj = TC // CHUNK
    npair = HB // 2
    jp = [(j, p) for j in range(nj) for p in range(npair)]
    rows_of = lambda j: slice(j * CHUNK, (j + 1) * CHUNK)
    lanes_of = lambda p: slice(p * PAIR, (p + 1) * PAIR)
    glast = [g[(j + 1) * CHUNK - 1:(j + 1) * CHUNK, :] for j in range(nj)]
    edec = [jnp.exp(glast[j] - g[rows_of(j), :]) for j in range(nj)]
    bdec = [kka[rows_of(j), :] * edec[j] for j in range(nj)]
    kdec = [kmod[rows_of(j), :] * edec[j] for j in range(nj)]
    eglast = [jnp.exp(glast[j]) for j in range(nj)]

    cut = lambda z: [z[rows_of(j), lanes_of(p)] for j, p in jp]
    ag_p, rg_p, bg_p, kg_p, v_p = cut(ag), cut(rg), cut(bg), cut(kg), cut(v)
    bdec_p = [bdec[j][:, lanes_of(p)] for j, p in jp]
    kdec_p = [kdec[j][:, lanes_of(p)] for j, p in jp]
    eglast_p = [eglast[j][:, lanes_of(p)] for j, p in jp]
    amat = [_dot(jnp.concatenate([a, b], axis=0),
                 jnp.concatenate([_pair_diag(c_), _pair_diag(d)], axis=0), NT)
            for a, b, c_, d in zip(ag_p, rg_p, bg_p, kg_p)]
    a_ab = [jnp.where(strict, m[0:CHUNK, 0:PAIR], 0.0) for m in amat]
    a_ak = [jnp.where(strict, m[0:CHUNK, PAIR:], 0.0) for m in amat]
    a_rb = [jnp.where(incl, m[CHUNK:, 0:PAIR], 0.0) for m in amat]
    a_rk = [jnp.where(incl, m[CHUNK:, PAIR:], 0.0) for m in amat]
    tinv = _tri_inv_pairs([-a for a in a_ab])
    kv = [_dot(jnp.concatenate([a, b], axis=0), _pair_diag(c_)) for a, b, c_ in zip(a_ak, a_rk, v_p)]
    wu = [_dot(t, jnp.concatenate([_pair_diag(a), _pair_diag(x[0:CHUNK, :])], axis=1))
          for t, a, x in zip(tinv, ag_p, kv)]
    rwu = [_dot(a, jnp.concatenate([_pair_diag(x[:, 0:PAIR]), _pair_diag(x[:, PAIR:])], axis=1))
           for a, x in zip(a_rb, wu)]
    mn = [_dot(jnp.concatenate([a, b], axis=0),
               jnp.concatenate([x, jnp.concatenate([zero, c_], axis=1)], axis=0), TN)
          for a, b, x, c_ in zip(bdec_p, kdec_p, wu, v_p)]
    lhs = [jnp.concatenate([jnp.where(eye, e, 0.0) + pack(m[:, 0:PAIR]), a + x[:, 0:PAIR]], axis=0)
           for e, m, a, x in zip(eglast_p, mn, rg_p, rwu)]

    y_rows = []
    for j in range(nj):
        idx = [j * npair + p for p in range(npair)]
        res = [_dot(lhs[i], _pair_diag(s_ref[p])) for p, i in enumerate(idx)]
        for p, i in enumerate(idx):
            s_ref[p] = res[p][0:HS_B, :] + pack(mn[i][:, PAIR:])
        y_rows.append(jnp.concatenate(
            [res[p][HS_B:, :] + rwu[i][:, PAIR:] + kv[i][CHUNK:, :] for p, i in enumerate(idx)],
            axis=-1))

    o_ref[...] = _rwkv_back(jnp.concatenate(y_rows, axis=0), r, kmod, v, gate,
                            rk_ref[...], lnw_ref[...], lnb_ref[...], ones_blk).astype(o_ref.dtype)
    for p in range(npair):
        st = s_ref[p].T
        s_out_ref[2 * p] = st[0:HS_B, :]
        s_out_ref[2 * p + 1] = st[HS_B:, :]


def _rwkv_prompt(pb, mu, w0, w2, a0, a2, k_k, k_a, rk, lnw, lnb, ones_blk):
    b, t, _ = pb.shape
    blk = lambda i, j: (i, j, 0)
    fixed = lambda i, j: (0, 0)
    vec = pl.BlockSpec((1, MIX_B), fixed)
    return pl.pallas_call(
        _rwkv_prompt_kernel,
        grid=(b, t // TC),
        in_specs=[pl.BlockSpec((None, TC, B_W), blk),
                  pl.BlockSpec((1, B_W), fixed),
                  vec, pl.BlockSpec((LORA, MIX_B), fixed),
                  vec, pl.BlockSpec((LORA, MIX_B), fixed),
                  vec, vec, vec, vec, vec,
                  pl.BlockSpec((MIX_B, MIX_B), fixed)],
        out_specs=[pl.BlockSpec((None, TC, MIX_B), blk),
                   pl.BlockSpec((None, HB, HS_B, HS_B), lambda i, j: (i, 0, 0, 0)),
                   pl.BlockSpec((None, 1, B_W), lambda i, j: (i, 0, 0))],
        out_shape=[jax.ShapeDtypeStruct((b, t, MIX_B), BF16),
                   jax.ShapeDtypeStruct((b, HB, HS_B, HS_B), F32),
                   jax.ShapeDtypeStruct((b, 1, B_W), F32)],
        scratch_shapes=[pltpu.VMEM((8, B_W), F32),
                        pltpu.VMEM((HB // 2, HS_B, PAIR), F32)],
        compiler_params=pltpu.CompilerParams(dimension_semantics=("arbitrary", "arbitrary"),
                                             vmem_limit_bytes=VMEM_LIMIT),
        name="rwkv_prompt",
    )(pb, mu, w0, w2, a0, a2, k_k, k_a, rk, lnw, lnb, ones_blk)


def _pad_rows(rows, width):
    return jnp.concatenate(rows + [jnp.zeros((8 - len(rows), width), F32)], axis=0)


def _gdn_step_kernel(pa_ref, bd_ref, sc_ref, s_ref, cw_ref, hp_ref, anw_ref,
                     o_ref, s_out_ref, c_out_ref):
    u = pa_ref[:, 0:A_QKV]
    cw = cw_ref[...]
    conv = (sc_ref[0] * cw[0:1, :] + sc_ref[1] * cw[1:2, :] + sc_ref[2] * cw[2:3, :] + u * cw[3:4, :])
    qkv = _silu(conv)
    c_out_ref[0] = sc_ref[1]
    c_out_ref[1] = sc_ref[2]
    c_out_ref[2] = u

    qn, kn = _gdn_qk(qkv)
    beta_all, g_all = _gdn_gates(bd_ref[...], hp_ref[...])
    eg_all = jnp.exp(g_all)
    lanes_of = lambda h: slice(h * DK_A, (h + 1) * DK_A)
    q = [qn[:, lanes_of(h)] for h in range(HA)]
    k = [kn[:, lanes_of(h)] for h in range(HA)]
    v = [qkv[:, 2 * MIX_A + h * DK_A:2 * MIX_A + (h + 1) * DK_A] for h in range(HA)]
    beta = [beta_all[:, h:h + 1] for h in range(HA)]
    eg = [eg_all[:, HA + h:HA + h + 1] for h in range(HA)]

    bh = [(b, h) for b in range(SB) for h in range(HA)]
    s0 = [s_ref[b, h] for b, h in bh]
    ks_qs = [_dot(_pad_rows([k[h][b:b + 1, :], q[h][b:b + 1, :]], DK_A), s) for (b, h), s in zip(bh, s0)]
    o_heads = []
    for h in range(HA):
        ks = jnp.concatenate([ks_qs[b * HA + h][0:1, :] for b in range(SB)], axis=0)
        qs = jnp.concatenate([ks_qs[b * HA + h][1:2, :] for b in range(SB)], axis=0)
        u2 = beta[h] * (v[h] - eg[h] * ks)
        for b in range(SB):
            s_out_ref[b, h] = (eg[h][b:b + 1, :] * s0[b * HA + h]
                               + _dot(_pad_rows([k[h][b:b + 1, :]], DK_A),
                                      _pad_rows([u2[b:b + 1, :]], DK_A), TN))
        o_heads.append(eg[h] * qs + jnp.sum(q[h] * k[h], axis=-1, keepdims=True) * u2)

    o_ref[...] = _gdn_out(jnp.concatenate(o_heads, axis=-1), pa_ref[:, A_QKV:A_MAIN], anw_ref[...])


def _gdn_step(pa, bd, sconv_t, s_a, conv_w, hp, anw):
    n = pa.shape[0]
    row = lambda i: (i, 0)
    fixed = lambda i: (0, 0)
    return pl.pallas_call(
        _gdn_step_kernel,
        grid=(n // SB,),
        in_specs=[pl.BlockSpec((SB, A_MAIN), row),
                  pl.BlockSpec((SB, BD_W), row),
                  pl.BlockSpec((CONV_W - 1, SB, A_QKV), lambda i: (0, i, 0)),
                  pl.BlockSpec((SB, HA, DK_A, DK_A), lambda i: (i, 0, 0, 0)),
                  pl.BlockSpec((CONV_W, A_QKV), fixed),
                  pl.BlockSpec((8, BD_W), fixed),
                  pl.BlockSpec((1, MIX_A), fixed)],
        out_specs=[pl.BlockSpec((SB, MIX_A), row),
                   pl.BlockSpec((SB, HA, DK_A, DK_A), lambda i: (i, 0, 0, 0)),
                   pl.BlockSpec((CONV_W - 1, SB, A_QKV), lambda i: (0, i, 0))],
        out_shape=[jax.ShapeDtypeStruct((n, MIX_A), F32),
                   jax.ShapeDtypeStruct((n, HA, DK_A, DK_A), F32),
                   jax.ShapeDtypeStruct((CONV_W - 1, n, A_QKV), F32)],
        compiler_params=pltpu.CompilerParams(dimension_semantics=("arbitrary",),
                                             vmem_limit_bytes=VMEM_LIMIT),
        name="gdn_step",
    )(pa, bd, sconv_t, s_a, conv_w, hp, anw)


def _rwkv_step_lanes_kernel(pb_ref, sh_ref, s_ref, mu_ref, w0_ref, w2_ref, a0_ref, a2_ref, kk_ref, ka_ref,
                            rk_ref, lnw_ref, lnb_ref, ones_ref,
                            o_ref, s_out_ref, vec_scr, row_scr, y_scr):
    h = pl.program_id(0)
    ones_blk = ones_ref[...]

    @pl.when(h == 0)
    def _():
        pb = pb_ref[...]
        xb = pb + (sh_ref[...] - pb) * mu_ref[...]
        r, kmod, v, gate, logw, a, kk = _rwkv_front(
            xb, w0_ref[...], w2_ref[...], a0_ref[...], a2_ref[...], kk_ref[...], ka_ref[...], ones_blk)
        for slot, z in enumerate((-kk, kk * a, kmod, v, r, jnp.exp(logw))):
            vec_scr[slot] = z.T
        for slot, z in enumerate((r, kmod, v, gate)):
            row_scr[slot] = z

    base = pl.multiple_of(h * HS_B, HS_B)
    chan = pl.ds(base, HS_B)
    nkk = vec_scr[0, chan, :]
    kka = vec_scr[1, chan, :]
    kmod_t = vec_scr[2, chan, :]
    r_t = vec_scr[4, chan, :]
    wdec = vec_scr[5, chan, :]

    def body(vi, carry):
        s0 = s_ref[0, vi]
        sa = jnp.sum(s0 * nkk, axis=0, keepdims=True)
        s1 = s0 * wdec + sa * kka + vec_scr[3, pl.ds(base + vi, 1), :] * kmod_t
        s_out_ref[0, vi] = s1
        y_scr[pl.ds(base + vi, 1), :] = jnp.sum(s1 * r_t, axis=0, keepdims=True)
        return carry

    lax.fori_loop(0, HS_B, body, 0, unroll=4)

    @pl.when(h == HB - 1)
    def _():
        o_ref[...] = _rwkv_back(y_scr[...].T, row_scr[0], row_scr[1], row_scr[2], row_scr[3],
                                rk_ref[...], lnw_ref[...], lnb_ref[...], ones_blk)


def _rwkv_step_lanes(pb, shift, s_hvkb, mu, w0, w2, a0, a2, k_k, k_a, rk, lnw, lnb, ones_blk):
    n = pb.shape[0]
    fixed = lambda i: (0, 0)
    vec = pl.BlockSpec((1, MIX_B), fixed)
    state = pl.BlockSpec((1, HS_B, HS_B, n), lambda i: (i, 0, 0, 0))
    return pl.pallas_call(
        _rwkv_step_lanes_kernel,
        grid=(HB,),
        in_specs=[pl.BlockSpec((n, B_W), fixed),
                  pl.BlockSpec((n, B_W), fixed),
                  state,
                  pl.BlockSpec((1, B_W), fixed),
                  vec, pl.BlockSpec((LORA, MIX_B), fixed),
                  vec, pl.BlockSpec((LORA, MIX_B), fixed),
                  vec, vec, vec, vec, vec,
                  pl.BlockSpec((MIX_B, MIX_B), fixed)],
        out_specs=[pl.BlockSpec((n, MIX_B), fixed), state],
        out_shape=[jax.ShapeDtypeStruct((n, MIX_B), F32),
                   jax.ShapeDtypeStruct((HB, HS_B, HS_B, n), F32)],
        scratch_shapes=[pltpu.VMEM((6, MIX_B, n), F32),
                        pltpu.VMEM((4, n, MIX_B), F32),
                        pltpu.VMEM((MIX_B, n), F32)],
        compiler_params=pltpu.CompilerParams(dimension_semantics=("arbitrary",),
                                             vmem_limit_bytes=VMEM_LIMIT),
        name="rwkv_step",
    )(pb, shift, s_hvkb, mu, w0, w2, a0, a2, k_k, k_a, rk, lnw, lnb, ones_blk)


def _block_ones(width, group):
    idx = jnp.arange(width) // group
    return (idx[:, None] == idx[None, :]).astype(BF16)


def kernel(x_prompt, x_sample, state_a_mat, state_a_conv, state_b_mat, state_b_shift, norm_w, w_in,
           conv_w, a_log, dt_bias, a_norm_w, mu, w0, w2, a0, a2, k_k, k_a, r_k, ln_w, ln_b, w_out,
           final_norm_w):
    bsz, seq, _ = x_prompt.shape
    nsmp = x_sample.shape[0]
    a_w = A_MAIN + 2 * HA

    w = w_in[0]
    w_parts = (w[:, :A_MAIN].astype(BF16),
               jnp.concatenate([w[:, a_w:], w[:, A_MAIN:a_w],
                                jnp.zeros((D_MODEL, BD_W - 2 * HA), F32)], axis=1).astype(BF16))
    w_o = w_out[0].astype(BF16)
    hp = jnp.zeros((8, BD_W), F32)
    hp = hp.at[0, HA:2 * HA].set(a_log[0]).at[1, HA:2 * HA].set(dt_bias[0])
    nw = norm_w[0].reshape(1, D_MODEL)
    fw = final_norm_w.reshape(1, D_MODEL)
    anw = jnp.tile(a_norm_w[0].reshape(1, DK_A), (1, HA))
    ones_b = _block_ones(MIX_B, HS_B)
    row = lambda z: z[0].reshape(1, -1)
    b_params = (row(mu), row(w0), w2[0], row(a0), a2[0], row(k_k), row(k_a), row(r_k), row(ln_w), row(ln_b),
                ones_b)

    xp = x_prompt.reshape(bsz * seq, D_MODEL)
    pa, pb, bd = _inproj(xp, nw, *w_parts)
    ob, p_bmat, p_bshift = _rwkv_prompt(pb.reshape(bsz, seq, B_W), *b_params)
    y_prompt, p_amat, p_aconv = _gdn_prompt(pa.reshape(bsz, seq, A_MAIN), bd.reshape(bsz, seq, BD_W),
                                            conv_w[0], hp, anw, x_prompt, ob, w_o, fw)

    xs = x_sample.reshape(nsmp, D_MODEL)
    sa_, sb_, sbd = _inproj(xs, nw, *w_parts)
    sconv_t = jnp.swapaxes(state_a_conv[0], 0, 1)
    soa, s_amat, s_aconv_t = _gdn_step(sa_, sbd, sconv_t, state_a_mat[0], conv_w[0], hp, anw)
    sob, s_bmat_t = _rwkv_step_lanes(sb_, state_b_shift[0], jnp.transpose(state_b_mat[0], (1, 2, 3, 0)),
                                     *b_params)
    s_bmat = jnp.transpose(s_bmat_t, (3, 0, 1, 2))
    y_sample = _outproj(xs, soa, sob, w_o, fw)

    return (y_prompt, y_sample.reshape(nsmp, 1, D_MODEL),
            p_amat[None], p_aconv[None], p_bmat[None], p_bshift.reshape(1, bsz, B_W),
            s_amat[None], jnp.swapaxes(s_aconv_t, 0, 1)[None], s_bmat[None], sb_[None])
```

```python
import jax
import jax.numpy as jnp
from jax import lax
from jax.experimental import pallas as pl
from jax.experimental.pallas import tpu as pltpu

F32 = jnp.float32
BF16 = jnp.bfloat16

D_MODEL = 1024
MIX_A = 512
MIX_B = 512
HA = 4
DK_A = 128
HB = 8
HS_B = 64
PAIR = 2 * HS_B
CONV_W = 4
LORA = 64
A_QKV = 3 * MIX_A
A_MAIN = A_QKV + MIX_A
B_W = 4 * MIX_B + 2 * LORA
BD_W = 128
RMS_EPS = 1e-6
GN_EPS = 64e-5
CHUNK = 64
TC = 256
TC_A = 256
SB = 16
VMEM_LIMIT = 56 * 1024 * 1024

HIGHEST = lax.Precision.HIGHEST
NN = (((1,), (0,)), ((), ()))
NT = (((1,), (1,)), ((), ()))
TN = (((0,), (0,)), ((), ()))


def _dot(a, b, dn=NN, prec=None):
    return lax.dot_general(a, b, dn, precision=prec, preferred_element_type=F32)


def _sigmoid(x):
    return 0.5 * jnp.tanh(0.5 * x) + 0.5


def _silu(x):
    h = 0.5 * x
    return h * jnp.tanh(h) + h


def _softplus(x):
    return jnp.maximum(x, 0.0) + jnp.log(1.0 + jnp.exp(-jnp.abs(x)))


def _l2norm(x):
    return x * lax.rsqrt(jnp.sum(x * x, axis=-1, keepdims=True) + 1e-12)


def _group_sum(x, ones_blk):
    return _dot(x.astype(BF16), ones_blk)


def _iota2(shape, dim):
    return lax.broadcasted_iota(jnp.int32, shape, dim)


def _chunk_tril(n):
    r = _iota2((n, n), 0)
    c = _iota2((n, n), 1)
    same = (r // CHUNK) == (c // CHUNK)
    return jnp.where(same & (c <= r), 1.0, 0.0).astype(F32)


def _tri_inv_many(lows):
    n = lows[0].shape[0]
    r = _iota2((n, n), 0)
    c = _iota2((n, n), 1)
    eye = jnp.where(r == c, 1.0, 0.0).astype(F32)
    base = (r // 8) == (c // 8)
    l0 = [jnp.where(base, low, 0.0) for low in lows]
    l2 = [_dot(a, a) for a in l0]
    l4 = [_dot(a, a) for a in l2]
    d = [_dot(eye - a, eye + b) for a, b in zip(l0, l2)]
    d = [_dot(a, eye + b) for a, b in zip(d, l4)]
    s = 8
    while s < n:
        sel = ((r // (2 * s)) == (c // (2 * s))) & ((r // s) != (c // s))
        t = [_dot(jnp.where(sel, low, 0.0), a) for low, a in zip(lows, d)]
        d = [a - _dot(a, b) for a, b in zip(d, t)]
        s *= 2
    return d


def _pair_diag(x):
    left = _iota2(x.shape, 1) < x.shape[1] // 2
    return jnp.concatenate([jnp.where(left, x, 0.0), jnp.where(left, 0.0, x)], axis=0)


def _tri_inv_pairs(lows):
    n = lows[0].shape[0]
    r = _iota2((n, 2 * n), 0)
    lane = _iota2((n, 2 * n), 1)
    c = jnp.where(lane >= n, lane - n, lane)
    mul = lambda a, b: _dot(a, _pair_diag(b))
    eye = jnp.where(r == c, 1.0, 0.0).astype(F32)
    base = (r // 8) == (c // 8)
    l0 = [jnp.where(base, low, 0.0) for low in lows]
    l2 = [mul(a, a) for a in l0]
    l4 = [mul(a, a) for a in l2]
    d = [mul(eye - a, eye + b) for a, b in zip(l0, l2)]
    d = [mul(a, eye + b) for a, b in zip(d, l4)]
    s = 8
    while s < n:
        sel = ((r // (2 * s)) == (c // (2 * s))) & ((r // s) != (c // s))
        t = [mul(jnp.where(sel, low, 0.0), a) for low, a in zip(lows, d)]
        d = [a - mul(a, b) for a, b in zip(d, t)]
        s *= 2
    return d


def _inproj_kernel(x_ref, nw_ref, w_ref, oa_ref, ob_ref, obd_ref, wb_scr):
    a_w = A_MAIN + 2 * HA

    @pl.when(pl.program_id(0) == 0)
    def _():
        wb_scr[...] = jnp.concatenate(
            [w_ref[:, a_w:], w_ref[:, A_MAIN:a_w], jnp.zeros((D_MODEL, BD_W - 2 * HA), BF16)], axis=1)

    x = x_ref[...]
    xn = (x * lax.rsqrt(jnp.mean(x * x, axis=-1, keepdims=True) + RMS_EPS) * nw_ref[...]).astype(BF16)
    oa_ref[...] = _dot(xn, w_ref[:, 0:A_MAIN])
    pb = _dot(xn, wb_scr[...])
    ob_ref[...] = pb[:, :B_W]
    obd_ref[...] = pb[:, B_W:]


def _inproj(x2d, norm_w, w_all):
    n = x2d.shape[0]
    tm = min(512, n)
    row = lambda i: (i, 0)
    fixed = lambda i: (0, 0)
    return pl.pallas_call(
        _inproj_kernel,
        grid=(n // tm,),
        in_specs=[pl.BlockSpec((tm, D_MODEL), row),
                  pl.BlockSpec((1, D_MODEL), fixed),
                  pl.BlockSpec(w_all.shape, fixed)],
        out_specs=[pl.BlockSpec((tm, A_MAIN), row),
                   pl.BlockSpec((tm, B_W), row),
                   pl.BlockSpec((tm, BD_W), row)],
        out_shape=[jax.ShapeDtypeStruct((n, A_MAIN), F32),
                   jax.ShapeDtypeStruct((n, B_W), F32),
                   jax.ShapeDtypeStruct((n, BD_W), F32)],
        scratch_shapes=[pltpu.VMEM((D_MODEL, B_W + BD_W), BF16)],
        compiler_params=pltpu.CompilerParams(dimension_semantics=("arbitrary",),
                                             vmem_limit_bytes=VMEM_LIMIT),
        name="inproj",
    )(x2d, norm_w, w_all)


def _outproj_kernel(x_ref, ma_ref, mb_ref, w_ref, fw_ref, y_ref):
    h = (x_ref[...]
         + _dot(ma_ref[...].astype(BF16), w_ref[0:MIX_A, :])
         + _dot(mb_ref[...].astype(BF16), w_ref[MIX_A:, :]))
    y_ref[...] = h * lax.rsqrt(jnp.mean(h * h, axis=-1, keepdims=True) + RMS_EPS) * fw_ref[...]


def _outproj(x2d, mix_a, mix_b, w_out, final_w):
    n = x2d.shape[0]
    tm = min(1024, n)
    row = lambda i: (i, 0)
    fixed = lambda i: (0, 0)
    return pl.pallas_call(
        _outproj_kernel,
        grid=(n // tm,),
        in_specs=[pl.BlockSpec((tm, D_MODEL), row),
                  pl.BlockSpec((tm, MIX_A), row),
                  pl.BlockSpec((tm, MIX_B), row),
                  pl.BlockSpec((D_MODEL, D_MODEL), fixed),
                  pl.BlockSpec((1, D_MODEL), fixed)],
        out_specs=pl.BlockSpec((tm, D_MODEL), row),
        out_shape=jax.ShapeDtypeStruct((n, D_MODEL), F32),
        compiler_params=pltpu.CompilerParams(dimension_semantics=("arbitrary",),
                                             vmem_limit_bytes=VMEM_LIMIT),
        name="outproj",
    )(x2d, mix_a, mix_b, w_out, final_w)


def _gdn_gates(bd, hp):
    beta = _sigmoid(bd)
    g = -jnp.exp(hp[0:1, :]) * _softplus(bd + hp[1:2, :])
    return beta, g


def _gdn_qk(qkv):
    heads = lambda z: [z[:, h * DK_A:(h + 1) * DK_A] for h in range(HA)]
    qn = jnp.concatenate([_l2norm(x) * (DK_A ** -0.5) for x in heads(qkv[:, 0:MIX_A])], axis=-1)
    kn = jnp.concatenate([_l2norm(x) for x in heads(qkv[:, MIX_A:2 * MIX_A])], axis=-1)
    return qn, kn


def _gdn_out(o, gate, anw):
    on = jnp.concatenate(
        [x * lax.rsqrt(jnp.mean(x * x, axis=-1, keepdims=True) + RMS_EPS)
         for x in [o[:, h * DK_A:(h + 1) * DK_A] for h in range(HA)]], axis=-1)
    return on * anw * _silu(gate)


def _gdn_prompt_kernel(pa_ref, bd_ref, cw_ref, hp_ref, anw_ref, x_ref, mb_ref, wo_ref, fw_ref,
                       y_ref, s_out_ref, c_out_ref, ext_ref, s_ref):
    c = pl.program_id(1)

    @pl.when(c == 0)
    def _():
        ext_ref[0:8, :] = jnp.zeros((8, A_QKV), F32)
        s_ref[...] = jnp.zeros_like(s_ref)

    tc = pa_ref.shape[0]
    ext_ref[8:8 + tc, :] = pa_ref[:, 0:A_QKV]
    cw = cw_ref[...]
    conv = (ext_ref[5:5 + tc, :] * cw[0:1, :] + ext_ref[6:6 + tc, :] * cw[1:2, :]
            + ext_ref[7:7 + tc, :] * cw[2:3, :] + ext_ref[8:8 + tc, :] * cw[3:4, :])
    qkv = _silu(conv)
    c_out_ref[...] = ext_ref[tc + 5:tc + 8, :]
    ext_ref[0:8, :] = ext_ref[tc:tc + 8, :]

    qn, kn = _gdn_qk(qkv)
    beta_all, g_all = _gdn_gates(bd_ref[...], hp_ref[...])
    gcum = _dot(_chunk_tril(tc), g_all, prec=HIGHEST)
    gcum_t = gcum.T

    r = _iota2((CHUNK, CHUNK), 0)
    cc = _iota2((CHUNK, CHUNK), 1)
    incl = cc <= r
    strict = cc < r

    nj = tc // CHUNK
    jh = [(j, h) for j in range(nj) for h in range(HA)]
    rows_of = lambda j: slice(j * CHUNK, (j + 1) * CHUNK)
    lanes_of = lambda h: slice(h * DK_A, (h + 1) * DK_A)
    q = [qn[rows_of(j), lanes_of(h)] for j, h in jh]
    k = [kn[rows_of(j), lanes_of(h)] for j, h in jh]
    v = [qkv[rows_of(j), 2 * MIX_A + h * DK_A:2 * MIX_A + (h + 1) * DK_A] for j, h in jh]
    beta = [beta_all[rows_of(j), h:h + 1] for j, h in jh]
    gc = [gcum[rows_of(j), HA + h:HA + h + 1] for j, h in jh]
    gr = [gcum_t[HA + h:HA + h + 1, rows_of(j)] for j, h in jh]
    glast = [gcum[(j + 1) * CHUNK - 1:(j + 1) * CHUNK, HA + h:HA + h + 1] for j, h in jh]
    decay = [jnp.where(incl, jnp.exp(jnp.where(incl, a - b, 0.0)), 0.0) for a, b in zip(gc, gr)]
    kq = [_dot(jnp.concatenate([a, b], axis=0), a, NT) for a, b in zip(k, q)]
    low = [jnp.where(strict, b * d * m[0:CHUNK, :], 0.0) for b, d, m in zip(beta, decay, kq)]
    tinv = _tri_inv_many(low)
    eg = [jnp.exp(a) for a in gc]
    wu = [_dot(t, jnp.concatenate([(b * e) * a, b * c_], axis=1))
          for t, b, e, a, c_ in zip(tinv, beta, eg, k, v)]
    kwu = [_dot(a * jnp.exp(gl - g), x, TN) for a, gl, g, x in zip(k, glast, gc, wu)]
    qwu = [_dot(d * m[CHUNK:, :], x) for d, m, x in zip(decay, kq, wu)]
    rd = _iota2((DK_A, DK_A), 0)
    cd = _iota2((DK_A, DK_A), 1)
    lhs = [jnp.concatenate([jnp.where(rd == cd, jnp.exp(gl), 0.0) - kx[:, 0:DK_A],
                            a * e - qx[:, 0:DK_A]], axis=0)
           for gl, kx, a, e, qx in zip(glast, kwu, q, eg, qwu)]

    o_rows = []
    for j in range(nj):
        idx = [j * HA + h for h in range(HA)]
        res = [_dot(lhs[i], s_ref[h]) for h, i in enumerate(idx)]
        for h, i in enumerate(idx):
            s_ref[h] = res[h][0:DK_A, :] + kwu[i][:, DK_A:]
        o_rows.append(jnp.concatenate(
            [res[h][DK_A:, :] + qwu[i][:, DK_A:] for h, i in enumerate(idx)], axis=-1))

    mix_a = _gdn_out(jnp.concatenate(o_rows, axis=0), pa_ref[:, A_QKV:A_MAIN], anw_ref[...])
    h = (x_ref[...] + _dot(mb_ref[...], wo_ref[MIX_A:, :])
         + _dot(mix_a.astype(BF16), wo_ref[0:MIX_A, :]))
    y_ref[...] = h * lax.rsqrt(jnp.mean(h * h, axis=-1, keepdims=True) + RMS_EPS) * fw_ref[...]
    s_out_ref[...] = s_ref[...]


def _gdn_prompt(pa, bd, conv_w, hp, anw, x, mix_b, w_out, final_w):
    b, t, _ = pa.shape
    blk = lambda i, j: (i, j, 0)
    fixed = lambda i, j: (0, 0)
    return pl.pallas_call(
        _gdn_prompt_kernel,
        grid=(b, t // TC_A),
        in_specs=[pl.BlockSpec((None, TC_A, A_MAIN), blk),
                  pl.BlockSpec((None, TC_A, BD_W), blk),
                  pl.BlockSpec((CONV_W, A_QKV), fixed),
                  pl.BlockSpec((8, BD_W), fixed),
                  pl.BlockSpec((1, MIX_A), fixed),
                  pl.BlockSpec((None, TC_A, D_MODEL), blk),
                  pl.BlockSpec((None, TC_A, MIX_B), blk),
                  pl.BlockSpec((D_MODEL, D_MODEL), fixed),
                  pl.BlockSpec((1, D_MODEL), fixed)],
        out_specs=[pl.BlockSpec((None, TC_A, D_MODEL), blk),
                   pl.BlockSpec((None, HA, DK_A, DK_A), lambda i, j: (i, 0, 0, 0)),
                   pl.BlockSpec((None, CONV_W - 1, A_QKV), lambda i, j: (i, 0, 0))],
        out_shape=[jax.ShapeDtypeStruct((b, t, D_MODEL), F32),
                   jax.ShapeDtypeStruct((b, HA, DK_A, DK_A), F32),
                   jax.ShapeDtypeStruct((b, CONV_W - 1, A_QKV), F32)],
        scratch_shapes=[pltpu.VMEM((TC_A + 8, A_QKV), F32),
                        pltpu.VMEM((HA, DK_A, DK_A), F32)],
        compiler_params=pltpu.CompilerParams(dimension_semantics=("arbitrary", "arbitrary"),
                                             vmem_limit_bytes=VMEM_LIMIT),
        name="gdn_prompt",
    )(pa, bd, conv_w, hp, anw, x, mix_b, w_out, final_w)


def _rwkv_front(xb, w0, w2, a0, a2, k_k, k_a, ones_blk):
    r = xb[:, 0:MIX_B]
    kr = xb[:, MIX_B:2 * MIX_B]
    vr = xb[:, 2 * MIX_B:3 * MIX_B]
    gate = xb[:, 3 * MIX_B:4 * MIX_B]
    wd = xb[:, 4 * MIX_B:4 * MIX_B + LORA]
    ad = xb[:, 4 * MIX_B + LORA:]
    w_ll = -_softplus(-(w0 + _dot(jnp.tanh(wd), w2))) - 0.5
    logw = -jnp.exp(w_ll)
    a = _sigmoid(a0 + _dot(ad, a2))
    kraw = kr * k_k
    kk = kraw * lax.rsqrt(_group_sum(kraw * kraw, ones_blk) + 1e-12)
    kmod = kr * (1.0 + (a - 1.0) * k_a)
    return r, kmod, vr, gate, logw, a, kk


def _rwkv_back(y, r, kmod, v, gate, rk, lnw, lnb, ones_blk):
    inv = 1.0 / HS_B
    yc = y - _group_sum(y, ones_blk) * inv
    var = _group_sum(yc * yc, ones_blk) * inv
    gn = yc * lax.rsqrt(var + GN_EPS) * lnw + lnb
    bonus = _group_sum(r * kmod * rk, ones_blk) * v
    return (gn + bonus) * _silu(gate)


def _rwkv_prompt_kernel(pb_ref, mu_ref, w0_ref, w2_ref, a0_ref, a2_ref, kk_ref, ka_ref,
                        rk_ref, lnw_ref, lnb_ref, ones_ref,
                        o_ref, s_out_ref, sh_out_ref, carry_ref, s_ref):
    c = pl.program_id(1)

    @pl.when(c == 0)
    def _():
        carry_ref[...] = jnp.zeros_like(carry_ref)
        s_ref[...] = jnp.zeros_like(s_ref)

    ones_blk = ones_ref[...]
    pb = pb_ref[...]
    rowi = _iota2((TC, 1), 0)
    prev = jnp.where(rowi == 0, carry_ref[0:1, :], pltpu.roll(pb, 1, 0))
    carry_ref[0:1, :] = pb[TC - 1:TC, :]
    sh_out_ref[...] = pb[TC - 1:TC, :]
    xb = pb + (prev - pb) * mu_ref[...]
    r, kmod, v, gate, logw, a, kk = _rwkv_front(
        xb, w0_ref[...], w2_ref[...], a0_ref[...], a2_ref[...], kk_ref[...], ka_ref[...], ones_blk)

    g = _dot(_chunk_tril(TC), logw, prec=HIGHEST)
    eg = jnp.exp(g)
    egn = jnp.exp(-g)
    ag = -kk * jnp.exp(g - logw)
    kka = kk * a
    bg = kka * egn
    kg = kmod * egn
    rg = r * eg

    rowp = _iota2((CHUNK, PAIR), 0)
    lane = _iota2((CHUNK, PAIR), 1)
    colp = jnp.where(lane >= HS_B, lane - HS_B, lane)
    left = lane < HS_B
    incl = colp <= rowp
    strict = colp < rowp
    eye = colp == rowp
    zero = jnp.zeros((CHUNK, PAIR), F32)
    pack = lambda f: jnp.where(left, f[0:HS_B, :], f[HS_B:, :])

    nj = TC // CHUNK
    npair = HB // 2
    jp = [(j, p) for j in range(nj) for p in range(npair)]
    rows_of = lambda j: slice(j * CHUNK, (j + 1) * CHUNK)
    lanes_of = lambda p: slice(p * PAIR, (p + 1) * PAIR)
    glast = [g[(j + 1) * CHUNK - 1:(j + 1) * CHUNK, :] for j in range(nj)]
    eglast = [jnp.exp(glast[j]) for j in range(nj)]
    bdec = [bg[rows_of(j), :] * eglast[j] for j in range(nj)]
    kdec = [kg[rows_of(j), :] * eglast[j] for j in range(nj)]

    cut = lambda z: [z[rows_of(j), lanes_of(p)] for j, p in jp]
    ag_p, rg_p, bg_p, kg_p, v_p = cut(ag), cut(rg), cut(bg), cut(kg), cut(v)
    bdec_p = [bdec[j][:, lanes_of(p)] for j, p in jp]
    kdec_p = [kdec[j][:, lanes_of(p)] for j, p in jp]
    eglast_p = [eglast[j][:, lanes_of(p)] for j, p in jp]
    amat = [_dot(jnp.concatenate([a, b], axis=0),
                 jnp.concatenate([_pair_diag(c_), _pair_diag(d)], axis=0), NT)
            for a, b, c_, d in zip(ag_p, rg_p, bg_p, kg_p)]
    a_ab = [jnp.where(strict, m[0:CHUNK, 0:PAIR], 0.0) for m in amat]
    a_ak = [jnp.where(strict, m[0:CHUNK, PAIR:], 0.0) for m in amat]
    a_rb = [jnp.where(incl, m[CHUNK:, 0:PAIR], 0.0) for m in amat]
    a_rk = [jnp.where(incl, m[CHUNK:, PAIR:], 0.0) for m in amat]
    tinv = _tri_inv_pairs([-a for a in a_ab])
    kv = [_dot(jnp.concatenate([a, b], axis=0), _pair_diag(c_)) for a, b, c_ in zip(a_ak, a_rk, v_p)]
    wu = [_dot(t, jnp.concatenate([_pair_diag(a), _pair_diag(x[0:CHUNK, :])], axis=1))
          for t, a, x in zip(tinv, ag_p, kv)]
    rwu = [_dot(a, jnp.concatenate([_pair_diag(x[:, 0:PAIR]), _pair_diag(x[:, PAIR:])], axis=1))
           for a, x in zip(a_rb, wu)]
    mn = [_dot(jnp.concatenate([a, b], axis=0),
               jnp.concatenate([x, jnp.concatenate([zero, c_], axis=1)], axis=0), TN)
          for a, b, x, c_ in zip(bdec_p, kdec_p, wu, v_p)]
    lhs = [jnp.concatenate([jnp.where(eye, e, 0.0) + pack(m[:, 0:PAIR]), a + x[:, 0:PAIR]], axis=0)
           for e, m, a, x in zip(eglast_p, mn, rg_p, rwu)]

    y_rows = []
    for j in range(nj):
        idx = [j * npair + p for p in range(npair)]
        res = [_dot(lhs[i], _pair_diag(s_ref[p])) for p, i in enumerate(idx)]
        for p, i in enumerate(idx):
            s_ref[p] = res[p][0:HS_B, :] + pack(mn[i][:, PAIR:])
        y_rows.append(jnp.concatenate(
            [res[p][HS_B:, :] + rwu[i][:, PAIR:] + kv[i][CHUNK:, :] for p, i in enumerate(idx)],
            axis=-1))

    o_ref[...] = _rwkv_back(jnp.concatenate(y_rows, axis=0), r, kmod, v, gate,
                            rk_ref[...], lnw_ref[...], lnb_ref[...], ones_blk).astype(o_ref.dtype)
    for p in range(npair):
        st = s_ref[p].T
        s_out_ref[2 * p] = st[0:HS_B, :]
        s_out_ref[2 * p + 1] = st[HS_B:, :]


def _rwkv_prompt(pb, mu, w0, w2, a0, a2, k_k, k_a, rk, lnw, lnb, ones_blk):
    b, t, _ = pb.shape
    blk = lambda i, j: (i, j, 0)
    fixed = lambda i, j: (0, 0)
    vec = pl.BlockSpec((1, MIX_B), fixed)
    return pl.pallas_call(
        _rwkv_prompt_kernel,
        grid=(b, t // TC),
        in_specs=[pl.BlockSpec((None, TC, B_W), blk),
                  pl.BlockSpec((1, B_W), fixed),
                  vec, pl.BlockSpec((LORA, MIX_B), fixed),
                  vec, pl.BlockSpec((LORA, MIX_B), fixed),
                  vec, vec, vec, vec, vec,
                  pl.BlockSpec((MIX_B, MIX_B), fixed)],
        out_specs=[pl.BlockSpec((None, TC, MIX_B), blk),
                   pl.BlockSpec((None, HB, HS_B, HS_B), lambda i, j: (i, 0, 0, 0)),
                   pl.BlockSpec((None, 1, B_W), lambda i, j: (i, 0, 0))],
        out_shape=[jax.ShapeDtypeStruct((b, t, MIX_B), BF16),
                   jax.ShapeDtypeStruct((b, HB, HS_B, HS_B), F32),
                   jax.ShapeDtypeStruct((b, 1, B_W), F32)],
        scratch_shapes=[pltpu.VMEM((8, B_W), F32),
                        pltpu.VMEM((HB // 2, HS_B, PAIR), F32)],
        compiler_params=pltpu.CompilerParams(dimension_semantics=("arbitrary", "arbitrary"),
                                             vmem_limit_bytes=VMEM_LIMIT),
        name="rwkv_prompt",
    )(pb, mu, w0, w2, a0, a2, k_k, k_a, rk, lnw, lnb, ones_blk)


def _pad_rows(rows, width):
    return jnp.concatenate(rows + [jnp.zeros((8 - len(rows), width), F32)], axis=0)


def _gdn_step_kernel(pa_ref, bd_ref, sc_ref, s_ref, cw_ref, hp_ref, anw_ref,
                     o_ref, s_out_ref, c_out_ref):
    u = pa_ref[:, 0:A_QKV]
    cw = cw_ref[...]
    conv = (sc_ref[0] * cw[0:1, :] + sc_ref[1] * cw[1:2, :] + sc_ref[2] * cw[2:3, :] + u * cw[3:4, :])
    qkv = _silu(conv)
    c_out_ref[0] = sc_ref[1]
    c_out_ref[1] = sc_ref[2]
    c_out_ref[2] = u

    qn, kn = _gdn_qk(qkv)
    beta_all, g_all = _gdn_gates(bd_ref[...], hp_ref[...])
    eg_all = jnp.exp(g_all)
    lanes_of = lambda h: slice(h * DK_A, (h + 1) * DK_A)
    q = [qn[:, lanes_of(h)] for h in range(HA)]
    k = [kn[:, lanes_of(h)] for h in range(HA)]
    v = [qkv[:, 2 * MIX_A + h * DK_A:2 * MIX_A + (h + 1) * DK_A] for h in range(HA)]
    beta = [beta_all[:, h:h + 1] for h in range(HA)]
    eg = [eg_all[:, HA + h:HA + h + 1] for h in range(HA)]

    bh = [(b, h) for b in range(SB) for h in range(HA)]
    s0 = [s_ref[b, h] for b, h in bh]
    ks_qs = [_dot(_pad_rows([k[h][b:b + 1, :], q[h][b:b + 1, :]], DK_A), s) for (b, h), s in zip(bh, s0)]
    o_heads = []
    for h in range(HA):
        ks = jnp.concatenate([ks_qs[b * HA + h][0:1, :] for b in range(SB)], axis=0)
        qs = jnp.concatenate([ks_qs[b * HA + h][1:2, :] for b in range(SB)], axis=0)
        u2 = beta[h] * (v[h] - eg[h] * ks)
        for b in range(SB):
            s_out_ref[b, h] = (eg[h][b:b + 1, :] * s0[b * HA + h]
                               + _dot(_pad_rows([k[h][b:b + 1, :]], DK_A),
                                      _pad_rows([u2[b:b + 1, :]], DK_A), TN))
        o_heads.append(eg[h] * qs + jnp.sum(q[h] * k[h], axis=-1, keepdims=True) * u2)

    o_ref[...] = _gdn_out(jnp.concatenate(o_heads, axis=-1), pa_ref[:, A_QKV:A_MAIN], anw_ref[...])


def _gdn_step(pa, bd, sconv_t, s_a, conv_w, hp, anw):
    n = pa.shape[0]
    row = lambda i: (i, 0)
    fixed = lambda i: (0, 0)
    return pl.pallas_call(
        _gdn_step_kernel,
        grid=(n // SB,),
        in_specs=[pl.BlockSpec((SB, A_MAIN), row),
                  pl.BlockSpec((SB, BD_W), row),
                  pl.BlockSpec((CONV_W - 1, SB, A_QKV), lambda i: (0, i, 0)),
                  pl.BlockSpec((SB, HA, DK_A, DK_A), lambda i: (i, 0, 0, 0)),
                  pl.BlockSpec((CONV_W, A_QKV), fixed),
                  pl.BlockSpec((8, BD_W), fixed),
                  pl.BlockSpec((1, MIX_A), fixed)],
        out_specs=[pl.BlockSpec((SB, MIX_A), row),
                   pl.BlockSpec((SB, HA, DK_A, DK_A), lambda i: (i, 0, 0, 0)),
                   pl.BlockSpec((CONV_W - 1, SB, A_QKV), lambda i: (0, i, 0))],
        out_shape=[jax.ShapeDtypeStruct((n, MIX_A), F32),
                   jax.ShapeDtypeStruct((n, HA, DK_A, DK_A), F32),
                   jax.ShapeDtypeStruct((CONV_W - 1, n, A_QKV), F32)],
        compiler_params=pltpu.CompilerParams(dimension_semantics=("arbitrary",),
                                             vmem_limit_bytes=VMEM_LIMIT),
        name="gdn_step",
    )(pa, bd, sconv_t, s_a, conv_w, hp, anw)


def _rwkv_step_lanes_kernel(pb_ref, sh_ref, s_ref, mu_ref, w0_ref, w2_ref, a0_ref, a2_ref, kk_ref, ka_ref,
                            rk_ref, lnw_ref, lnb_ref, ones_ref,
                            o_ref, s_out_ref, vec_scr, row_scr, y_scr):
    h = pl.program_id(0)
    ones_blk = ones_ref[...]

    @pl.when(h == 0)
    def _():
        pb = pb_ref[...]
        xb = pb + (sh_ref[...] - pb) * mu_ref[...]
        r, kmod, v, gate, logw, a, kk = _rwkv_front(
            xb, w0_ref[...], w2_ref[...], a0_ref[...], a2_ref[...], kk_ref[...], ka_ref[...], ones_blk)
        for slot, z in enumerate((-kk, kk * a, kmod, v, r, jnp.exp(logw))):
            vec_scr[slot] = z.T
        for slot, z in enumerate((r, kmod, v, gate)):
            row_scr[slot] = z

    base = pl.multiple_of(h * HS_B, HS_B)
    chan = pl.ds(base, HS_B)
    nkk = vec_scr[0, chan, :]
    kka = vec_scr[1, chan, :]
    kmod_t = vec_scr[2, chan, :]
    r_t = vec_scr[4, chan, :]
    wdec = vec_scr[5, chan, :]

    def body(vi, carry):
        s0 = s_ref[0, vi]
        sa = jnp.sum(s0 * nkk, axis=0, keepdims=True)
        s1 = s0 * wdec + sa * kka + vec_scr[3, pl.ds(base + vi, 1), :] * kmod_t
        s_out_ref[0, vi] = s1
        y_scr[pl.ds(base + vi, 1), :] = jnp.sum(s1 * r_t, axis=0, keepdims=True)
        return carry

    lax.fori_loop(0, HS_B, body, 0, unroll=4)

    @pl.when(h == HB - 1)
    def _():
        o_ref[...] = _rwkv_back(y_scr[...].T, row_scr[0], row_scr[1], row_scr[2], row_scr[3],
                                rk_ref[...], lnw_ref[...], lnb_ref[...], ones_blk)


def _rwkv_step_lanes(pb, shift, s_hvkb, mu, w0, w2, a0, a2, k_k, k_a, rk, lnw, lnb, ones_blk):
    n = pb.shape[0]
    fixed = lambda i: (0, 0)
    vec = pl.BlockSpec((1, MIX_B), fixed)
    state = pl.BlockSpec((1, HS_B, HS_B, n), lambda i: (i, 0, 0, 0))
    return pl.pallas_call(
        _rwkv_step_lanes_kernel,
        grid=(HB,),
        in_specs=[pl.BlockSpec((n, B_W), fixed),
                  pl.BlockSpec((n, B_W), fixed),
                  state,
                  pl.BlockSpec((1, B_W), fixed),
                  vec, pl.BlockSpec((LORA, MIX_B), fixed),
                  vec, pl.BlockSpec((LORA, MIX_B), fixed),
                  vec, vec, vec, vec, vec,
                  pl.BlockSpec((MIX_B, MIX_B), fixed)],
        out_specs=[pl.BlockSpec((n, MIX_B), fixed), state],
        out_shape=[jax.ShapeDtypeStruct((n, MIX_B), F32),
                   jax.ShapeDtypeStruct((HB, HS_B, HS_B, n), F32)],
        scratch_shapes=[pltpu.VMEM((6, MIX_B, n), F32),
                        pltpu.VMEM((4, n, MIX_B), F32),
                        pltpu.VMEM((MIX_B, n), F32)],
        compiler_params=pltpu.CompilerParams(dimension_semantics=("arbitrary",),
                                             vmem_limit_bytes=VMEM_LIMIT),
        name="rwkv_step",
    )(pb, shift, s_hvkb, mu, w0, w2, a0, a2, k_k, k_a, rk, lnw, lnb, ones_blk)


def _block_ones(width, group):
    idx = jnp.arange(width) // group
    return (idx[:, None] == idx[None, :]).astype(BF16)


def kernel(x_prompt, x_sample, state_a_mat, state_a_conv, state_b_mat, state_b_shift, norm_w, w_in,
           conv_w, a_log, dt_bias, a_norm_w, mu, w0, w2, a0, a2, k_k, k_a, r_k, ln_w, ln_b, w_out,
           final_norm_w):
    bsz, seq, _ = x_prompt.shape
    nsmp = x_sample.shape[0]

    w_all = w_in[0].astype(BF16)
    w_o = w_out[0].astype(BF16)
    hp = jnp.zeros((8, BD_W), F32)
    hp = hp.at[0, HA:2 * HA].set(a_log[0]).at[1, HA:2 * HA].set(dt_bias[0])
    nw = norm_w[0].reshape(1, D_MODEL)
    fw = final_norm_w.reshape(1, D_MODEL)
    anw = jnp.tile(a_norm_w[0].reshape(1, DK_A), (1, HA))
    ones_b = _block_ones(MIX_B, HS_B)
    row = lambda z: z[0].reshape(1, -1)
    b_params = (row(mu), row(w0), w2[0], row(a0), a2[0], row(k_k), row(k_a), row(r_k), row(ln_w), row(ln_b),
                ones_b)

    xp = x_prompt.reshape(bsz * seq, D_MODEL)
    pa, pb, bd = _inproj(xp, nw, w_all)
    ob, p_bmat, p_bshift = _rwkv_prompt(pb.reshape(bsz, seq, B_W), *b_params)
    y_prompt, p_amat, p_aconv = _gdn_prompt(pa.reshape(bsz, seq, A_MAIN), bd.reshape(bsz, seq, BD_W),
                                            conv_w[0], hp, anw, x_prompt, ob, w_o, fw)

    xs = x_sample.reshape(nsmp, D_MODEL)
    sa_, sb_, sbd = _inproj(xs, nw, w_all)
    sconv_t = jnp.swapaxes(state_a_conv[0], 0, 1)
    soa, s_amat, s_aconv_t = _gdn_step(sa_, sbd, sconv_t, state_a_mat[0], conv_w[0], hp, anw)
    sob, s_bmat_t = _rwkv_step_lanes(sb_, state_b_shift[0], jnp.transpose(state_b_mat[0], (1, 2, 3, 0)),
                                     *b_params)
    s_bmat = jnp.transpose(s_bmat_t, (3, 0, 1, 2))
    y_sample = _outproj(xs, soa, sob, w_o, fw)

    return (y_prompt, y_sample.reshape(nsmp, 1, D_MODEL),
            p_amat[None], p_aconv[None], p_bmat[None], p_bshift.reshape(1, bsz, B_W),
            s_amat[None], jnp.swapaxes(s_aconv_t, 0, 1)[None], s_bmat[None], sb_[None])
```

```python
import jax
import jax.numpy as jnp
from jax import lax
from jax.experimental import pallas as pl
from jax.experimental.pallas import tpu as pltpu

F32 = jnp.float32
BF16 = jnp.bfloat16

D_MODEL = 1024
MIX_A = 512
MIX_B = 512
HA = 4
DK_A = 128
HB = 8
HS_B = 64
PAIR = 2 * HS_B
CONV_W = 4
LORA = 64
A_QKV = 3 * MIX_A
A_MAIN = A_QKV + MIX_A
B_W = 4 * MIX_B + 2 * LORA
BD_W = 128
RMS_EPS = 1e-6
GN_EPS = 64e-5
CHUNK = 64
TC = 256
TC_A = 256
SB = 16
VMEM_LIMIT = 56 * 1024 * 1024

EXP_NEG_HALF = 0.6065306597126334
NN = (((1,), (0,)), ((), ()))
NT = (((1,), (1,)), ((), ()))
TN = (((0,), (0,)), ((), ()))


def _dot(a, b, dn=NN, prec=None):
    return lax.dot_general(a, b, dn, precision=prec, preferred_element_type=F32)


def _sigmoid(x):
    return 0.5 * jnp.tanh(0.5 * x) + 0.5


def _silu(x):
    h = 0.5 * x
    return h * jnp.tanh(h) + h


def _softplus(x):
    return jnp.maximum(x, 0.0) + jnp.log(1.0 + jnp.exp(-jnp.abs(x)))


def _l2norm(x):
    return x * lax.rsqrt(jnp.sum(x * x, axis=-1, keepdims=True) + 1e-12)


def _group_sum(x, ones_blk):
    return _dot(x.astype(BF16), ones_blk)


def _iota2(shape, dim):
    return lax.broadcasted_iota(jnp.int32, shape, dim)


def _chunk_cumsum(x):
    n = x.shape[0]
    r = _iota2((n, n), 0)
    c = _iota2((n, n), 1)
    tril = jnp.where(((r // CHUNK) == (c // CHUNK)) & (c <= r), 1.0, 0.0).astype(BF16)
    hi = x.astype(BF16)
    rest = x - hi.astype(F32)
    mid = rest.astype(BF16)
    lo = (rest - mid.astype(F32)).astype(BF16)
    return _dot(tril, hi) + _dot(tril, mid) + _dot(tril, lo)


def _tri_inv_many(lows):
    n = lows[0].shape[0]
    r = _iota2((n, n), 0)
    c = _iota2((n, n), 1)
    eye = jnp.where(r == c, 1.0, 0.0).astype(F32)
    base = (r // 8) == (c // 8)
    l0 = [jnp.where(base, low, 0.0) for low in lows]
    l2 = [_dot(a, a) for a in l0]
    l4 = [_dot(a, a) for a in l2]
    d = [_dot(eye - a, eye + b) for a, b in zip(l0, l2)]
    d = [_dot(a, eye + b) for a, b in zip(d, l4)]
    s = 8
    while s < n:
        sel = ((r // (2 * s)) == (c // (2 * s))) & ((r // s) != (c // s))
        t = [_dot(jnp.where(sel, low, 0.0), a) for low, a in zip(lows, d)]
        d = [a - _dot(a, b) for a, b in zip(d, t)]
        s *= 2
    return d


def _pair_diag(x):
    left = _iota2(x.shape, 1) < x.shape[1] // 2
    return jnp.concatenate([jnp.where(left, x, 0.0), jnp.where(left, 0.0, x)], axis=0)


def _tri_inv_pairs(lows):
    n = lows[0].shape[0]
    r = _iota2((n, 2 * n), 0)
    lane = _iota2((n, 2 * n), 1)
    c = jnp.where(lane >= n, lane - n, lane)
    mul = lambda a, b: _dot(a, _pair_diag(b))
    eye = jnp.where(r == c, 1.0, 0.0).astype(F32)
    base = (r // 8) == (c // 8)
    l0 = [jnp.where(base, low, 0.0) for low in lows]
    l2 = [mul(a, a) for a in l0]
    l4 = [mul(a, a) for a in l2]
    d = [mul(eye - a, eye + b) for a, b in zip(l0, l2)]
    d = [mul(a, eye + b) for a, b in zip(d, l4)]
    s = 8
    while s < n:
        sel = ((r // (2 * s)) == (c // (2 * s))) & ((r // s) != (c // s))
        t = [mul(jnp.where(sel, low, 0.0), a) for low, a in zip(lows, d)]
        d = [a - mul(a, b) for a, b in zip(d, t)]
        s *= 2
    return d


def _inproj_kernel(x_ref, nw_ref, w_ref, oa_ref, ob_ref, obd_ref, wb_scr):
    a_w = A_MAIN + 2 * HA

    @pl.when(pl.program_id(0) == 0)
    def _():
        wb_scr[...] = jnp.concatenate(
            [w_ref[:, a_w:], w_ref[:, A_MAIN:a_w], jnp.zeros((D_MODEL, BD_W - 2 * HA), BF16)], axis=1)

    x = x_ref[...]
    xn = (x * lax.rsqrt(jnp.mean(x * x, axis=-1, keepdims=True) + RMS_EPS) * nw_ref[...]).astype(BF16)
    oa_ref[...] = _dot(xn, w_ref[:, 0:A_MAIN])
    pb = _dot(xn, wb_scr[...])
    ob_ref[...] = pb[:, :B_W]
    obd_ref[...] = pb[:, B_W:]


def _inproj(x2d, norm_w, w_all):
    n = x2d.shape[0]
    tm = min(512, n)
    row = lambda i: (i, 0)
    fixed = lambda i: (0, 0)
    return pl.pallas_call(
        _inproj_kernel,
        grid=(n // tm,),
        in_specs=[pl.BlockSpec((tm, D_MODEL), row),
                  pl.BlockSpec((1, D_MODEL), fixed),
                  pl.BlockSpec(w_all.shape, fixed)],
        out_specs=[pl.BlockSpec((tm, A_MAIN), row),
                   pl.BlockSpec((tm, B_W), row),
                   pl.BlockSpec((tm, BD_W), row)],
        out_shape=[jax.ShapeDtypeStruct((n, A_MAIN), F32),
                   jax.ShapeDtypeStruct((n, B_W), F32),
                   jax.ShapeDtypeStruct((n, BD_W), F32)],
        scratch_shapes=[pltpu.VMEM((D_MODEL, B_W + BD_W), BF16)],
        compiler_params=pltpu.CompilerParams(dimension_semantics=("arbitrary",),
                                             vmem_limit_bytes=VMEM_LIMIT),
        name="inproj",
    )(x2d, norm_w, w_all)


def _outproj_kernel(x_ref, ma_ref, mb_ref, w_ref, fw_ref, y_ref):
    h = (x_ref[...]
         + _dot(ma_ref[...].astype(BF16), w_ref[0:MIX_A, :])
         + _dot(mb_ref[...].astype(BF16), w_ref[MIX_A:, :]))
    y_ref[...] = h * lax.rsqrt(jnp.mean(h * h, axis=-1, keepdims=True) + RMS_EPS) * fw_ref[...]


def _outproj(x2d, mix_a, mix_b, w_out, final_w):
    n = x2d.shape[0]
    tm = min(1024, n)
    row = lambda i: (i, 0)
    fixed = lambda i: (0, 0)
    return pl.pallas_call(
        _outproj_kernel,
        grid=(n // tm,),
        in_specs=[pl.BlockSpec((tm, D_MODEL), row),
                  pl.BlockSpec((tm, MIX_A), row),
                  pl.BlockSpec((tm, MIX_B), row),
                  pl.BlockSpec((D_MODEL, D_MODEL), fixed),
                  pl.BlockSpec((1, D_MODEL), fixed)],
        out_specs=pl.BlockSpec((tm, D_MODEL), row),
        out_shape=jax.ShapeDtypeStruct((n, D_MODEL), F32),
        compiler_params=pltpu.CompilerParams(dimension_semantics=("arbitrary",),
                                             vmem_limit_bytes=VMEM_LIMIT),
        name="outproj",
    )(x2d, mix_a, mix_b, w_out, final_w)


def _gdn_gates(bd, hp):
    beta = _sigmoid(bd)
    g = -jnp.exp(hp[0:1, :]) * _softplus(bd + hp[1:2, :])
    return beta, g


def _gdn_qk(qkv):
    heads = lambda z: [z[:, h * DK_A:(h + 1) * DK_A] for h in range(HA)]
    qn = jnp.concatenate([_l2norm(x) * (DK_A ** -0.5) for x in heads(qkv[:, 0:MIX_A])], axis=-1)
    kn = jnp.concatenate([_l2norm(x) for x in heads(qkv[:, MIX_A:2 * MIX_A])], axis=-1)
    return qn, kn


def _gdn_out(o, gate, anw):
    on = jnp.concatenate(
        [x * lax.rsqrt(jnp.mean(x * x, axis=-1, keepdims=True) + RMS_EPS)
         for x in [o[:, h * DK_A:(h + 1) * DK_A] for h in range(HA)]], axis=-1)
    return on * anw * _silu(gate)


def _gdn_prompt_kernel(pa_ref, bd_ref, cw_ref, hp_ref, anw_ref, x_ref, mb_ref, wo_ref, fw_ref,
                       y_ref, s_out_ref, c_out_ref, ext_ref, s_ref):
    c = pl.program_id(1)

    @pl.when(c == 0)
    def _():
        ext_ref[0:8, :] = jnp.zeros((8, A_QKV), F32)
        s_ref[...] = jnp.zeros_like(s_ref)

    tc = pa_ref.shape[0]
    ext_ref[8:8 + tc, :] = pa_ref[:, 0:A_QKV]
    cw = cw_ref[...]
    conv = (ext_ref[5:5 + tc, :] * cw[0:1, :] + ext_ref[6:6 + tc, :] * cw[1:2, :]
            + ext_ref[7:7 + tc, :] * cw[2:3, :] + ext_ref[8:8 + tc, :] * cw[3:4, :])
    qkv = _silu(conv)
    c_out_ref[...] = ext_ref[tc + 5:tc + 8, :]
    ext_ref[0:8, :] = ext_ref[tc:tc + 8, :]

    qn, kn = _gdn_qk(qkv)
    beta_all, g_all = _gdn_gates(bd_ref[...], hp_ref[...])
    gcum = _chunk_cumsum(g_all)
    gcum_t = gcum.T

    r = _iota2((CHUNK, CHUNK), 0)
    cc = _iota2((CHUNK, CHUNK), 1)
    incl = cc <= r
    strict = cc < r

    nj = tc // CHUNK
    jh = [(j, h) for j in range(nj) for h in range(HA)]
    rows_of = lambda j: slice(j * CHUNK, (j + 1) * CHUNK)
    lanes_of = lambda h: slice(h * DK_A, (h + 1) * DK_A)
    q = [qn[rows_of(j), lanes_of(h)] for j, h in jh]
    k = [kn[rows_of(j), lanes_of(h)] for j, h in jh]
    v = [qkv[rows_of(j), 2 * MIX_A + h * DK_A:2 * MIX_A + (h + 1) * DK_A] for j, h in jh]
    beta = [beta_all[rows_of(j), h:h + 1] for j, h in jh]
    gc = [gcum[rows_of(j), HA + h:HA + h + 1] for j, h in jh]
    gr = [gcum_t[HA + h:HA + h + 1, rows_of(j)] for j, h in jh]
    glast = [gcum[(j + 1) * CHUNK - 1:(j + 1) * CHUNK, HA + h:HA + h + 1] for j, h in jh]
    decay = [jnp.where(incl, jnp.exp(jnp.where(incl, a - b, 0.0)), 0.0) for a, b in zip(gc, gr)]
    kq = [_dot(jnp.concatenate([a, b], axis=0), a, NT) for a, b in zip(k, q)]
    low = [jnp.where(strict, b * d * m[0:CHUNK, :], 0.0) for b, d, m in zip(beta, decay, kq)]
    tinv = _tri_inv_many(low)
    eg = [jnp.exp(a) for a in gc]
    wu = [_dot(t, jnp.concatenate([(b * e) * a, b * c_], axis=1))
          for t, b, e, a, c_ in zip(tinv, beta, eg, k, v)]
    kwu = [_dot(a * jnp.exp(gl - g), x, TN) for a, gl, g, x in zip(k, glast, gc, wu)]
    qwu = [_dot(d * m[CHUNK:, :], x) for d, m, x in zip(decay, kq, wu)]
    rd = _iota2((DK_A, DK_A), 0)
    cd = _iota2((DK_A, DK_A), 1)
    lhs = [jnp.concatenate([jnp.where(rd == cd, jnp.exp(gl), 0.0) - kx[:, 0:DK_A],
                            a * e - qx[:, 0:DK_A]], axis=0)
           for gl, kx, a, e, qx in zip(glast, kwu, q, eg, qwu)]

    o_rows = []
    for j in range(nj):
        idx = [j * HA + h for h in range(HA)]
        res = [_dot(lhs[i], s_ref[h]) for h, i in enumerate(idx)]
        for h, i in enumerate(idx):
            s_ref[h] = res[h][0:DK_A, :] + kwu[i][:, DK_A:]
        o_rows.append(jnp.concatenate(
            [res[h][DK_A:, :] + qwu[i][:, DK_A:] for h, i in enumerate(idx)], axis=-1))

    mix_a = _gdn_out(jnp.concatenate(o_rows, axis=0), pa_ref[:, A_QKV:A_MAIN], anw_ref[...])
    h = (x_ref[...] + _dot(mb_ref[...], wo_ref[MIX_A:, :])
         + _dot(mix_a.astype(BF16), wo_ref[0:MIX_A, :]))
    y_ref[...] = h * lax.rsqrt(jnp.mean(h * h, axis=-1, keepdims=True) + RMS_EPS) * fw_ref[...]
    s_out_ref[...] = s_ref[...]


def _gdn_prompt(pa, bd, conv_w, hp, anw, x, mix_b, w_out, final_w):
    b, t, _ = pa.shape
    blk = lambda i, j: (i, j, 0)
    fixed = lambda i, j: (0, 0)
    return pl.pallas_call(
        _gdn_prompt_kernel,
        grid=(b, t // TC_A),
        in_specs=[pl.BlockSpec((None, TC_A, A_MAIN), blk),
                  pl.BlockSpec((None, TC_A, BD_W), blk),
                  pl.BlockSpec((CONV_W, A_QKV), fixed),
                  pl.BlockSpec((8, BD_W), fixed),
                  pl.BlockSpec((1, MIX_A), fixed),
                  pl.BlockSpec((None, TC_A, D_MODEL), blk),
                  pl.BlockSpec((None, TC_A, MIX_B), blk),
                  pl.BlockSpec((D_MODEL, D_MODEL), fixed),
                  pl.BlockSpec((1, D_MODEL), fixed)],
        out_specs=[pl.BlockSpec((None, TC_A, D_MODEL), blk),
                   pl.BlockSpec((None, HA, DK_A, DK_A), lambda i, j: (i, 0, 0, 0)),
                   pl.BlockSpec((None, CONV_W - 1, A_QKV), lambda i, j: (i, 0, 0))],
        out_shape=[jax.ShapeDtypeStruct((b, t, D_MODEL), F32),
                   jax.ShapeDtypeStruct((b, HA, DK_A, DK_A), F32),
                   jax.ShapeDtypeStruct((b, CONV_W - 1, A_QKV), F32)],
        scratch_shapes=[pltpu.VMEM((TC_A + 8, A_QKV), F32),
                        pltpu.VMEM((HA, DK_A, DK_A), F32)],
        compiler_params=pltpu.CompilerParams(dimension_semantics=("arbitrary", "arbitrary"),
                                             vmem_limit_bytes=VMEM_LIMIT),
        name="gdn_prompt",
    )(pa, bd, conv_w, hp, anw, x, mix_b, w_out, final_w)


def _rwkv_front(xb, w0, w2, a0, a2, k_k, k_a, ones_blk):
    r = xb[:, 0:MIX_B]
    kr = xb[:, MIX_B:2 * MIX_B]
    vr = xb[:, 2 * MIX_B:3 * MIX_B]
    gate = xb[:, 3 * MIX_B:4 * MIX_B]
    wd = xb[:, 4 * MIX_B:4 * MIX_B + LORA]
    ad = xb[:, 4 * MIX_B + LORA:]
    logw = -(EXP_NEG_HALF * _sigmoid(w0 + _dot(jnp.tanh(wd), w2)))
    a = _sigmoid(a0 + _dot(ad, a2))
    kraw = kr * k_k
    kk = kraw * lax.rsqrt(_group_sum(kraw * kraw, ones_blk) + 1e-12)
    kmod = kr * (1.0 + (a - 1.0) * k_a)
    return r, kmod, vr, gate, logw, a, kk


def _rwkv_back(y, r, kmod, v, gate, rk, lnw, lnb, ones_blk):
    inv = 1.0 / HS_B
    yc = y - _group_sum(y, ones_blk) * inv
    var = _group_sum(yc * yc, ones_blk) * inv
    gn = yc * lax.rsqrt(var + GN_EPS) * lnw + lnb
    bonus = _group_sum(r * kmod * rk, ones_blk) * v
    return (gn + bonus) * _silu(gate)


def _rwkv_prompt_kernel(pb_ref, mu_ref, w0_ref, w2_ref, a0_ref, a2_ref, kk_ref, ka_ref,
                        rk_ref, lnw_ref, lnb_ref, ones_ref,
                        o_ref, s_out_ref, sh_out_ref, carry_ref, s_ref):
    c = pl.program_id(1)

    @pl.when(c == 0)
    def _():
        carry_ref[...] = jnp.zeros_like(carry_ref)
        s_ref[...] = jnp.zeros_like(s_ref)

    ones_blk = ones_ref[...]
    pb = pb_ref[...]
    rowi = _iota2((TC, 1), 0)
    prev = jnp.where(rowi == 0, carry_ref[0:1, :], pltpu.roll(pb, 1, 0))
    carry_ref[0:1, :] = pb[TC - 1:TC, :]
    sh_out_ref[...] = pb[TC - 1:TC, :]
    xb = pb + (prev - pb) * mu_ref[...]
    r, kmod, v, gate, logw, a, kk = _rwkv_front(
        xb, w0_ref[...], w2_ref[...], a0_ref[...], a2_ref[...], kk_ref[...], ka_ref[...], ones_blk)

    g = _chunk_cumsum(logw)
    eg = jnp.exp(g)
    egn = jnp.exp(-g)
    ag = -kk * jnp.exp(g - logw)
    kka = kk * a
    bg = kka * egn
    kg = kmod * egn
    rg = r * eg

    rowp = _iota2((CHUNK, PAIR), 0)
    lane = _iota2((CHUNK, PAIR), 1)
    colp = jnp.where(lane >= HS_B, lane - HS_B, lane)
    left = lane < HS_B
    incl = colp <= rowp
    strict = colp < rowp
    eye = colp == rowp
    zero = jnp.zeros((CHUNK, PAIR), F32)
    pack = lambda f: jnp.where(left, f[0:HS_B, :], f[HS_B:, :])

    nj = TC // CHUNK
    npair = HB // 2
    jp = [(j, p) for j in range(nj) for p in range(npair)]
    rows_of = lambda j: slice(j * CHUNK, (j + 1) * CHUNK)
    lanes_of = lambda p: slice(p * PAIR, (p + 1) * PAIR)
    glast = [g[(j + 1) * CHUNK - 1:(j + 1) * CHUNK, :] for j in range(nj)]
    eglast = [jnp.exp(glast[j]) for j in range(nj)]
    bdec = [bg[rows_of(j), :] * eglast[j] for j in range(nj)]
    kdec = [kg[rows_of(j), :] * eglast[j] for j in range(nj)]

    cut = lambda z: [z[rows_of(j), lanes_of(p)] for j, p in jp]
    ag_p, rg_p, bg_p, kg_p, v_p = cut(ag), cut(rg), cut(bg), cut(kg), cut(v)
    bdec_p = [bdec[j][:, lanes_of(p)] for j, p in jp]
    kdec_p = [kdec[j][:, lanes_of(p)] for j, p in jp]
    eglast_p = [eglast[j][:, lanes_of(p)] for j, p in jp]
    amat = [_dot(jnp.concatenate([a, b], axis=0),
                 jnp.concatenate([_pair_diag(c_), _pair_diag(d)], axis=0), NT)
            for a, b, c_, d in zip(ag_p, rg_p, bg_p, kg_p)]
    a_ab = [jnp.where(strict, m[0:CHUNK, 0:PAIR], 0.0) for m in amat]
    a_ak = [jnp.where(strict, m[0:CHUNK, PAIR:], 0.0) for m in amat]
    a_rb = [jnp.where(incl, m[CHUNK:, 0:PAIR], 0.0) for m in amat]
    a_rk = [jnp.where(incl, m[CHUNK:, PAIR:], 0.0) for m in amat]
    tinv = _tri_inv_pairs([-a for a in a_ab])
    kv = [_dot(jnp.concatenate([a, b], axis=0), _pair_diag(c_)) for a, b, c_ in zip(a_ak, a_rk, v_p)]
    wu = [_dot(t, jnp.concatenate([_pair_diag(a), _pair_diag(x[0:CHUNK, :])], axis=1))
          for t, a, x in zip(tinv, ag_p, kv)]
    rwu = [_dot(a, jnp.concatenate([_pair_diag(x[:, 0:PAIR]), _pair_diag(x[:, PAIR:])], axis=1))
           for a, x in zip(a_rb, wu)]
    mn = [_dot(jnp.concatenate([a, b], axis=0),
               jnp.concatenate([x, jnp.concatenate([zero, c_], axis=1)], axis=0), TN)
          for a, b, x, c_ in zip(bdec_p, kdec_p, wu, v_p)]
    lhs = [jnp.concatenate([jnp.where(eye, e, 0.0) + pack(m[:, 0:PAIR]), a + x[:, 0:PAIR]], axis=0)
           for e, m, a, x in zip(eglast_p, mn, rg_p, rwu)]

    y_rows = []
    for j in range(nj):
        idx = [j * npair + p for p in range(npair)]
        res = [_dot(lhs[i], _pair_diag(s_ref[p])) for p, i in enumerate(idx)]
        for p, i in enumerate(idx):
            s_ref[p] = res[p][0:HS_B, :] + pack(mn[i][:, PAIR:])
        y_rows.append(jnp.concatenate(
            [res[p][HS_B:, :] + rwu[i][:, PAIR:] + kv[i][CHUNK:, :] for p, i in enumerate(idx)],
            axis=-1))

    o_ref[...] = _rwkv_back(jnp.concatenate(y_rows, axis=0), r, kmod, v, gate,
                            rk_ref[...], lnw_ref[...], lnb_ref[...], ones_blk).astype(o_ref.dtype)
    for p in range(npair):
        st = s_ref[p].T
        s_out_ref[2 * p] = st[0:HS_B, :]
        s_out_ref[2 * p + 1] = st[HS_B:, :]


def _rwkv_prompt(pb, mu, w0, w2, a0, a2, k_k, k_a, rk, lnw, lnb, ones_blk):
    b, t, _ = pb.shape
    blk = lambda i, j: (i, j, 0)
    fixed = lambda i, j: (0, 0)
    vec = pl.BlockSpec((1, MIX_B), fixed)
    return pl.pallas_call(
        _rwkv_prompt_kernel,
        grid=(b, t // TC),
        in_specs=[pl.BlockSpec((None, TC, B_W), blk),
                  pl.BlockSpec((1, B_W), fixed),
                  vec, pl.BlockSpec((LORA, MIX_B), fixed),
                  vec, pl.BlockSpec((LORA, MIX_B), fixed),
                  vec, vec, vec, vec, vec,
                  pl.BlockSpec((MIX_B, MIX_B), fixed)],
        out_specs=[pl.BlockSpec((None, TC, MIX_B), blk),
                   pl.BlockSpec((None, HB, HS_B, HS_B), lambda i, j: (i, 0, 0, 0)),
                   pl.BlockSpec((None, 1, B_W), lambda i, j: (i, 0, 0))],
        out_shape=[jax.ShapeDtypeStruct((b, t, MIX_B), BF16),
                   jax.ShapeDtypeStruct((b, HB, HS_B, HS_B), F32),
                   jax.ShapeDtypeStruct((b, 1, B_W), F32)],
        scratch_shapes=[pltpu.VMEM((8, B_W), F32),
                        pltpu.VMEM((HB // 2, HS_B, PAIR), F32)],
        compiler_params=pltpu.CompilerParams(dimension_semantics=("arbitrary", "arbitrary"),
                                             vmem_limit_bytes=VMEM_LIMIT),
        name="rwkv_prompt",
    )(pb, mu, w0, w2, a0, a2, k_k, k_a, rk, lnw, lnb, ones_blk)


def _pad_rows(rows, width):
    return jnp.concatenate(rows + [jnp.zeros((8 - len(rows), width), F32)], axis=0)


def _gdn_step_kernel(pa_ref, bd_ref, sc_ref, s_ref, cw_ref, hp_ref, anw_ref,
                     o_ref, s_out_ref, c_out_ref):
    u = pa_ref[:, 0:A_QKV]
    cw = cw_ref[...]
    conv = (sc_ref[0] * cw[0:1, :] + sc_ref[1] * cw[1:2, :] + sc_ref[2] * cw[2:3, :] + u * cw[3:4, :])
    qkv = _silu(conv)
    c_out_ref[0] = sc_ref[1]
    c_out_ref[1] = sc_ref[2]
    c_out_ref[2] = u

    qn, kn = _gdn_qk(qkv)
    beta_all, g_all = _gdn_gates(bd_ref[...], hp_ref[...])
    eg_all = jnp.exp(g_all)
    lanes_of = lambda h: slice(h * DK_A, (h + 1) * DK_A)
    q = [qn[:, lanes_of(h)] for h in range(HA)]
    k = [kn[:, lanes_of(h)] for h in range(HA)]
    v = [qkv[:, 2 * MIX_A + h * DK_A:2 * MIX_A + (h + 1) * DK_A] for h in range(HA)]
    beta = [beta_all[:, h:h + 1] for h in range(HA)]
    eg = [eg_all[:, HA + h:HA + h + 1] for h in range(HA)]

    bh = [(b, h) for b in range(SB) for h in range(HA)]
    s0 = [s_ref[b, h] for b, h in bh]
    ks_qs = [_dot(_pad_rows([k[h][b:b + 1, :], q[h][b:b + 1, :]], DK_A), s) for (b, h), s in zip(bh, s0)]
    o_heads = []
    for h in range(HA):
        ks = jnp.concatenate([ks_qs[b * HA + h][0:1, :] for b in range(SB)], axis=0)
        qs = jnp.concatenate([ks_qs[b * HA + h][1:2, :] for b in range(SB)], axis=0)
        u2 = beta[h] * (v[h] - eg[h] * ks)
        for b in range(SB):
            s_out_ref[b, h] = (eg[h][b:b + 1, :] * s0[b * HA + h]
                               + _dot(_pad_rows([k[h][b:b + 1, :]], DK_A),
                                      _pad_rows([u2[b:b + 1, :]], DK_A), TN))
        o_heads.append(eg[h] * qs + jnp.sum(q[h] * k[h], axis=-1, keepdims=True) * u2)

    o_ref[...] = _gdn_out(jnp.concatenate(o_heads, axis=-1), pa_ref[:, A_QKV:A_MAIN], anw_ref[...])


def _gdn_step(pa, bd, sconv_t, s_a, conv_w, hp, anw):
    n = pa.shape[0]
    row = lambda i: (i, 0)
    fixed = lambda i: (0, 0)
    return pl.pallas_call(
        _gdn_step_kernel,
        grid=(n // SB,),
        in_specs=[pl.BlockSpec((SB, A_MAIN), row),
                  pl.BlockSpec((SB, BD_W), row),
                  pl.BlockSpec((CONV_W - 1, SB, A_QKV), lambda i: (0, i, 0)),
                  pl.BlockSpec((SB, HA, DK_A, DK_A), lambda i: (i, 0, 0, 0)),
                  pl.BlockSpec((CONV_W, A_QKV), fixed),
                  pl.BlockSpec((8, BD_W), fixed),
                  pl.BlockSpec((1, MIX_A), fixed)],
        out_specs=[pl.BlockSpec((SB, MIX_A), row),
                   pl.BlockSpec((SB, HA, DK_A, DK_A), lambda i: (i, 0, 0, 0)),
                   pl.BlockSpec((CONV_W - 1, SB, A_QKV), lambda i: (0, i, 0))],
        out_shape=[jax.ShapeDtypeStruct((n, MIX_A), F32),
                   jax.ShapeDtypeStruct((n, HA, DK_A, DK_A), F32),
                   jax.ShapeDtypeStruct((CONV_W - 1, n, A_QKV), F32)],
        compiler_params=pltpu.CompilerParams(dimension_semantics=("arbitrary",),
                                             vmem_limit_bytes=VMEM_LIMIT),
        name="gdn_step",
    )(pa, bd, sconv_t, s_a, conv_w, hp, anw)


def _rwkv_step_lanes_kernel(pb_ref, sh_ref, s_ref, mu_ref, w0_ref, w2_ref, a0_ref, a2_ref, kk_ref, ka_ref,
                            rk_ref, lnw_ref, lnb_ref, ones_ref,
                            o_ref, s_out_ref, vec_scr, row_scr, y_scr):
    h = pl.program_id(0)
    ones_blk = ones_ref[...]

    @pl.when(h == 0)
    def _():
        pb = pb_ref[...]
        xb = pb + (sh_ref[...] - pb) * mu_ref[...]
        r, kmod, v, gate, logw, a, kk = _rwkv_front(
            xb, w0_ref[...], w2_ref[...], a0_ref[...], a2_ref[...], kk_ref[...], ka_ref[...], ones_blk)
        for slot, z in enumerate((-kk, kk * a, kmod, v, r, jnp.exp(logw))):
            vec_scr[slot] = z.T
        for slot, z in enumerate((r, kmod, v, gate)):
            row_scr[slot] = z

    base = pl.multiple_of(h * HS_B, HS_B)
    chan = pl.ds(base, HS_B)
    nkk = vec_scr[0, chan, :]
    kka = vec_scr[1, chan, :]
    kmod_t = vec_scr[2, chan, :]
    r_t = vec_scr[4, chan, :]
    wdec = vec_scr[5, chan, :]

    def body(vi, carry):
        s0 = s_ref[0, vi]
        sa = jnp.sum(s0 * nkk, axis=0, keepdims=True)
        s1 = s0 * wdec + sa * kka + vec_scr[3, pl.ds(base + vi, 1), :] * kmod_t
        s_out_ref[0, vi] = s1
        y_scr[pl.ds(base + vi, 1), :] = jnp.sum(s1 * r_t, axis=0, keepdims=True)
        return carry

    lax.fori_loop(0, HS_B, body, 0, unroll=4)

    @pl.when(h == HB - 1)
    def _():
        o_ref[...] = _rwkv_back(y_scr[...].T, row_scr[0], row_scr[1], row_scr[2], row_scr[3],
                                rk_ref[...], lnw_ref[...], lnb_ref[...], ones_blk)


def _rwkv_step_lanes(pb, shift, s_hvkb, mu, w0, w2, a0, a2, k_k, k_a, rk, lnw, lnb, ones_blk):
    n = pb.shape[0]
    fixed = lambda i: (0, 0)
    vec = pl.BlockSpec((1, MIX_B), fixed)
    state = pl.BlockSpec((1, HS_B, HS_B, n), lambda i: (i, 0, 0, 0))
    return pl.pallas_call(
        _rwkv_step_lanes_kernel,
        grid=(HB,),
        in_specs=[pl.BlockSpec((n, B_W), fixed),
                  pl.BlockSpec((n, B_W), fixed),
                  state,
                  pl.BlockSpec((1, B_W), fixed),
                  vec, pl.BlockSpec((LORA, MIX_B), fixed),
                  vec, pl.BlockSpec((LORA, MIX_B), fixed),
                  vec, vec, vec, vec, vec,
                  pl.BlockSpec((MIX_B, MIX_B), fixed)],
        out_specs=[pl.BlockSpec((n, MIX_B), fixed), state],
        out_shape=[jax.ShapeDtypeStruct((n, MIX_B), F32),
                   jax.ShapeDtypeStruct((HB, HS_B, HS_B, n), F32)],
        scratch_shapes=[pltpu.VMEM((6, MIX_B, n), F32),
                        pltpu.VMEM((4, n, MIX_B), F32),
                        pltpu.VMEM((MIX_B, n), F32)],
        compiler_params=pltpu.CompilerParams(dimension_semantics=("arbitrary",),
                                             vmem_limit_bytes=VMEM_LIMIT),
        name="rwkv_step",
    )(pb, shift, s_hvkb, mu, w0, w2, a0, a2, k_k, k_a, rk, lnw, lnb, ones_blk)


def _block_ones(width, group):
    idx = jnp.arange(width) // group
    return (idx[:, None] == idx[None, :]).astype(BF16)


def kernel(x_prompt, x_sample, state_a_mat, state_a_conv, state_b_mat, state_b_shift, norm_w, w_in,
           conv_w, a_log, dt_bias, a_norm_w, mu, w0, w2, a0, a2, k_k, k_a, r_k, ln_w, ln_b, w_out,
           final_norm_w):
    bsz, seq, _ = x_prompt.shape
    nsmp = x_sample.shape[0]

    w_all = w_in[0].astype(BF16)
    w_o = w_out[0].astype(BF16)
    hp = jnp.zeros((8, BD_W), F32)
    hp = hp.at[0, HA:2 * HA].set(a_log[0]).at[1, HA:2 * HA].set(dt_bias[0])
    nw = norm_w[0].reshape(1, D_MODEL)
    fw = final_norm_w.reshape(1, D_MODEL)
    anw = jnp.tile(a_norm_w[0].reshape(1, DK_A), (1, HA))
    ones_b = _block_ones(MIX_B, HS_B)
    row = lambda z: z[0].reshape(1, -1)
    b_params = (row(mu), row(w0), w2[0], row(a0), a2[0], row(k_k), row(k_a), row(r_k), row(ln_w), row(ln_b),
                ones_b)

    xp = x_prompt.reshape(bsz * seq, D_MODEL)
    pa, pb, bd = _inproj(xp, nw, w_all)
    ob, p_bmat, p_bshift = _rwkv_prompt(pb.reshape(bsz, seq, B_W), *b_params)
    y_prompt, p_amat, p_aconv = _gdn_prompt(pa.reshape(bsz, seq, A_MAIN), bd.reshape(bsz, seq, BD_W),
                                            conv_w[0], hp, anw, x_prompt, ob, w_o, fw)

    xs = x_sample.reshape(nsmp, D_MODEL)
    sa_, sb_, sbd = _inproj(xs, nw, w_all)
    sconv_t = jnp.swapaxes(state_a_conv[0], 0, 1)
    soa, s_amat, s_aconv_t = _gdn_step(sa_, sbd, sconv_t, state_a_mat[0], conv_w[0], hp, anw)
    sob, s_bmat_t = _rwkv_step_lanes(sb_, state_b_shift[0], jnp.transpose(state_b_mat[0], (1, 2, 3, 0)),
                                     *b_params)
    s_bmat = jnp.transpose(s_bmat_t, (3, 0, 1, 2))
    y_sample = _outproj(xs, soa, sob, w_o, fw)

    return (y_prompt, y_sample.reshape(nsmp, 1, D_MODEL),
            p_amat[None], p_aconv[None], p_bmat[None], p_bshift.reshape(1, bsz, B_W),
            s_amat[None], jnp.swapaxes(s_aconv_t, 0, 1)[None], s_bmat[None], sb_[None])
```

```python
import jax
import jax.numpy as jnp
from jax import lax
from jax.experimental import pallas as pl
from jax.experimental.pallas import tpu as pltpu

F32 = jnp.float32
BF16 = jnp.bfloat16

D_MODEL = 1024
MIX_A = 512
MIX_B = 512
HA = 4
DK_A = 128
HB = 8
HS_B = 64
PAIR = 2 * HS_B
CONV_W = 4
LORA = 64
A_QKV = 3 * MIX_A
A_MAIN = A_QKV + MIX_A
B_W = 4 * MIX_B + 2 * LORA
BD_W = 128
RMS_EPS = 1e-6
GN_EPS = 64e-5
CHUNK = 64
TC = 256
TC_A = 256
NSEQ = 4
SB = 16
VMEM_LIMIT = 56 * 1024 * 1024

EXP_NEG_HALF = 0.6065306597126334
NN = (((1,), (0,)), ((), ()))
NT = (((1,), (1,)), ((), ()))
TN = (((0,), (0,)), ((), ()))


def _dot(a, b, dn=NN):
    return lax.dot_general(a, b, dn, preferred_element_type=F32)


def _sigmoid(x):
    return 0.5 * jnp.tanh(0.5 * x) + 0.5


def _silu(x):
    h = 0.5 * x
    return h * jnp.tanh(h) + h


def _softplus(x):
    return jnp.maximum(x, 0.0) + jnp.log(1.0 + jnp.exp(-jnp.abs(x)))


def _l2norm(x):
    return x * lax.rsqrt(jnp.sum(x * x, axis=-1, keepdims=True) + 1e-12)


def _group_sum(x, ones_blk):
    return _dot(x.astype(BF16), ones_blk)


def _iota2(shape, dim):
    return lax.broadcasted_iota(jnp.int32, shape, dim)


def _chunk_cumsum(x):
    n = x.shape[0]
    r = _iota2((n, n), 0)
    c = _iota2((n, n), 1)
    tril = jnp.where(((r // CHUNK) == (c // CHUNK)) & (c <= r), 1.0, 0.0).astype(BF16)
    hi = x.astype(BF16)
    rest = x - hi.astype(F32)
    mid = rest.astype(BF16)
    lo = (rest - mid.astype(F32)).astype(BF16)
    return _dot(tril, hi) + _dot(tril, mid) + _dot(tril, lo)


def _tri_inv_many(lows):
    n = lows[0].shape[0]
    r = _iota2((n, n), 0)
    c = _iota2((n, n), 1)
    eye = jnp.where(r == c, 1.0, 0.0).astype(F32)
    base = (r // 8) == (c // 8)
    l0 = [jnp.where(base, low, 0.0) for low in lows]
    l2 = [_dot(a, a) for a in l0]
    l4 = [_dot(a, a) for a in l2]
    d = [_dot(eye - a, eye + b) for a, b in zip(l0, l2)]
    d = [_dot(a, eye + b) for a, b in zip(d, l4)]
    s = 8
    while s < n:
        sel = ((r // (2 * s)) == (c // (2 * s))) & ((r // s) != (c // s))
        t = [_dot(jnp.where(sel, low, 0.0), a) for low, a in zip(lows, d)]
        d = [a - _dot(a, b) for a, b in zip(d, t)]
        s *= 2
    return d


def _pair_diag(x):
    left = _iota2(x.shape, 1) < x.shape[1] // 2
    return jnp.concatenate([jnp.where(left, x, 0.0), jnp.where(left, 0.0, x)], axis=0)


def _tri_inv_pairs(lows):
    n = lows[0].shape[0]
    r = _iota2((n, 2 * n), 0)
    lane = _iota2((n, 2 * n), 1)
    c = jnp.where(lane >= n, lane - n, lane)
    mul = lambda a, b: _dot(a, _pair_diag(b))
    eye = jnp.where(r == c, 1.0, 0.0).astype(F32)
    base = (r // 8) == (c // 8)
    l0 = [jnp.where(base, low, 0.0) for low in lows]
    l2 = [mul(a, a) for a in l0]
    l4 = [mul(a, a) for a in l2]
    d = [mul(eye - a, eye + b) for a, b in zip(l0, l2)]
    d = [mul(a, eye + b) for a, b in zip(d, l4)]
    s = 8
    while s < n:
        sel = ((r // (2 * s)) == (c // (2 * s))) & ((r // s) != (c // s))
        t = [mul(jnp.where(sel, low, 0.0), a) for low, a in zip(lows, d)]
        d = [a - mul(a, b) for a, b in zip(d, t)]
        s *= 2
    return d


def _inproj_kernel(x_ref, nw_ref, w_ref, oa_ref, ob_ref, obd_ref, wb_scr):
    a_w = A_MAIN + 2 * HA

    @pl.when(pl.program_id(0) == 0)
    def _():
        wb_scr[...] = jnp.concatenate(
            [w_ref[:, a_w:], w_ref[:, A_MAIN:a_w], jnp.zeros((D_MODEL, BD_W - 2 * HA), BF16)], axis=1)

    x = x_ref[...]
    xn = (x * lax.rsqrt(jnp.mean(x * x, axis=-1, keepdims=True) + RMS_EPS) * nw_ref[...]).astype(BF16)
    oa_ref[...] = _dot(xn, w_ref[:, 0:A_MAIN])
    pb = _dot(xn, wb_scr[...])
    ob_ref[...] = pb[:, :B_W]
    obd_ref[...] = pb[:, B_W:]


def _inproj(x2d, norm_w, w_all):
    n = x2d.shape[0]
    tm = min(512, n)
    row = lambda i: (i, 0)
    fixed = lambda i: (0, 0)
    return pl.pallas_call(
        _inproj_kernel,
        grid=(n // tm,),
        in_specs=[pl.BlockSpec((tm, D_MODEL), row),
                  pl.BlockSpec((1, D_MODEL), fixed),
                  pl.BlockSpec(w_all.shape, fixed)],
        out_specs=[pl.BlockSpec((tm, A_MAIN), row),
                   pl.BlockSpec((tm, B_W), row),
                   pl.BlockSpec((tm, BD_W), row)],
        out_shape=[jax.ShapeDtypeStruct((n, A_MAIN), F32),
                   jax.ShapeDtypeStruct((n, B_W), F32),
                   jax.ShapeDtypeStruct((n, BD_W), F32)],
        scratch_shapes=[pltpu.VMEM((D_MODEL, B_W + BD_W), BF16)],
        compiler_params=pltpu.CompilerParams(dimension_semantics=("arbitrary",),
                                             vmem_limit_bytes=VMEM_LIMIT),
        name="inproj",
    )(x2d, norm_w, w_all)


def _outproj_kernel(x_ref, ma_ref, mb_ref, w_ref, fw_ref, y_ref):
    h = (x_ref[...]
         + _dot(ma_ref[...].astype(BF16), w_ref[0:MIX_A, :])
         + _dot(mb_ref[...].astype(BF16), w_ref[MIX_A:, :]))
    y_ref[...] = h * lax.rsqrt(jnp.mean(h * h, axis=-1, keepdims=True) + RMS_EPS) * fw_ref[...]


def _outproj(x2d, mix_a, mix_b, w_out, final_w):
    n = x2d.shape[0]
    tm = min(1024, n)
    row = lambda i: (i, 0)
    fixed = lambda i: (0, 0)
    return pl.pallas_call(
        _outproj_kernel,
        grid=(n // tm,),
        in_specs=[pl.BlockSpec((tm, D_MODEL), row),
                  pl.BlockSpec((tm, MIX_A), row),
                  pl.BlockSpec((tm, MIX_B), row),
                  pl.BlockSpec((D_MODEL, D_MODEL), fixed),
                  pl.BlockSpec((1, D_MODEL), fixed)],
        out_specs=pl.BlockSpec((tm, D_MODEL), row),
        out_shape=jax.ShapeDtypeStruct((n, D_MODEL), F32),
        compiler_params=pltpu.CompilerParams(dimension_semantics=("arbitrary",),
                                             vmem_limit_bytes=VMEM_LIMIT),
        name="outproj",
    )(x2d, mix_a, mix_b, w_out, final_w)


def _gdn_gates(bd, hp):
    beta = _sigmoid(bd)
    g = -jnp.exp(hp[0:1, :]) * _softplus(bd + hp[1:2, :])
    return beta, g


def _gdn_qk(qkv):
    heads = lambda z: [z[:, h * DK_A:(h + 1) * DK_A] for h in range(HA)]
    qn = jnp.concatenate([_l2norm(x) * (DK_A ** -0.5) for x in heads(qkv[:, 0:MIX_A])], axis=-1)
    kn = jnp.concatenate([_l2norm(x) for x in heads(qkv[:, MIX_A:2 * MIX_A])], axis=-1)
    return qn, kn


def _gdn_out(o, gate, anw):
    on = jnp.concatenate(
        [x * lax.rsqrt(jnp.mean(x * x, axis=-1, keepdims=True) + RMS_EPS)
         for x in [o[:, h * DK_A:(h + 1) * DK_A] for h in range(HA)]], axis=-1)
    return on * anw * _silu(gate)


def _gdn_prompt_kernel(pa_ref, bd_ref, cw_ref, hp_ref, anw_ref, x_ref, mb_ref, wo_ref, fw_ref,
                       y_ref, s_out_ref, c_out_ref, ext_ref, s_ref):
    c = pl.program_id(1)

    @pl.when(c == 0)
    def _():
        ext_ref[0:8, :] = jnp.zeros((8, A_QKV), F32)
        s_ref[...] = jnp.zeros_like(s_ref)

    tc = pa_ref.shape[0]
    ext_ref[8:8 + tc, :] = pa_ref[:, 0:A_QKV]
    cw = cw_ref[...]
    conv = (ext_ref[5:5 + tc, :] * cw[0:1, :] + ext_ref[6:6 + tc, :] * cw[1:2, :]
            + ext_ref[7:7 + tc, :] * cw[2:3, :] + ext_ref[8:8 + tc, :] * cw[3:4, :])
    qkv = _silu(conv)
    c_out_ref[...] = ext_ref[tc + 5:tc + 8, :]
    ext_ref[0:8, :] = ext_ref[tc:tc + 8, :]

    qn, kn = _gdn_qk(qkv)
    beta_all, g_all = _gdn_gates(bd_ref[...], hp_ref[...])
    gcum = _chunk_cumsum(g_all)
    gcum_t = gcum.T

    r = _iota2((CHUNK, CHUNK), 0)
    cc = _iota2((CHUNK, CHUNK), 1)
    incl = cc <= r
    strict = cc < r

    nj = tc // CHUNK
    jh = [(j, h) for j in range(nj) for h in range(HA)]
    rows_of = lambda j: slice(j * CHUNK, (j + 1) * CHUNK)
    lanes_of = lambda h: slice(h * DK_A, (h + 1) * DK_A)
    q = [qn[rows_of(j), lanes_of(h)] for j, h in jh]
    k = [kn[rows_of(j), lanes_of(h)] for j, h in jh]
    v = [qkv[rows_of(j), 2 * MIX_A + h * DK_A:2 * MIX_A + (h + 1) * DK_A] for j, h in jh]
    beta = [beta_all[rows_of(j), h:h + 1] for j, h in jh]
    gc = [gcum[rows_of(j), HA + h:HA + h + 1] for j, h in jh]
    gr = [gcum_t[HA + h:HA + h + 1, rows_of(j)] for j, h in jh]
    glast = [gcum[(j + 1) * CHUNK - 1:(j + 1) * CHUNK, HA + h:HA + h + 1] for j, h in jh]
    decay = [jnp.where(incl, jnp.exp(jnp.where(incl, a - b, 0.0)), 0.0) for a, b in zip(gc, gr)]
    kq = [_dot(jnp.concatenate([a, b], axis=0), a, NT) for a, b in zip(k, q)]
    low = [jnp.where(strict, b * d * m[0:CHUNK, :], 0.0) for b, d, m in zip(beta, decay, kq)]
    tinv = _tri_inv_many(low)
    eg = [jnp.exp(a) for a in gc]
    wu = [_dot(t, jnp.concatenate([(b * e) * a, b * c_], axis=1))
          for t, b, e, a, c_ in zip(tinv, beta, eg, k, v)]
    kwu = [_dot(a * jnp.exp(gl - g), x, TN) for a, gl, g, x in zip(k, glast, gc, wu)]
    qwu = [_dot(d * m[CHUNK:, :], x) for d, m, x in zip(decay, kq, wu)]
    rd = _iota2((DK_A, DK_A), 0)
    cd = _iota2((DK_A, DK_A), 1)
    lhs = [jnp.concatenate([jnp.where(rd == cd, jnp.exp(gl), 0.0) - kx[:, 0:DK_A],
                            a * e - qx[:, 0:DK_A]], axis=0)
           for gl, kx, a, e, qx in zip(glast, kwu, q, eg, qwu)]

    o_rows = []
    for j in range(nj):
        idx = [j * HA + h for h in range(HA)]
        res = [_dot(lhs[i], s_ref[h]) for h, i in enumerate(idx)]
        for h, i in enumerate(idx):
            s_ref[h] = res[h][0:DK_A, :] + kwu[i][:, DK_A:]
        o_rows.append(jnp.concatenate(
            [res[h][DK_A:, :] + qwu[i][:, DK_A:] for h, i in enumerate(idx)], axis=-1))

    mix_a = _gdn_out(jnp.concatenate(o_rows, axis=0), pa_ref[:, A_QKV:A_MAIN], anw_ref[...])
    h = (x_ref[...] + _dot(mb_ref[...], wo_ref[MIX_A:, :])
         + _dot(mix_a.astype(BF16), wo_ref[0:MIX_A, :]))
    y_ref[...] = h * lax.rsqrt(jnp.mean(h * h, axis=-1, keepdims=True) + RMS_EPS) * fw_ref[...]
    s_out_ref[...] = s_ref[...]


def _gdn_prompt(pa, bd, conv_w, hp, anw, x, mix_b, w_out, final_w):
    b, t, _ = pa.shape
    blk = lambda i, j: (i, j, 0)
    fixed = lambda i, j: (0, 0)
    return pl.pallas_call(
        _gdn_prompt_kernel,
        grid=(b, t // TC_A),
        in_specs=[pl.BlockSpec((None, TC_A, A_MAIN), blk),
                  pl.BlockSpec((None, TC_A, BD_W), blk),
                  pl.BlockSpec((CONV_W, A_QKV), fixed),
                  pl.BlockSpec((8, BD_W), fixed),
                  pl.BlockSpec((1, MIX_A), fixed),
                  pl.BlockSpec((None, TC_A, D_MODEL), blk),
                  pl.BlockSpec((None, TC_A, MIX_B), blk),
                  pl.BlockSpec((D_MODEL, D_MODEL), fixed),
                  pl.BlockSpec((1, D_MODEL), fixed)],
        out_specs=[pl.BlockSpec((None, TC_A, D_MODEL), blk),
                   pl.BlockSpec((None, HA, DK_A, DK_A), lambda i, j: (i, 0, 0, 0)),
                   pl.BlockSpec((None, CONV_W - 1, A_QKV), lambda i, j: (i, 0, 0))],
        out_shape=[jax.ShapeDtypeStruct((b, t, D_MODEL), F32),
                   jax.ShapeDtypeStruct((b, HA, DK_A, DK_A), F32),
                   jax.ShapeDtypeStruct((b, CONV_W - 1, A_QKV), F32)],
        scratch_shapes=[pltpu.VMEM((TC_A + 8, A_QKV), F32),
                        pltpu.VMEM((HA, DK_A, DK_A), F32)],
        compiler_params=pltpu.CompilerParams(dimension_semantics=("arbitrary", "arbitrary"),
                                             vmem_limit_bytes=VMEM_LIMIT),
        name="gdn_prompt",
    )(pa, bd, conv_w, hp, anw, x, mix_b, w_out, final_w)


def _rwkv_front(xb, w0, w2, a0, a2, k_k, k_a, ones_blk):
    r = xb[:, 0:MIX_B]
    kr = xb[:, MIX_B:2 * MIX_B]
    vr = xb[:, 2 * MIX_B:3 * MIX_B]
    gate = xb[:, 3 * MIX_B:4 * MIX_B]
    wd = xb[:, 4 * MIX_B:4 * MIX_B + LORA]
    ad = xb[:, 4 * MIX_B + LORA:]
    logw = -(EXP_NEG_HALF * _sigmoid(w0 + _dot(jnp.tanh(wd), w2)))
    a = _sigmoid(a0 + _dot(ad, a2))
    kraw = kr * k_k
    kk = kraw * lax.rsqrt(_group_sum(kraw * kraw, ones_blk) + 1e-12)
    kmod = kr * (1.0 + (a - 1.0) * k_a)
    return r, kmod, vr, gate, logw, a, kk


def _rwkv_back(y, r, kmod, v, gate, rk, lnw, lnb, ones_blk):
    inv = 1.0 / HS_B
    yc = y - _group_sum(y, ones_blk) * inv
    var = _group_sum(yc * yc, ones_blk) * inv
    gn = yc * lax.rsqrt(var + GN_EPS) * lnw + lnb
    bonus = _group_sum(r * kmod * rk, ones_blk) * v
    return (gn + bonus) * _silu(gate)


def _rwkv_prompt_kernel(pb_ref, mu_ref, w0_ref, w2_ref, a0_ref, a2_ref, kk_ref, ka_ref,
                        rk_ref, lnw_ref, lnb_ref, ones_ref,
                        o_ref, s_out_ref, sh_out_ref, carry_ref, s_ref):
    c = pl.program_id(1)

    @pl.when(c == 0)
    def _():
        carry_ref[...] = jnp.zeros_like(carry_ref)
        s_ref[...] = jnp.zeros_like(s_ref)

    ones_blk = ones_ref[...]
    pb = pb_ref[...].reshape(NSEQ * TC, B_W)
    rowi = _iota2((NSEQ * TC, 1), 0)
    prev = pltpu.roll(pb, 1, 0)
    for s in range(NSEQ):
        prev = jnp.where(rowi == s * TC, carry_ref[s, 0:1, :], prev)
        last = pb[(s + 1) * TC - 1:(s + 1) * TC, :]
        carry_ref[s, 0:1, :] = last
        sh_out_ref[s] = last
    xb = pb + (prev - pb) * mu_ref[...]
    r, kmod, v, gate, logw, a, kk = _rwkv_front(
        xb, w0_ref[...], w2_ref[...], a0_ref[...], a2_ref[...], kk_ref[...], ka_ref[...], ones_blk)

    g = jnp.concatenate([_chunk_cumsum(logw[s * TC:(s + 1) * TC, :]) for s in range(NSEQ)],
                        axis=0)
    eg = jnp.exp(g)
    egn = jnp.exp(-g)
    ag = -kk * jnp.exp(g - logw)
    kka = kk * a
    bg = kka * egn
    kg = kmod * egn
    rg = r * eg

    rowp = _iota2((CHUNK, PAIR), 0)
    lane = _iota2((CHUNK, PAIR), 1)
    colp = jnp.where(lane >= HS_B, lane - HS_B, lane)
    left = lane < HS_B
    incl = colp <= rowp
    strict = colp < rowp
    eye = colp == rowp
    zero = jnp.zeros((CHUNK, PAIR), F32)
    pack = lambda f: jnp.where(left, f[0:HS_B, :], f[HS_B:, :])

    nj = NSEQ * TC // CHUNK
    npair = HB // 2
    jp = [(j, p) for j in range(nj) for p in range(npair)]
    rows_of = lambda j: slice(j * CHUNK, (j + 1) * CHUNK)
    lanes_of = lambda p: slice(p * PAIR, (p + 1) * PAIR)
    glast = [g[(j + 1) * CHUNK - 1:(j + 1) * CHUNK, :] for j in range(nj)]
    eglast = [jnp.exp(glast[j]) for j in range(nj)]
    bdec = [bg[rows_of(j), :] * eglast[j] for j in range(nj)]
    kdec = [kg[rows_of(j), :] * eglast[j] for j in range(nj)]

    cut = lambda z: [z[rows_of(j), lanes_of(p)] for j, p in jp]
    ag_p, rg_p, bg_p, kg_p, v_p = cut(ag), cut(rg), cut(bg), cut(kg), cut(v)
    bdec_p = [bdec[j][:, lanes_of(p)] for j, p in jp]
    kdec_p = [kdec[j][:, lanes_of(p)] for j, p in jp]
    eglast_p = [eglast[j][:, lanes_of(p)] for j, p in jp]
    amat = [_dot(jnp.concatenate([a, b], axis=0),
                 jnp.concatenate([_pair_diag(c_), _pair_diag(d)], axis=0), NT)
            for a, b, c_, d in zip(ag_p, rg_p, bg_p, kg_p)]
    a_ab = [jnp.where(strict, m[0:CHUNK, 0:PAIR], 0.0) for m in amat]
    a_ak = [jnp.where(strict, m[0:CHUNK, PAIR:], 0.0) for m in amat]
    a_rb = [jnp.where(incl, m[CHUNK:, 0:PAIR], 0.0) for m in amat]
    a_rk = [jnp.where(incl, m[CHUNK:, PAIR:], 0.0) for m in amat]
    tinv = _tri_inv_pairs([-a for a in a_ab])
    kv = [_dot(jnp.concatenate([a, b], axis=0), _pair_diag(c_)) for a, b, c_ in zip(a_ak, a_rk, v_p)]
    wu = [_dot(t, jnp.concatenate([_pair_diag(a), _pair_diag(x[0:CHUNK, :])], axis=1))
          for t, a, x in zip(tinv, ag_p, kv)]
    rwu = [_dot(a, jnp.concatenate([_pair_diag(x[:, 0:PAIR]), _pair_diag(x[:, PAIR:])], axis=1))
           for a, x in zip(a_rb, wu)]
    mn = [_dot(jnp.concatenate([a, b], axis=0),
               jnp.concatenate([x, jnp.concatenate([zero, c_], axis=1)], axis=0), TN)
          for a, b, x, c_ in zip(bdec_p, kdec_p, wu, v_p)]
    lhs = [jnp.concatenate([jnp.where(eye, e, 0.0) + pack(m[:, 0:PAIR]), a + x[:, 0:PAIR]], axis=0)
           for e, m, a, x in zip(eglast_p, mn, rg_p, rwu)]

    y_rows = [None] * nj
    for jj in range(TC // CHUNK):
        for s in range(NSEQ):
            j = s * (TC // CHUNK) + jj
            idx = [j * npair + p for p in range(npair)]
            res = [_dot(lhs[i], _pair_diag(s_ref[s, p])) for p, i in enumerate(idx)]
            for p, i in enumerate(idx):
                s_ref[s, p] = res[p][0:HS_B, :] + pack(mn[i][:, PAIR:])
            y_rows[j] = jnp.concatenate(
                [res[p][HS_B:, :] + rwu[i][:, PAIR:] + kv[i][CHUNK:, :] for p, i in enumerate(idx)],
                axis=-1)

    out = _rwkv_back(jnp.concatenate(y_rows, axis=0), r, kmod, v, gate,
                     rk_ref[...], lnw_ref[...], lnb_ref[...], ones_blk)
    o_ref[...] = out.reshape(NSEQ, TC, MIX_B).astype(o_ref.dtype)
    for s in range(NSEQ):
        for p in range(npair):
            st = s_ref[s, p].T
            s_out_ref[s, 2 * p] = st[0:HS_B, :]
            s_out_ref[s, 2 * p + 1] = st[HS_B:, :]


def _rwkv_prompt(pb, mu, w0, w2, a0, a2, k_k, k_a, rk, lnw, lnb, ones_blk):
    b, t, _ = pb.shape
    blk = lambda i, j: (i, j, 0)
    fixed = lambda i, j: (0, 0)
    vec = pl.BlockSpec((1, MIX_B), fixed)
    return pl.pallas_call(
        _rwkv_prompt_kernel,
        grid=(b // NSEQ, t // TC),
        in_specs=[pl.BlockSpec((NSEQ, TC, B_W), blk),
                  pl.BlockSpec((1, B_W), fixed),
                  vec, pl.BlockSpec((LORA, MIX_B), fixed),
                  vec, pl.BlockSpec((LORA, MIX_B), fixed),
                  vec, vec, vec, vec, vec,
                  pl.BlockSpec((MIX_B, MIX_B), fixed)],
        out_specs=[pl.BlockSpec((NSEQ, TC, MIX_B), blk),
                   pl.BlockSpec((NSEQ, HB, HS_B, HS_B), lambda i, j: (i, 0, 0, 0)),
                   pl.BlockSpec((NSEQ, 1, B_W), lambda i, j: (i, 0, 0))],
        out_shape=[jax.ShapeDtypeStruct((b, t, MIX_B), BF16),
                   jax.ShapeDtypeStruct((b, HB, HS_B, HS_B), F32),
                   jax.ShapeDtypeStruct((b, 1, B_W), F32)],
        scratch_shapes=[pltpu.VMEM((NSEQ, 8, B_W), F32),
                        pltpu.VMEM((NSEQ, HB // 2, HS_B, PAIR), F32)],
        compiler_params=pltpu.CompilerParams(dimension_semantics=("arbitrary", "arbitrary"),
                                             vmem_limit_bytes=VMEM_LIMIT),
        name="rwkv_prompt",
    )(pb, mu, w0, w2, a0, a2, k_k, k_a, rk, lnw, lnb, ones_blk)


def _pad_rows(rows, width):
    return jnp.concatenate(rows + [jnp.zeros((8 - len(rows), width), F32)], axis=0)


def _gdn_step_kernel(pa_ref, bd_ref, sc_ref, s_ref, cw_ref, hp_ref, anw_ref,
                     o_ref, s_out_ref, c_out_ref):
    u = pa_ref[:, 0:A_QKV]
    cw = cw_ref[...]
    conv = (sc_ref[0] * cw[0:1, :] + sc_ref[1] * cw[1:2, :] + sc_ref[2] * cw[2:3, :] + u * cw[3:4, :])
    qkv = _silu(conv)
    c_out_ref[0] = sc_ref[1]
    c_out_ref[1] = sc_ref[2]
    c_out_ref[2] = u

    qn, kn = _gdn_qk(qkv)
    beta_all, g_all = _gdn_gates(bd_ref[...], hp_ref[...])
    eg_all = jnp.exp(g_all)
    lanes_of = lambda h: slice(h * DK_A, (h + 1) * DK_A)
    q = [qn[:, lanes_of(h)] for h in range(HA)]
    k = [kn[:, lanes_of(h)] for h in range(HA)]
    v = [qkv[:, 2 * MIX_A + h * DK_A:2 * MIX_A + (h + 1) * DK_A] for h in range(HA)]
    beta = [beta_all[:, h:h + 1] for h in range(HA)]
    eg = [eg_all[:, HA + h:HA + h + 1] for h in range(HA)]

    bh = [(b, h) for b in range(SB) for h in range(HA)]
    s0 = [s_ref[b, h] for b, h in bh]
    ks_qs = [_dot(_pad_rows([k[h][b:b + 1, :], q[h][b:b + 1, :]], DK_A), s) for (b, h), s in zip(bh, s0)]
    o_heads = []
    for h in range(HA):
        ks = jnp.concatenate([ks_qs[b * HA + h][0:1, :] for b in range(SB)], axis=0)
        qs = jnp.concatenate([ks_qs[b * HA + h][1:2, :] for b in range(SB)], axis=0)
        u2 = beta[h] * (v[h] - eg[h] * ks)
        for b in range(SB):
            s_out_ref[b, h] = (eg[h][b:b + 1, :] * s0[b * HA + h]
                               + _dot(_pad_rows([k[h][b:b + 1, :]], DK_A),
                                      _pad_rows([u2[b:b + 1, :]], DK_A), TN))
        o_heads.append(eg[h] * qs + jnp.sum(q[h] * k[h], axis=-1, keepdims=True) * u2)

    o_ref[...] = _gdn_out(jnp.concatenate(o_heads, axis=-1), pa_ref[:, A_QKV:A_MAIN], anw_ref[...])


def _gdn_step(pa, bd, sconv_t, s_a, conv_w, hp, anw):
    n = pa.shape[0]
    row = lambda i: (i, 0)
    fixed = lambda i: (0, 0)
    return pl.pallas_call(
        _gdn_step_kernel,
        grid=(n // SB,),
        in_specs=[pl.BlockSpec((SB, A_MAIN), row),
                  pl.BlockSpec((SB, BD_W), row),
                  pl.BlockSpec((CONV_W - 1, SB, A_QKV), lambda i: (0, i, 0)),
                  pl.BlockSpec((SB, HA, DK_A, DK_A), lambda i: (i, 0, 0, 0)),
                  pl.BlockSpec((CONV_W, A_QKV), fixed),
                  pl.BlockSpec((8, BD_W), fixed),
                  pl.BlockSpec((1, MIX_A), fixed)],
        out_specs=[pl.BlockSpec((SB, MIX_A), row),
                   pl.BlockSpec((SB, HA, DK_A, DK_A), lambda i: (i, 0, 0, 0)),
                   pl.BlockSpec((CONV_W - 1, SB, A_QKV), lambda i: (0, i, 0))],
        out_shape=[jax.ShapeDtypeStruct((n, MIX_A), F32),
                   jax.ShapeDtypeStruct((n, HA, DK_A, DK_A), F32),
                   jax.ShapeDtypeStruct((CONV_W - 1, n, A_QKV), F32)],
        compiler_params=pltpu.CompilerParams(dimension_semantics=("arbitrary",),
                                             vmem_limit_bytes=VMEM_LIMIT),
        name="gdn_step",
    )(pa, bd, sconv_t, s_a, conv_w, hp, anw)


def _rwkv_step_lanes_kernel(pb_ref, sh_ref, s_ref, mu_ref, w0_ref, w2_ref, a0_ref, a2_ref, kk_ref, ka_ref,
                            rk_ref, lnw_ref, lnb_ref, ones_ref,
                            o_ref, s_out_ref, vec_scr, row_scr, y_scr):
    h = pl.program_id(0)
    ones_blk = ones_ref[...]

    @pl.when(h == 0)
    def _():
        pb = pb_ref[...]
        xb = pb + (sh_ref[...] - pb) * mu_ref[...]
        r, kmod, v, gate, logw, a, kk = _rwkv_front(
            xb, w0_ref[...], w2_ref[...], a0_ref[...], a2_ref[...], kk_ref[...], ka_ref[...], ones_blk)
        for slot, z in enumerate((-kk, kk * a, kmod, v, r, jnp.exp(logw))):
            vec_scr[slot] = z.T
        for slot, z in enumerate((r, kmod, v, gate)):
            row_scr[slot] = z

    base = pl.multiple_of(h * HS_B, HS_B)
    chan = pl.ds(base, HS_B)
    nkk = vec_scr[0, chan, :]
    kka = vec_scr[1, chan, :]
    kmod_t = vec_scr[2, chan, :]
    r_t = vec_scr[4, chan, :]
    wdec = vec_scr[5, chan, :]

    def body(vi, carry):
        s0 = s_ref[0, vi]
        sa = jnp.sum(s0 * nkk, axis=0, keepdims=True)
        s1 = s0 * wdec + sa * kka + vec_scr[3, pl.ds(base + vi, 1), :] * kmod_t
        s_out_ref[0, vi] = s1
        y_scr[pl.ds(base + vi, 1), :] = jnp.sum(s1 * r_t, axis=0, keepdims=True)
        return carry

    lax.fori_loop(0, HS_B, body, 0, unroll=4)

    @pl.when(h == HB - 1)
    def _():
        o_ref[...] = _rwkv_back(y_scr[...].T, row_scr[0], row_scr[1], row_scr[2], row_scr[3],
                                rk_ref[...], lnw_ref[...], lnb_ref[...], ones_blk)


def _rwkv_step_lanes(pb, shift, s_hvkb, mu, w0, w2, a0, a2, k_k, k_a, rk, lnw, lnb, ones_blk):
    n = pb.shape[0]
    fixed = lambda i: (0, 0)
    vec = pl.BlockSpec((1, MIX_B), fixed)
    state = pl.BlockSpec((1, HS_B, HS_B, n), lambda i: (i, 0, 0, 0))
    return pl.pallas_call(
        _rwkv_step_lanes_kernel,
        grid=(HB,),
        in_specs=[pl.BlockSpec((n, B_W), fixed),
                  pl.BlockSpec((n, B_W), fixed),
                  state,
                  pl.BlockSpec((1, B_W), fixed),
                  vec, pl.BlockSpec((LORA, MIX_B), fixed),
                  vec, pl.BlockSpec((LORA, MIX_B), fixed),
                  vec, vec, vec, vec, vec,
                  pl.BlockSpec((MIX_B, MIX_B), fixed)],
        out_specs=[pl.BlockSpec((n, MIX_B), fixed), state],
        out_shape=[jax.ShapeDtypeStruct((n, MIX_B), F32),
                   jax.ShapeDtypeStruct((HB, HS_B, HS_B, n), F32)],
        scratch_shapes=[pltpu.VMEM((6, MIX_B, n), F32),
                        pltpu.VMEM((4, n, MIX_B), F32),
                        pltpu.VMEM((MIX_B, n), F32)],
        compiler_params=pltpu.CompilerParams(dimension_semantics=("arbitrary",),
                                             vmem_limit_bytes=VMEM_LIMIT),
        name="rwkv_step",
    )(pb, shift, s_hvkb, mu, w0, w2, a0, a2, k_k, k_a, rk, lnw, lnb, ones_blk)


def _block_ones(width, group):
    idx = jnp.arange(width) // group
    return (idx[:, None] == idx[None, :]).astype(BF16)


def kernel(x_prompt, x_sample, state_a_mat, state_a_conv, state_b_mat, state_b_shift, norm_w, w_in,
           conv_w, a_log, dt_bias, a_norm_w, mu, w0, w2, a0, a2, k_k, k_a, r_k, ln_w, ln_b, w_out,
           final_norm_w):
    bsz, seq, _ = x_prompt.shape
    nsmp = x_sample.shape[0]

    w_all = w_in[0].astype(BF16)
    w_o = w_out[0].astype(BF16)
    hp = jnp.pad(jnp.concatenate([a_log, dt_bias], axis=0), ((0, 6), (HA, BD_W - 2 * HA)))
    nw = norm_w[0].reshape(1, D_MODEL)
    fw = final_norm_w.reshape(1, D_MODEL)
    anw = jnp.tile(a_norm_w[0].reshape(1, DK_A), (1, HA))
    ones_b = _block_ones(MIX_B, HS_B)
    row = lambda z: z[0].reshape(1, -1)
    b_params = (row(mu), row(w0), w2[0], row(a0), a2[0], row(k_k), row(k_a), row(r_k), row(ln_w), row(ln_b),
                ones_b)

    xp = x_prompt.reshape(bsz * seq, D_MODEL)
    pa, pb, bd = _inproj(xp, nw, w_all)
    ob, p_bmat, p_bshift = _rwkv_prompt(pb.reshape(bsz, seq, B_W), *b_params)
    y_prompt, p_amat, p_aconv = _gdn_prompt(pa.reshape(bsz, seq, A_MAIN), bd.reshape(bsz, seq, BD_W),
                                            conv_w[0], hp, anw, x_prompt, ob, w_o, fw)

    xs = x_sample.reshape(nsmp, D_MODEL)
    sa_, sb_, sbd = _inproj(xs, nw, w_all)
    sconv_t = jnp.swapaxes(state_a_conv[0], 0, 1)
    soa, s_amat, s_aconv_t = _gdn_step(sa_, sbd, sconv_t, state_a_mat[0], conv_w[0], hp, anw)
    sob, s_bmat_t = _rwkv_step_lanes(sb_, state_b_shift[0], jnp.transpose(state_b_mat[0], (1, 2, 3, 0)),
                                     *b_params)
    s_bmat = jnp.transpose(s_bmat_t, (3, 0, 1, 2))
    y_sample = _outproj(xs, soa, sob, w_o, fw)

    return (y_prompt, y_sample.reshape(nsmp, 1, D_MODEL),
            p_amat[None], p_aconv[None], p_bmat[None], p_bshift.reshape(1, bsz, B_W),
            s_amat[None], jnp.swapaxes(s_aconv_t, 0, 1)[None], s_bmat[None], sb_[None])
```

```python
import jax
import jax.numpy as jnp
from jax import lax
from jax.experimental import pallas as pl
from jax.experimental.pallas import tpu as pltpu

F32 = jnp.float32
BF16 = jnp.bfloat16

D_MODEL = 1024
MIX_A = 512
MIX_B = 512
HA = 4
DK_A = 128
HB = 8
HS_B = 64
PAIR = 2 * HS_B
CONV_W = 4
LORA = 64
A_QKV = 3 * MIX_A
A_MAIN = A_QKV + MIX_A
B_W = 4 * MIX_B + 2 * LORA
BD_W = 128
RMS_EPS = 1e-6
GN_EPS = 64e-5
CHUNK = 64
TC = 256
TC_A = 256
NSEQ_A = 2
NSEQ = 4
SB = 16
VMEM_LIMIT = 56 * 1024 * 1024

EXP_NEG_HALF = 0.6065306597126334
NN = (((1,), (0,)), ((), ()))
NT = (((1,), (1,)), ((), ()))
TN = (((0,), (0,)), ((), ()))


def _dot(a, b, dn=NN):
    return lax.dot_general(a, b, dn, preferred_element_type=F32)


def _sigmoid(x):
    return 0.5 * jnp.tanh(0.5 * x) + 0.5


def _silu(x):
    h = 0.5 * x
    return h * jnp.tanh(h) + h


def _softplus(x):
    return jnp.maximum(x, 0.0) + jnp.log(1.0 + jnp.exp(-jnp.abs(x)))


def _l2norm(x):
    return x * lax.rsqrt(jnp.sum(x * x, axis=-1, keepdims=True) + 1e-12)


def _group_sum(x, ones_blk):
    return _dot(x.astype(BF16), ones_blk)


def _iota2(shape, dim):
    return lax.broadcasted_iota(jnp.int32, shape, dim)


def _chunk_cumsum(x):
    n = x.shape[0]
    r = _iota2((n, n), 0)
    c = _iota2((n, n), 1)
    tril = jnp.where(((r // CHUNK) == (c // CHUNK)) & (c <= r), 1.0, 0.0).astype(BF16)
    hi = x.astype(BF16)
    rest = x - hi.astype(F32)
    mid = rest.astype(BF16)
    lo = (rest - mid.astype(F32)).astype(BF16)
    return _dot(tril, hi) + _dot(tril, mid) + _dot(tril, lo)


def _tri_inv_many(lows):
    n = lows[0].shape[0]
    r = _iota2((n, n), 0)
    c = _iota2((n, n), 1)
    eye = jnp.where(r == c, 1.0, 0.0).astype(F32)
    base = (r // 8) == (c // 8)
    l0 = [jnp.where(base, low, 0.0) for low in lows]
    l2 = [_dot(a, a) for a in l0]
    l4 = [_dot(a, a) for a in l2]
    d = [_dot(eye - a, eye + b) for a, b in zip(l0, l2)]
    d = [_dot(a, eye + b) for a, b in zip(d, l4)]
    s = 8
    while s < n:
        sel = ((r // (2 * s)) == (c // (2 * s))) & ((r // s) != (c // s))
        t = [_dot(jnp.where(sel, low, 0.0), a) for low, a in zip(lows, d)]
        d = [a - _dot(a, b) for a, b in zip(d, t)]
        s *= 2
    return d


def _pair_diag(x):
    left = _iota2(x.shape, 1) < x.shape[1] // 2
    return jnp.concatenate([jnp.where(left, x, 0.0), jnp.where(left, 0.0, x)], axis=0)


def _tri_inv_pairs(lows):
    n = lows[0].shape[0]
    r = _iota2((n, 2 * n), 0)
    lane = _iota2((n, 2 * n), 1)
    c = jnp.where(lane >= n, lane - n, lane)
    mul = lambda a, b: _dot(a, _pair_diag(b))
    eye = jnp.where(r == c, 1.0, 0.0).astype(F32)
    base = (r // 8) == (c // 8)
    l0 = [jnp.where(base, low, 0.0) for low in lows]
    l2 = [mul(a, a) for a in l0]
    l4 = [mul(a, a) for a in l2]
    d = [mul(eye - a, eye + b) for a, b in zip(l0, l2)]
    d = [mul(a, eye + b) for a, b in zip(d, l4)]
    s = 8
    while s < n:
        sel = ((r // (2 * s)) == (c // (2 * s))) & ((r // s) != (c // s))
        t = [mul(jnp.where(sel, low, 0.0), a) for low, a in zip(lows, d)]
        d = [a - mul(a, b) for a, b in zip(d, t)]
        s *= 2
    return d


def _inproj_kernel(x_ref, nw_ref, w_ref, oa_ref, ob_ref, obd_ref, wb_scr):
    a_w = A_MAIN + 2 * HA

    @pl.when(pl.program_id(0) == 0)
    def _():
        wb_scr[...] = jnp.concatenate(
            [w_ref[:, a_w:], w_ref[:, A_MAIN:a_w], jnp.zeros((D_MODEL, BD_W - 2 * HA), BF16)], axis=1)

    x = x_ref[...]
    xn = (x * lax.rsqrt(jnp.mean(x * x, axis=-1, keepdims=True) + RMS_EPS) * nw_ref[...]).astype(BF16)
    oa_ref[...] = _dot(xn, w_ref[:, 0:A_MAIN])
    pb = _dot(xn, wb_scr[...])
    ob_ref[...] = pb[:, :B_W]
    obd_ref[...] = pb[:, B_W:]


def _inproj(x2d, norm_w, w_all):
    n = x2d.shape[0]
    tm = min(512, n)
    row = lambda i: (i, 0)
    fixed = lambda i: (0, 0)
    return pl.pallas_call(
        _inproj_kernel,
        grid=(n // tm,),
        in_specs=[pl.BlockSpec((tm, D_MODEL), row),
                  pl.BlockSpec((1, D_MODEL), fixed),
                  pl.BlockSpec(w_all.shape, fixed)],
        out_specs=[pl.BlockSpec((tm, A_MAIN), row),
                   pl.BlockSpec((tm, B_W), row),
                   pl.BlockSpec((tm, BD_W), row)],
        out_shape=[jax.ShapeDtypeStruct((n, A_MAIN), F32),
                   jax.ShapeDtypeStruct((n, B_W), F32),
                   jax.ShapeDtypeStruct((n, BD_W), F32)],
        scratch_shapes=[pltpu.VMEM((D_MODEL, B_W + BD_W), BF16)],
        compiler_params=pltpu.CompilerParams(dimension_semantics=("arbitrary",),
                                             vmem_limit_bytes=VMEM_LIMIT),
        name="inproj",
    )(x2d, norm_w, w_all)


def _outproj_kernel(x_ref, ma_ref, mb_ref, w_ref, fw_ref, y_ref):
    h = (x_ref[...]
         + _dot(ma_ref[...].astype(BF16), w_ref[0:MIX_A, :])
         + _dot(mb_ref[...].astype(BF16), w_ref[MIX_A:, :]))
    y_ref[...] = h * lax.rsqrt(jnp.mean(h * h, axis=-1, keepdims=True) + RMS_EPS) * fw_ref[...]


def _outproj(x2d, mix_a, mix_b, w_out, final_w):
    n = x2d.shape[0]
    tm = min(1024, n)
    row = lambda i: (i, 0)
    fixed = lambda i: (0, 0)
    return pl.pallas_call(
        _outproj_kernel,
        grid=(n // tm,),
        in_specs=[pl.BlockSpec((tm, D_MODEL), row),
                  pl.BlockSpec((tm, MIX_A), row),
                  pl.BlockSpec((tm, MIX_B), row),
                  pl.BlockSpec((D_MODEL, D_MODEL), fixed),
                  pl.BlockSpec((1, D_MODEL), fixed)],
        out_specs=pl.BlockSpec((tm, D_MODEL), row),
        out_shape=jax.ShapeDtypeStruct((n, D_MODEL), F32),
        compiler_params=pltpu.CompilerParams(dimension_semantics=("arbitrary",),
                                             vmem_limit_bytes=VMEM_LIMIT),
        name="outproj",
    )(x2d, mix_a, mix_b, w_out, final_w)


def _gdn_gates(bd, hp):
    beta = _sigmoid(bd)
    g = -jnp.exp(hp[0:1, :]) * _softplus(bd + hp[1:2, :])
    return beta, g


def _gdn_qk(qkv):
    heads = lambda z: [z[:, h * DK_A:(h + 1) * DK_A] for h in range(HA)]
    qn = jnp.concatenate([_l2norm(x) * (DK_A ** -0.5) for x in heads(qkv[:, 0:MIX_A])], axis=-1)
    kn = jnp.concatenate([_l2norm(x) for x in heads(qkv[:, MIX_A:2 * MIX_A])], axis=-1)
    return qn, kn


def _gdn_out(o, gate, anw):
    on = jnp.concatenate(
        [x * lax.rsqrt(jnp.mean(x * x, axis=-1, keepdims=True) + RMS_EPS)
         for x in [o[:, h * DK_A:(h + 1) * DK_A] for h in range(HA)]], axis=-1)
    return on * anw * _silu(gate)


def _gdn_prompt_kernel(pa_ref, bd_ref, cw_ref, hp_ref, anw_ref, x_ref, mb_ref, wo_ref, fw_ref,
                       y_ref, s_out_ref, c_out_ref, ext_ref, s_ref):
    c = pl.program_id(1)

    nseq, tc = pa_ref.shape[0], pa_ref.shape[1]

    @pl.when(c == 0)
    def _():
        for s in range(nseq):
            ext_ref[s, 0:8, :] = jnp.zeros((8, A_QKV), F32)
        s_ref[...] = jnp.zeros_like(s_ref)

    cw = cw_ref[...]
    conv = []
    for s in range(nseq):
        ext = ext_ref.at[s]
        ext[8:8 + tc, :] = pa_ref[s, :, 0:A_QKV]
        conv.append(ext[5:5 + tc, :] * cw[0:1, :] + ext[6:6 + tc, :] * cw[1:2, :]
                    + ext[7:7 + tc, :] * cw[2:3, :] + ext[8:8 + tc, :] * cw[3:4, :])
        c_out_ref[s] = ext[tc + 5:tc + 8, :]
        ext[0:8, :] = ext[tc:tc + 8, :]
    qkv = _silu(jnp.concatenate(conv, axis=0))

    qn, kn = _gdn_qk(qkv)
    beta_all, g_all = _gdn_gates(bd_ref[...].reshape(nseq * tc, BD_W), hp_ref[...])
    gcum = jnp.concatenate([_chunk_cumsum(g_all[s * tc:(s + 1) * tc, :]) for s in range(nseq)],
                           axis=0)
    gcum_t = gcum.T

    r = _iota2((CHUNK, CHUNK), 0)
    cc = _iota2((CHUNK, CHUNK), 1)
    incl = cc <= r
    strict = cc < r

    nj = nseq * tc // CHUNK
    jh = [(j, h) for j in range(nj) for h in range(HA)]
    rows_of = lambda j: slice(j * CHUNK, (j + 1) * CHUNK)
    lanes_of = lambda h: slice(h * DK_A, (h + 1) * DK_A)
    q = [qn[rows_of(j), lanes_of(h)] for j, h in jh]
    k = [kn[rows_of(j), lanes_of(h)] for j, h in jh]
    v = [qkv[rows_of(j), 2 * MIX_A + h * DK_A:2 * MIX_A + (h + 1) * DK_A] for j, h in jh]
    beta = [beta_all[rows_of(j), h:h + 1] for j, h in jh]
    gc = [gcum[rows_of(j), HA + h:HA + h + 1] for j, h in jh]
    gr = [gcum_t[HA + h:HA + h + 1, rows_of(j)] for j, h in jh]
    glast = [gcum[(j + 1) * CHUNK - 1:(j + 1) * CHUNK, HA + h:HA + h + 1] for j, h in jh]
    decay = [jnp.where(incl, jnp.exp(jnp.where(incl, a - b, 0.0)), 0.0) for a, b in zip(gc, gr)]
    kq = [_dot(jnp.concatenate([a, b], axis=0), a, NT) for a, b in zip(k, q)]
    low = [jnp.where(strict, b * d * m[0:CHUNK, :], 0.0) for b, d, m in zip(beta, decay, kq)]
    tinv = _tri_inv_many(low)
    eg = [jnp.exp(a) for a in gc]
    wu = [_dot(t, jnp.concatenate([(b * e) * a, b * c_], axis=1))
          for t, b, e, a, c_ in zip(tinv, beta, eg, k, v)]
    kwu = [_dot(a * jnp.exp(gl - g), x, TN) for a, gl, g, x in zip(k, glast, gc, wu)]
    qwu = [_dot(d * m[CHUNK:, :], x) for d, m, x in zip(decay, kq, wu)]
    rd = _iota2((DK_A, DK_A), 0)
    cd = _iota2((DK_A, DK_A), 1)
    lhs = [jnp.concatenate([jnp.where(rd == cd, jnp.exp(gl), 0.0) - kx[:, 0:DK_A],
                            a * e - qx[:, 0:DK_A]], axis=0)
           for gl, kx, a, e, qx in zip(glast, kwu, q, eg, qwu)]

    o_rows = [None] * nj
    for jj in range(tc // CHUNK):
        for s in range(nseq):
            j = s * (tc // CHUNK) + jj
            idx = [j * HA + h for h in range(HA)]
            res = [_dot(lhs[i], s_ref[s, h]) for h, i in enumerate(idx)]
            for h, i in enumerate(idx):
                s_ref[s, h] = res[h][0:DK_A, :] + kwu[i][:, DK_A:]
            o_rows[j] = jnp.concatenate(
                [res[h][DK_A:, :] + qwu[i][:, DK_A:] for h, i in enumerate(idx)], axis=-1)

    gate = pa_ref[:, :, A_QKV:A_MAIN].reshape(nseq * tc, MIX_A)
    mix_a = _gdn_out(jnp.concatenate(o_rows, axis=0), gate, anw_ref[...])
    h = (x_ref[...].reshape(nseq * tc, D_MODEL)
         + _dot(mb_ref[...].reshape(nseq * tc, MIX_B), wo_ref[MIX_A:, :])
         + _dot(mix_a.astype(BF16), wo_ref[0:MIX_A, :]))
    y = h * lax.rsqrt(jnp.mean(h * h, axis=-1, keepdims=True) + RMS_EPS) * fw_ref[...]
    y_ref[...] = y.reshape(nseq, tc, D_MODEL)
    s_out_ref[...] = s_ref[...]


def _gdn_prompt(pa, bd, conv_w, hp, anw, x, mix_b, w_out, final_w):
    b, t, _ = pa.shape
    blk = lambda i, j: (i, j, 0)
    fixed = lambda i, j: (0, 0)
    return pl.pallas_call(
        _gdn_prompt_kernel,
        grid=(b // NSEQ_A, t // TC_A),
        in_specs=[pl.BlockSpec((NSEQ_A, TC_A, A_MAIN), blk),
                  pl.BlockSpec((NSEQ_A, TC_A, BD_W), blk),
                  pl.BlockSpec((CONV_W, A_QKV), fixed),
                  pl.BlockSpec((8, BD_W), fixed),
                  pl.BlockSpec((1, MIX_A), fixed),
                  pl.BlockSpec((NSEQ_A, TC_A, D_MODEL), blk),
                  pl.BlockSpec((NSEQ_A, TC_A, MIX_B), blk),
                  pl.BlockSpec((D_MODEL, D_MODEL), fixed),
                  pl.BlockSpec((1, D_MODEL), fixed)],
        out_specs=[pl.BlockSpec((NSEQ_A, TC_A, D_MODEL), blk),
                   pl.BlockSpec((NSEQ_A, HA, DK_A, DK_A), lambda i, j: (i, 0, 0, 0)),
                   pl.BlockSpec((NSEQ_A, CONV_W - 1, A_QKV), lambda i, j: (i, 0, 0))],
        out_shape=[jax.ShapeDtypeStruct((b, t, D_MODEL), F32),
                   jax.ShapeDtypeStruct((b, HA, DK_A, DK_A), F32),
                   jax.ShapeDtypeStruct((b, CONV_W - 1, A_QKV), F32)],
        scratch_shapes=[pltpu.VMEM((NSEQ_A, TC_A + 8, A_QKV), F32),
                        pltpu.VMEM((NSEQ_A, HA, DK_A, DK_A), F32)],
        compiler_params=pltpu.CompilerParams(dimension_semantics=("arbitrary", "arbitrary"),
                                             vmem_limit_bytes=VMEM_LIMIT),
        name="gdn_prompt",
    )(pa, bd, conv_w, hp, anw, x, mix_b, w_out, final_w)


def _rwkv_front(xb, w0, w2, a0, a2, k_k, k_a, ones_blk):
    r = xb[:, 0:MIX_B]
    kr = xb[:, MIX_B:2 * MIX_B]
    vr = xb[:, 2 * MIX_B:3 * MIX_B]
    gate = xb[:, 3 * MIX_B:4 * MIX_B]
    wd = xb[:, 4 * MIX_B:4 * MIX_B + LORA]
    ad = xb[:, 4 * MIX_B + LORA:]
    logw = -(EXP_NEG_HALF * _sigmoid(w0 + _dot(jnp.tanh(wd), w2)))
    a = _sigmoid(a0 + _dot(ad, a2))
    kraw = kr * k_k
    kk = kraw * lax.rsqrt(_group_sum(kraw * kraw, ones_blk) + 1e-12)
    kmod = kr * (1.0 + (a - 1.0) * k_a)
    return r, kmod, vr, gate, logw, a, kk


def _rwkv_back(y, r, kmod, v, gate, rk, lnw, lnb, ones_blk):
    inv = 1.0 / HS_B
    yc = y - _group_sum(y, ones_blk) * inv
    var = _group_sum(yc * yc, ones_blk) * inv
    gn = yc * lax.rsqrt(var + GN_EPS) * lnw + lnb
    bonus = _group_sum(r * kmod * rk, ones_blk) * v
    return (gn + bonus) * _silu(gate)


def _rwkv_prompt_kernel(pb_ref, mu_ref, w0_ref, w2_ref, a0_ref, a2_ref, kk_ref, ka_ref,
                        rk_ref, lnw_ref, lnb_ref, ones_ref,
                        o_ref, s_out_ref, sh_out_ref, carry_ref, s_ref):
    c = pl.program_id(1)

    @pl.when(c == 0)
    def _():
        carry_ref[...] = jnp.zeros_like(carry_ref)
        s_ref[...] = jnp.zeros_like(s_ref)

    ones_blk = ones_ref[...]
    pb = pb_ref[...].reshape(NSEQ * TC, B_W)
    rowi = _iota2((NSEQ * TC, 1), 0)
    prev = pltpu.roll(pb, 1, 0)
    for s in range(NSEQ):
        prev = jnp.where(rowi == s * TC, carry_ref[s, 0:1, :], prev)
        last = pb[(s + 1) * TC - 1:(s + 1) * TC, :]
        carry_ref[s, 0:1, :] = last
        sh_out_ref[s] = last
    xb = pb + (prev - pb) * mu_ref[...]
    r, kmod, v, gate, logw, a, kk = _rwkv_front(
        xb, w0_ref[...], w2_ref[...], a0_ref[...], a2_ref[...], kk_ref[...], ka_ref[...], ones_blk)

    g = jnp.concatenate([_chunk_cumsum(logw[s * TC:(s + 1) * TC, :]) for s in range(NSEQ)],
                        axis=0)
    eg = jnp.exp(g)
    egn = jnp.exp(-g)
    ag = -kk * jnp.exp(g - logw)
    kka = kk * a
    bg = kka * egn
    kg = kmod * egn
    rg = r * eg

    rowp = _iota2((CHUNK, PAIR), 0)
    lane = _iota2((CHUNK, PAIR), 1)
    colp = jnp.where(lane >= HS_B, lane - HS_B, lane)
    left = lane < HS_B
    incl = colp <= rowp
    strict = colp < rowp
    eye = colp == rowp
    zero = jnp.zeros((CHUNK, PAIR), F32)
    pack = lambda f: jnp.where(left, f[0:HS_B, :], f[HS_B:, :])

    nj = NSEQ * TC // CHUNK
    npair = HB // 2
    jp = [(j, p) for j in range(nj) for p in range(npair)]
    rows_of = lambda j: slice(j * CHUNK, (j + 1) * CHUNK)
    lanes_of = lambda p: slice(p * PAIR, (p + 1) * PAIR)
    glast = [g[(j + 1) * CHUNK - 1:(j + 1) * CHUNK, :] for j in range(nj)]
    eglast = [jnp.exp(glast[j]) for j in range(nj)]
    bdec = [bg[rows_of(j), :] * eglast[j] for j in range(nj)]
    kdec = [kg[rows_of(j), :] * eglast[j] for j in range(nj)]

    cut = lambda z: [z[rows_of(j), lanes_of(p)] for j, p in jp]
    ag_p, rg_p, bg_p, kg_p, v_p = cut(ag), cut(rg), cut(bg), cut(kg), cut(v)
    bdec_p = [bdec[j][:, lanes_of(p)] for j, p in jp]
    kdec_p = [kdec[j][:, lanes_of(p)] for j, p in jp]
    eglast_p = [eglast[j][:, lanes_of(p)] for j, p in jp]
    amat = [_dot(jnp.concatenate([a, b], axis=0),
                 jnp.concatenate([_pair_diag(c_), _pair_diag(d)], axis=0), NT)
            for a, b, c_, d in zip(ag_p, rg_p, bg_p, kg_p)]
    a_ab = [jnp.where(strict, m[0:CHUNK, 0:PAIR], 0.0) for m in amat]
    a_ak = [jnp.where(strict, m[0:CHUNK, PAIR:], 0.0) for m in amat]
    a_rb = [jnp.where(incl, m[CHUNK:, 0:PAIR], 0.0) for m in amat]
    a_rk = [jnp.where(incl, m[CHUNK:, PAIR:], 0.0) for m in amat]
    tinv = _tri_inv_pairs([-a for a in a_ab])
    kv = [_dot(jnp.concatenate([a, b], axis=0), _pair_diag(c_)) for a, b, c_ in zip(a_ak, a_rk, v_p)]
    wu = [_dot(t, jnp.concatenate([_pair_diag(a), _pair_diag(x[0:CHUNK, :])], axis=1))
          for t, a, x in zip(tinv, ag_p, kv)]
    rwu = [_dot(a, jnp.concatenate([_pair_diag(x[:, 0:PAIR]), _pair_diag(x[:, PAIR:])], axis=1))
           for a, x in zip(a_rb, wu)]
    mn = [_dot(jnp.concatenate([a, b], axis=0),
               jnp.concatenate([x, jnp.concatenate([zero, c_], axis=1)], axis=0), TN)
          for a, b, x, c_ in zip(bdec_p, kdec_p, wu, v_p)]
    lhs = [jnp.concatenate([jnp.where(eye, e, 0.0) + pack(m[:, 0:PAIR]), a + x[:, 0:PAIR]], axis=0)
           for e, m, a, x in zip(eglast_p, mn, rg_p, rwu)]

    y_rows = [None] * nj
    for jj in range(TC // CHUNK):
        for s in range(NSEQ):
            j = s * (TC // CHUNK) + jj
            idx = [j * npair + p for p in range(npair)]
            res = [_dot(lhs[i], _pair_diag(s_ref[s, p])) for p, i in enumerate(idx)]
            for p, i in enumerate(idx):
                s_ref[s, p] = res[p][0:HS_B, :] + pack(mn[i][:, PAIR:])
            y_rows[j] = jnp.concatenate(
                [res[p][HS_B:, :] + rwu[i][:, PAIR:] + kv[i][CHUNK:, :] for p, i in enumerate(idx)],
                axis=-1)

    out = _rwkv_back(jnp.concatenate(y_rows, axis=0), r, kmod, v, gate,
                     rk_ref[...], lnw_ref[...], lnb_ref[...], ones_blk)
    o_ref[...] = out.reshape(NSEQ, TC, MIX_B).astype(o_ref.dtype)
    for s in range(NSEQ):
        for p in range(npair):
            st = s_ref[s, p].T
            s_out_ref[s, 2 * p] = st[0:HS_B, :]
            s_out_ref[s, 2 * p + 1] = st[HS_B:, :]


def _rwkv_prompt(pb, mu, w0, w2, a0, a2, k_k, k_a, rk, lnw, lnb, ones_blk):
    b, t, _ = pb.shape
    blk = lambda i, j: (i, j, 0)
    fixed = lambda i, j: (0, 0)
    vec = pl.BlockSpec((1, MIX_B), fixed)
    return pl.pallas_call(
        _rwkv_prompt_kernel,
        grid=(b // NSEQ, t // TC),
        in_specs=[pl.BlockSpec((NSEQ, TC, B_W), blk),
                  pl.BlockSpec((1, B_W), fixed),
                  vec, pl.BlockSpec((LORA, MIX_B), fixed),
                  vec, pl.BlockSpec((LORA, MIX_B), fixed),
                  vec, vec, vec, vec, vec,
                  pl.BlockSpec((MIX_B, MIX_B), fixed)],
        out_specs=[pl.BlockSpec((NSEQ, TC, MIX_B), blk),
                   pl.BlockSpec((NSEQ, HB, HS_B, HS_B), lambda i, j: (i, 0, 0, 0)),
                   pl.BlockSpec((NSEQ, 1, B_W), lambda i, j: (i, 0, 0))],
        out_shape=[jax.ShapeDtypeStruct((b, t, MIX_B), BF16),
                   jax.ShapeDtypeStruct((b, HB, HS_B, HS_B), F32),
                   jax.ShapeDtypeStruct((b, 1, B_W), F32)],
        scratch_shapes=[pltpu.VMEM((NSEQ, 8, B_W), F32),
                        pltpu.VMEM((NSEQ, HB // 2, HS_B, PAIR), F32)],
        compiler_params=pltpu.CompilerParams(dimension_semantics=("arbitrary", "arbitrary"),
                                             vmem_limit_bytes=VMEM_LIMIT),
        name="rwkv_prompt",
    )(pb, mu, w0, w2, a0, a2, k_k, k_a, rk, lnw, lnb, ones_blk)


def _pad_rows(rows, width):
    return jnp.concatenate(rows + [jnp.zeros((8 - len(rows), width), F32)], axis=0)


def _gdn_step_kernel(pa_ref, bd_ref, sc_ref, s_ref, cw_ref, hp_ref, anw_ref,
                     o_ref, s_out_ref, c_out_ref):
    u = pa_ref[:, 0:A_QKV]
    cw = cw_ref[...]
    conv = (sc_ref[0] * cw[0:1, :] + sc_ref[1] * cw[1:2, :] + sc_ref[2] * cw[2:3, :] + u * cw[3:4, :])
    qkv = _silu(conv)
    c_out_ref[0] = sc_ref[1]
    c_out_ref[1] = sc_ref[2]
    c_out_ref[2] = u

    qn, kn = _gdn_qk(qkv)
    beta_all, g_all = _gdn_gates(bd_ref[...], hp_ref[...])
    eg_all = jnp.exp(g_all)
    lanes_of = lambda h: slice(h * DK_A, (h + 1) * DK_A)
    q = [qn[:, lanes_of(h)] for h in range(HA)]
    k = [kn[:, lanes_of(h)] for h in range(HA)]
    v = [qkv[:, 2 * MIX_A + h * DK_A:2 * MIX_A + (h + 1) * DK_A] for h in range(HA)]
    beta = [beta_all[:, h:h + 1] for h in range(HA)]
    eg = [eg_all[:, HA + h:HA + h + 1] for h in range(HA)]

    bh = [(b, h) for b in range(SB) for h in range(HA)]
    s0 = [s_ref[b, h] for b, h in bh]
    ks_qs = [_dot(_pad_rows([k[h][b:b + 1, :], q[h][b:b + 1, :]], DK_A), s) for (b, h), s in zip(bh, s0)]
    o_heads = []
    for h in range(HA):
        ks = jnp.concatenate([ks_qs[b * HA + h][0:1, :] for b in range(SB)], axis=0)
        qs = jnp.concatenate([ks_qs[b * HA + h][1:2, :] for b in range(SB)], axis=0)
        u2 = beta[h] * (v[h] - eg[h] * ks)
        for b in range(SB):
            s_out_ref[b, h] = (eg[h][b:b + 1, :] * s0[b * HA + h]
                               + _dot(_pad_rows([k[h][b:b + 1, :]], DK_A),
                                      _pad_rows([u2[b:b + 1, :]], DK_A), TN))
        o_heads.append(eg[h] * qs + jnp.sum(q[h] * k[h], axis=-1, keepdims=True) * u2)

    o_ref[...] = _gdn_out(jnp.concatenate(o_heads, axis=-1), pa_ref[:, A_QKV:A_MAIN], anw_ref[...])


def _gdn_step(pa, bd, sconv_t, s_a, conv_w, hp, anw):
    n = pa.shape[0]
    row = lambda i: (i, 0)
    fixed = lambda i: (0, 0)
    return pl.pallas_call(
        _gdn_step_kernel,
        grid=(n // SB,),
        in_specs=[pl.BlockSpec((SB, A_MAIN), row),
                  pl.BlockSpec((SB, BD_W), row),
                  pl.BlockSpec((CONV_W - 1, SB, A_QKV), lambda i: (0, i, 0)),
                  pl.BlockSpec((SB, HA, DK_A, DK_A), lambda i: (i, 0, 0, 0)),
                  pl.BlockSpec((CONV_W, A_QKV), fixed),
                  pl.BlockSpec((8, BD_W), fixed),
                  pl.BlockSpec((1, MIX_A), fixed)],
        out_specs=[pl.BlockSpec((SB, MIX_A), row),
                   pl.BlockSpec((SB, HA, DK_A, DK_A), lambda i: (i, 0, 0, 0)),
                   pl.BlockSpec((CONV_W - 1, SB, A_QKV), lambda i: (0, i, 0))],
        out_shape=[jax.ShapeDtypeStruct((n, MIX_A), F32),
                   jax.ShapeDtypeStruct((n, HA, DK_A, DK_A), F32),
                   jax.ShapeDtypeStruct((CONV_W - 1, n, A_QKV), F32)],
        compiler_params=pltpu.CompilerParams(dimension_semantics=("arbitrary",),
                                             vmem_limit_bytes=VMEM_LIMIT),
        name="gdn_step",
    )(pa, bd, sconv_t, s_a, conv_w, hp, anw)


def _rwkv_step_lanes_kernel(pb_ref, sh_ref, s_ref, mu_ref, w0_ref, w2_ref, a0_ref, a2_ref, kk_ref, ka_ref,
                            rk_ref, lnw_ref, lnb_ref, ones_ref,
                            o_ref, s_out_ref, vec_scr, row_scr, y_scr):
    h = pl.program_id(0)
    ones_blk = ones_ref[...]

    @pl.when(h == 0)
    def _():
        pb = pb_ref[...]
        xb = pb + (sh_ref[...] - pb) * mu_ref[...]
        r, kmod, v, gate, logw, a, kk = _rwkv_front(
            xb, w0_ref[...], w2_ref[...], a0_ref[...], a2_ref[...], kk_ref[...], ka_ref[...], ones_blk)
        for slot, z in enumerate((-kk, kk * a, kmod, v, r, jnp.exp(logw))):
            vec_scr[slot] = z.T
        for slot, z in enumerate((r, kmod, v, gate)):
            row_scr[slot] = z

    base = pl.multiple_of(h * HS_B, HS_B)
    chan = pl.ds(base, HS_B)
    nkk = vec_scr[0, chan, :]
    kka = vec_scr[1, chan, :]
    kmod_t = vec_scr[2, chan, :]
    r_t = vec_scr[4, chan, :]
    wdec = vec_scr[5, chan, :]

    def body(vi, carry):
        s0 = s_ref[0, vi]
        sa = jnp.sum(s0 * nkk, axis=0, keepdims=True)
        s1 = s0 * wdec + sa * kka + vec_scr[3, pl.ds(base + vi, 1), :] * kmod_t
        s_out_ref[0, vi] = s1
        y_scr[pl.ds(base + vi, 1), :] = jnp.sum(s1 * r_t, axis=0, keepdims=True)
        return carry

    lax.fori_loop(0, HS_B, body, 0, unroll=4)

    @pl.when(h == HB - 1)
    def _():
        o_ref[...] = _rwkv_back(y_scr[...].T, row_scr[0], row_scr[1], row_scr[2], row_scr[3],
                                rk_ref[...], lnw_ref[...], lnb_ref[...], ones_blk)


def _rwkv_step_lanes(pb, shift, s_hvkb, mu, w0, w2, a0, a2, k_k, k_a, rk, lnw, lnb, ones_blk):
    n = pb.shape[0]
    fixed = lambda i: (0, 0)
    vec = pl.BlockSpec((1, MIX_B), fixed)
    state = pl.BlockSpec((1, HS_B, HS_B, n), lambda i: (i, 0, 0, 0))
    return pl.pallas_call(
        _rwkv_step_lanes_kernel,
        grid=(HB,),
        in_specs=[pl.BlockSpec((n, B_W), fixed),
                  pl.BlockSpec((n, B_W), fixed),
                  state,
                  pl.BlockSpec((1, B_W), fixed),
                  vec, pl.BlockSpec((LORA, MIX_B), fixed),
                  vec, pl.BlockSpec((LORA, MIX_B), fixed),
                  vec, vec, vec, vec, vec,
                  pl.BlockSpec((MIX_B, MIX_B), fixed)],
        out_specs=[pl.BlockSpec((n, MIX_B), fixed), state],
        out_shape=[jax.ShapeDtypeStruct((n, MIX_B), F32),
                   jax.ShapeDtypeStruct((HB, HS_B, HS_B, n), F32)],
        scratch_shapes=[pltpu.VMEM((6, MIX_B, n), F32),
                        pltpu.VMEM((4, n, MIX_B), F32),
                        pltpu.VMEM((MIX_B, n), F32)],
        compiler_params=pltpu.CompilerParams(dimension_semantics=("arbitrary",),
                                             vmem_limit_bytes=VMEM_LIMIT),
        name="rwkv_step",
    )(pb, shift, s_hvkb, mu, w0, w2, a0, a2, k_k, k_a, rk, lnw, lnb, ones_blk)


def _block_ones(width, group):
    idx = jnp.arange(width) // group
    return (idx[:, None] == idx[None, :]).astype(BF16)


def kernel(x_prompt, x_sample, state_a_mat, state_a_conv, state_b_mat, state_b_shift, norm_w, w_in,
           conv_w, a_log, dt_bias, a_norm_w, mu, w0, w2, a0, a2, k_k, k_a, r_k, ln_w, ln_b, w_out,
           final_norm_w):
    bsz, seq, _ = x_prompt.shape
    nsmp = x_sample.shape[0]

    w_all = w_in[0].astype(BF16)
    w_o = w_out[0].astype(BF16)
    hp = jnp.pad(jnp.concatenate([a_log, dt_bias], axis=0), ((0, 6), (HA, BD_W - 2 * HA)))
    nw = norm_w[0].reshape(1, D_MODEL)
    fw = final_norm_w.reshape(1, D_MODEL)
    anw = jnp.tile(a_norm_w[0].reshape(1, DK_A), (1, HA))
    ones_b = _block_ones(MIX_B, HS_B)
    row = lambda z: z[0].reshape(1, -1)
    b_params = (row(mu), row(w0), w2[0], row(a0), a2[0], row(k_k), row(k_a), row(r_k), row(ln_w), row(ln_b),
                ones_b)

    xp = x_prompt.reshape(bsz * seq, D_MODEL)
    pa, pb, bd = _inproj(xp, nw, w_all)
    ob, p_bmat, p_bshift = _rwkv_prompt(pb.reshape(bsz, seq, B_W), *b_params)
    y_prompt, p_amat, p_aconv = _gdn_prompt(pa.reshape(bsz, seq, A_MAIN), bd.reshape(bsz, seq, BD_W),
                                            conv_w[0], hp, anw, x_prompt, ob, w_o, fw)

    xs = x_sample.reshape(nsmp, D_MODEL)
    sa_, sb_, sbd = _inproj(xs, nw, w_all)
    sconv_t = jnp.swapaxes(state_a_conv[0], 0, 1)
    soa, s_amat, s_aconv_t = _gdn_step(sa_, sbd, sconv_t, state_a_mat[0], conv_w[0], hp, anw)
    sob, s_bmat_t = _rwkv_step_lanes(sb_, state_b_shift[0], jnp.transpose(state_b_mat[0], (1, 2, 3, 0)),
                                     *b_params)
    s_bmat = jnp.transpose(s_bmat_t, (3, 0, 1, 2))
    y_sample = _outproj(xs, soa, sob, w_o, fw)

    return (y_prompt, y_sample.reshape(nsmp, 1, D_MODEL),
            p_amat[None], p_aconv[None], p_bmat[None], p_bshift.reshape(1, bsz, B_W),
            s_amat[None], jnp.swapaxes(s_aconv_t, 0, 1)[None], s_bmat[None], sb_[None])
```

```python
import jax
import jax.numpy as jnp
from jax import lax
from jax.experimental import pallas as pl
from jax.experimental.pallas import tpu as pltpu

F32 = jnp.float32
BF16 = jnp.bfloat16

D_MODEL = 1024
MIX_A = 512
MIX_B = 512
HA = 4
DK_A = 128
HB = 8
HS_B = 64
PAIR = 2 * HS_B
CONV_W = 4
LORA = 64
A_QKV = 3 * MIX_A
A_MAIN = A_QKV + MIX_A
B_W = 4 * MIX_B + 2 * LORA
BD_W = 128
RMS_EPS = 1e-6
GN_EPS = 64e-5
CHUNK = 64
TC = 256
TC_A = 256
NSEQ_A = 2
NSEQ = 4
SB = 16
VMEM_LIMIT = 56 * 1024 * 1024

EXP_NEG_HALF = 0.6065306597126334
NN = (((1,), (0,)), ((), ()))
NT = (((1,), (1,)), ((), ()))
TN = (((0,), (0,)), ((), ()))


def _dot(a, b, dn=NN):
    return lax.dot_general(a, b, dn, preferred_element_type=F32)


def _sigmoid(x):
    return 0.5 * jnp.tanh(0.5 * x) + 0.5


def _silu(x):
    h = 0.5 * x
    return h * jnp.tanh(h) + h


def _softplus(x):
    return jnp.maximum(x, 0.0) + jnp.log(1.0 + jnp.exp(-jnp.abs(x)))


def _l2norm(x):
    return x * lax.rsqrt(jnp.sum(x * x, axis=-1, keepdims=True) + 1e-12)


def _group_sum(x, ones_blk):
    return _dot(x.astype(BF16), ones_blk)


def _iota2(shape, dim):
    return lax.broadcasted_iota(jnp.int32, shape, dim)


def _chunk_cumsum(x):
    n = x.shape[0]
    r = _iota2((n, n), 0)
    c = _iota2((n, n), 1)
    tril = jnp.where(((r // CHUNK) == (c // CHUNK)) & (c <= r), 1.0, 0.0).astype(BF16)
    hi = x.astype(BF16)
    rest = x - hi.astype(F32)
    mid = rest.astype(BF16)
    lo = (rest - mid.astype(F32)).astype(BF16)
    return _dot(tril, hi) + _dot(tril, mid) + _dot(tril, lo)


def _tri_inv_many(lows):
    n = lows[0].shape[0]
    r = _iota2((n, n), 0)
    c = _iota2((n, n), 1)
    eye = jnp.where(r == c, 1.0, 0.0).astype(F32)
    base = (r // 8) == (c // 8)
    l0 = [jnp.where(base, low, 0.0) for low in lows]
    l2 = [_dot(a, a) for a in l0]
    l4 = [_dot(a, a) for a in l2]
    d = [_dot(eye - a, eye + b) for a, b in zip(l0, l2)]
    d = [_dot(a, eye + b) for a, b in zip(d, l4)]
    s = 8
    while s < n:
        sel = ((r // (2 * s)) == (c // (2 * s))) & ((r // s) != (c // s))
        t = [_dot(jnp.where(sel, low, 0.0), a) for low, a in zip(lows, d)]
        d = [a - _dot(a, b) for a, b in zip(d, t)]
        s *= 2
    return d


def _pair_diag(x):
    left = _iota2(x.shape, 1) < x.shape[1] // 2
    return jnp.concatenate([jnp.where(left, x, 0.0), jnp.where(left, 0.0, x)], axis=0)


def _tri_inv_pairs(lows):
    n = lows[0].shape[0]
    r = _iota2((n, 2 * n), 0)
    lane = _iota2((n, 2 * n), 1)
    c = jnp.where(lane >= n, lane - n, lane)
    mul = lambda a, b: _dot(a, _pair_diag(b))
    eye = jnp.where(r == c, 1.0, 0.0).astype(F32)
    base = (r // 8) == (c // 8)
    l0 = [jnp.where(base, low, 0.0) for low in lows]
    l2 = [mul(a, a) for a in l0]
    l4 = [mul(a, a) for a in l2]
    d = [mul(eye - a, eye + b) for a, b in zip(l0, l2)]
    d = [mul(a, eye + b) for a, b in zip(d, l4)]
    s = 8
    while s < n:
        sel = ((r // (2 * s)) == (c // (2 * s))) & ((r // s) != (c // s))
        t = [mul(jnp.where(sel, low, 0.0), a) for low, a in zip(lows, d)]
        d = [a - mul(a, b) for a, b in zip(d, t)]
        s *= 2
    return d


def _inproj_kernel(x_ref, nw_ref, w_ref, oa_ref, ob_ref, obd_ref, wb_scr):
    a_w = A_MAIN + 2 * HA

    @pl.when(pl.program_id(0) == 0)
    def _():
        wb_scr[...] = jnp.concatenate(
            [w_ref[:, a_w:], w_ref[:, A_MAIN:a_w], jnp.zeros((D_MODEL, BD_W - 2 * HA), BF16)], axis=1)

    x = x_ref[...]
    xn = (x * lax.rsqrt(jnp.mean(x * x, axis=-1, keepdims=True) + RMS_EPS) * nw_ref[...]).astype(BF16)
    oa_ref[...] = _dot(xn, w_ref[:, 0:A_MAIN])
    pb = _dot(xn, wb_scr[...])
    ob_ref[...] = pb[:, :B_W]
    obd_ref[...] = pb[:, B_W:]


def _inproj(x2d, norm_w, w_all):
    n = x2d.shape[0]
    tm = min(512, n)
    row = lambda i: (i, 0)
    fixed = lambda i: (0, 0)
    return pl.pallas_call(
        _inproj_kernel,
        grid=(n // tm,),
        in_specs=[pl.BlockSpec((tm, D_MODEL), row),
                  pl.BlockSpec((1, D_MODEL), fixed),
                  pl.BlockSpec(w_all.shape, fixed)],
        out_specs=[pl.BlockSpec((tm, A_MAIN), row),
                   pl.BlockSpec((tm, B_W), row),
                   pl.BlockSpec((tm, BD_W), row)],
        out_shape=[jax.ShapeDtypeStruct((n, A_MAIN), F32),
                   jax.ShapeDtypeStruct((n, B_W), F32),
                   jax.ShapeDtypeStruct((n, BD_W), F32)],
        scratch_shapes=[pltpu.VMEM((D_MODEL, B_W + BD_W), BF16)],
        compiler_params=pltpu.CompilerParams(dimension_semantics=("arbitrary",),
                                             vmem_limit_bytes=VMEM_LIMIT),
        name="inproj",
    )(x2d, norm_w, w_all)


def _outproj_kernel(x_ref, ma_ref, mb_ref, w_ref, fw_ref, y_ref):
    h = (x_ref[...]
         + _dot(ma_ref[...].astype(BF16), w_ref[0:MIX_A, :])
         + _dot(mb_ref[...].astype(BF16), w_ref[MIX_A:, :]))
    y_ref[...] = h * lax.rsqrt(jnp.mean(h * h, axis=-1, keepdims=True) + RMS_EPS) * fw_ref[...]


def _outproj(x2d, mix_a, mix_b, w_out, final_w):
    n = x2d.shape[0]
    tm = min(1024, n)
    row = lambda i: (i, 0)
    fixed = lambda i: (0, 0)
    return pl.pallas_call(
        _outproj_kernel,
        grid=(n // tm,),
        in_specs=[pl.BlockSpec((tm, D_MODEL), row),
                  pl.BlockSpec((tm, MIX_A), row),
                  pl.BlockSpec((tm, MIX_B), row),
                  pl.BlockSpec((D_MODEL, D_MODEL), fixed),
                  pl.BlockSpec((1, D_MODEL), fixed)],
        out_specs=pl.BlockSpec((tm, D_MODEL), row),
        out_shape=jax.ShapeDtypeStruct((n, D_MODEL), F32),
        compiler_params=pltpu.CompilerParams(dimension_semantics=("arbitrary",),
                                             vmem_limit_bytes=VMEM_LIMIT),
        name="outproj",
    )(x2d, mix_a, mix_b, w_out, final_w)


def _gdn_gates(bd, hp):
    beta = _sigmoid(bd)
    g = -jnp.exp(hp[0:1, :]) * _softplus(bd + hp[1:2, :])
    return beta, g


def _gdn_qk(qkv):
    heads = lambda z: [z[:, h * DK_A:(h + 1) * DK_A] for h in range(HA)]
    qn = jnp.concatenate([_l2norm(x) * (DK_A ** -0.5) for x in heads(qkv[:, 0:MIX_A])], axis=-1)
    kn = jnp.concatenate([_l2norm(x) for x in heads(qkv[:, MIX_A:2 * MIX_A])], axis=-1)
    return qn, kn


def _gdn_out(o, gate, anw):
    on = jnp.concatenate(
        [x * lax.rsqrt(jnp.mean(x * x, axis=-1, keepdims=True) + RMS_EPS)
         for x in [o[:, h * DK_A:(h + 1) * DK_A] for h in range(HA)]], axis=-1)
    return on * anw * _silu(gate)


def _gdn_prompt_kernel(pa_ref, bd_ref, cw_ref, hp_ref, anw_ref, x_ref, mb_ref, wo_ref, fw_ref,
                       y_ref, s_out_ref, c_out_ref, ext_ref, s_ref):
    c = pl.program_id(1)

    nseq, tc = pa_ref.shape[0], pa_ref.shape[1]

    @pl.when(c == 0)
    def _():
        ext_ref[...] = jnp.zeros_like(ext_ref)
        s_ref[...] = jnp.zeros_like(s_ref)

    cw = cw_ref[...]
    conv = []
    for s in range(nseq):
        u = pa_ref[s, :, 0:A_QKV]
        full = jnp.concatenate([ext_ref[s], u], axis=0)
        conv.append(u * cw[3:4, :] + sum(
            pltpu.roll(full, CONV_W - 1 - i, 0)[8:, :] * cw[i:i + 1, :] for i in range(CONV_W - 1)))
        ext_ref[s] = u[tc - 8:, :]
        c_out_ref[s] = ext_ref[s, 8 - (CONV_W - 1):, :]
    qkv = _silu(jnp.concatenate(conv, axis=0))

    qn, kn = _gdn_qk(qkv)
    beta_all, g_all = _gdn_gates(bd_ref[...].reshape(nseq * tc, BD_W), hp_ref[...])
    gcum = jnp.concatenate([_chunk_cumsum(g_all[s * tc:(s + 1) * tc, :]) for s in range(nseq)],
                           axis=0)
    gcum_t = gcum.T

    r = _iota2((CHUNK, CHUNK), 0)
    cc = _iota2((CHUNK, CHUNK), 1)
    incl = cc <= r
    strict = cc < r

    nj = nseq * tc // CHUNK
    jh = [(j, h) for j in range(nj) for h in range(HA)]
    rows_of = lambda j: slice(j * CHUNK, (j + 1) * CHUNK)
    lanes_of = lambda h: slice(h * DK_A, (h + 1) * DK_A)
    q = [qn[rows_of(j), lanes_of(h)] for j, h in jh]
    k = [kn[rows_of(j), lanes_of(h)] for j, h in jh]
    v = [qkv[rows_of(j), 2 * MIX_A + h * DK_A:2 * MIX_A + (h + 1) * DK_A] for j, h in jh]
    beta = [beta_all[rows_of(j), h:h + 1] for j, h in jh]
    gc = [gcum[rows_of(j), HA + h:HA + h + 1] for j, h in jh]
    gr = [gcum_t[HA + h:HA + h + 1, rows_of(j)] for j, h in jh]
    glast = [gcum[(j + 1) * CHUNK - 1:(j + 1) * CHUNK, HA + h:HA + h + 1] for j, h in jh]
    decay = [jnp.where(incl, jnp.exp(jnp.where(incl, a - b, 0.0)), 0.0) for a, b in zip(gc, gr)]
    kq = [_dot(jnp.concatenate([a, b], axis=0), a, NT) for a, b in zip(k, q)]
    low = [jnp.where(strict, b * d * m[0:CHUNK, :], 0.0) for b, d, m in zip(beta, decay, kq)]
    tinv = _tri_inv_many(low)
    eg = [jnp.exp(a) for a in gc]
    wu = [_dot(t, jnp.concatenate([(b * e) * a, b * c_], axis=1))
          for t, b, e, a, c_ in zip(tinv, beta, eg, k, v)]
    kwu = [_dot(a * jnp.exp(gl - g), x, TN) for a, gl, g, x in zip(k, glast, gc, wu)]
    qwu = [_dot(d * m[CHUNK:, :], x) for d, m, x in zip(decay, kq, wu)]
    rd = _iota2((DK_A, DK_A), 0)
    cd = _iota2((DK_A, DK_A), 1)
    lhs = [jnp.concatenate([jnp.where(rd == cd, jnp.exp(gl), 0.0) - kx[:, 0:DK_A],
                            a * e - qx[:, 0:DK_A]], axis=0)
           for gl, kx, a, e, qx in zip(glast, kwu, q, eg, qwu)]

    o_rows = [None] * nj
    for jj in range(tc // CHUNK):
        for s in range(nseq):
            j = s * (tc // CHUNK) + jj
            idx = [j * HA + h for h in range(HA)]
            res = [_dot(lhs[i], s_ref[s, h]) for h, i in enumerate(idx)]
            for h, i in enumerate(idx):
                s_ref[s, h] = res[h][0:DK_A, :] + kwu[i][:, DK_A:]
            o_rows[j] = jnp.concatenate(
                [res[h][DK_A:, :] + qwu[i][:, DK_A:] for h, i in enumerate(idx)], axis=-1)

    gate = pa_ref[:, :, A_QKV:A_MAIN].reshape(nseq * tc, MIX_A)
    mix_a = _gdn_out(jnp.concatenate(o_rows, axis=0), gate, anw_ref[...])
    h = (x_ref[...].reshape(nseq * tc, D_MODEL)
         + _dot(mb_ref[...].reshape(nseq * tc, MIX_B), wo_ref[MIX_A:, :])
         + _dot(mix_a.astype(BF16), wo_ref[0:MIX_A, :]))
    y = h * lax.rsqrt(jnp.mean(h * h, axis=-1, keepdims=True) + RMS_EPS) * fw_ref[...]
    y_ref[...] = y.reshape(nseq, tc, D_MODEL)
    s_out_ref[...] = s_ref[...]


def _gdn_prompt(pa, bd, conv_w, hp, anw, x, mix_b, w_out, final_w):
    b, t, _ = pa.shape
    blk = lambda i, j: (i, j, 0)
    fixed = lambda i, j: (0, 0)
    return pl.pallas_call(
        _gdn_prompt_kernel,
        grid=(b // NSEQ_A, t // TC_A),
        in_specs=[pl.BlockSpec((NSEQ_A, TC_A, A_MAIN), blk),
                  pl.BlockSpec((NSEQ_A, TC_A, BD_W), blk),
                  pl.BlockSpec((CONV_W, A_QKV), fixed),
                  pl.BlockSpec((8, BD_W), fixed),
                  pl.BlockSpec((1, MIX_A), fixed),
                  pl.BlockSpec((NSEQ_A, TC_A, D_MODEL), blk),
                  pl.BlockSpec((NSEQ_A, TC_A, MIX_B), blk),
                  pl.BlockSpec((D_MODEL, D_MODEL), fixed),
                  pl.BlockSpec((1, D_MODEL), fixed)],
        out_specs=[pl.BlockSpec((NSEQ_A, TC_A, D_MODEL), blk),
                   pl.BlockSpec((NSEQ_A, HA, DK_A, DK_A), lambda i, j: (i, 0, 0, 0)),
                   pl.BlockSpec((NSEQ_A, CONV_W - 1, A_QKV), lambda i, j: (i, 0, 0))],
        out_shape=[jax.ShapeDtypeStruct((b, t, D_MODEL), F32),
                   jax.ShapeDtypeStruct((b, HA, DK_A, DK_A), F32),
                   jax.ShapeDtypeStruct((b, CONV_W - 1, A_QKV), F32)],
        scratch_shapes=[pltpu.VMEM((NSEQ_A, 8, A_QKV), F32),
                        pltpu.VMEM((NSEQ_A, HA, DK_A, DK_A), F32)],
        compiler_params=pltpu.CompilerParams(dimension_semantics=("arbitrary", "arbitrary"),
                                             vmem_limit_bytes=VMEM_LIMIT),
        name="gdn_prompt",
    )(pa, bd, conv_w, hp, anw, x, mix_b, w_out, final_w)


def _rwkv_front(xb, w0, w2, a0, a2, k_k, k_a, ones_blk):
    r = xb[:, 0:MIX_B]
    kr = xb[:, MIX_B:2 * MIX_B]
    vr = xb[:, 2 * MIX_B:3 * MIX_B]
    gate = xb[:, 3 * MIX_B:4 * MIX_B]
    wd = xb[:, 4 * MIX_B:4 * MIX_B + LORA]
    ad = xb[:, 4 * MIX_B + LORA:]
    logw = -(EXP_NEG_HALF * _sigmoid(w0 + _dot(jnp.tanh(wd), w2)))
    a = _sigmoid(a0 + _dot(ad, a2))
    kraw = kr * k_k
    kk = kraw * lax.rsqrt(_group_sum(kraw * kraw, ones_blk) + 1e-12)
    kmod = kr * (1.0 + (a - 1.0) * k_a)
    return r, kmod, vr, gate, logw, a, kk


def _rwkv_back(y, r, kmod, v, gate, rk, lnw, lnb, ones_blk):
    inv = 1.0 / HS_B
    yc = y - _group_sum(y, ones_blk) * inv
    var = _group_sum(yc * yc, ones_blk) * inv
    gn = yc * lax.rsqrt(var + GN_EPS) * lnw + lnb
    bonus = _group_sum(r * kmod * rk, ones_blk) * v
    return (gn + bonus) * _silu(gate)


def _rwkv_prompt_kernel(pb_ref, mu_ref, w0_ref, w2_ref, a0_ref, a2_ref, kk_ref, ka_ref,
                        rk_ref, lnw_ref, lnb_ref, ones_ref,
                        o_ref, s_out_ref, sh_out_ref, carry_ref, s_ref):
    c = pl.program_id(1)

    @pl.when(c == 0)
    def _():
        carry_ref[...] = jnp.zeros_like(carry_ref)
        s_ref[...] = jnp.zeros_like(s_ref)

    ones_blk = ones_ref[...]
    pb = pb_ref[...].reshape(NSEQ * TC, B_W)
    prev = []
    for s in range(NSEQ):
        pb_s = pb_ref[s]
        prev.append(pltpu.roll(jnp.concatenate([carry_ref[s], pb_s], axis=0), 1, 0)[8:, :])
        carry_ref[s] = pb_s[TC - 8:, :]
        sh_out_ref[s] = pb_s[TC - 1:, :]
    prev = jnp.concatenate(prev, axis=0)
    xb = pb + (prev - pb) * mu_ref[...]
    r, kmod, v, gate, logw, a, kk = _rwkv_front(
        xb, w0_ref[...], w2_ref[...], a0_ref[...], a2_ref[...], kk_ref[...], ka_ref[...], ones_blk)

    g = jnp.concatenate([_chunk_cumsum(logw[s * TC:(s + 1) * TC, :]) for s in range(NSEQ)],
                        axis=0)
    eg = jnp.exp(g)
    egn = jnp.exp(-g)
    ag = -kk * jnp.exp(g - logw)
    kka = kk * a
    bg = kka * egn
    kg = kmod * egn
    rg = r * eg

    rowp = _iota2((CHUNK, PAIR), 0)
    lane = _iota2((CHUNK, PAIR), 1)
    colp = jnp.where(lane >= HS_B, lane - HS_B, lane)
    left = lane < HS_B
    incl = colp <= rowp
    strict = colp < rowp
    eye = colp == rowp
    zero = jnp.zeros((CHUNK, PAIR), F32)
    pack = lambda f: jnp.where(left, f[0:HS_B, :], f[HS_B:, :])

    nj = NSEQ * TC // CHUNK
    npair = HB // 2
    jp = [(j, p) for j in range(nj) for p in range(npair)]
    rows_of = lambda j: slice(j * CHUNK, (j + 1) * CHUNK)
    lanes_of = lambda p: slice(p * PAIR, (p + 1) * PAIR)
    glast = [g[(j + 1) * CHUNK - 1:(j + 1) * CHUNK, :] for j in range(nj)]
    eglast = [jnp.exp(glast[j]) for j in range(nj)]
    bdec = [bg[rows_of(j), :] * eglast[j] for j in range(nj)]
    kdec = [kg[rows_of(j), :] * eglast[j] for j in range(nj)]

    cut = lambda z: [z[rows_of(j), lanes_of(p)] for j, p in jp]
    ag_p, rg_p, bg_p, kg_p, v_p = cut(ag), cut(rg), cut(bg), cut(kg), cut(v)
    bdec_p = [bdec[j][:, lanes_of(p)] for j, p in jp]
    kdec_p = [kdec[j][:, lanes_of(p)] for j, p in jp]
    eglast_p = [eglast[j][:, lanes_of(p)] for j, p in jp]
    amat = [_dot(jnp.concatenate([a, b], axis=0),
                 jnp.concatenate([_pair_diag(c_), _pair_diag(d)], axis=0), NT)
            for a, b, c_, d in zip(ag_p, rg_p, bg_p, kg_p)]
    a_ab = [jnp.where(strict, m[0:CHUNK, 0:PAIR], 0.0) for m in amat]
    a_ak = [jnp.where(strict, m[0:CHUNK, PAIR:], 0.0) for m in amat]
    a_rb = [jnp.where(incl, m[CHUNK:, 0:PAIR], 0.0) for m in amat]
    a_rk = [jnp.where(incl, m[CHUNK:, PAIR:], 0.0) for m in amat]
    tinv = _tri_inv_pairs([-a for a in a_ab])
    kv = [_dot(jnp.concatenate([a, b], axis=0), _pair_diag(c_)) for a, b, c_ in zip(a_ak, a_rk, v_p)]
    wu = [_dot(t, jnp.concatenate([_pair_diag(a), _pair_diag(x[0:CHUNK, :])], axis=1))
          for t, a, x in zip(tinv, ag_p, kv)]
    rwu = [_dot(a, jnp.concatenate([_pair_diag(x[:, 0:PAIR]), _pair_diag(x[:, PAIR:])], axis=1))
           for a, x in zip(a_rb, wu)]
    mn = [_dot(jnp.concatenate([a, b], axis=0),
               jnp.concatenate([x, jnp.concatenate([zero, c_], axis=1)], axis=0), TN)
          for a, b, x, c_ in zip(bdec_p, kdec_p, wu, v_p)]
    lhs = [jnp.concatenate([jnp.where(eye, e, 0.0) + pack(m[:, 0:PAIR]), a + x[:, 0:PAIR]], axis=0)
           for e, m, a, x in zip(eglast_p, mn, rg_p, rwu)]

    y_rows = [None] * nj
    for jj in range(TC // CHUNK):
        for s in range(NSEQ):
            j = s * (TC // CHUNK) + jj
            idx = [j * npair + p for p in range(npair)]
            res = [_dot(lhs[i], _pair_diag(s_ref[s, p])) for p, i in enumerate(idx)]
            for p, i in enumerate(idx):
                s_ref[s, p] = res[p][0:HS_B, :] + pack(mn[i][:, PAIR:])
            y_rows[j] = jnp.concatenate(
                [res[p][HS_B:, :] + rwu[i][:, PAIR:] + kv[i][CHUNK:, :] for p, i in enumerate(idx)],
                axis=-1)

    out = _rwkv_back(jnp.concatenate(y_rows, axis=0), r, kmod, v, gate,
                     rk_ref[...], lnw_ref[...], lnb_ref[...], ones_blk)
    o_ref[...] = out.reshape(NSEQ, TC, MIX_B).astype(o_ref.dtype)
    for s in range(NSEQ):
        for p in range(npair):
            st = s_ref[s, p].T
            s_out_ref[s, 2 * p] = st[0:HS_B, :]
            s_out_ref[s, 2 * p + 1] = st[HS_B:, :]


def _rwkv_prompt(pb, mu, w0, w2, a0, a2, k_k, k_a, rk, lnw, lnb, ones_blk):
    b, t, _ = pb.shape
    blk = lambda i, j: (i, j, 0)
    fixed = lambda i, j: (0, 0)
    vec = pl.BlockSpec((1, MIX_B), fixed)
    return pl.pallas_call(
        _rwkv_prompt_kernel,
        grid=(b // NSEQ, t // TC),
        in_specs=[pl.BlockSpec((NSEQ, TC, B_W), blk),
                  pl.BlockSpec((1, B_W), fixed),
                  vec, pl.BlockSpec((LORA, MIX_B), fixed),
                  vec, pl.BlockSpec((LORA, MIX_B), fixed),
                  vec, vec, vec, vec, vec,
                  pl.BlockSpec((MIX_B, MIX_B), fixed)],
        out_specs=[pl.BlockSpec((NSEQ, TC, MIX_B), blk),
                   pl.BlockSpec((NSEQ, HB, HS_B, HS_B), lambda i, j: (i, 0, 0, 0)),
                   pl.BlockSpec((NSEQ, 1, B_W), lambda i, j: (i, 0, 0))],
        out_shape=[jax.ShapeDtypeStruct((b, t, MIX_B), BF16),
                   jax.ShapeDtypeStruct((b, HB, HS_B, HS_B), F32),
                   jax.ShapeDtypeStruct((b, 1, B_W), F32)],
        scratch_shapes=[pltpu.VMEM((NSEQ, 8, B_W), F32),
                        pltpu.VMEM((NSEQ, HB // 2, HS_B, PAIR), F32)],
        compiler_params=pltpu.CompilerParams(dimension_semantics=("arbitrary", "arbitrary"),
                                             vmem_limit_bytes=VMEM_LIMIT),
        name="rwkv_prompt",
    )(pb, mu, w0, w2, a0, a2, k_k, k_a, rk, lnw, lnb, ones_blk)


def _pad_rows(rows, width):
    return jnp.concatenate(rows + [jnp.zeros((8 - len(rows), width), F32)], axis=0)


def _gdn_step_kernel(pa_ref, bd_ref, sc_ref, s_ref, cw_ref, hp_ref, anw_ref,
                     o_ref, s_out_ref, c_out_ref):
    u = pa_ref[:, 0:A_QKV]
    cw = cw_ref[...]
    conv = (sc_ref[0] * cw[0:1, :] + sc_ref[1] * cw[1:2, :] + sc_ref[2] * cw[2:3, :] + u * cw[3:4, :])
    qkv = _silu(conv)
    c_out_ref[0] = sc_ref[1]
    c_out_ref[1] = sc_ref[2]
    c_out_ref[2] = u

    qn, kn = _gdn_qk(qkv)
    beta_all, g_all = _gdn_gates(bd_ref[...], hp_ref[...])
    eg_all = jnp.exp(g_all)
    lanes_of = lambda h: slice(h * DK_A, (h + 1) * DK_A)
    q = [qn[:, lanes_of(h)] for h in range(HA)]
    k = [kn[:, lanes_of(h)] for h in range(HA)]
    v = [qkv[:, 2 * MIX_A + h * DK_A:2 * MIX_A + (h + 1) * DK_A] for h in range(HA)]
    beta = [beta_all[:, h:h + 1] for h in range(HA)]
    eg = [eg_all[:, HA + h:HA + h + 1] for h in range(HA)]

    bh = [(b, h) for b in range(SB) for h in range(HA)]
    s0 = [s_ref[b, h] for b, h in bh]
    ks_qs = [_dot(_pad_rows([k[h][b:b + 1, :], q[h][b:b + 1, :]], DK_A), s) for (b, h), s in zip(bh, s0)]
    o_heads = []
    for h in range(HA):
        ks = jnp.concatenate([ks_qs[b * HA + h][0:1, :] for b in range(SB)], axis=0)
        qs = jnp.concatenate([ks_qs[b * HA + h][1:2, :] for b in range(SB)], axis=0)
        u2 = beta[h] * (v[h] - eg[h] * ks)
        for b in range(SB):
            s_out_ref[b, h] = (eg[h][b:b + 1, :] * s0[b * HA + h]
                               + _dot(_pad_rows([k[h][b:b + 1, :]], DK_A),
                                      _pad_rows([u2[b:b + 1, :]], DK_A), TN))
        o_heads.append(eg[h] * qs + jnp.sum(q[h] * k[h], axis=-1, keepdims=True) * u2)

    o_ref[...] = _gdn_out(jnp.concatenate(o_heads, axis=-1), pa_ref[:, A_QKV:A_MAIN], anw_ref[...])


def _gdn_step(pa, bd, sconv_t, s_a, conv_w, hp, anw):
    n = pa.shape[0]
    row = lambda i: (i, 0)
    fixed = lambda i: (0, 0)
    return pl.pallas_call(
        _gdn_step_kernel,
        grid=(n // SB,),
        in_specs=[pl.BlockSpec((SB, A_MAIN), row),
                  pl.BlockSpec((SB, BD_W), row),
                  pl.BlockSpec((CONV_W - 1, SB, A_QKV), lambda i: (0, i, 0)),
                  pl.BlockSpec((SB, HA, DK_A, DK_A), lambda i: (i, 0, 0, 0)),
                  pl.BlockSpec((CONV_W, A_QKV), fixed),
                  pl.BlockSpec((8, BD_W), fixed),
                  pl.BlockSpec((1, MIX_A), fixed)],
        out_specs=[pl.BlockSpec((SB, MIX_A), row),
                   pl.BlockSpec((SB, HA, DK_A, DK_A), lambda i: (i, 0, 0, 0)),
                   pl.BlockSpec((CONV_W - 1, SB, A_QKV), lambda i: (0, i, 0))],
        out_shape=[jax.ShapeDtypeStruct((n, MIX_A), F32),
                   jax.ShapeDtypeStruct((n, HA, DK_A, DK_A), F32),
                   jax.ShapeDtypeStruct((CONV_W - 1, n, A_QKV), F32)],
        compiler_params=pltpu.CompilerParams(dimension_semantics=("arbitrary",),
                                             vmem_limit_bytes=VMEM_LIMIT),
        name="gdn_step",
    )(pa, bd, sconv_t, s_a, conv_w, hp, anw)


def _rwkv_step_lanes_kernel(pb_ref, sh_ref, s_ref, mu_ref, w0_ref, w2_ref, a0_ref, a2_ref, kk_ref, ka_ref,
                            rk_ref, lnw_ref, lnb_ref, ones_ref,
                            o_ref, s_out_ref, vec_scr, row_scr, y_scr):
    h = pl.program_id(0)
    ones_blk = ones_ref[...]

    @pl.when(h == 0)
    def _():
        pb = pb_ref[...]
        xb = pb + (sh_ref[...] - pb) * mu_ref[...]
        r, kmod, v, gate, logw, a, kk = _rwkv_front(
            xb, w0_ref[...], w2_ref[...], a0_ref[...], a2_ref[...], kk_ref[...], ka_ref[...], ones_blk)
        for slot, z in enumerate((-kk, kk * a, kmod, v, r, jnp.exp(logw))):
            vec_scr[slot] = z.T
        for slot, z in enumerate((r, kmod, v, gate)):
            row_scr[slot] = z

    base = pl.multiple_of(h * HS_B, HS_B)
    chan = pl.ds(base, HS_B)
    nkk = vec_scr[0, chan, :]
    kka = vec_scr[1, chan, :]
    kmod_t = vec_scr[2, chan, :]
    r_t = vec_scr[4, chan, :]
    wdec = vec_scr[5, chan, :]

    def body(vi, carry):
        s0 = s_ref[0, vi]
        sa = jnp.sum(s0 * nkk, axis=0, keepdims=True)
        s1 = s0 * wdec + sa * kka + vec_scr[3, pl.ds(base + vi, 1), :] * kmod_t
        s_out_ref[0, vi] = s1
        y_scr[pl.ds(base + vi, 1), :] = jnp.sum(s1 * r_t, axis=0, keepdims=True)
        return carry

    lax.fori_loop(0, HS_B, body, 0, unroll=4)

    @pl.when(h == HB - 1)
    def _():
        o_ref[...] = _rwkv_back(y_scr[...].T, row_scr[0], row_scr[1], row_scr[2], row_scr[3],
                                rk_ref[...], lnw_ref[...], lnb_ref[...], ones_blk)


def _rwkv_step_lanes(pb, shift, s_hvkb, mu, w0, w2, a0, a2, k_k, k_a, rk, lnw, lnb, ones_blk):
    n = pb.shape[0]
    fixed = lambda i: (0, 0)
    vec = pl.BlockSpec((1, MIX_B), fixed)
    state = pl.BlockSpec((1, HS_B, HS_B, n), lambda i: (i, 0, 0, 0))
    return pl.pallas_call(
        _rwkv_step_lanes_kernel,
        grid=(HB,),
        in_specs=[pl.BlockSpec((n, B_W), fixed),
                  pl.BlockSpec((n, B_W), fixed),
                  state,
                  pl.BlockSpec((1, B_W), fixed),
                  vec, pl.BlockSpec((LORA, MIX_B), fixed),
                  vec, pl.BlockSpec((LORA, MIX_B), fixed),
                  vec, vec, vec, vec, vec,
                  pl.BlockSpec((MIX_B, MIX_B), fixed)],
        out_specs=[pl.BlockSpec((n, MIX_B), fixed), state],
        out_shape=[jax.ShapeDtypeStruct((n, MIX_B), F32),
                   jax.ShapeDtypeStruct((HB, HS_B, HS_B, n), F32)],
        scratch_shapes=[pltpu.VMEM((6, MIX_B, n), F32),
                        pltpu.VMEM((4, n, MIX_B), F32),
                        pltpu.VMEM((MIX_B, n), F32)],
        compiler_params=pltpu.CompilerParams(dimension_semantics=("arbitrary",),
                                             vmem_limit_bytes=VMEM_LIMIT),
        name="rwkv_step",
    )(pb, shift, s_hvkb, mu, w0, w2, a0, a2, k_k, k_a, rk, lnw, lnb, ones_blk)


def _block_ones(width, group):
    idx = jnp.arange(width) // group
    return (idx[:, None] == idx[None, :]).astype(BF16)


def kernel(x_prompt, x_sample, state_a_mat, state_a_conv, state_b_mat, state_b_shift, norm_w, w_in,
           conv_w, a_log, dt_bias, a_norm_w, mu, w0, w2, a0, a2, k_k, k_a, r_k, ln_w, ln_b, w_out,
           final_norm_w):
    bsz, seq, _ = x_prompt.shape
    nsmp = x_sample.shape[0]
    assert bsz % NSEQ == 0 and bsz % NSEQ_A == 0 and seq % TC == 0 and seq % TC_A == 0
    assert nsmp % SB == 0 and x_sample.shape[1] == 1

    w_all = w_in[0].astype(BF16)
    w_o = w_out[0].astype(BF16)
    hp = jnp.pad(jnp.concatenate([a_log, dt_bias], axis=0), ((0, 6), (HA, BD_W - 2 * HA)))
    nw = norm_w[0].reshape(1, D_MODEL)
    fw = final_norm_w.reshape(1, D_MODEL)
    anw = jnp.tile(a_norm_w[0].reshape(1, DK_A), (1, HA))
    ones_b = _block_ones(MIX_B, HS_B)
    row = lambda z: z[0].reshape(1, -1)
    b_params = (row(mu), row(w0), w2[0], row(a0), a2[0], row(k_k), row(k_a), row(r_k), row(ln_w), row(ln_b),
                ones_b)

    xp = x_prompt.reshape(bsz * seq, D_MODEL)
    pa, pb, bd = _inproj(xp, nw, w_all)
    ob, p_bmat, p_bshift = _rwkv_prompt(pb.reshape(bsz, seq, B_W), *b_params)
    y_prompt, p_amat, p_aconv = _gdn_prompt(pa.reshape(bsz, seq, A_MAIN), bd.reshape(bsz, seq, BD_W),
                                            conv_w[0], hp, anw, x_prompt, ob, w_o, fw)

    xs = x_sample.reshape(nsmp, D_MODEL)
    sa_, sb_, sbd = _inproj(xs, nw, w_all)
    sconv_t = jnp.swapaxes(state_a_conv[0], 0, 1)
    soa, s_amat, s_aconv_t = _gdn_step(sa_, sbd, sconv_t, state_a_mat[0], conv_w[0], hp, anw)
    sob, s_bmat_t = _rwkv_step_lanes(sb_, state_b_shift[0], jnp.transpose(state_b_mat[0], (1, 2, 3, 0)),
                                     *b_params)
    s_bmat = jnp.transpose(s_bmat_t, (3, 0, 1, 2))
    y_sample = _outproj(xs, soa, sob, w_o, fw)

    return (y_prompt, y_sample.reshape(nsmp, 1, D_MODEL),
            p_amat[None], p_aconv[None], p_bmat[None], p_bshift.reshape(1, bsz, B_W),
            s_amat[None], jnp.swapaxes(s_aconv_t, 0, 1)[None], s_bmat[None], sb_[None])
```

```python
import jax
import jax.numpy as jnp
from jax import lax
from jax.experimental import pallas as pl
from jax.experimental.pallas import tpu as pltpu

F32 = jnp.float32
BF16 = jnp.bfloat16

D_MODEL = 1024
MIX_A = 512
MIX_B = 512
HA = 4
DK_A = 128
HB = 8
HS_B = 64
PAIR = 2 * HS_B
CONV_W = 4
LORA = 64
A_QKV = 3 * MIX_A
A_MAIN = A_QKV + MIX_A
B_W = 4 * MIX_B + 2 * LORA
BD_W = 128
RMS_EPS = 1e-6
GN_EPS = 64e-5
CHUNK = 64
TC = 256
TC_A = 256
NSEQ_A = 2
NSEQ = 4
SB = 16
VMEM_LIMIT = 56 * 1024 * 1024

EXP_NEG_HALF = 0.6065306597126334
NN = (((1,), (0,)), ((), ()))
NT = (((1,), (1,)), ((), ()))
TN = (((0,), (0,)), ((), ()))


def _dot(a, b, dn=NN):
    return lax.dot_general(a, b, dn, preferred_element_type=F32)


def _sigmoid(x):
    return 0.5 * jnp.tanh(0.5 * x) + 0.5


def _silu(x):
    h = 0.5 * x
    return h * jnp.tanh(h) + h


def _softplus(x):
    return jnp.maximum(x, 0.0) + jnp.log(1.0 + jnp.exp(-jnp.abs(x)))


def _l2norm(x):
    return x * lax.rsqrt(jnp.sum(x * x, axis=-1, keepdims=True) + 1e-12)


def _group_sum(x, ones_blk):
    return _dot(x.astype(BF16), ones_blk)


def _iota2(shape, dim):
    return lax.broadcasted_iota(jnp.int32, shape, dim)


def _chunk_cumsum(x):
    n = x.shape[0]
    r = _iota2((n, n), 0)
    c = _iota2((n, n), 1)
    tril = jnp.where(((r // CHUNK) == (c // CHUNK)) & (c <= r), 1.0, 0.0).astype(BF16)
    hi = x.astype(BF16)
    rest = x - hi.astype(F32)
    mid = rest.astype(BF16)
    lo = (rest - mid.astype(F32)).astype(BF16)
    return _dot(tril, hi) + _dot(tril, mid) + _dot(tril, lo)


def _tri_inv_many(lows):
    n = lows[0].shape[0]
    r = _iota2((n, n), 0)
    c = _iota2((n, n), 1)
    eye = jnp.where(r == c, 1.0, 0.0).astype(F32)
    base = (r // 8) == (c // 8)
    l0 = [jnp.where(base, low, 0.0) for low in lows]
    l2 = [_dot(a, a) for a in l0]
    l4 = [_dot(a, a) for a in l2]
    d = [_dot(eye - a, eye + b) for a, b in zip(l0, l2)]
    d = [_dot(a, eye + b) for a, b in zip(d, l4)]
    s = 8
    while s < n:
        sel = ((r // (2 * s)) == (c // (2 * s))) & ((r // s) != (c // s))
        t = [_dot(jnp.where(sel, low, 0.0), a) for low, a in zip(lows, d)]
        d = [a - _dot(a, b) for a, b in zip(d, t)]
        s *= 2
    return d


def _pair_diag(x):
    left = _iota2(x.shape, 1) < x.shape[1] // 2
    return jnp.concatenate([jnp.where(left, x, 0.0), jnp.where(left, 0.0, x)], axis=0)


def _tri_inv_pairs(lows):
    n = lows[0].shape[0]
    r = _iota2((n, 2 * n), 0)
    lane = _iota2((n, 2 * n), 1)
    c = jnp.where(lane >= n, lane - n, lane)
    mul = lambda a, b: _dot(a, _pair_diag(b))
    eye = jnp.where(r == c, 1.0, 0.0).astype(F32)
    base = (r // 8) == (c // 8)
    l0 = [jnp.where(base, low, 0.0) for low in lows]
    l2 = [mul(a, a) for a in l0]
    l4 = [mul(a, a) for a in l2]
    d = [mul(eye - a, eye + b) for a, b in zip(l0, l2)]
    d = [mul(a, eye + b) for a, b in zip(d, l4)]
    s = 8
    while s < n:
        sel = ((r // (2 * s)) == (c // (2 * s))) & ((r // s) != (c // s))
        t = [mul(jnp.where(sel, low, 0.0), a) for low, a in zip(lows, d)]
        d = [a - mul(a, b) for a, b in zip(d, t)]
        s *= 2
    return d


def _inproj_kernel(x_ref, nw_ref, w_ref, oa_ref, ob_ref, obd_ref, wb_scr):
    a_w = A_MAIN + 2 * HA

    @pl.when(pl.program_id(0) == 0)
    def _():
        wb_scr[...] = jnp.concatenate(
            [w_ref[:, a_w:], w_ref[:, A_MAIN:a_w], jnp.zeros((D_MODEL, BD_W - 2 * HA), BF16)], axis=1)

    x = x_ref[...]
    xn = (x * lax.rsqrt(jnp.mean(x * x, axis=-1, keepdims=True) + RMS_EPS) * nw_ref[...]).astype(BF16)
    oa_ref[...] = _dot(xn, w_ref[:, 0:A_MAIN])
    pb = _dot(xn, wb_scr[...])
    ob_ref[...] = pb[:, :B_W]
    obd_ref[...] = pb[:, B_W:]


def _inproj(x2d, norm_w, w_all):
    n = x2d.shape[0]
    tm = min(512, n)
    row = lambda i: (i, 0)
    fixed = lambda i: (0, 0)
    return pl.pallas_call(
        _inproj_kernel,
        grid=(n // tm,),
        in_specs=[pl.BlockSpec((tm, D_MODEL), row),
                  pl.BlockSpec((1, D_MODEL), fixed),
                  pl.BlockSpec(w_all.shape, fixed)],
        out_specs=[pl.BlockSpec((tm, A_MAIN), row),
                   pl.BlockSpec((tm, B_W), row),
                   pl.BlockSpec((tm, BD_W), row)],
        out_shape=[jax.ShapeDtypeStruct((n, A_MAIN), F32),
                   jax.ShapeDtypeStruct((n, B_W), F32),
                   jax.ShapeDtypeStruct((n, BD_W), F32)],
        scratch_shapes=[pltpu.VMEM((D_MODEL, B_W + BD_W), BF16)],
        compiler_params=pltpu.CompilerParams(dimension_semantics=("arbitrary",),
                                             vmem_limit_bytes=VMEM_LIMIT),
        name="inproj",
    )(x2d, norm_w, w_all)


def _outproj_kernel(x_ref, ma_ref, mb_ref, w_ref, fw_ref, y_ref):
    h = (x_ref[...]
         + _dot(ma_ref[...].astype(BF16), w_ref[0:MIX_A, :])
         + _dot(mb_ref[...].astype(BF16), w_ref[MIX_A:, :]))
    y_ref[...] = h * lax.rsqrt(jnp.mean(h * h, axis=-1, keepdims=True) + RMS_EPS) * fw_ref[...]


def _outproj(x2d, mix_a, mix_b, w_out, final_w):
    n = x2d.shape[0]
    tm = min(1024, n)
    row = lambda i: (i, 0)
    fixed = lambda i: (0, 0)
    return pl.pallas_call(
        _outproj_kernel,
        grid=(n // tm,),
        in_specs=[pl.BlockSpec((tm, D_MODEL), row),
                  pl.BlockSpec((tm, MIX_A), row),
                  pl.BlockSpec((tm, MIX_B), row),
                  pl.BlockSpec((D_MODEL, D_MODEL), fixed),
                  pl.BlockSpec((1, D_MODEL), fixed)],
        out_specs=pl.BlockSpec((tm, D_MODEL), row),
        out_shape=jax.ShapeDtypeStruct((n, D_MODEL), F32),
        compiler_params=pltpu.CompilerParams(dimension_semantics=("arbitrary",),
                                             vmem_limit_bytes=VMEM_LIMIT),
        name="outproj",
    )(x2d, mix_a, mix_b, w_out, final_w)


def _gdn_gates(bd, hp):
    beta = _sigmoid(bd)
    g = -jnp.exp(hp[0:1, :]) * _softplus(bd + hp[1:2, :])
    return beta, g


def _gdn_qk(qkv):
    heads = lambda z: [z[:, h * DK_A:(h + 1) * DK_A] for h in range(HA)]
    qn = jnp.concatenate([_l2norm(x) * (DK_A ** -0.5) for x in heads(qkv[:, 0:MIX_A])], axis=-1)
    kn = jnp.concatenate([_l2norm(x) for x in heads(qkv[:, MIX_A:2 * MIX_A])], axis=-1)
    return qn, kn


def _gdn_out(o, gate, anw):
    on = jnp.concatenate(
        [x * lax.rsqrt(jnp.mean(x * x, axis=-1, keepdims=True) + RMS_EPS)
         for x in [o[:, h * DK_A:(h + 1) * DK_A] for h in range(HA)]], axis=-1)
    return on * anw * _silu(gate)


def _gdn_prompt_kernel(pa_ref, bd_ref, cw_ref, hp_ref, anw_ref, x_ref, mb_ref, wo_ref, fw_ref,
                       y_ref, s_out_ref, c_out_ref, ext_ref, s_ref):
    c = pl.program_id(1)

    nseq, tc = pa_ref.shape[0], pa_ref.shape[1]

    @pl.when(c == 0)
    def _():
        ext_ref[...] = jnp.zeros_like(ext_ref)
        s_ref[...] = jnp.zeros_like(s_ref)

    cw = cw_ref[...]
    conv = []
    for s in range(nseq):
        u = pa_ref[s, :, 0:A_QKV]
        full = jnp.concatenate([ext_ref[s], u], axis=0)
        conv.append(u * cw[3:4, :] + sum(
            pltpu.roll(full, CONV_W - 1 - i, 0)[8:, :] * cw[i:i + 1, :] for i in range(CONV_W - 1)))
        ext_ref[s] = u[tc - 8:, :]
        c_out_ref[s] = ext_ref[s, 8 - (CONV_W - 1):, :]
    qkv = _silu(jnp.concatenate(conv, axis=0))

    qn, kn = _gdn_qk(qkv)
    beta_all, g_all = _gdn_gates(bd_ref[...].reshape(nseq * tc, BD_W), hp_ref[...])
    gcum = jnp.concatenate([_chunk_cumsum(g_all[s * tc:(s + 1) * tc, :]) for s in range(nseq)],
                           axis=0)
    gcum_t = gcum.T

    nrow = nseq * tc
    wide = lambda z, first: jnp.concatenate(
        [jnp.broadcast_to(z[:, first + h:first + h + 1], (nrow, DK_A)) for h in range(HA)], axis=1)
    bw, gw = wide(beta_all, 0), wide(gcum, HA)
    half = _iota2((nrow, PAIR), 1) < CHUNK
    narrow = lambda zw: jnp.concatenate(
        [jnp.where(half, zw[:, 2 * p * DK_A:(2 * p + 1) * DK_A], zw[:, (2 * p + 1) * DK_A:(2 * p + 2) * DK_A])
         for p in range(HA // 2)], axis=1)
    b64, g64 = narrow(bw), narrow(gw)
    v_all = qkv[:, 2 * MIX_A:]
    egw = jnp.exp(gw)
    bek = bw * egw * kn
    bv = bw * v_all
    qg = qn * egw

    rowp = _iota2((CHUNK, PAIR), 0)
    lane = _iota2((CHUNK, PAIR), 1)
    colp = jnp.where(lane >= CHUNK, lane - CHUNK, lane)
    incl = colp <= rowp
    strict = colp < rowp

    nj = nseq * tc // CHUNK
    npair = HA // 2
    jp = [(j, p) for j in range(nj) for p in range(npair)]
    jh = [(j, h) for j in range(nj) for h in range(HA)]
    rows_of = lambda j: slice(j * CHUNK, (j + 1) * CHUNK)
    lanes_of = lambda h: slice(h * DK_A, (h + 1) * DK_A)
    pair64 = lambda p: slice(p * PAIR, (p + 1) * PAIR)
    pairw = lambda p: slice(2 * p * DK_A, 2 * (p + 1) * DK_A)
    glast = [gw[(j + 1) * CHUNK - 1:(j + 1) * CHUNK, :] for j in range(nj)]
    kdec = [kn[rows_of(j), :] * jnp.exp(glast[j] - gw[rows_of(j), :]) for j in range(nj)]
    eglast = [jnp.exp(glast[j]) for j in range(nj)]
    gr = [jnp.concatenate([gcum_t[HA + 2 * p:HA + 2 * p + 1, rows_of(j)],
                           gcum_t[HA + 2 * p + 1:HA + 2 * p + 2, rows_of(j)]], axis=1) for j, p in jp]
    decay = [jnp.where(incl, jnp.exp(jnp.where(incl, g64[rows_of(j), pair64(p)] - b, 0.0)), 0.0)
             for (j, p), b in zip(jp, gr)]
    kq = [_dot(jnp.concatenate([kn[rows_of(j), pairw(p)], qn[rows_of(j), pairw(p)]], axis=0),
               _pair_diag(kn[rows_of(j), pairw(p)]), NT) for j, p in jp]
    low = [jnp.where(strict, b64[rows_of(j), pair64(p)] * d * m[0:CHUNK, :], 0.0)
           for (j, p), d, m in zip(jp, decay, kq)]
    tinv = _tri_inv_pairs(low)
    zw = jnp.zeros((CHUNK, 2 * DK_A), F32)
    wu = [_dot(t, jnp.concatenate(
        [jnp.concatenate([bek[rows_of(j), lanes_of(2 * p)], bv[rows_of(j), lanes_of(2 * p)], zw], axis=1),
         jnp.concatenate([zw, bek[rows_of(j), lanes_of(2 * p + 1)], bv[rows_of(j), lanes_of(2 * p + 1)]],
                         axis=1)], axis=0))
          for t, (j, p) in zip(tinv, jp)]
    qwu_p = [_dot(d * m[CHUNK:, :], _pair_diag(x)) for d, m, x in zip(decay, kq, wu)]
    head_of = lambda z, j, h: z[j * npair + h // 2][:, (h % 2) * 2 * DK_A:(h % 2 + 1) * 2 * DK_A]
    qwu = [head_of(qwu_p, j, h) for j, h in jh]
    kwu = [_dot(kdec[j][:, lanes_of(h)], head_of(wu, j, h), TN) for j, h in jh]
    rd = _iota2((DK_A, DK_A), 0)
    cd = _iota2((DK_A, DK_A), 1)
    lhs = [jnp.concatenate([jnp.where(rd == cd, eglast[j][:, lanes_of(h)], 0.0) - kx[:, 0:DK_A],
                            qg[rows_of(j), lanes_of(h)] - qx[:, 0:DK_A]], axis=0)
           for (j, h), kx, qx in zip(jh, kwu, qwu)]

    o_rows = [None] * nj
    for jj in range(tc // CHUNK):
        for s in range(nseq):
            j = s * (tc // CHUNK) + jj
            idx = [j * HA + h for h in range(HA)]
            res = [_dot(lhs[i], s_ref[s, h]) for h, i in enumerate(idx)]
            for h, i in enumerate(idx):
                s_ref[s, h] = res[h][0:DK_A, :] + kwu[i][:, DK_A:]
            o_rows[j] = jnp.concatenate(
                [res[h][DK_A:, :] + qwu[i][:, DK_A:] for h, i in enumerate(idx)], axis=-1)

    gate = pa_ref[:, :, A_QKV:A_MAIN].reshape(nseq * tc, MIX_A)
    mix_a = _gdn_out(jnp.concatenate(o_rows, axis=0), gate, anw_ref[...])
    h = (x_ref[...].reshape(nseq * tc, D_MODEL)
         + _dot(mb_ref[...].reshape(nseq * tc, MIX_B), wo_ref[MIX_A:, :])
         + _dot(mix_a.astype(BF16), wo_ref[0:MIX_A, :]))
    y = h * lax.rsqrt(jnp.mean(h * h, axis=-1, keepdims=True) + RMS_EPS) * fw_ref[...]
    y_ref[...] = y.reshape(nseq, tc, D_MODEL)
    s_out_ref[...] = s_ref[...]


def _gdn_prompt(pa, bd, conv_w, hp, anw, x, mix_b, w_out, final_w):
    b, t, _ = pa.shape
    blk = lambda i, j: (i, j, 0)
    fixed = lambda i, j: (0, 0)
    return pl.pallas_call(
        _gdn_prompt_kernel,
        grid=(b // NSEQ_A, t // TC_A),
        in_specs=[pl.BlockSpec((NSEQ_A, TC_A, A_MAIN), blk),
                  pl.BlockSpec((NSEQ_A, TC_A, BD_W), blk),
                  pl.BlockSpec((CONV_W, A_QKV), fixed),
                  pl.BlockSpec((8, BD_W), fixed),
                  pl.BlockSpec((1, MIX_A), fixed),
                  pl.BlockSpec((NSEQ_A, TC_A, D_MODEL), blk),
                  pl.BlockSpec((NSEQ_A, TC_A, MIX_B), blk),
                  pl.BlockSpec((D_MODEL, D_MODEL), fixed),
                  pl.BlockSpec((1, D_MODEL), fixed)],
        out_specs=[pl.BlockSpec((NSEQ_A, TC_A, D_MODEL), blk),
                   pl.BlockSpec((NSEQ_A, HA, DK_A, DK_A), lambda i, j: (i, 0, 0, 0)),
                   pl.BlockSpec((NSEQ_A, CONV_W - 1, A_QKV), lambda i, j: (i, 0, 0))],
        out_shape=[jax.ShapeDtypeStruct((b, t, D_MODEL), F32),
                   jax.ShapeDtypeStruct((b, HA, DK_A, DK_A), F32),
                   jax.ShapeDtypeStruct((b, CONV_W - 1, A_QKV), F32)],
        scratch_shapes=[pltpu.VMEM((NSEQ_A, 8, A_QKV), F32),
                        pltpu.VMEM((NSEQ_A, HA, DK_A, DK_A), F32)],
        compiler_params=pltpu.CompilerParams(dimension_semantics=("arbitrary", "arbitrary"),
                                             vmem_limit_bytes=VMEM_LIMIT),
        name="gdn_prompt",
    )(pa, bd, conv_w, hp, anw, x, mix_b, w_out, final_w)


def _rwkv_front(xb, w0, w2, a0, a2, k_k, k_a, ones_blk):
    r = xb[:, 0:MIX_B]
    kr = xb[:, MIX_B:2 * MIX_B]
    vr = xb[:, 2 * MIX_B:3 * MIX_B]
    gate = xb[:, 3 * MIX_B:4 * MIX_B]
    wd = xb[:, 4 * MIX_B:4 * MIX_B + LORA]
    ad = xb[:, 4 * MIX_B + LORA:]
    logw = -(EXP_NEG_HALF * _sigmoid(w0 + _dot(jnp.tanh(wd), w2)))
    a = _sigmoid(a0 + _dot(ad, a2))
    kraw = kr * k_k
    kk = kraw * lax.rsqrt(_group_sum(kraw * kraw, ones_blk) + 1e-12)
    kmod = kr * (1.0 + (a - 1.0) * k_a)
    return r, kmod, vr, gate, logw, a, kk


def _rwkv_back(y, r, kmod, v, gate, rk, lnw, lnb, ones_blk):
    inv = 1.0 / HS_B
    yc = y - _group_sum(y, ones_blk) * inv
    var = _group_sum(yc * yc, ones_blk) * inv
    gn = yc * lax.rsqrt(var + GN_EPS) * lnw + lnb
    bonus = _group_sum(r * kmod * rk, ones_blk) * v
    return (gn + bonus) * _silu(gate)


def _rwkv_prompt_kernel(pb_ref, mu_ref, w0_ref, w2_ref, a0_ref, a2_ref, kk_ref, ka_ref,
                        rk_ref, lnw_ref, lnb_ref, ones_ref,
                        o_ref, s_out_ref, sh_out_ref, carry_ref, s_ref):
    c = pl.program_id(1)

    @pl.when(c == 0)
    def _():
        carry_ref[...] = jnp.zeros_like(carry_ref)
        s_ref[...] = jnp.zeros_like(s_ref)

    ones_blk = ones_ref[...]
    pb = pb_ref[...].reshape(NSEQ * TC, B_W)
    prev = []
    for s in range(NSEQ):
        pb_s = pb_ref[s]
        prev.append(pltpu.roll(jnp.concatenate([carry_ref[s], pb_s], axis=0), 1, 0)[8:, :])
        carry_ref[s] = pb_s[TC - 8:, :]
        sh_out_ref[s] = pb_s[TC - 1:, :]
    prev = jnp.concatenate(prev, axis=0)
    xb = pb + (prev - pb) * mu_ref[...]
    r, kmod, v, gate, logw, a, kk = _rwkv_front(
        xb, w0_ref[...], w2_ref[...], a0_ref[...], a2_ref[...], kk_ref[...], ka_ref[...], ones_blk)

    g = jnp.concatenate([_chunk_cumsum(logw[s * TC:(s + 1) * TC, :]) for s in range(NSEQ)],
                        axis=0)
    eg = jnp.exp(g)
    egn = jnp.exp(-g)
    ag = -kk * jnp.exp(g - logw)
    kka = kk * a
    bg = kka * egn
    kg = kmod * egn
    rg = r * eg

    rowp = _iota2((CHUNK, PAIR), 0)
    lane = _iota2((CHUNK, PAIR), 1)
    colp = jnp.where(lane >= HS_B, lane - HS_B, lane)
    left = lane < HS_B
    incl = colp <= rowp
    strict = colp < rowp
    eye = colp == rowp
    zero = jnp.zeros((CHUNK, PAIR), F32)
    pack = lambda f: jnp.where(left, f[0:HS_B, :], f[HS_B:, :])

    nj = NSEQ * TC // CHUNK
    npair = HB // 2
    jp = [(j, p) for j in range(nj) for p in range(npair)]
    rows_of = lambda j: slice(j * CHUNK, (j + 1) * CHUNK)
    lanes_of = lambda p: slice(p * PAIR, (p + 1) * PAIR)
    glast = [g[(j + 1) * CHUNK - 1:(j + 1) * CHUNK, :] for j in range(nj)]
    eglast = [jnp.exp(glast[j]) for j in range(nj)]
    bdec = [bg[rows_of(j), :] * eglast[j] for j in range(nj)]
    kdec = [kg[rows_of(j), :] * eglast[j] for j in range(nj)]

    cut = lambda z: [z[rows_of(j), lanes_of(p)] for j, p in jp]
    ag_p, rg_p, bg_p, kg_p, v_p = cut(ag), cut(rg), cut(bg), cut(kg), cut(v)
    bdec_p = [bdec[j][:, lanes_of(p)] for j, p in jp]
    kdec_p = [kdec[j][:, lanes_of(p)] for j, p in jp]
    eglast_p = [eglast[j][:, lanes_of(p)] for j, p in jp]
    amat = [_dot(jnp.concatenate([a, b], axis=0),
                 jnp.concatenate([_pair_diag(c_), _pair_diag(d)], axis=0), NT)
            for a, b, c_, d in zip(ag_p, rg_p, bg_p, kg_p)]
    a_ab = [jnp.where(strict, m[0:CHUNK, 0:PAIR], 0.0) for m in amat]
    a_ak = [jnp.where(strict, m[0:CHUNK, PAIR:], 0.0) for m in amat]
    a_rb = [jnp.where(incl, m[CHUNK:, 0:PAIR], 0.0) for m in amat]
    a_rk = [jnp.where(incl, m[CHUNK:, PAIR:], 0.0) for m in amat]
    tinv = _tri_inv_pairs([-a for a in a_ab])
    kv = [_dot(jnp.concatenate([a, b], axis=0), _pair_diag(c_)) for a, b, c_ in zip(a_ak, a_rk, v_p)]
    wu = [_dot(t, jnp.concatenate([_pair_diag(a), _pair_diag(x[0:CHUNK, :])], axis=1))
          for t, a, x in zip(tinv, ag_p, kv)]
    rwu = [_dot(a, jnp.concatenate([_pair_diag(x[:, 0:PAIR]), _pair_diag(x[:, PAIR:])], axis=1))
           for a, x in zip(a_rb, wu)]
    mn = [_dot(jnp.concatenate([a, b], axis=0),
               jnp.concatenate([x, jnp.concatenate([zero, c_], axis=1)], axis=0), TN)
          for a, b, x, c_ in zip(bdec_p, kdec_p, wu, v_p)]
    lhs = [jnp.concatenate([jnp.where(eye, e, 0.0) + pack(m[:, 0:PAIR]), a + x[:, 0:PAIR]], axis=0)
           for e, m, a, x in zip(eglast_p, mn, rg_p, rwu)]

    y_rows = [None] * nj
    for jj in range(TC // CHUNK):
        for s in range(NSEQ):
            j = s * (TC // CHUNK) + jj
            idx = [j * npair + p for p in range(npair)]
            res = [_dot(lhs[i], _pair_diag(s_ref[s, p])) for p, i in enumerate(idx)]
            for p, i in enumerate(idx):
                s_ref[s, p] = res[p][0:HS_B, :] + pack(mn[i][:, PAIR:])
            y_rows[j] = jnp.concatenate(
                [res[p][HS_B:, :] + rwu[i][:, PAIR:] + kv[i][CHUNK:, :] for p, i in enumerate(idx)],
                axis=-1)

    out = _rwkv_back(jnp.concatenate(y_rows, axis=0), r, kmod, v, gate,
                     rk_ref[...], lnw_ref[...], lnb_ref[...], ones_blk)
    o_ref[...] = out.reshape(NSEQ, TC, MIX_B).astype(o_ref.dtype)
    for s in range(NSEQ):
        for p in range(npair):
            st = s_ref[s, p].T
            s_out_ref[s, 2 * p] = st[0:HS_B, :]
            s_out_ref[s, 2 * p + 1] = st[HS_B:, :]


def _rwkv_prompt(pb, mu, w0, w2, a0, a2, k_k, k_a, rk, lnw, lnb, ones_blk):
    b, t, _ = pb.shape
    blk = lambda i, j: (i, j, 0)
    fixed = lambda i, j: (0, 0)
    vec = pl.BlockSpec((1, MIX_B), fixed)
    return pl.pallas_call(
        _rwkv_prompt_kernel,
        grid=(b // NSEQ, t // TC),
        in_specs=[pl.BlockSpec((NSEQ, TC, B_W), blk),
                  pl.BlockSpec((1, B_W), fixed),
                  vec, pl.BlockSpec((LORA, MIX_B), fixed),
                  vec, pl.BlockSpec((LORA, MIX_B), fixed),
                  vec, vec, vec, vec, vec,
                  pl.BlockSpec((MIX_B, MIX_B), fixed)],
        out_specs=[pl.BlockSpec((NSEQ, TC, MIX_B), blk),
                   pl.BlockSpec((NSEQ, HB, HS_B, HS_B), lambda i, j: (i, 0, 0, 0)),
                   pl.BlockSpec((NSEQ, 1, B_W), lambda i, j: (i, 0, 0))],
        out_shape=[jax.ShapeDtypeStruct((b, t, MIX_B), BF16),
                   jax.ShapeDtypeStruct((b, HB, HS_B, HS_B), F32),
                   jax.ShapeDtypeStruct((b, 1, B_W), F32)],
        scratch_shapes=[pltpu.VMEM((NSEQ, 8, B_W), F32),
                        pltpu.VMEM((NSEQ, HB // 2, HS_B, PAIR), F32)],
        compiler_params=pltpu.CompilerParams(dimension_semantics=("arbitrary", "arbitrary"),
                                             vmem_limit_bytes=VMEM_LIMIT),
        name="rwkv_prompt",
    )(pb, mu, w0, w2, a0, a2, k_k, k_a, rk, lnw, lnb, ones_blk)


def _pad_rows(rows, width):
    return jnp.concatenate(rows + [jnp.zeros((8 - len(rows), width), F32)], axis=0)


def _gdn_step_kernel(pa_ref, bd_ref, sc_ref, s_ref, cw_ref, hp_ref, anw_ref,
                     o_ref, s_out_ref, c_out_ref):
    u = pa_ref[:, 0:A_QKV]
    cw = cw_ref[...]
    conv = (sc_ref[0] * cw[0:1, :] + sc_ref[1] * cw[1:2, :] + sc_ref[2] * cw[2:3, :] + u * cw[3:4, :])
    qkv = _silu(conv)
    c_out_ref[0] = sc_ref[1]
    c_out_ref[1] = sc_ref[2]
    c_out_ref[2] = u

    qn, kn = _gdn_qk(qkv)
    beta_all, g_all = _gdn_gates(bd_ref[...], hp_ref[...])
    eg_all = jnp.exp(g_all)
    lanes_of = lambda h: slice(h * DK_A, (h + 1) * DK_A)
    q = [qn[:, lanes_of(h)] for h in range(HA)]
    k = [kn[:, lanes_of(h)] for h in range(HA)]
    v = [qkv[:, 2 * MIX_A + h * DK_A:2 * MIX_A + (h + 1) * DK_A] for h in range(HA)]
    beta = [beta_all[:, h:h + 1] for h in range(HA)]
    eg = [eg_all[:, HA + h:HA + h + 1] for h in range(HA)]

    bh = [(b, h) for b in range(SB) for h in range(HA)]
    s0 = [s_ref[b, h] for b, h in bh]
    ks_qs = [_dot(_pad_rows([k[h][b:b + 1, :], q[h][b:b + 1, :]], DK_A), s) for (b, h), s in zip(bh, s0)]
    o_heads = []
    for h in range(HA):
        ks = jnp.concatenate([ks_qs[b * HA + h][0:1, :] for b in range(SB)], axis=0)
        qs = jnp.concatenate([ks_qs[b * HA + h][1:2, :] for b in range(SB)], axis=0)
        u2 = beta[h] * (v[h] - eg[h] * ks)
        for b in range(SB):
            s_out_ref[b, h] = (eg[h][b:b + 1, :] * s0[b * HA + h]
                               + _dot(_pad_rows([k[h][b:b + 1, :]], DK_A),
                                      _pad_rows([u2[b:b + 1, :]], DK_A), TN))
        o_heads.append(eg[h] * qs + jnp.sum(q[h] * k[h], axis=-1, keepdims=True) * u2)

    o_ref[...] = _gdn_out(jnp.concatenate(o_heads, axis=-1), pa_ref[:, A_QKV:A_MAIN], anw_ref[...])


def _gdn_step(pa, bd, sconv_t, s_a, conv_w, hp, anw):
    n = pa.shape[0]
    row = lambda i: (i, 0)
    fixed = lambda i: (0, 0)
    return pl.pallas_call(
        _gdn_step_kernel,
        grid=(n // SB,),
        in_specs=[pl.BlockSpec((SB, A_MAIN), row),
                  pl.BlockSpec((SB, BD_W), row),
                  pl.BlockSpec((CONV_W - 1, SB, A_QKV), lambda i: (0, i, 0)),
                  pl.BlockSpec((SB, HA, DK_A, DK_A), lambda i: (i, 0, 0, 0)),
                  pl.BlockSpec((CONV_W, A_QKV), fixed),
                  pl.BlockSpec((8, BD_W), fixed),
                  pl.BlockSpec((1, MIX_A), fixed)],
        out_specs=[pl.BlockSpec((SB, MIX_A), row),
                   pl.BlockSpec((SB, HA, DK_A, DK_A), lambda i: (i, 0, 0, 0)),
                   pl.BlockSpec((CONV_W - 1, SB, A_QKV), lambda i: (0, i, 0))],
        out_shape=[jax.ShapeDtypeStruct((n, MIX_A), F32),
                   jax.ShapeDtypeStruct((n, HA, DK_A, DK_A), F32),
                   jax.ShapeDtypeStruct((CONV_W - 1, n, A_QKV), F32)],
        compiler_params=pltpu.CompilerParams(dimension_semantics=("arbitrary",),
                                             vmem_limit_bytes=VMEM_LIMIT),
        name="gdn_step",
    )(pa, bd, sconv_t, s_a, conv_w, hp, anw)


def _rwkv_step_lanes_kernel(pb_ref, sh_ref, s_ref, mu_ref, w0_ref, w2_ref, a0_ref, a2_ref, kk_ref, ka_ref,
                            rk_ref, lnw_ref, lnb_ref, ones_ref,
                            o_ref, s_out_ref, vec_scr, row_scr, y_scr):
    h = pl.program_id(0)
    ones_blk = ones_ref[...]

    @pl.when(h == 0)
    def _():
        pb = pb_ref[...]
        xb = pb + (sh_ref[...] - pb) * mu_ref[...]
        r, kmod, v, gate, logw, a, kk = _rwkv_front(
            xb, w0_ref[...], w2_ref[...], a0_ref[...], a2_ref[...], kk_ref[...], ka_ref[...], ones_blk)
        for slot, z in enumerate((-kk, kk * a, kmod, v, r, jnp.exp(logw))):
            vec_scr[slot] = z.T
        for slot, z in enumerate((r, kmod, v, gate)):
            row_scr[slot] = z

    base = pl.multiple_of(h * HS_B, HS_B)
    chan = pl.ds(base, HS_B)
    nkk = vec_scr[0, chan, :]
    kka = vec_scr[1, chan, :]
    kmod_t = vec_scr[2, chan, :]
    r_t = vec_scr[4, chan, :]
    wdec = vec_scr[5, chan, :]

    def body(vi, carry):
        s0 = s_ref[0, vi]
        sa = jnp.sum(s0 * nkk, axis=0, keepdims=True)
        s1 = s0 * wdec + sa * kka + vec_scr[3, pl.ds(base + vi, 1), :] * kmod_t
        s_out_ref[0, vi] = s1
        y_scr[pl.ds(base + vi, 1), :] = jnp.sum(s1 * r_t, axis=0, keepdims=True)
        return carry

    lax.fori_loop(0, HS_B, body, 0, unroll=4)

    @pl.when(h == HB - 1)
    def _():
        o_ref[...] = _rwkv_back(y_scr[...].T, row_scr[0], row_scr[1], row_scr[2], row_scr[3],
                                rk_ref[...], lnw_ref[...], lnb_ref[...], ones_blk)


def _rwkv_step_lanes(pb, shift, s_hvkb, mu, w0, w2, a0, a2, k_k, k_a, rk, lnw, lnb, ones_blk):
    n = pb.shape[0]
    fixed = lambda i: (0, 0)
    vec = pl.BlockSpec((1, MIX_B), fixed)
    state = pl.BlockSpec((1, HS_B, HS_B, n), lambda i: (i, 0, 0, 0))
    return pl.pallas_call(
        _rwkv_step_lanes_kernel,
        grid=(HB,),
        in_specs=[pl.BlockSpec((n, B_W), fixed),
                  pl.BlockSpec((n, B_W), fixed),
                  state,
                  pl.BlockSpec((1, B_W), fixed),
                  vec, pl.BlockSpec((LORA, MIX_B), fixed),
                  vec, pl.BlockSpec((LORA, MIX_B), fixed),
                  vec, vec, vec, vec, vec,
                  pl.BlockSpec((MIX_B, MIX_B), fixed)],
        out_specs=[pl.BlockSpec((n, MIX_B), fixed), state],
        out_shape=[jax.ShapeDtypeStruct((n, MIX_B), F32),
                   jax.ShapeDtypeStruct((HB, HS_B, HS_B, n), F32)],
        scratch_shapes=[pltpu.VMEM((6, MIX_B, n), F32),
                        pltpu.VMEM((4, n, MIX_B), F32),
                        pltpu.VMEM((MIX_B, n), F32)],
        compiler_params=pltpu.CompilerParams(dimension_semantics=("arbitrary",),
                                             vmem_limit_bytes=VMEM_LIMIT),
        name="rwkv_step",
    )(pb, shift, s_hvkb, mu, w0, w2, a0, a2, k_k, k_a, rk, lnw, lnb, ones_blk)


def _block_ones(width, group):
    idx = jnp.arange(width) // group
    return (idx[:, None] == idx[None, :]).astype(BF16)


def kernel(x_prompt, x_sample, state_a_mat, state_a_conv, state_b_mat, state_b_shift, norm_w, w_in,
           conv_w, a_log, dt_bias, a_norm_w, mu, w0, w2, a0, a2, k_k, k_a, r_k, ln_w, ln_b, w_out,
           final_norm_w):
    bsz, seq, _ = x_prompt.shape
    nsmp = x_sample.shape[0]
    assert bsz % NSEQ == 0 and bsz % NSEQ_A == 0 and seq % TC == 0 and seq % TC_A == 0
    assert nsmp % SB == 0 and x_sample.shape[1] == 1

    w_all = w_in[0].astype(BF16)
    w_o = w_out[0].astype(BF16)
    hp = jnp.pad(jnp.concatenate([a_log, dt_bias], axis=0), ((0, 6), (HA, BD_W - 2 * HA)))
    nw = norm_w[0].reshape(1, D_MODEL)
    fw = final_norm_w.reshape(1, D_MODEL)
    anw = jnp.tile(a_norm_w[0].reshape(1, DK_A), (1, HA))
    ones_b = _block_ones(MIX_B, HS_B)
    row = lambda z: z[0].reshape(1, -1)
    b_params = (row(mu), row(w0), w2[0], row(a0), a2[0], row(k_k), row(k_a), row(r_k), row(ln_w), row(ln_b),
                ones_b)

    xp = x_prompt.reshape(bsz * seq, D_MODEL)
    pa, pb, bd = _inproj(xp, nw, w_all)
    ob, p_bmat, p_bshift = _rwkv_prompt(pb.reshape(bsz, seq, B_W), *b_params)
    y_prompt, p_amat, p_aconv = _gdn_prompt(pa.reshape(bsz, seq, A_MAIN), bd.reshape(bsz, seq, BD_W),
                                            conv_w[0], hp, anw, x_prompt, ob, w_o, fw)

    xs = x_sample.reshape(nsmp, D_MODEL)
    sa_, sb_, sbd = _inproj(xs, nw, w_all)
    sconv_t = jnp.swapaxes(state_a_conv[0], 0, 1)
    soa, s_amat, s_aconv_t = _gdn_step(sa_, sbd, sconv_t, state_a_mat[0], conv_w[0], hp, anw)
    sob, s_bmat_t = _rwkv_step_lanes(sb_, state_b_shift[0], jnp.transpose(state_b_mat[0], (1, 2, 3, 0)),
                                     *b_params)
    s_bmat = jnp.transpose(s_bmat_t, (3, 0, 1, 2))
    y_sample = _outproj(xs, soa, sob, w_o, fw)

    return (y_prompt, y_sample.reshape(nsmp, 1, D_MODEL),
            p_amat[None], p_aconv[None], p_bmat[None], p_bshift.reshape(1, bsz, B_W),
            s_amat[None], jnp.swapaxes(s_aconv_t, 0, 1)[None], s_bmat[None], sb_[None])
```

```python
import jax
import jax.numpy as jnp
from jax import lax
from jax.experimental import pallas as pl
from jax.experimental.pallas import tpu as pltpu

F32 = jnp.float32
BF16 = jnp.bfloat16

D_MODEL = 1024
MIX_A = 512
MIX_B = 512
HA = 4
DK_A = 128
HB = 8
HS_B = 64
PAIR = 2 * HS_B
CONV_W = 4
LORA = 64
A_QKV = 3 * MIX_A
A_MAIN = A_QKV + MIX_A
B_W = 4 * MIX_B + 2 * LORA
BD_W = 128
RMS_EPS = 1e-6
GN_EPS = 64e-5
CHUNK = 64
TC = 256
TC_A = 256
NSEQ_A = 2
NSEQ = 4
SB = 16
VMEM_LIMIT = 56 * 1024 * 1024

EXP_NEG_HALF = 0.6065306597126334
NN = (((1,), (0,)), ((), ()))
NT = (((1,), (1,)), ((), ()))
TN = (((0,), (0,)), ((), ()))


def _dot(a, b, dn=NN):
    return lax.dot_general(a, b, dn, preferred_element_type=F32)


def _sigmoid(x):
    return 0.5 * jnp.tanh(0.5 * x) + 0.5


def _silu(x):
    h = 0.5 * x
    return h * jnp.tanh(h) + h


def _softplus(x):
    return jnp.maximum(x, 0.0) + jnp.log(1.0 + jnp.exp(-jnp.abs(x)))


def _l2norm(x):
    return x * lax.rsqrt(jnp.sum(x * x, axis=-1, keepdims=True) + 1e-12)


def _group_sum(x, ones_blk):
    return _dot(x.astype(BF16), ones_blk)


def _iota2(shape, dim):
    return lax.broadcasted_iota(jnp.int32, shape, dim)


def _chunk_cumsum(x):
    n = x.shape[0]
    r = _iota2((n, n), 0)
    c = _iota2((n, n), 1)
    tril = jnp.where(((r // CHUNK) == (c // CHUNK)) & (c <= r), 1.0, 0.0).astype(BF16)
    hi = x.astype(BF16)
    rest = x - hi.astype(F32)
    mid = rest.astype(BF16)
    lo = (rest - mid.astype(F32)).astype(BF16)
    return _dot(tril, hi) + _dot(tril, mid) + _dot(tril, lo)


def _tri_inv_many(lows):
    n = lows[0].shape[0]
    r = _iota2((n, n), 0)
    c = _iota2((n, n), 1)
    eye = jnp.where(r == c, 1.0, 0.0).astype(F32)
    base = (r // 8) == (c // 8)
    l0 = [jnp.where(base, low, 0.0) for low in lows]
    l2 = [_dot(a, a) for a in l0]
    l4 = [_dot(a, a) for a in l2]
    d = [_dot(eye - a, eye + b) for a, b in zip(l0, l2)]
    d = [_dot(a, eye + b) for a, b in zip(d, l4)]
    s = 8
    while s < n:
        sel = ((r // (2 * s)) == (c // (2 * s))) & ((r // s) != (c // s))
        t = [_dot(jnp.where(sel, low, 0.0), a) for low, a in zip(lows, d)]
        d = [a - _dot(a, b) for a, b in zip(d, t)]
        s *= 2
    return d


def _pair_diag(x):
    left = _iota2(x.shape, 1) < x.shape[1] // 2
    return jnp.concatenate([jnp.where(left, x, 0.0), jnp.where(left, 0.0, x)], axis=0)


def _tri_inv_pairs(lows):
    n = lows[0].shape[0]
    r = _iota2((n, 2 * n), 0)
    lane = _iota2((n, 2 * n), 1)
    c = jnp.where(lane >= n, lane - n, lane)
    mul = lambda a, b: _dot(a, _pair_diag(b))
    eye = jnp.where(r == c, 1.0, 0.0).astype(F32)
    base = (r // 8) == (c // 8)
    l0 = [jnp.where(base, low, 0.0) for low in lows]
    l2 = [mul(a, a) for a in l0]
    l4 = [mul(a, a) for a in l2]
    d = [mul(eye - a, eye + b) for a, b in zip(l0, l2)]
    d = [mul(a, eye + b) for a, b in zip(d, l4)]
    s = 8
    while s < n:
        sel = ((r // (2 * s)) == (c // (2 * s))) & ((r // s) != (c // s))
        t = [mul(jnp.where(sel, low, 0.0), a) for low, a in zip(lows, d)]
        d = [a - mul(a, b) for a, b in zip(d, t)]
        s *= 2
    return d


def _inproj_kernel(x_ref, nw_ref, wt_ref, oa_ref, ob_ref, obd_ref, w_scr):
    a_w = A_MAIN + 2 * HA

    @pl.when(pl.program_id(0) == 0)
    def _():
        w_scr[0:A_MAIN, :] = wt_ref[0:A_MAIN, :].astype(BF16)
        w_scr[A_MAIN:A_MAIN + B_W, :] = wt_ref[a_w:, :].astype(BF16)
        w_scr[A_MAIN + B_W:, :] = jnp.concatenate(
            [wt_ref[A_MAIN:a_w, :], jnp.zeros((BD_W - 2 * HA, D_MODEL), F32)], axis=0).astype(BF16)

    x = x_ref[...]
    xn = (x * lax.rsqrt(jnp.mean(x * x, axis=-1, keepdims=True) + RMS_EPS) * nw_ref[...]).astype(BF16)
    oa_ref[...] = _dot(xn, w_scr[0:A_MAIN, :], NT)
    pb = _dot(xn, w_scr[A_MAIN:, :], NT)
    ob_ref[...] = pb[:, :B_W]
    obd_ref[...] = pb[:, B_W:]


def _inproj(x2d, norm_w, w_all):
    n = x2d.shape[0]
    tm = min(512, n)
    row = lambda i: (i, 0)
    fixed = lambda i: (0, 0)
    return pl.pallas_call(
        _inproj_kernel,
        grid=(n // tm,),
        in_specs=[pl.BlockSpec((tm, D_MODEL), row),
                  pl.BlockSpec((1, D_MODEL), fixed),
                  pl.BlockSpec(w_all.shape, fixed)],
        out_specs=[pl.BlockSpec((tm, A_MAIN), row),
                   pl.BlockSpec((tm, B_W), row),
                   pl.BlockSpec((tm, BD_W), row)],
        out_shape=[jax.ShapeDtypeStruct((n, A_MAIN), F32),
                   jax.ShapeDtypeStruct((n, B_W), F32),
                   jax.ShapeDtypeStruct((n, BD_W), F32)],
        scratch_shapes=[pltpu.VMEM((A_MAIN + B_W + BD_W, D_MODEL), BF16)],
        compiler_params=pltpu.CompilerParams(dimension_semantics=("arbitrary",),
                                             vmem_limit_bytes=VMEM_LIMIT),
        name="inproj",
    )(x2d, norm_w, w_all)


def _outproj_kernel(x_ref, ma_ref, mb_ref, w_ref, fw_ref, y_ref):
    h = (x_ref[...]
         + _dot(ma_ref[...].astype(BF16), w_ref[0:MIX_A, :])
         + _dot(mb_ref[...].astype(BF16), w_ref[MIX_A:, :]))
    y_ref[...] = h * lax.rsqrt(jnp.mean(h * h, axis=-1, keepdims=True) + RMS_EPS) * fw_ref[...]


def _outproj(x2d, mix_a, mix_b, w_out, final_w):
    n = x2d.shape[0]
    tm = min(1024, n)
    row = lambda i: (i, 0)
    fixed = lambda i: (0, 0)
    return pl.pallas_call(
        _outproj_kernel,
        grid=(n // tm,),
        in_specs=[pl.BlockSpec((tm, D_MODEL), row),
                  pl.BlockSpec((tm, MIX_A), row),
                  pl.BlockSpec((tm, MIX_B), row),
                  pl.BlockSpec((D_MODEL, D_MODEL), fixed),
                  pl.BlockSpec((1, D_MODEL), fixed)],
        out_specs=pl.BlockSpec((tm, D_MODEL), row),
        out_shape=jax.ShapeDtypeStruct((n, D_MODEL), F32),
        compiler_params=pltpu.CompilerParams(dimension_semantics=("arbitrary",),
                                             vmem_limit_bytes=VMEM_LIMIT),
        name="outproj",
    )(x2d, mix_a, mix_b, w_out, final_w)


def _gdn_gates(bd, hp):
    beta = _sigmoid(bd)
    g = -jnp.exp(hp[0:1, :]) * _softplus(bd + hp[1:2, :])
    return beta, g


def _gdn_qk(qkv):
    heads = lambda z: [z[:, h * DK_A:(h + 1) * DK_A] for h in range(HA)]
    qn = jnp.concatenate([_l2norm(x) * (DK_A ** -0.5) for x in heads(qkv[:, 0:MIX_A])], axis=-1)
    kn = jnp.concatenate([_l2norm(x) for x in heads(qkv[:, MIX_A:2 * MIX_A])], axis=-1)
    return qn, kn


def _gdn_out(o, gate, anw):
    on = jnp.concatenate(
        [x * lax.rsqrt(jnp.mean(x * x, axis=-1, keepdims=True) + RMS_EPS)
         for x in [o[:, h * DK_A:(h + 1) * DK_A] for h in range(HA)]], axis=-1)
    return on * anw * _silu(gate)


def _gdn_prompt_kernel(pa_ref, bd_ref, cw_ref, hp_ref, anw_ref, x_ref, mb_ref, wo_ref, fw_ref,
                       y_ref, s_out_ref, c_out_ref, ext_ref, s_ref):
    c = pl.program_id(1)

    nseq, tc = pa_ref.shape[0], pa_ref.shape[1]

    @pl.when(c == 0)
    def _():
        ext_ref[...] = jnp.zeros_like(ext_ref)
        s_ref[...] = jnp.zeros_like(s_ref)

    cw = cw_ref[...]
    conv = []
    for s in range(nseq):
        u = pa_ref[s, :, 0:A_QKV]
        full = jnp.concatenate([ext_ref[s], u], axis=0)
        conv.append(u * cw[3:4, :] + sum(
            pltpu.roll(full, CONV_W - 1 - i, 0)[8:, :] * cw[i:i + 1, :] for i in range(CONV_W - 1)))
        ext_ref[s] = u[tc - 8:, :]
        c_out_ref[s] = ext_ref[s, 8 - (CONV_W - 1):, :]
    qkv = _silu(jnp.concatenate(conv, axis=0))

    qn, kn = _gdn_qk(qkv)
    beta_all, g_all = _gdn_gates(bd_ref[...].reshape(nseq * tc, BD_W), hp_ref[...])
    gcum = jnp.concatenate([_chunk_cumsum(g_all[s * tc:(s + 1) * tc, :]) for s in range(nseq)],
                           axis=0)
    gcum_t = gcum.T

    nrow = nseq * tc
    wide = lambda z, first: jnp.concatenate(
        [jnp.broadcast_to(z[:, first + h:first + h + 1], (nrow, DK_A)) for h in range(HA)], axis=1)
    bw, gw = wide(beta_all, 0), wide(gcum, HA)
    half = _iota2((nrow, PAIR), 1) < CHUNK
    narrow = lambda zw: jnp.concatenate(
        [jnp.where(half, zw[:, 2 * p * DK_A:(2 * p + 1) * DK_A], zw[:, (2 * p + 1) * DK_A:(2 * p + 2) * DK_A])
         for p in range(HA // 2)], axis=1)
    b64, g64 = narrow(bw), narrow(gw)
    v_all = qkv[:, 2 * MIX_A:]
    egw = jnp.exp(gw)
    bek = bw * egw * kn
    bv = bw * v_all
    qg = qn * egw

    rowp = _iota2((CHUNK, PAIR), 0)
    lane = _iota2((CHUNK, PAIR), 1)
    colp = jnp.where(lane >= CHUNK, lane - CHUNK, lane)
    incl = colp <= rowp
    strict = colp < rowp

    nj = nseq * tc // CHUNK
    npair = HA // 2
    jp = [(j, p) for j in range(nj) for p in range(npair)]
    jh = [(j, h) for j in range(nj) for h in range(HA)]
    rows_of = lambda j: slice(j * CHUNK, (j + 1) * CHUNK)
    lanes_of = lambda h: slice(h * DK_A, (h + 1) * DK_A)
    pair64 = lambda p: slice(p * PAIR, (p + 1) * PAIR)
    pairw = lambda p: slice(2 * p * DK_A, 2 * (p + 1) * DK_A)
    glast = [gw[(j + 1) * CHUNK - 1:(j + 1) * CHUNK, :] for j in range(nj)]
    kdec = [kn[rows_of(j), :] * jnp.exp(glast[j] - gw[rows_of(j), :]) for j in range(nj)]
    eglast = [jnp.exp(glast[j]) for j in range(nj)]
    gr = [jnp.concatenate([gcum_t[HA + 2 * p:HA + 2 * p + 1, rows_of(j)],
                           gcum_t[HA + 2 * p + 1:HA + 2 * p + 2, rows_of(j)]], axis=1) for j, p in jp]
    decay = [jnp.where(incl, jnp.exp(jnp.where(incl, g64[rows_of(j), pair64(p)] - b, 0.0)), 0.0)
             for (j, p), b in zip(jp, gr)]
    kq = [_dot(jnp.concatenate([kn[rows_of(j), pairw(p)], qn[rows_of(j), pairw(p)]], axis=0),
               _pair_diag(kn[rows_of(j), pairw(p)]), NT) for j, p in jp]
    low = [jnp.where(strict, b64[rows_of(j), pair64(p)] * d * m[0:CHUNK, :], 0.0)
           for (j, p), d, m in zip(jp, decay, kq)]
    tinv = _tri_inv_pairs(low)
    zw = jnp.zeros((CHUNK, 2 * DK_A), F32)
    wu = [_dot(t, jnp.concatenate(
        [jnp.concatenate([bek[rows_of(j), lanes_of(2 * p)], bv[rows_of(j), lanes_of(2 * p)], zw], axis=1),
         jnp.concatenate([zw, bek[rows_of(j), lanes_of(2 * p + 1)], bv[rows_of(j), lanes_of(2 * p + 1)]],
                         axis=1)], axis=0))
          for t, (j, p) in zip(tinv, jp)]
    qwu_p = [_dot(d * m[CHUNK:, :], _pair_diag(x)) for d, m, x in zip(decay, kq, wu)]
    head_of = lambda z, j, h: z[j * npair + h // 2][:, (h % 2) * 2 * DK_A:(h % 2 + 1) * 2 * DK_A]
    qwu = [head_of(qwu_p, j, h) for j, h in jh]
    kwu = [_dot(kdec[j][:, lanes_of(h)], head_of(wu, j, h), TN) for j, h in jh]
    rd = _iota2((DK_A, DK_A), 0)
    cd = _iota2((DK_A, DK_A), 1)
    lhs = [jnp.concatenate([jnp.where(rd == cd, eglast[j][:, lanes_of(h)], 0.0) - kx[:, 0:DK_A],
                            qg[rows_of(j), lanes_of(h)] - qx[:, 0:DK_A]], axis=0)
           for (j, h), kx, qx in zip(jh, kwu, qwu)]

    o_rows = [None] * nj
    for jj in range(tc // CHUNK):
        for s in range(nseq):
            j = s * (tc // CHUNK) + jj
            idx = [j * HA + h for h in range(HA)]
            res = [_dot(lhs[i], s_ref[s, h]) for h, i in enumerate(idx)]
            for h, i in enumerate(idx):
                s_ref[s, h] = res[h][0:DK_A, :] + kwu[i][:, DK_A:]
            o_rows[j] = jnp.concatenate(
                [res[h][DK_A:, :] + qwu[i][:, DK_A:] for h, i in enumerate(idx)], axis=-1)

    gate = pa_ref[:, :, A_QKV:A_MAIN].reshape(nseq * tc, MIX_A)
    mix_a = _gdn_out(jnp.concatenate(o_rows, axis=0), gate, anw_ref[...])
    h = (x_ref[...].reshape(nseq * tc, D_MODEL)
         + _dot(mb_ref[...].reshape(nseq * tc, MIX_B), wo_ref[MIX_A:, :])
         + _dot(mix_a.astype(BF16), wo_ref[0:MIX_A, :]))
    y = h * lax.rsqrt(jnp.mean(h * h, axis=-1, keepdims=True) + RMS_EPS) * fw_ref[...]
    y_ref[...] = y.reshape(nseq, tc, D_MODEL)
    s_out_ref[...] = s_ref[...]


def _gdn_prompt(pa, bd, conv_w, hp, anw, x, mix_b, w_out, final_w):
    b, t, _ = pa.shape
    blk = lambda i, j: (i, j, 0)
    fixed = lambda i, j: (0, 0)
    return pl.pallas_call(
        _gdn_prompt_kernel,
        grid=(b // NSEQ_A, t // TC_A),
        in_specs=[pl.BlockSpec((NSEQ_A, TC_A, A_MAIN), blk),
                  pl.BlockSpec((NSEQ_A, TC_A, BD_W), blk),
                  pl.BlockSpec((CONV_W, A_QKV), fixed),
                  pl.BlockSpec((8, BD_W), fixed),
                  pl.BlockSpec((1, MIX_A), fixed),
                  pl.BlockSpec((NSEQ_A, TC_A, D_MODEL), blk),
                  pl.BlockSpec((NSEQ_A, TC_A, MIX_B), blk),
                  pl.BlockSpec((D_MODEL, D_MODEL), fixed),
                  pl.BlockSpec((1, D_MODEL), fixed)],
        out_specs=[pl.BlockSpec((NSEQ_A, TC_A, D_MODEL), blk),
                   pl.BlockSpec((NSEQ_A, HA, DK_A, DK_A), lambda i, j: (i, 0, 0, 0)),
                   pl.BlockSpec((NSEQ_A, CONV_W - 1, A_QKV), lambda i, j: (i, 0, 0))],
        out_shape=[jax.ShapeDtypeStruct((b, t, D_MODEL), F32),
                   jax.ShapeDtypeStruct((b, HA, DK_A, DK_A), F32),
                   jax.ShapeDtypeStruct((b, CONV_W - 1, A_QKV), F32)],
        scratch_shapes=[pltpu.VMEM((NSEQ_A, 8, A_QKV), F32),
                        pltpu.VMEM((NSEQ_A, HA, DK_A, DK_A), F32)],
        compiler_params=pltpu.CompilerParams(dimension_semantics=("arbitrary", "arbitrary"),
                                             vmem_limit_bytes=VMEM_LIMIT),
        name="gdn_prompt",
    )(pa, bd, conv_w, hp, anw, x, mix_b, w_out, final_w)


def _rwkv_front(xb, w0, w2, a0, a2, k_k, k_a, ones_blk):
    r = xb[:, 0:MIX_B]
    kr = xb[:, MIX_B:2 * MIX_B]
    vr = xb[:, 2 * MIX_B:3 * MIX_B]
    gate = xb[:, 3 * MIX_B:4 * MIX_B]
    wd = xb[:, 4 * MIX_B:4 * MIX_B + LORA]
    ad = xb[:, 4 * MIX_B + LORA:]
    logw = -(EXP_NEG_HALF * _sigmoid(w0 + _dot(jnp.tanh(wd), w2)))
    a = _sigmoid(a0 + _dot(ad, a2))
    kraw = kr * k_k
    kk = kraw * lax.rsqrt(_group_sum(kraw * kraw, ones_blk) + 1e-12)
    kmod = kr * (1.0 + (a - 1.0) * k_a)
    return r, kmod, vr, gate, logw, a, kk


def _rwkv_back(y, r, kmod, v, gate, rk, lnw, lnb, ones_blk):
    inv = 1.0 / HS_B
    yc = y - _group_sum(y, ones_blk) * inv
    var = _group_sum(yc * yc, ones_blk) * inv
    gn = yc * lax.rsqrt(var + GN_EPS) * lnw + lnb
    bonus = _group_sum(r * kmod * rk, ones_blk) * v
    return (gn + bonus) * _silu(gate)


def _rwkv_prompt_kernel(pb_ref, mu_ref, w0_ref, w2_ref, a0_ref, a2_ref, kk_ref, ka_ref,
                        rk_ref, lnw_ref, lnb_ref, ones_ref,
                        o_ref, s_out_ref, sh_out_ref, carry_ref, s_ref):
    c = pl.program_id(1)

    @pl.when(c == 0)
    def _():
        carry_ref[...] = jnp.zeros_like(carry_ref)
        s_ref[...] = jnp.zeros_like(s_ref)

    ones_blk = ones_ref[...]
    pb = pb_ref[...].reshape(NSEQ * TC, B_W)
    prev = []
    for s in range(NSEQ):
        pb_s = pb_ref[s]
        prev.append(pltpu.roll(jnp.concatenate([carry_ref[s], pb_s], axis=0), 1, 0)[8:, :])
        carry_ref[s] = pb_s[TC - 8:, :]
        sh_out_ref[s] = pb_s[TC - 1:, :]
    prev = jnp.concatenate(prev, axis=0)
    xb = pb + (prev - pb) * mu_ref[...]
    r, kmod, v, gate, logw, a, kk = _rwkv_front(
        xb, w0_ref[...], w2_ref[...], a0_ref[...], a2_ref[...], kk_ref[...], ka_ref[...], ones_blk)

    g = jnp.concatenate([_chunk_cumsum(logw[s * TC:(s + 1) * TC, :]) for s in range(NSEQ)],
                        axis=0)
    eg = jnp.exp(g)
    egn = jnp.exp(-g)
    ag = -kk * jnp.exp(g - logw)
    kka = kk * a
    bg = kka * egn
    kg = kmod * egn
    rg = r * eg

    rowp = _iota2((CHUNK, PAIR), 0)
    lane = _iota2((CHUNK, PAIR), 1)
    colp = jnp.where(lane >= HS_B, lane - HS_B, lane)
    left = lane < HS_B
    incl = colp <= rowp
    strict = colp < rowp
    eye = colp == rowp
    zero = jnp.zeros((CHUNK, PAIR), F32)
    pack = lambda f: jnp.where(left, f[0:HS_B, :], f[HS_B:, :])

    nj = NSEQ * TC // CHUNK
    npair = HB // 2
    jp = [(j, p) for j in range(nj) for p in range(npair)]
    rows_of = lambda j: slice(j * CHUNK, (j + 1) * CHUNK)
    lanes_of = lambda p: slice(p * PAIR, (p + 1) * PAIR)
    glast = [g[(j + 1) * CHUNK - 1:(j + 1) * CHUNK, :] for j in range(nj)]
    eglast = [jnp.exp(glast[j]) for j in range(nj)]
    bdec = [bg[rows_of(j), :] * eglast[j] for j in range(nj)]
    kdec = [kg[rows_of(j), :] * eglast[j] for j in range(nj)]

    cut = lambda z: [z[rows_of(j), lanes_of(p)] for j, p in jp]
    ag_p, rg_p, bg_p, kg_p, v_p = cut(ag), cut(rg), cut(bg), cut(kg), cut(v)
    bdec_p = [bdec[j][:, lanes_of(p)] for j, p in jp]
    kdec_p = [kdec[j][:, lanes_of(p)] for j, p in jp]
    eglast_p = [eglast[j][:, lanes_of(p)] for j, p in jp]
    amat = [_dot(jnp.concatenate([a, b], axis=0),
                 jnp.concatenate([_pair_diag(c_), _pair_diag(d)], axis=0), NT)
            for a, b, c_, d in zip(ag_p, rg_p, bg_p, kg_p)]
    a_ab = [jnp.where(strict, m[0:CHUNK, 0:PAIR], 0.0) for m in amat]
    a_ak = [jnp.where(strict, m[0:CHUNK, PAIR:], 0.0) for m in amat]
    a_rb = [jnp.where(incl, m[CHUNK:, 0:PAIR], 0.0) for m in amat]
    a_rk = [jnp.where(incl, m[CHUNK:, PAIR:], 0.0) for m in amat]
    tinv = _tri_inv_pairs([-a for a in a_ab])
    kv = [_dot(jnp.concatenate([a, b], axis=0), _pair_diag(c_)) for a, b, c_ in zip(a_ak, a_rk, v_p)]
    wu = [_dot(t, jnp.concatenate([_pair_diag(a), _pair_diag(x[0:CHUNK, :])], axis=1))
          for t, a, x in zip(tinv, ag_p, kv)]
    rwu = [_dot(a, jnp.concatenate([_pair_diag(x[:, 0:PAIR]), _pair_diag(x[:, PAIR:])], axis=1))
           for a, x in zip(a_rb, wu)]
    mn = [_dot(jnp.concatenate([a, b], axis=0),
               jnp.concatenate([x, jnp.concatenate([zero, c_], axis=1)], axis=0), TN)
          for a, b, x, c_ in zip(bdec_p, kdec_p, wu, v_p)]
    lhs = [jnp.concatenate([jnp.where(eye, e, 0.0) + pack(m[:, 0:PAIR]), a + x[:, 0:PAIR]], axis=0)
           for e, m, a, x in zip(eglast_p, mn, rg_p, rwu)]

    y_rows = [None] * nj
    for jj in range(TC // CHUNK):
        for s in range(NSEQ):
            j = s * (TC // CHUNK) + jj
            idx = [j * npair + p for p in range(npair)]
            res = [_dot(lhs[i], _pair_diag(s_ref[s, p])) for p, i in enumerate(idx)]
            for p, i in enumerate(idx):
                s_ref[s, p] = res[p][0:HS_B, :] + pack(mn[i][:, PAIR:])
            y_rows[j] = jnp.concatenate(
                [res[p][HS_B:, :] + rwu[i][:, PAIR:] + kv[i][CHUNK:, :] for p, i in enumerate(idx)],
                axis=-1)

    out = _rwkv_back(jnp.concatenate(y_rows, axis=0), r, kmod, v, gate,
                     rk_ref[...], lnw_ref[...], lnb_ref[...], ones_blk)
    o_ref[...] = out.reshape(NSEQ, TC, MIX_B).astype(o_ref.dtype)
    for s in range(NSEQ):
        for p in range(npair):
            st = s_ref[s, p].T
            s_out_ref[s, 2 * p] = st[0:HS_B, :]
            s_out_ref[s, 2 * p + 1] = st[HS_B:, :]


def _rwkv_prompt(pb, mu, w0, w2, a0, a2, k_k, k_a, rk, lnw, lnb, ones_blk):
    b, t, _ = pb.shape
    blk = lambda i, j: (i, j, 0)
    fixed = lambda i, j: (0, 0)
    vec = pl.BlockSpec((1, MIX_B), fixed)
    return pl.pallas_call(
        _rwkv_prompt_kernel,
        grid=(b // NSEQ, t // TC),
        in_specs=[pl.BlockSpec((NSEQ, TC, B_W), blk),
                  pl.BlockSpec((1, B_W), fixed),
                  vec, pl.BlockSpec((LORA, MIX_B), fixed),
                  vec, pl.BlockSpec((LORA, MIX_B), fixed),
                  vec, vec, vec, vec, vec,
                  pl.BlockSpec((MIX_B, MIX_B), fixed)],
        out_specs=[pl.BlockSpec((NSEQ, TC, MIX_B), blk),
                   pl.BlockSpec((NSEQ, HB, HS_B, HS_B), lambda i, j: (i, 0, 0, 0)),
                   pl.BlockSpec((NSEQ, 1, B_W), lambda i, j: (i, 0, 0))],
        out_shape=[jax.ShapeDtypeStruct((b, t, MIX_B), BF16),
                   jax.ShapeDtypeStruct((b, HB, HS_B, HS_B), F32),
                   jax.ShapeDtypeStruct((b, 1, B_W), F32)],
        scratch_shapes=[pltpu.VMEM((NSEQ, 8, B_W), F32),
                        pltpu.VMEM((NSEQ, HB // 2, HS_B, PAIR), F32)],
        compiler_params=pltpu.CompilerParams(dimension_semantics=("arbitrary", "arbitrary"),
                                             vmem_limit_bytes=VMEM_LIMIT),
        name="rwkv_prompt",
    )(pb, mu, w0, w2, a0, a2, k_k, k_a, rk, lnw, lnb, ones_blk)


def _pad_rows(rows, width):
    return jnp.concatenate(rows + [jnp.zeros((8 - len(rows), width), F32)], axis=0)


def _gdn_step_kernel(pa_ref, bd_ref, sc_ref, s_ref, cw_ref, hp_ref, anw_ref,
                     o_ref, s_out_ref, c_out_ref):
    u = pa_ref[:, 0:A_QKV]
    cw = cw_ref[...]
    conv = (sc_ref[0] * cw[0:1, :] + sc_ref[1] * cw[1:2, :] + sc_ref[2] * cw[2:3, :] + u * cw[3:4, :])
    qkv = _silu(conv)
    c_out_ref[0] = sc_ref[1]
    c_out_ref[1] = sc_ref[2]
    c_out_ref[2] = u

    qn, kn = _gdn_qk(qkv)
    beta_all, g_all = _gdn_gates(bd_ref[...], hp_ref[...])
    eg_all = jnp.exp(g_all)
    lanes_of = lambda h: slice(h * DK_A, (h + 1) * DK_A)
    q = [qn[:, lanes_of(h)] for h in range(HA)]
    k = [kn[:, lanes_of(h)] for h in range(HA)]
    v = [qkv[:, 2 * MIX_A + h * DK_A:2 * MIX_A + (h + 1) * DK_A] for h in range(HA)]
    beta = [beta_all[:, h:h + 1] for h in range(HA)]
    eg = [eg_all[:, HA + h:HA + h + 1] for h in range(HA)]

    bh = [(b, h) for b in range(SB) for h in range(HA)]
    s0 = [s_ref[b, h] for b, h in bh]
    ks_qs = [_dot(_pad_rows([k[h][b:b + 1, :], q[h][b:b + 1, :]], DK_A), s) for (b, h), s in zip(bh, s0)]
    o_heads = []
    for h in range(HA):
        ks = jnp.concatenate([ks_qs[b * HA + h][0:1, :] for b in range(SB)], axis=0)
        qs = jnp.concatenate([ks_qs[b * HA + h][1:2, :] for b in range(SB)], axis=0)
        u2 = beta[h] * (v[h] - eg[h] * ks)
        for b in range(SB):
            s_out_ref[b, h] = (eg[h][b:b + 1, :] * s0[b * HA + h]
                               + _dot(_pad_rows([k[h][b:b + 1, :]], DK_A),
                                      _pad_rows([u2[b:b + 1, :]], DK_A), TN))
        o_heads.append(eg[h] * qs + jnp.sum(q[h] * k[h], axis=-1, keepdims=True) * u2)

    o_ref[...] = _gdn_out(jnp.concatenate(o_heads, axis=-1), pa_ref[:, A_QKV:A_MAIN], anw_ref[...])


def _gdn_step(pa, bd, sconv_t, s_a, conv_w, hp, anw):
    n = pa.shape[0]
    row = lambda i: (i, 0)
    fixed = lambda i: (0, 0)
    return pl.pallas_call(
        _gdn_step_kernel,
        grid=(n // SB,),
        in_specs=[pl.BlockSpec((SB, A_MAIN), row),
                  pl.BlockSpec((SB, BD_W), row),
                  pl.BlockSpec((CONV_W - 1, SB, A_QKV), lambda i: (0, i, 0)),
                  pl.BlockSpec((SB, HA, DK_A, DK_A), lambda i: (i, 0, 0, 0)),
                  pl.BlockSpec((CONV_W, A_QKV), fixed),
                  pl.BlockSpec((8, BD_W), fixed),
                  pl.BlockSpec((1, MIX_A), fixed)],
        out_specs=[pl.BlockSpec((SB, MIX_A), row),
                   pl.BlockSpec((SB, HA, DK_A, DK_A), lambda i: (i, 0, 0, 0)),
                   pl.BlockSpec((CONV_W - 1, SB, A_QKV), lambda i: (0, i, 0))],
        out_shape=[jax.ShapeDtypeStruct((n, MIX_A), F32),
                   jax.ShapeDtypeStruct((n, HA, DK_A, DK_A), F32),
                   jax.ShapeDtypeStruct((CONV_W - 1, n, A_QKV), F32)],
        compiler_params=pltpu.CompilerParams(dimension_semantics=("arbitrary",),
                                             vmem_limit_bytes=VMEM_LIMIT),
        name="gdn_step",
    )(pa, bd, sconv_t, s_a, conv_w, hp, anw)


def _rwkv_step_lanes_kernel(pb_ref, sh_ref, s_ref, mu_ref, w0_ref, w2_ref, a0_ref, a2_ref, kk_ref, ka_ref,
                            rk_ref, lnw_ref, lnb_ref, ones_ref,
                            o_ref, s_out_ref, vec_scr, row_scr, y_scr):
    h = pl.program_id(0)
    ones_blk = ones_ref[...]

    @pl.when(h == 0)
    def _():
        pb = pb_ref[...]
        xb = pb + (sh_ref[...] - pb) * mu_ref[...]
        r, kmod, v, gate, logw, a, kk = _rwkv_front(
            xb, w0_ref[...], w2_ref[...], a0_ref[...], a2_ref[...], kk_ref[...], ka_ref[...], ones_blk)
        for slot, z in enumerate((-kk, kk * a, kmod, v, r, jnp.exp(logw))):
            vec_scr[slot] = z.T
        for slot, z in enumerate((r, kmod, v, gate)):
            row_scr[slot] = z

    base = pl.multiple_of(h * HS_B, HS_B)
    chan = pl.ds(base, HS_B)
    nkk = vec_scr[0, chan, :]
    kka = vec_scr[1, chan, :]
    kmod_t = vec_scr[2, chan, :]
    r_t = vec_scr[4, chan, :]
    wdec = vec_scr[5, chan, :]

    def body(vi, carry):
        s0 = s_ref[0, vi]
        sa = jnp.sum(s0 * nkk, axis=0, keepdims=True)
        s1 = s0 * wdec + sa * kka + vec_scr[3, pl.ds(base + vi, 1), :] * kmod_t
        s_out_ref[0, vi] = s1
        y_scr[pl.ds(base + vi, 1), :] = jnp.sum(s1 * r_t, axis=0, keepdims=True)
        return carry

    lax.fori_loop(0, HS_B, body, 0, unroll=4)

    @pl.when(h == HB - 1)
    def _():
        o_ref[...] = _rwkv_back(y_scr[...].T, row_scr[0], row_scr[1], row_scr[2], row_scr[3],
                                rk_ref[...], lnw_ref[...], lnb_ref[...], ones_blk)


def _rwkv_step_lanes(pb, shift, s_hvkb, mu, w0, w2, a0, a2, k_k, k_a, rk, lnw, lnb, ones_blk):
    n = pb.shape[0]
    fixed = lambda i: (0, 0)
    vec = pl.BlockSpec((1, MIX_B), fixed)
    state = pl.BlockSpec((1, HS_B, HS_B, n), lambda i: (i, 0, 0, 0))
    return pl.pallas_call(
        _rwkv_step_lanes_kernel,
        grid=(HB,),
        in_specs=[pl.BlockSpec((n, B_W), fixed),
                  pl.BlockSpec((n, B_W), fixed),
                  state,
                  pl.BlockSpec((1, B_W), fixed),
                  vec, pl.BlockSpec((LORA, MIX_B), fixed),
                  vec, pl.BlockSpec((LORA, MIX_B), fixed),
                  vec, vec, vec, vec, vec,
                  pl.BlockSpec((MIX_B, MIX_B), fixed)],
        out_specs=[pl.BlockSpec((n, MIX_B), fixed), state],
        out_shape=[jax.ShapeDtypeStruct((n, MIX_B), F32),
                   jax.ShapeDtypeStruct((HB, HS_B, HS_B, n), F32)],
        scratch_shapes=[pltpu.VMEM((6, MIX_B, n), F32),
                        pltpu.VMEM((4, n, MIX_B), F32),
                        pltpu.VMEM((MIX_B, n), F32)],
        compiler_params=pltpu.CompilerParams(dimension_semantics=("arbitrary",),
                                             vmem_limit_bytes=VMEM_LIMIT),
        name="rwkv_step",
    )(pb, shift, s_hvkb, mu, w0, w2, a0, a2, k_k, k_a, rk, lnw, lnb, ones_blk)


def _block_ones(width, group):
    idx = jnp.arange(width) // group
    return (idx[:, None] == idx[None, :]).astype(BF16)


def kernel(x_prompt, x_sample, state_a_mat, state_a_conv, state_b_mat, state_b_shift, norm_w, w_in,
           conv_w, a_log, dt_bias, a_norm_w, mu, w0, w2, a0, a2, k_k, k_a, r_k, ln_w, ln_b, w_out,
           final_norm_w):
    bsz, seq, _ = x_prompt.shape
    nsmp = x_sample.shape[0]
    assert bsz % NSEQ == 0 and bsz % NSEQ_A == 0 and seq % TC == 0 and seq % TC_A == 0
    assert nsmp % SB == 0 and x_sample.shape[1] == 1

    w_all = jnp.transpose(w_in[0])
    w_o = w_out[0].astype(BF16)
    hp = jnp.pad(jnp.concatenate([a_log, dt_bias], axis=0), ((0, 6), (HA, BD_W - 2 * HA)))
    nw = norm_w[0].reshape(1, D_MODEL)
    fw = final_norm_w.reshape(1, D_MODEL)
    anw = jnp.tile(a_norm_w[0].reshape(1, DK_A), (1, HA))
    ones_b = _block_ones(MIX_B, HS_B)
    row = lambda z: z[0].reshape(1, -1)
    b_params = (row(mu), row(w0), w2[0], row(a0), a2[0], row(k_k), row(k_a), row(r_k), row(ln_w), row(ln_b),
                ones_b)

    xp = x_prompt.reshape(bsz * seq, D_MODEL)
    pa, pb, bd = _inproj(xp, nw, w_all)
    ob, p_bmat, p_bshift = _rwkv_prompt(pb.reshape(bsz, seq, B_W), *b_params)
    y_prompt, p_amat, p_aconv = _gdn_prompt(pa.reshape(bsz, seq, A_MAIN), bd.reshape(bsz, seq, BD_W),
                                            conv_w[0], hp, anw, x_prompt, ob, w_o, fw)

    xs = x_sample.reshape(nsmp, D_MODEL)
    sa_, sb_, sbd = _inproj(xs, nw, w_all)
    sconv_t = jnp.swapaxes(state_a_conv[0], 0, 1)
    soa, s_amat, s_aconv_t = _gdn_step(sa_, sbd, sconv_t, state_a_mat[0], conv_w[0], hp, anw)
    sob, s_bmat_t = _rwkv_step_lanes(sb_, state_b_shift[0], jnp.transpose(state_b_mat[0], (1, 2, 3, 0)),
                                     *b_params)
    s_bmat = jnp.transpose(s_bmat_t, (3, 0, 1, 2))
    y_sample = _outproj(xs, soa, sob, w_o, fw)

    return (y_prompt, y_sample.reshape(nsmp, 1, D_MODEL),
            p_amat[None], p_aconv[None], p_bmat[None], p_bshift.reshape(1, bsz, B_W),
            s_amat[None], jnp.swapaxes(s_aconv_t, 0, 1)[None], s_bmat[None], sb_[None])
```

```python
import jax
import jax.numpy as jnp
from jax import lax
from jax.experimental import pallas as pl
from jax.experimental.pallas import tpu as pltpu

F32 = jnp.float32
BF16 = jnp.bfloat16

D_MODEL = 1024
MIX_A = 512
MIX_B = 512
HA = 4
DK_A = 128
HB = 8
HS_B = 64
PAIR = 2 * HS_B
CONV_W = 4
LORA = 64
A_QKV = 3 * MIX_A
A_MAIN = A_QKV + MIX_A
B_W = 4 * MIX_B + 2 * LORA
BD_W = 128
RMS_EPS = 1e-6
GN_EPS = 64e-5
CHUNK = 64
TC = 256
TC_A = 256
NSEQ_A = 2
NSEQ = 4
SB = 16
VMEM_LIMIT = 56 * 1024 * 1024

EXP_NEG_HALF = 0.6065306597126334
NN = (((1,), (0,)), ((), ()))
NT = (((1,), (1,)), ((), ()))
TN = (((0,), (0,)), ((), ()))


def _dot(a, b, dn=NN):
    return lax.dot_general(a, b, dn, preferred_element_type=F32)


def _sigmoid(x):
    return 0.5 * jnp.tanh(0.5 * x) + 0.5


def _silu(x):
    h = 0.5 * x
    return h * jnp.tanh(h) + h


def _softplus(x):
    return jnp.maximum(x, 0.0) + jnp.log(1.0 + jnp.exp(-jnp.abs(x)))


def _l2norm(x):
    return x * lax.rsqrt(jnp.sum(x * x, axis=-1, keepdims=True) + 1e-12)


def _group_sum(x, ones_blk):
    return _dot(x.astype(BF16), ones_blk)


def _iota2(shape, dim):
    return lax.broadcasted_iota(jnp.int32, shape, dim)


def _chunk_cumsum(x):
    n = x.shape[0]
    r = _iota2((n, n), 0)
    c = _iota2((n, n), 1)
    tril = jnp.where(((r // CHUNK) == (c // CHUNK)) & (c <= r), 1.0, 0.0).astype(BF16)
    hi = x.astype(BF16)
    rest = x - hi.astype(F32)
    mid = rest.astype(BF16)
    lo = (rest - mid.astype(F32)).astype(BF16)
    return _dot(tril, hi) + _dot(tril, mid) + _dot(tril, lo)


def _tri_inv_many(lows):
    n = lows[0].shape[0]
    r = _iota2((n, n), 0)
    c = _iota2((n, n), 1)
    eye = jnp.where(r == c, 1.0, 0.0).astype(F32)
    base = (r // 8) == (c // 8)
    l0 = [jnp.where(base, low, 0.0) for low in lows]
    l2 = [_dot(a, a) for a in l0]
    l4 = [_dot(a, a) for a in l2]
    d = [_dot(eye - a, eye + b) for a, b in zip(l0, l2)]
    d = [_dot(a, eye + b) for a, b in zip(d, l4)]
    s = 8
    while s < n:
        sel = ((r // (2 * s)) == (c // (2 * s))) & ((r // s) != (c // s))
        t = [_dot(jnp.where(sel, low, 0.0), a) for low, a in zip(lows, d)]
        d = [a - _dot(a, b) for a, b in zip(d, t)]
        s *= 2
    return d


def _pair_diag(x):
    left = _iota2(x.shape, 1) < x.shape[1] // 2
    return jnp.concatenate([jnp.where(left, x, 0.0), jnp.where(left, 0.0, x)], axis=0)


def _tri_inv_pairs(lows):
    n = lows[0].shape[0]
    r = _iota2((n, 2 * n), 0)
    lane = _iota2((n, 2 * n), 1)
    c = jnp.where(lane >= n, lane - n, lane)
    mul = lambda a, b: _dot(a, _pair_diag(b))
    eye = jnp.where(r == c, 1.0, 0.0).astype(F32)
    base = (r // 8) == (c // 8)
    l0 = [jnp.where(base, low, 0.0) for low in lows]
    l2 = [mul(a, a) for a in l0]
    l4 = [mul(a, a) for a in l2]
    d = [mul(eye - a, eye + b) for a, b in zip(l0, l2)]
    d = [mul(a, eye + b) for a, b in zip(d, l4)]
    s = 8
    while s < n:
        sel = ((r // (2 * s)) == (c // (2 * s))) & ((r // s) != (c // s))
        t = [mul(jnp.where(sel, low, 0.0), a) for low, a in zip(lows, d)]
        d = [a - mul(a, b) for a, b in zip(d, t)]
        s *= 2
    return d


def _inproj_kernel(x_ref, xs_ref, nw_ref, wt_ref, oa_ref, ob_ref, obd_ref, sa_ref, sb_ref, sbd_ref, w_scr):
    a_w = A_MAIN + 2 * HA

    def project(x, pa_ref, pb_ref, pbd_ref):
        xn = (x * lax.rsqrt(jnp.mean(x * x, axis=-1, keepdims=True) + RMS_EPS) * nw_ref[...]).astype(BF16)
        pa_ref[...] = _dot(xn, w_scr[0:A_MAIN, :], NT)
        pb = _dot(xn, w_scr[A_MAIN:, :], NT)
        pb_ref[...] = pb[:, :B_W]
        pbd_ref[...] = pb[:, B_W:]

    @pl.when(pl.program_id(0) == 0)
    def _():
        w_scr[0:A_MAIN, :] = wt_ref[0:A_MAIN, :].astype(BF16)
        w_scr[A_MAIN:A_MAIN + B_W, :] = wt_ref[a_w:, :].astype(BF16)
        w_scr[A_MAIN + B_W:, :] = jnp.concatenate(
            [wt_ref[A_MAIN:a_w, :], jnp.zeros((BD_W - 2 * HA, D_MODEL), F32)], axis=0).astype(BF16)
        project(xs_ref[...], sa_ref, sb_ref, sbd_ref)

    project(x_ref[...], oa_ref, ob_ref, obd_ref)


def _inproj(x2d, xs2d, norm_w, w_all):
    n, ns = x2d.shape[0], xs2d.shape[0]
    tm = min(512, n)
    row = lambda i: (i, 0)
    fixed = lambda i: (0, 0)
    widths = (A_MAIN, B_W, BD_W)
    return pl.pallas_call(
        _inproj_kernel,
        grid=(n // tm,),
        in_specs=[pl.BlockSpec((tm, D_MODEL), row),
                  pl.BlockSpec((ns, D_MODEL), fixed),
                  pl.BlockSpec((1, D_MODEL), fixed),
                  pl.BlockSpec(w_all.shape, fixed)],
        out_specs=[pl.BlockSpec((tm, w), row) for w in widths] + [pl.BlockSpec((ns, w), fixed) for w in widths],
        out_shape=[jax.ShapeDtypeStruct((n, w), F32) for w in widths]
                  + [jax.ShapeDtypeStruct((ns, w), F32) for w in widths],
        scratch_shapes=[pltpu.VMEM((A_MAIN + B_W + BD_W, D_MODEL), BF16)],
        compiler_params=pltpu.CompilerParams(dimension_semantics=("arbitrary",),
                                             vmem_limit_bytes=VMEM_LIMIT),
        name="inproj",
    )(x2d, xs2d, norm_w, w_all)


def _outproj_kernel(x_ref, ma_ref, mb_ref, w_ref, fw_ref, y_ref):
    h = (x_ref[...]
         + _dot(ma_ref[...].astype(BF16), w_ref[0:MIX_A, :])
         + _dot(mb_ref[...].astype(BF16), w_ref[MIX_A:, :]))
    y_ref[...] = h * lax.rsqrt(jnp.mean(h * h, axis=-1, keepdims=True) + RMS_EPS) * fw_ref[...]


def _outproj(x2d, mix_a, mix_b, w_out, final_w):
    n = x2d.shape[0]
    tm = min(1024, n)
    row = lambda i: (i, 0)
    fixed = lambda i: (0, 0)
    return pl.pallas_call(
        _outproj_kernel,
        grid=(n // tm,),
        in_specs=[pl.BlockSpec((tm, D_MODEL), row),
                  pl.BlockSpec((tm, MIX_A), row),
                  pl.BlockSpec((tm, MIX_B), row),
                  pl.BlockSpec((D_MODEL, D_MODEL), fixed),
                  pl.BlockSpec((1, D_MODEL), fixed)],
        out_specs=pl.BlockSpec((tm, D_MODEL), row),
        out_shape=jax.ShapeDtypeStruct((n, D_MODEL), F32),
        compiler_params=pltpu.CompilerParams(dimension_semantics=("arbitrary",),
                                             vmem_limit_bytes=VMEM_LIMIT),
        name="outproj",
    )(x2d, mix_a, mix_b, w_out, final_w)


def _gdn_gates(bd, hp):
    beta = _sigmoid(bd)
    g = -jnp.exp(hp[0:1, :]) * _softplus(bd + hp[1:2, :])
    return beta, g


def _gdn_qk(qkv):
    heads = lambda z: [z[:, h * DK_A:(h + 1) * DK_A] for h in range(HA)]
    qn = jnp.concatenate([_l2norm(x) * (DK_A ** -0.5) for x in heads(qkv[:, 0:MIX_A])], axis=-1)
    kn = jnp.concatenate([_l2norm(x) for x in heads(qkv[:, MIX_A:2 * MIX_A])], axis=-1)
    return qn, kn


def _gdn_out(o, gate, anw):
    on = jnp.concatenate(
        [x * lax.rsqrt(jnp.mean(x * x, axis=-1, keepdims=True) + RMS_EPS)
         for x in [o[:, h * DK_A:(h + 1) * DK_A] for h in range(HA)]], axis=-1)
    return on * anw * _silu(gate)


def _gdn_prompt_kernel(pa_ref, bd_ref, cw_ref, hp_ref, anw_ref, x_ref, mb_ref, wo_ref, fw_ref,
                       y_ref, s_out_ref, c_out_ref, ext_ref, s_ref):
    c = pl.program_id(1)

    nseq, tc = pa_ref.shape[0], pa_ref.shape[1]

    @pl.when(c == 0)
    def _():
        ext_ref[...] = jnp.zeros_like(ext_ref)
        s_ref[...] = jnp.zeros_like(s_ref)

    cw = cw_ref[...]
    conv = []
    for s in range(nseq):
        u = pa_ref[s, :, 0:A_QKV]
        full = jnp.concatenate([ext_ref[s], u], axis=0)
        conv.append(u * cw[3:4, :] + sum(
            pltpu.roll(full, CONV_W - 1 - i, 0)[8:, :] * cw[i:i + 1, :] for i in range(CONV_W - 1)))
        ext_ref[s] = u[tc - 8:, :]
        c_out_ref[s] = ext_ref[s, 8 - (CONV_W - 1):, :]
    qkv = _silu(jnp.concatenate(conv, axis=0))

    qn, kn = _gdn_qk(qkv)
    beta_all, g_all = _gdn_gates(bd_ref[...].reshape(nseq * tc, BD_W), hp_ref[...])
    gcum = jnp.concatenate([_chunk_cumsum(g_all[s * tc:(s + 1) * tc, :]) for s in range(nseq)],
                           axis=0)
    gcum_t = gcum.T

    nrow = nseq * tc
    wide = lambda z, first: jnp.concatenate(
        [jnp.broadcast_to(z[:, first + h:first + h + 1], (nrow, DK_A)) for h in range(HA)], axis=1)
    bw, gw = wide(beta_all, 0), wide(gcum, HA)
    half = _iota2((nrow, PAIR), 1) < CHUNK
    narrow = lambda zw: jnp.concatenate(
        [jnp.where(half, zw[:, 2 * p * DK_A:(2 * p + 1) * DK_A], zw[:, (2 * p + 1) * DK_A:(2 * p + 2) * DK_A])
         for p in range(HA // 2)], axis=1)
    b64, g64 = narrow(bw), narrow(gw)
    v_all = qkv[:, 2 * MIX_A:]
    egw = jnp.exp(gw)
    bek = bw * egw * kn
    bv = bw * v_all
    qg = qn * egw

    rowp = _iota2((CHUNK, PAIR), 0)
    lane = _iota2((CHUNK, PAIR), 1)
    colp = jnp.where(lane >= CHUNK, lane - CHUNK, lane)
    incl = colp <= rowp
    strict = colp < rowp

    nj = nseq * tc // CHUNK
    npair = HA // 2
    jp = [(j, p) for j in range(nj) for p in range(npair)]
    jh = [(j, h) for j in range(nj) for h in range(HA)]
    rows_of = lambda j: slice(j * CHUNK, (j + 1) * CHUNK)
    lanes_of = lambda h: slice(h * DK_A, (h + 1) * DK_A)
    pair64 = lambda p: slice(p * PAIR, (p + 1) * PAIR)
    pairw = lambda p: slice(2 * p * DK_A, 2 * (p + 1) * DK_A)
    glast = [gw[(j + 1) * CHUNK - 1:(j + 1) * CHUNK, :] for j in range(nj)]
    kdec = [kn[rows_of(j), :] * jnp.exp(glast[j] - gw[rows_of(j), :]) for j in range(nj)]
    eglast = [jnp.exp(glast[j]) for j in range(nj)]
    gr = [jnp.concatenate([gcum_t[HA + 2 * p:HA + 2 * p + 1, rows_of(j)],
                           gcum_t[HA + 2 * p + 1:HA + 2 * p + 2, rows_of(j)]], axis=1) for j, p in jp]
    decay = [jnp.where(incl, jnp.exp(jnp.where(incl, g64[rows_of(j), pair64(p)] - b, 0.0)), 0.0)
             for (j, p), b in zip(jp, gr)]
    kq = [_dot(jnp.concatenate([kn[rows_of(j), pairw(p)], qn[rows_of(j), pairw(p)]], axis=0),
               _pair_diag(kn[rows_of(j), pairw(p)]), NT) for j, p in jp]
    low = [jnp.where(strict, b64[rows_of(j), pair64(p)] * d * m[0:CHUNK, :], 0.0)
           for (j, p), d, m in zip(jp, decay, kq)]
    tinv = _tri_inv_pairs(low)
    zw = jnp.zeros((CHUNK, 2 * DK_A), F32)
    wu = [_dot(t, jnp.concatenate(
        [jnp.concatenate([bek[rows_of(j), lanes_of(2 * p)], bv[rows_of(j), lanes_of(2 * p)], zw], axis=1),
         jnp.concatenate([zw, bek[rows_of(j), lanes_of(2 * p + 1)], bv[rows_of(j), lanes_of(2 * p + 1)]],
                         axis=1)], axis=0))
          for t, (j, p) in zip(tinv, jp)]
    qwu_p = [_dot(d * m[CHUNK:, :], _pair_diag(x)) for d, m, x in zip(decay, kq, wu)]
    head_of = lambda z, j, h: z[j * npair + h // 2][:, (h % 2) * 2 * DK_A:(h % 2 + 1) * 2 * DK_A]
    qwu = [head_of(qwu_p, j, h) for j, h in jh]
    kwu = [_dot(kdec[j][:, lanes_of(h)], head_of(wu, j, h), TN) for j, h in jh]
    rd = _iota2((DK_A, DK_A), 0)
    cd = _iota2((DK_A, DK_A), 1)
    lhs = [jnp.concatenate([jnp.where(rd == cd, eglast[j][:, lanes_of(h)], 0.0) - kx[:, 0:DK_A],
                            qg[rows_of(j), lanes_of(h)] - qx[:, 0:DK_A]], axis=0)
           for (j, h), kx, qx in zip(jh, kwu, qwu)]

    o_rows = [None] * nj
    for jj in range(tc // CHUNK):
        for s in range(nseq):
            j = s * (tc // CHUNK) + jj
            idx = [j * HA + h for h in range(HA)]
            res = [_dot(lhs[i], s_ref[s, h]) for h, i in enumerate(idx)]
            for h, i in enumerate(idx):
                s_ref[s, h] = res[h][0:DK_A, :] + kwu[i][:, DK_A:]
            o_rows[j] = jnp.concatenate(
                [res[h][DK_A:, :] + qwu[i][:, DK_A:] for h, i in enumerate(idx)], axis=-1)

    gate = pa_ref[:, :, A_QKV:A_MAIN].reshape(nseq * tc, MIX_A)
    mix_a = _gdn_out(jnp.concatenate(o_rows, axis=0), gate, anw_ref[...])
    h = (x_ref[...].reshape(nseq * tc, D_MODEL)
         + _dot(mb_ref[...].reshape(nseq * tc, MIX_B), wo_ref[MIX_A:, :])
         + _dot(mix_a.astype(BF16), wo_ref[0:MIX_A, :]))
    y = h * lax.rsqrt(jnp.mean(h * h, axis=-1, keepdims=True) + RMS_EPS) * fw_ref[...]
    y_ref[...] = y.reshape(nseq, tc, D_MODEL)
    s_out_ref[...] = s_ref[...]


def _gdn_prompt(pa, bd, conv_w, hp, anw, x, mix_b, w_out, final_w):
    b, t, _ = pa.shape
    blk = lambda i, j: (i, j, 0)
    fixed = lambda i, j: (0, 0)
    return pl.pallas_call(
        _gdn_prompt_kernel,
        grid=(b // NSEQ_A, t // TC_A),
        in_specs=[pl.BlockSpec((NSEQ_A, TC_A, A_MAIN), blk),
                  pl.BlockSpec((NSEQ_A, TC_A, BD_W), blk),
                  pl.BlockSpec((CONV_W, A_QKV), fixed),
                  pl.BlockSpec((8, BD_W), fixed),
                  pl.BlockSpec((1, MIX_A), fixed),
                  pl.BlockSpec((NSEQ_A, TC_A, D_MODEL), blk),
                  pl.BlockSpec((NSEQ_A, TC_A, MIX_B), blk),
                  pl.BlockSpec((D_MODEL, D_MODEL), fixed),
                  pl.BlockSpec((1, D_MODEL), fixed)],
        out_specs=[pl.BlockSpec((NSEQ_A, TC_A, D_MODEL), blk),
                   pl.BlockSpec((NSEQ_A, HA, DK_A, DK_A), lambda i, j: (i, 0, 0, 0)),
                   pl.BlockSpec((NSEQ_A, CONV_W - 1, A_QKV), lambda i, j: (i, 0, 0))],
        out_shape=[jax.ShapeDtypeStruct((b, t, D_MODEL), F32),
                   jax.ShapeDtypeStruct((b, HA, DK_A, DK_A), F32),
                   jax.ShapeDtypeStruct((b, CONV_W - 1, A_QKV), F32)],
        scratch_shapes=[pltpu.VMEM((NSEQ_A, 8, A_QKV), F32),
                        pltpu.VMEM((NSEQ_A, HA, DK_A, DK_A), F32)],
        compiler_params=pltpu.CompilerParams(dimension_semantics=("arbitrary", "arbitrary"),
                                             vmem_limit_bytes=VMEM_LIMIT),
        name="gdn_prompt",
    )(pa, bd, conv_w, hp, anw, x, mix_b, w_out, final_w)


def _rwkv_front(xb, w0, w2, a0, a2, k_k, k_a, ones_blk):
    r = xb[:, 0:MIX_B]
    kr = xb[:, MIX_B:2 * MIX_B]
    vr = xb[:, 2 * MIX_B:3 * MIX_B]
    gate = xb[:, 3 * MIX_B:4 * MIX_B]
    wd = xb[:, 4 * MIX_B:4 * MIX_B + LORA]
    ad = xb[:, 4 * MIX_B + LORA:]
    logw = -(EXP_NEG_HALF * _sigmoid(w0 + _dot(jnp.tanh(wd), w2)))
    a = _sigmoid(a0 + _dot(ad, a2))
    kraw = kr * k_k
    kk = kraw * lax.rsqrt(_group_sum(kraw * kraw, ones_blk) + 1e-12)
    kmod = kr * (1.0 + (a - 1.0) * k_a)
    return r, kmod, vr, gate, logw, a, kk


def _rwkv_back(y, r, kmod, v, gate, rk, lnw, lnb, ones_blk):
    inv = 1.0 / HS_B
    yc = y - _group_sum(y, ones_blk) * inv
    var = _group_sum(yc * yc, ones_blk) * inv
    gn = yc * lax.rsqrt(var + GN_EPS) * lnw + lnb
    bonus = _group_sum(r * kmod * rk, ones_blk) * v
    return (gn + bonus) * _silu(gate)


def _rwkv_prompt_kernel(pb_ref, mu_ref, w0_ref, w2_ref, a0_ref, a2_ref, kk_ref, ka_ref,
                        rk_ref, lnw_ref, lnb_ref, ones_ref,
                        o_ref, s_out_ref, sh_out_ref, carry_ref, s_ref):
    c = pl.program_id(1)

    @pl.when(c == 0)
    def _():
        carry_ref[...] = jnp.zeros_like(carry_ref)
        s_ref[...] = jnp.zeros_like(s_ref)

    ones_blk = ones_ref[...]
    pb = pb_ref[...].reshape(NSEQ * TC, B_W)
    prev = []
    for s in range(NSEQ):
        pb_s = pb_ref[s]
        prev.append(pltpu.roll(jnp.concatenate([carry_ref[s], pb_s], axis=0), 1, 0)[8:, :])
        carry_ref[s] = pb_s[TC - 8:, :]
        sh_out_ref[s] = pb_s[TC - 1:, :]
    prev = jnp.concatenate(prev, axis=0)
    xb = pb + (prev - pb) * mu_ref[...]
    r, kmod, v, gate, logw, a, kk = _rwkv_front(
        xb, w0_ref[...], w2_ref[...], a0_ref[...], a2_ref[...], kk_ref[...], ka_ref[...], ones_blk)

    g = jnp.concatenate([_chunk_cumsum(logw[s * TC:(s + 1) * TC, :]) for s in range(NSEQ)],
                        axis=0)
    eg = jnp.exp(g)
    egn = jnp.exp(-g)
    ag = -kk * jnp.exp(g - logw)
    kka = kk * a
    bg = kka * egn
    kg = kmod * egn
    rg = r * eg

    rowp = _iota2((CHUNK, PAIR), 0)
    lane = _iota2((CHUNK, PAIR), 1)
    colp = jnp.where(lane >= HS_B, lane - HS_B, lane)
    left = lane < HS_B
    incl = colp <= rowp
    strict = colp < rowp
    eye = colp == rowp
    zero = jnp.zeros((CHUNK, PAIR), F32)
    pack = lambda f: jnp.where(left, f[0:HS_B, :], f[HS_B:, :])

    nj = NSEQ * TC // CHUNK
    npair = HB // 2
    jp = [(j, p) for j in range(nj) for p in range(npair)]
    rows_of = lambda j: slice(j * CHUNK, (j + 1) * CHUNK)
    lanes_of = lambda p: slice(p * PAIR, (p + 1) * PAIR)
    glast = [g[(j + 1) * CHUNK - 1:(j + 1) * CHUNK, :] for j in range(nj)]
    eglast = [jnp.exp(glast[j]) for j in range(nj)]
    bdec = [bg[rows_of(j), :] * eglast[j] for j in range(nj)]
    kdec = [kg[rows_of(j), :] * eglast[j] for j in range(nj)]

    cut = lambda z: [z[rows_of(j), lanes_of(p)] for j, p in jp]
    ag_p, rg_p, bg_p, kg_p, v_p = cut(ag), cut(rg), cut(bg), cut(kg), cut(v)
    bdec_p = [bdec[j][:, lanes_of(p)] for j, p in jp]
    kdec_p = [kdec[j][:, lanes_of(p)] for j, p in jp]
    eglast_p = [eglast[j][:, lanes_of(p)] for j, p in jp]
    amat = [_dot(jnp.concatenate([a, b], axis=0),
                 jnp.concatenate([_pair_diag(c_), _pair_diag(d)], axis=0), NT)
            for a, b, c_, d in zip(ag_p, rg_p, bg_p, kg_p)]
    a_ab = [jnp.where(strict, m[0:CHUNK, 0:PAIR], 0.0) for m in amat]
    a_ak = [jnp.where(strict, m[0:CHUNK, PAIR:], 0.0) for m in amat]
    a_rb = [jnp.where(incl, m[CHUNK:, 0:PAIR], 0.0) for m in amat]
    a_rk = [jnp.where(incl, m[CHUNK:, PAIR:], 0.0) for m in amat]
    tinv = _tri_inv_pairs([-a for a in a_ab])
    kv = [_dot(jnp.concatenate([a, b], axis=0), _pair_diag(c_)) for a, b, c_ in zip(a_ak, a_rk, v_p)]
    wu = [_dot(t, jnp.concatenate([_pair_diag(a), _pair_diag(x[0:CHUNK, :])], axis=1))
          for t, a, x in zip(tinv, ag_p, kv)]
    rwu = [_dot(a, jnp.concatenate([_pair_diag(x[:, 0:PAIR]), _pair_diag(x[:, PAIR:])], axis=1))
           for a, x in zip(a_rb, wu)]
    mn = [_dot(jnp.concatenate([a, b], axis=0),
               jnp.concatenate([x, jnp.concatenate([zero, c_], axis=1)], axis=0), TN)
          for a, b, x, c_ in zip(bdec_p, kdec_p, wu, v_p)]
    lhs = [jnp.concatenate([jnp.where(eye, e, 0.0) + pack(m[:, 0:PAIR]), a + x[:, 0:PAIR]], axis=0)
           for e, m, a, x in zip(eglast_p, mn, rg_p, rwu)]

    y_rows = [None] * nj
    for jj in range(TC // CHUNK):
        for s in range(NSEQ):
            j = s * (TC // CHUNK) + jj
            idx = [j * npair + p for p in range(npair)]
            res = [_dot(lhs[i], _pair_diag(s_ref[s, p])) for p, i in enumerate(idx)]
            for p, i in enumerate(idx):
                s_ref[s, p] = res[p][0:HS_B, :] + pack(mn[i][:, PAIR:])
            y_rows[j] = jnp.concatenate(
                [res[p][HS_B:, :] + rwu[i][:, PAIR:] + kv[i][CHUNK:, :] for p, i in enumerate(idx)],
                axis=-1)

    out = _rwkv_back(jnp.concatenate(y_rows, axis=0), r, kmod, v, gate,
                     rk_ref[...], lnw_ref[...], lnb_ref[...], ones_blk)
    o_ref[...] = out.reshape(NSEQ, TC, MIX_B).astype(o_ref.dtype)
    for s in range(NSEQ):
        for p in range(npair):
            st = s_ref[s, p].T
            s_out_ref[s, 2 * p] = st[0:HS_B, :]
            s_out_ref[s, 2 * p + 1] = st[HS_B:, :]


def _rwkv_prompt(pb, mu, w0, w2, a0, a2, k_k, k_a, rk, lnw, lnb, ones_blk):
    b, t, _ = pb.shape
    blk = lambda i, j: (i, j, 0)
    fixed = lambda i, j: (0, 0)
    vec = pl.BlockSpec((1, MIX_B), fixed)
    return pl.pallas_call(
        _rwkv_prompt_kernel,
        grid=(b // NSEQ, t // TC),
        in_specs=[pl.BlockSpec((NSEQ, TC, B_W), blk),
                  pl.BlockSpec((1, B_W), fixed),
                  vec, pl.BlockSpec((LORA, MIX_B), fixed),
                  vec, pl.BlockSpec((LORA, MIX_B), fixed),
                  vec, vec, vec, vec, vec,
                  pl.BlockSpec((MIX_B, MIX_B), fixed)],
        out_specs=[pl.BlockSpec((NSEQ, TC, MIX_B), blk),
                   pl.BlockSpec((NSEQ, HB, HS_B, HS_B), lambda i, j: (i, 0, 0, 0)),
                   pl.BlockSpec((NSEQ, 1, B_W), lambda i, j: (i, 0, 0))],
        out_shape=[jax.ShapeDtypeStruct((b, t, MIX_B), BF16),
                   jax.ShapeDtypeStruct((b, HB, HS_B, HS_B), F32),
                   jax.ShapeDtypeStruct((b, 1, B_W), F32)],
        scratch_shapes=[pltpu.VMEM((NSEQ, 8, B_W), F32),
                        pltpu.VMEM((NSEQ, HB // 2, HS_B, PAIR), F32)],
        compiler_params=pltpu.CompilerParams(dimension_semantics=("arbitrary", "arbitrary"),
                                             vmem_limit_bytes=VMEM_LIMIT),
        name="rwkv_prompt",
    )(pb, mu, w0, w2, a0, a2, k_k, k_a, rk, lnw, lnb, ones_blk)


def _pad_rows(rows, width):
    return jnp.concatenate(rows + [jnp.zeros((8 - len(rows), width), F32)], axis=0)


def _gdn_step_kernel(pa_ref, bd_ref, sc_ref, s_ref, cw_ref, hp_ref, anw_ref,
                     o_ref, s_out_ref, c_out_ref):
    u = pa_ref[:, 0:A_QKV]
    cw = cw_ref[...]
    conv = (sc_ref[0] * cw[0:1, :] + sc_ref[1] * cw[1:2, :] + sc_ref[2] * cw[2:3, :] + u * cw[3:4, :])
    qkv = _silu(conv)
    c_out_ref[0] = sc_ref[1]
    c_out_ref[1] = sc_ref[2]
    c_out_ref[2] = u

    qn, kn = _gdn_qk(qkv)
    beta_all, g_all = _gdn_gates(bd_ref[...], hp_ref[...])
    eg_all = jnp.exp(g_all)
    lanes_of = lambda h: slice(h * DK_A, (h + 1) * DK_A)
    q = [qn[:, lanes_of(h)] for h in range(HA)]
    k = [kn[:, lanes_of(h)] for h in range(HA)]
    v = [qkv[:, 2 * MIX_A + h * DK_A:2 * MIX_A + (h + 1) * DK_A] for h in range(HA)]
    beta = [beta_all[:, h:h + 1] for h in range(HA)]
    eg = [eg_all[:, HA + h:HA + h + 1] for h in range(HA)]

    bh = [(b, h) for b in range(SB) for h in range(HA)]
    s0 = [s_ref[b, h] for b, h in bh]
    ks_qs = [_dot(_pad_rows([k[h][b:b + 1, :], q[h][b:b + 1, :]], DK_A), s) for (b, h), s in zip(bh, s0)]
    o_heads = []
    for h in range(HA):
        ks = jnp.concatenate([ks_qs[b * HA + h][0:1, :] for b in range(SB)], axis=0)
        qs = jnp.concatenate([ks_qs[b * HA + h][1:2, :] for b in range(SB)], axis=0)
        u2 = beta[h] * (v[h] - eg[h] * ks)
        for b in range(SB):
            s_out_ref[b, h] = (eg[h][b:b + 1, :] * s0[b * HA + h]
                               + _dot(_pad_rows([k[h][b:b + 1, :]], DK_A),
                                      _pad_rows([u2[b:b + 1, :]], DK_A), TN))
        o_heads.append(eg[h] * qs + jnp.sum(q[h] * k[h], axis=-1, keepdims=True) * u2)

    o_ref[...] = _gdn_out(jnp.concatenate(o_heads, axis=-1), pa_ref[:, A_QKV:A_MAIN], anw_ref[...])


def _gdn_step(pa, bd, sconv_t, s_a, conv_w, hp, anw):
    n = pa.shape[0]
    row = lambda i: (i, 0)
    fixed = lambda i: (0, 0)
    return pl.pallas_call(
        _gdn_step_kernel,
        grid=(n // SB,),
        in_specs=[pl.BlockSpec((SB, A_MAIN), row),
                  pl.BlockSpec((SB, BD_W), row),
                  pl.BlockSpec((CONV_W - 1, SB, A_QKV), lambda i: (0, i, 0)),
                  pl.BlockSpec((SB, HA, DK_A, DK_A), lambda i: (i, 0, 0, 0)),
                  pl.BlockSpec((CONV_W, A_QKV), fixed),
                  pl.BlockSpec((8, BD_W), fixed),
                  pl.BlockSpec((1, MIX_A), fixed)],
        out_specs=[pl.BlockSpec((SB, MIX_A), row),
                   pl.BlockSpec((SB, HA, DK_A, DK_A), lambda i: (i, 0, 0, 0)),
                   pl.BlockSpec((CONV_W - 1, SB, A_QKV), lambda i: (0, i, 0))],
        out_shape=[jax.ShapeDtypeStruct((n, MIX_A), F32),
                   jax.ShapeDtypeStruct((n, HA, DK_A, DK_A), F32),
                   jax.ShapeDtypeStruct((CONV_W - 1, n, A_QKV), F32)],
        compiler_params=pltpu.CompilerParams(dimension_semantics=("arbitrary",),
                                             vmem_limit_bytes=VMEM_LIMIT),
        name="gdn_step",
    )(pa, bd, sconv_t, s_a, conv_w, hp, anw)


def _rwkv_step_lanes_kernel(pb_ref, sh_ref, s_ref, mu_ref, w0_ref, w2_ref, a0_ref, a2_ref, kk_ref, ka_ref,
                            rk_ref, lnw_ref, lnb_ref, ones_ref,
                            o_ref, s_out_ref, vec_scr, row_scr, y_scr):
    h = pl.program_id(0)
    ones_blk = ones_ref[...]

    @pl.when(h == 0)
    def _():
        pb = pb_ref[...]
        xb = pb + (sh_ref[...] - pb) * mu_ref[...]
        r, kmod, v, gate, logw, a, kk = _rwkv_front(
            xb, w0_ref[...], w2_ref[...], a0_ref[...], a2_ref[...], kk_ref[...], ka_ref[...], ones_blk)
        for slot, z in enumerate((-kk, kk * a, kmod, v, r, jnp.exp(logw))):
            vec_scr[slot] = z.T
        for slot, z in enumerate((r, kmod, v, gate)):
            row_scr[slot] = z

    base = pl.multiple_of(h * HS_B, HS_B)
    chan = pl.ds(base, HS_B)
    nkk = vec_scr[0, chan, :]
    kka = vec_scr[1, chan, :]
    kmod_t = vec_scr[2, chan, :]
    r_t = vec_scr[4, chan, :]
    wdec = vec_scr[5, chan, :]

    def body(vi, carry):
        s0 = s_ref[0, vi]
        sa = jnp.sum(s0 * nkk, axis=0, keepdims=True)
        s1 = s0 * wdec + sa * kka + vec_scr[3, pl.ds(base + vi, 1), :] * kmod_t
        s_out_ref[0, vi] = s1
        y_scr[pl.ds(base + vi, 1), :] = jnp.sum(s1 * r_t, axis=0, keepdims=True)
        return carry

    lax.fori_loop(0, HS_B, body, 0, unroll=4)

    @pl.when(h == HB - 1)
    def _():
        o_ref[...] = _rwkv_back(y_scr[...].T, row_scr[0], row_scr[1], row_scr[2], row_scr[3],
                                rk_ref[...], lnw_ref[...], lnb_ref[...], ones_blk)


def _rwkv_step_lanes(pb, shift, s_hvkb, mu, w0, w2, a0, a2, k_k, k_a, rk, lnw, lnb, ones_blk):
    n = pb.shape[0]
    fixed = lambda i: (0, 0)
    vec = pl.BlockSpec((1, MIX_B), fixed)
    state = pl.BlockSpec((1, HS_B, HS_B, n), lambda i: (i, 0, 0, 0))
    return pl.pallas_call(
        _rwkv_step_lanes_kernel,
        grid=(HB,),
        in_specs=[pl.BlockSpec((n, B_W), fixed),
                  pl.BlockSpec((n, B_W), fixed),
                  state,
                  pl.BlockSpec((1, B_W), fixed),
                  vec, pl.BlockSpec((LORA, MIX_B), fixed),
                  vec, pl.BlockSpec((LORA, MIX_B), fixed),
                  vec, vec, vec, vec, vec,
                  pl.BlockSpec((MIX_B, MIX_B), fixed)],
        out_specs=[pl.BlockSpec((n, MIX_B), fixed), state],
        out_shape=[jax.ShapeDtypeStruct((n, MIX_B), F32),
                   jax.ShapeDtypeStruct((HB, HS_B, HS_B, n), F32)],
        scratch_shapes=[pltpu.VMEM((6, MIX_B, n), F32),
                        pltpu.VMEM((4, n, MIX_B), F32),
                        pltpu.VMEM((MIX_B, n), F32)],
        compiler_params=pltpu.CompilerParams(dimension_semantics=("arbitrary",),
                                             vmem_limit_bytes=VMEM_LIMIT),
        name="rwkv_step",
    )(pb, shift, s_hvkb, mu, w0, w2, a0, a2, k_k, k_a, rk, lnw, lnb, ones_blk)


def _block_ones(width, group):
    idx = jnp.arange(width) // group
    return (idx[:, None] == idx[None, :]).astype(BF16)


def kernel(x_prompt, x_sample, state_a_mat, state_a_conv, state_b_mat, state_b_shift, norm_w, w_in,
           conv_w, a_log, dt_bias, a_norm_w, mu, w0, w2, a0, a2, k_k, k_a, r_k, ln_w, ln_b, w_out,
           final_norm_w):
    bsz, seq, _ = x_prompt.shape
    nsmp = x_sample.shape[0]
    assert bsz % NSEQ == 0 and bsz % NSEQ_A == 0 and seq % TC == 0 and seq % TC_A == 0
    assert nsmp % SB == 0 and x_sample.shape[1] == 1

    w_all = jnp.transpose(w_in[0])
    w_o = w_out[0].astype(BF16)
    hp = jnp.pad(jnp.concatenate([a_log, dt_bias], axis=0), ((0, 6), (HA, BD_W - 2 * HA)))
    nw = norm_w[0].reshape(1, D_MODEL)
    fw = final_norm_w.reshape(1, D_MODEL)
    anw = jnp.tile(a_norm_w[0].reshape(1, DK_A), (1, HA))
    ones_b = _block_ones(MIX_B, HS_B)
    row = lambda z: z[0].reshape(1, -1)
    b_params = (row(mu), row(w0), w2[0], row(a0), a2[0], row(k_k), row(k_a), row(r_k), row(ln_w), row(ln_b),
                ones_b)

    xp = x_prompt.reshape(bsz * seq, D_MODEL)
    xs = x_sample.reshape(nsmp, D_MODEL)
    pa, pb, bd, sa_, sb_, sbd = _inproj(xp, xs, nw, w_all)

    ob, p_bmat, p_bshift = _rwkv_prompt(pb.reshape(bsz, seq, B_W), *b_params)
    y_prompt, p_amat, p_aconv = _gdn_prompt(pa.reshape(bsz, seq, A_MAIN), bd.reshape(bsz, seq, BD_W),
                                            conv_w[0], hp, anw, x_prompt, ob, w_o, fw)

    sconv_t = jnp.swapaxes(state_a_conv[0], 0, 1)
    soa, s_amat, s_aconv_t = _gdn_step(sa_, sbd, sconv_t, state_a_mat[0], conv_w[0], hp, anw)
    sob, s_bmat_t = _rwkv_step_lanes(sb_, state_b_shift[0], jnp.transpose(state_b_mat[0], (1, 2, 3, 0)),
                                     *b_params)
    s_bmat = jnp.transpose(s_bmat_t, (3, 0, 1, 2))
    y_sample = _outproj(xs, soa, sob, w_o, fw)

    return (y_prompt, y_sample.reshape(nsmp, 1, D_MODEL),
            p_amat[None], p_aconv[None], p_bmat[None], p_bshift.reshape(1, bsz, B_W),
            s_amat[None], jnp.swapaxes(s_aconv_t, 0, 1)[None], s_bmat[None], sb_[None])
```

```python
import jax
import jax.numpy as jnp
from jax import lax
from jax.experimental import pallas as pl
from jax.experimental.pallas import tpu as pltpu

F32 = jnp.float32
BF16 = jnp.bfloat16

D_MODEL = 1024
MIX_A = 512
MIX_B = 512
HA = 4
DK_A = 128
HB = 8
HS_B = 64
LANE = 128
PAIR = 2 * HS_B
CONV_W = 4
LORA = 64
A_QKV = 3 * MIX_A
A_MAIN = A_QKV + MIX_A
B_W = 4 * MIX_B + 2 * LORA
BD_W = 128
RMS_EPS = 1e-6
GN_EPS = 64e-5
CHUNK = 64
TC = 256
TC_A = 256
NSEQ_A = 2
NSEQ = 4
SB = 16
VMEM_LIMIT = 56 * 1024 * 1024

EXP_NEG_HALF = 0.6065306597126334
NN = (((1,), (0,)), ((), ()))
NT = (((1,), (1,)), ((), ()))
TN = (((0,), (0,)), ((), ()))


def _dot(a, b, dn=NN):
    return lax.dot_general(a, b, dn, preferred_element_type=F32)


def _sigmoid(x):
    return 0.5 * jnp.tanh(0.5 * x) + 0.5


def _silu(x):
    h = 0.5 * x
    return h * jnp.tanh(h) + h


def _softplus(x):
    return jnp.maximum(x, 0.0) + jnp.log(1.0 + jnp.exp(-jnp.abs(x)))


def _l2norm(x):
    return x * lax.rsqrt(jnp.sum(x * x, axis=-1, keepdims=True) + 1e-12)


def _group_sum(x, ones_blk):
    return _dot(x.astype(BF16), ones_blk)


def _iota2(shape, dim):
    return lax.broadcasted_iota(jnp.int32, shape, dim)


def _chunk_cumsum(x):
    n = x.shape[0]
    r = _iota2((n, n), 0)
    c = _iota2((n, n), 1)
    tril = jnp.where(((r // CHUNK) == (c // CHUNK)) & (c <= r), 1.0, 0.0).astype(BF16)
    hi = x.astype(BF16)
    rest = x - hi.astype(F32)
    mid = rest.astype(BF16)
    lo = (rest - mid.astype(F32)).astype(BF16)
    return _dot(tril, hi) + _dot(tril, mid) + _dot(tril, lo)


def _tri_inv_many(lows):
    n = lows[0].shape[0]
    r = _iota2((n, n), 0)
    c = _iota2((n, n), 1)
    eye = jnp.where(r == c, 1.0, 0.0).astype(F32)
    base = (r // 8) == (c // 8)
    l0 = [jnp.where(base, low, 0.0) for low in lows]
    l2 = [_dot(a, a) for a in l0]
    l4 = [_dot(a, a) for a in l2]
    d = [_dot(eye - a, eye + b) for a, b in zip(l0, l2)]
    d = [_dot(a, eye + b) for a, b in zip(d, l4)]
    s = 8
    while s < n:
        sel = ((r // (2 * s)) == (c // (2 * s))) & ((r // s) != (c // s))
        t = [_dot(jnp.where(sel, low, 0.0), a) for low, a in zip(lows, d)]
        d = [a - _dot(a, b) for a, b in zip(d, t)]
        s *= 2
    return d


def _pair_diag(x):
    left = _iota2(x.shape, 1) < x.shape[1] // 2
    return jnp.concatenate([jnp.where(left, x, 0.0), jnp.where(left, 0.0, x)], axis=0)


def _tri_inv_pairs(lows):
    n = lows[0].shape[0]
    r = _iota2((n, 2 * n), 0)
    lane = _iota2((n, 2 * n), 1)
    c = jnp.where(lane >= n, lane - n, lane)
    mul = lambda a, b: _dot(a, _pair_diag(b))
    eye = jnp.where(r == c, 1.0, 0.0).astype(F32)
    base = (r // 8) == (c // 8)
    l0 = [jnp.where(base, low, 0.0) for low in lows]
    l2 = [mul(a, a) for a in l0]
    l4 = [mul(a, a) for a in l2]
    d = [mul(eye - a, eye + b) for a, b in zip(l0, l2)]
    d = [mul(a, eye + b) for a, b in zip(d, l4)]
    s = 8
    while s < n:
        sel = ((r // (2 * s)) == (c // (2 * s))) & ((r // s) != (c // s))
        t = [mul(jnp.where(sel, low, 0.0), a) for low, a in zip(lows, d)]
        d = [a - mul(a, b) for a, b in zip(d, t)]
        s *= 2
    return d


def _sample_rows(ref, n):
    pieces = D_MODEL // LANE
    return jnp.concatenate([ref[pl.ds(c, n, stride=pieces), :] for c in range(pieces)], axis=1)


def _inproj_kernel(x_ref, xs_ref, nw_ref, wt_ref, oa_ref, ob_ref, obd_ref, sa_ref, sb_ref, sbd_ref, w_scr):
    a_w = A_MAIN + 2 * HA

    def project(x, pa_ref, pb_ref, pbd_ref):
        xn = (x * lax.rsqrt(jnp.mean(x * x, axis=-1, keepdims=True) + RMS_EPS) * nw_ref[...]).astype(BF16)
        pa_ref[...] = _dot(xn, w_scr[0:A_MAIN, :], NT)
        pb = _dot(xn, w_scr[A_MAIN:, :], NT)
        pb_ref[...] = pb[:, :B_W]
        pbd_ref[...] = pb[:, B_W:]

    @pl.when(pl.program_id(0) == 0)
    def _():
        w_scr[0:A_MAIN, :] = wt_ref[0:A_MAIN, :].astype(BF16)
        w_scr[A_MAIN:A_MAIN + B_W, :] = wt_ref[a_w:, :].astype(BF16)
        w_scr[A_MAIN + B_W:, :] = jnp.concatenate(
            [wt_ref[A_MAIN:a_w, :], jnp.zeros((BD_W - 2 * HA, D_MODEL), F32)], axis=0).astype(BF16)
        project(_sample_rows(xs_ref, sa_ref.shape[0]), sa_ref, sb_ref, sbd_ref)

    project(x_ref[...], oa_ref, ob_ref, obd_ref)


def _inproj(x2d, xs2d, norm_w, w_all):
    n, ns = x2d.shape[0], xs2d.shape[0] * LANE // D_MODEL
    tm = min(512, n)
    row = lambda i: (i, 0)
    fixed = lambda i: (0, 0)
    widths = (A_MAIN, B_W, BD_W)
    return pl.pallas_call(
        _inproj_kernel,
        grid=(n // tm,),
        in_specs=[pl.BlockSpec((tm, D_MODEL), row),
                  pl.BlockSpec(xs2d.shape, fixed),
                  pl.BlockSpec((1, D_MODEL), fixed),
                  pl.BlockSpec(w_all.shape, fixed)],
        out_specs=[pl.BlockSpec((tm, w), row) for w in widths] + [pl.BlockSpec((ns, w), fixed) for w in widths],
        out_shape=[jax.ShapeDtypeStruct((n, w), F32) for w in widths]
                  + [jax.ShapeDtypeStruct((ns, w), F32) for w in widths],
        scratch_shapes=[pltpu.VMEM((A_MAIN + B_W + BD_W, D_MODEL), BF16)],
        compiler_params=pltpu.CompilerParams(dimension_semantics=("arbitrary",),
                                             vmem_limit_bytes=VMEM_LIMIT),
        name="inproj",
    )(x2d, xs2d, norm_w, w_all)


def _outproj_kernel(x_ref, ma_ref, mb_ref, w_ref, fw_ref, y_ref):
    n = ma_ref.shape[0]
    h = (_sample_rows(x_ref, n)
         + _dot(ma_ref[...].astype(BF16), w_ref[0:MIX_A, :])
         + _dot(mb_ref[...].astype(BF16), w_ref[MIX_A:, :]))
    y = h * lax.rsqrt(jnp.mean(h * h, axis=-1, keepdims=True) + RMS_EPS) * fw_ref[...]
    pieces = D_MODEL // LANE
    for c in range(pieces):
        y_ref[pl.ds(c, n, stride=pieces), :] = y[:, c * LANE:(c + 1) * LANE]


def _outproj(xs2d, mix_a, mix_b, w_out, final_w):
    n = mix_a.shape[0]
    fixed = lambda i: (0, 0)
    return pl.pallas_call(
        _outproj_kernel,
        grid=(1,),
        in_specs=[pl.BlockSpec(xs2d.shape, fixed),
                  pl.BlockSpec((n, MIX_A), fixed),
                  pl.BlockSpec((n, MIX_B), fixed),
                  pl.BlockSpec((D_MODEL, D_MODEL), fixed),
                  pl.BlockSpec((1, D_MODEL), fixed)],
        out_specs=pl.BlockSpec(xs2d.shape, fixed),
        out_shape=jax.ShapeDtypeStruct(xs2d.shape, F32),
        compiler_params=pltpu.CompilerParams(dimension_semantics=("arbitrary",),
                                             vmem_limit_bytes=VMEM_LIMIT),
        name="outproj",
    )(xs2d, mix_a, mix_b, w_out, final_w)


def _gdn_gates(bd, hp):
    beta = _sigmoid(bd)
    g = -jnp.exp(hp[0:1, :]) * _softplus(bd + hp[1:2, :])
    return beta, g


def _gdn_qk(qkv):
    heads = lambda z: [z[:, h * DK_A:(h + 1) * DK_A] for h in range(HA)]
    qn = jnp.concatenate([_l2norm(x) * (DK_A ** -0.5) for x in heads(qkv[:, 0:MIX_A])], axis=-1)
    kn = jnp.concatenate([_l2norm(x) for x in heads(qkv[:, MIX_A:2 * MIX_A])], axis=-1)
    return qn, kn


def _gdn_out(o, gate, anw):
    on = jnp.concatenate(
        [x * lax.rsqrt(jnp.mean(x * x, axis=-1, keepdims=True) + RMS_EPS)
         for x in [o[:, h * DK_A:(h + 1) * DK_A] for h in range(HA)]], axis=-1)
    return on * anw * _silu(gate)


def _gdn_prompt_kernel(pa_ref, bd_ref, cw_ref, hp_ref, anw_ref, x_ref, mb_ref, wo_ref, fw_ref,
                       y_ref, s_out_ref, c_out_ref, ext_ref, s_ref):
    c = pl.program_id(1)

    nseq, tc = pa_ref.shape[0], pa_ref.shape[1]

    @pl.when(c == 0)
    def _():
        ext_ref[...] = jnp.zeros_like(ext_ref)
        s_ref[...] = jnp.zeros_like(s_ref)

    cw = cw_ref[...]
    conv = []
    for s in range(nseq):
        u = pa_ref[s, :, 0:A_QKV]
        full = jnp.concatenate([ext_ref[s], u], axis=0)
        conv.append(u * cw[3:4, :] + sum(
            pltpu.roll(full, CONV_W - 1 - i, 0)[8:, :] * cw[i:i + 1, :] for i in range(CONV_W - 1)))
        ext_ref[s] = u[tc - 8:, :]
        c_out_ref[s] = ext_ref[s, 8 - (CONV_W - 1):, :]
    qkv = _silu(jnp.concatenate(conv, axis=0))

    qn, kn = _gdn_qk(qkv)
    beta_all, g_all = _gdn_gates(bd_ref[...].reshape(nseq * tc, BD_W), hp_ref[...])
    gcum = jnp.concatenate([_chunk_cumsum(g_all[s * tc:(s + 1) * tc, :]) for s in range(nseq)],
                           axis=0)
    gcum_t = gcum.T

    nrow = nseq * tc
    wide = lambda z, first: jnp.concatenate(
        [jnp.broadcast_to(z[:, first + h:first + h + 1], (nrow, DK_A)) for h in range(HA)], axis=1)
    bw, gw = wide(beta_all, 0), wide(gcum, HA)
    half = _iota2((nrow, PAIR), 1) < CHUNK
    narrow = lambda zw: jnp.concatenate(
        [jnp.where(half, zw[:, 2 * p * DK_A:(2 * p + 1) * DK_A], zw[:, (2 * p + 1) * DK_A:(2 * p + 2) * DK_A])
         for p in range(HA // 2)], axis=1)
    b64, g64 = narrow(bw), narrow(gw)
    v_all = qkv[:, 2 * MIX_A:]
    egw = jnp.exp(gw)
    bek = bw * egw * kn
    bv = bw * v_all
    qg = qn * egw

    rowp = _iota2((CHUNK, PAIR), 0)
    lane = _iota2((CHUNK, PAIR), 1)
    colp = jnp.where(lane >= CHUNK, lane - CHUNK, lane)
    incl = colp <= rowp
    strict = colp < rowp

    nj = nseq * tc // CHUNK
    npair = HA // 2
    jp = [(j, p) for j in range(nj) for p in range(npair)]
    jh = [(j, h) for j in range(nj) for h in range(HA)]
    rows_of = lambda j: slice(j * CHUNK, (j + 1) * CHUNK)
    lanes_of = lambda h: slice(h * DK_A, (h + 1) * DK_A)
    pair64 = lambda p: slice(p * PAIR, (p + 1) * PAIR)
    pairw = lambda p: slice(2 * p * DK_A, 2 * (p + 1) * DK_A)
    glast = [gw[(j + 1) * CHUNK - 1:(j + 1) * CHUNK, :] for j in range(nj)]
    kdec = [kn[rows_of(j), :] * jnp.exp(glast[j] - gw[rows_of(j), :]) for j in range(nj)]
    eglast = [jnp.exp(glast[j]) for j in range(nj)]
    gr = [jnp.concatenate([gcum_t[HA + 2 * p:HA + 2 * p + 1, rows_of(j)],
                           gcum_t[HA + 2 * p + 1:HA + 2 * p + 2, rows_of(j)]], axis=1) for j, p in jp]
    decay = [jnp.where(incl, jnp.exp(jnp.where(incl, g64[rows_of(j), pair64(p)] - b, 0.0)), 0.0)
             for (j, p), b in zip(jp, gr)]
    kq = [_dot(jnp.concatenate([kn[rows_of(j), pairw(p)], qn[rows_of(j), pairw(p)]], axis=0),
               _pair_diag(kn[rows_of(j), pairw(p)]), NT) for j, p in jp]
    low = [jnp.where(strict, b64[rows_of(j), pair64(p)] * d * m[0:CHUNK, :], 0.0)
           for (j, p), d, m in zip(jp, decay, kq)]
    tinv = _tri_inv_pairs(low)
    zw = jnp.zeros((CHUNK, 2 * DK_A), F32)
    wu = [_dot(t, jnp.concatenate(
        [jnp.concatenate([bek[rows_of(j), lanes_of(2 * p)], bv[rows_of(j), lanes_of(2 * p)], zw], axis=1),
         jnp.concatenate([zw, bek[rows_of(j), lanes_of(2 * p + 1)], bv[rows_of(j), lanes_of(2 * p + 1)]],
                         axis=1)], axis=0))
          for t, (j, p) in zip(tinv, jp)]
    qwu_p = [_dot(d * m[CHUNK:, :], _pair_diag(x)) for d, m, x in zip(decay, kq, wu)]
    head_of = lambda z, j, h: z[j * npair + h // 2][:, (h % 2) * 2 * DK_A:(h % 2 + 1) * 2 * DK_A]
    qwu = [head_of(qwu_p, j, h) for j, h in jh]
    kwu = [_dot(kdec[j][:, lanes_of(h)], head_of(wu, j, h), TN) for j, h in jh]
    rd = _iota2((DK_A, DK_A), 0)
    cd = _iota2((DK_A, DK_A), 1)
    lhs = [jnp.concatenate([jnp.where(rd == cd, eglast[j][:, lanes_of(h)], 0.0) - kx[:, 0:DK_A],
                            qg[rows_of(j), lanes_of(h)] - qx[:, 0:DK_A]], axis=0)
           for (j, h), kx, qx in zip(jh, kwu, qwu)]

    o_rows = [None] * nj
    for jj in range(tc // CHUNK):
        for s in range(nseq):
            j = s * (tc // CHUNK) + jj
            idx = [j * HA + h for h in range(HA)]
            res = [_dot(lhs[i], s_ref[s, h]) for h, i in enumerate(idx)]
            for h, i in enumerate(idx):
                s_ref[s, h] = res[h][0:DK_A, :] + kwu[i][:, DK_A:]
            o_rows[j] = jnp.concatenate(
                [res[h][DK_A:, :] + qwu[i][:, DK_A:] for h, i in enumerate(idx)], axis=-1)

    gate = pa_ref[:, :, A_QKV:A_MAIN].reshape(nseq * tc, MIX_A)
    mix_a = _gdn_out(jnp.concatenate(o_rows, axis=0), gate, anw_ref[...])
    h = (x_ref[...].reshape(nseq * tc, D_MODEL)
         + _dot(mb_ref[...].reshape(nseq * tc, MIX_B), wo_ref[MIX_A:, :])
         + _dot(mix_a.astype(BF16), wo_ref[0:MIX_A, :]))
    y = h * lax.rsqrt(jnp.mean(h * h, axis=-1, keepdims=True) + RMS_EPS) * fw_ref[...]
    y_ref[...] = y.reshape(nseq, tc, D_MODEL)
    s_out_ref[...] = s_ref[...]


def _gdn_prompt(pa, bd, conv_w, hp, anw, x, mix_b, w_out, final_w):
    b, t, _ = pa.shape
    blk = lambda i, j: (i, j, 0)
    fixed = lambda i, j: (0, 0)
    return pl.pallas_call(
        _gdn_prompt_kernel,
        grid=(b // NSEQ_A, t // TC_A),
        in_specs=[pl.BlockSpec((NSEQ_A, TC_A, A_MAIN), blk),
                  pl.BlockSpec((NSEQ_A, TC_A, BD_W), blk),
                  pl.BlockSpec((CONV_W, A_QKV), fixed),
                  pl.BlockSpec((8, BD_W), fixed),
                  pl.BlockSpec((1, MIX_A), fixed),
                  pl.BlockSpec((NSEQ_A, TC_A, D_MODEL), blk),
                  pl.BlockSpec((NSEQ_A, TC_A, MIX_B), blk),
                  pl.BlockSpec((D_MODEL, D_MODEL), fixed),
                  pl.BlockSpec((1, D_MODEL), fixed)],
        out_specs=[pl.BlockSpec((NSEQ_A, TC_A, D_MODEL), blk),
                   pl.BlockSpec((NSEQ_A, HA, DK_A, DK_A), lambda i, j: (i, 0, 0, 0)),
                   pl.BlockSpec((NSEQ_A, CONV_W - 1, A_QKV), lambda i, j: (i, 0, 0))],
        out_shape=[jax.ShapeDtypeStruct((b, t, D_MODEL), F32),
                   jax.ShapeDtypeStruct((b, HA, DK_A, DK_A), F32),
                   jax.ShapeDtypeStruct((b, CONV_W - 1, A_QKV), F32)],
        scratch_shapes=[pltpu.VMEM((NSEQ_A, 8, A_QKV), F32),
                        pltpu.VMEM((NSEQ_A, HA, DK_A, DK_A), F32)],
        compiler_params=pltpu.CompilerParams(dimension_semantics=("arbitrary", "arbitrary"),
                                             vmem_limit_bytes=VMEM_LIMIT),
        name="gdn_prompt",
    )(pa, bd, conv_w, hp, anw, x, mix_b, w_out, final_w)


def _rwkv_front(xb, w0, w2, a0, a2, k_k, k_a, ones_blk):
    r = xb[:, 0:MIX_B]
    kr = xb[:, MIX_B:2 * MIX_B]
    vr = xb[:, 2 * MIX_B:3 * MIX_B]
    gate = xb[:, 3 * MIX_B:4 * MIX_B]
    wd = xb[:, 4 * MIX_B:4 * MIX_B + LORA]
    ad = xb[:, 4 * MIX_B + LORA:]
    logw = -(EXP_NEG_HALF * _sigmoid(w0 + _dot(jnp.tanh(wd), w2)))
    a = _sigmoid(a0 + _dot(ad, a2))
    kraw = kr * k_k
    kk = kraw * lax.rsqrt(_group_sum(kraw * kraw, ones_blk) + 1e-12)
    kmod = kr * (1.0 + (a - 1.0) * k_a)
    return r, kmod, vr, gate, logw, a, kk


def _rwkv_back(y, r, kmod, v, gate, rk, lnw, lnb, ones_blk):
    inv = 1.0 / HS_B
    yc = y - _group_sum(y, ones_blk) * inv
    var = _group_sum(yc * yc, ones_blk) * inv
    gn = yc * lax.rsqrt(var + GN_EPS) * lnw + lnb
    bonus = _group_sum(r * kmod * rk, ones_blk) * v
    return (gn + bonus) * _silu(gate)


def _rwkv_prompt_kernel(pb_ref, mu_ref, w0_ref, w2_ref, a0_ref, a2_ref, kk_ref, ka_ref,
                        rk_ref, lnw_ref, lnb_ref, ones_ref,
                        o_ref, s_out_ref, sh_out_ref, carry_ref, s_ref):
    c = pl.program_id(1)

    @pl.when(c == 0)
    def _():
        carry_ref[...] = jnp.zeros_like(carry_ref)
        s_ref[...] = jnp.zeros_like(s_ref)

    ones_blk = ones_ref[...]
    pb = pb_ref[...].reshape(NSEQ * TC, B_W)
    prev = []
    for s in range(NSEQ):
        pb_s = pb_ref[s]
        prev.append(pltpu.roll(jnp.concatenate([carry_ref[s], pb_s], axis=0), 1, 0)[8:, :])
        carry_ref[s] = pb_s[TC - 8:, :]
        sh_out_ref[s] = pb_s[TC - 1:, :]
    prev = jnp.concatenate(prev, axis=0)
    xb = pb + (prev - pb) * mu_ref[...]
    r, kmod, v, gate, logw, a, kk = _rwkv_front(
        xb, w0_ref[...], w2_ref[...], a0_ref[...], a2_ref[...], kk_ref[...], ka_ref[...], ones_blk)

    g = jnp.concatenate([_chunk_cumsum(logw[s * TC:(s + 1) * TC, :]) for s in range(NSEQ)],
                        axis=0)
    eg = jnp.exp(g)
    egn = jnp.exp(-g)
    ag = -kk * jnp.exp(g - logw)
    kka = kk * a
    bg = kka * egn
    kg = kmod * egn
    rg = r * eg

    rowp = _iota2((CHUNK, PAIR), 0)
    lane = _iota2((CHUNK, PAIR), 1)
    colp = jnp.where(lane >= HS_B, lane - HS_B, lane)
    left = lane < HS_B
    incl = colp <= rowp
    strict = colp < rowp
    eye = colp == rowp
    zero = jnp.zeros((CHUNK, PAIR), F32)
    pack = lambda f: jnp.where(left, f[0:HS_B, :], f[HS_B:, :])

    nj = NSEQ * TC // CHUNK
    npair = HB // 2
    jp = [(j, p) for j in range(nj) for p in range(npair)]
    rows_of = lambda j: slice(j * CHUNK, (j + 1) * CHUNK)
    lanes_of = lambda p: slice(p * PAIR, (p + 1) * PAIR)
    glast = [g[(j + 1) * CHUNK - 1:(j + 1) * CHUNK, :] for j in range(nj)]
    eglast = [jnp.exp(glast[j]) for j in range(nj)]
    bdec = [bg[rows_of(j), :] * eglast[j] for j in range(nj)]
    kdec = [kg[rows_of(j), :] * eglast[j] for j in range(nj)]

    cut = lambda z: [z[rows_of(j), lanes_of(p)] for j, p in jp]
    ag_p, rg_p, bg_p, kg_p, v_p = cut(ag), cut(rg), cut(bg), cut(kg), cut(v)
    bdec_p = [bdec[j][:, lanes_of(p)] for j, p in jp]
    kdec_p = [kdec[j][:, lanes_of(p)] for j, p in jp]
    eglast_p = [eglast[j][:, lanes_of(p)] for j, p in jp]
    amat = [_dot(jnp.concatenate([a, b], axis=0),
                 jnp.concatenate([_pair_diag(c_), _pair_diag(d)], axis=0), NT)
            for a, b, c_, d in zip(ag_p, rg_p, bg_p, kg_p)]
    a_ab = [jnp.where(strict, m[0:CHUNK, 0:PAIR], 0.0) for m in amat]
    a_ak = [jnp.where(strict, m[0:CHUNK, PAIR:], 0.0) for m in amat]
    a_rb = [jnp.where(incl, m[CHUNK:, 0:PAIR], 0.0) for m in amat]
    a_rk = [jnp.where(incl, m[CHUNK:, PAIR:], 0.0) for m in amat]
    tinv = _tri_inv_pairs([-a for a in a_ab])
    kv = [_dot(jnp.concatenate([a, b], axis=0), _pair_diag(c_)) for a, b, c_ in zip(a_ak, a_rk, v_p)]
    wu = [_dot(t, jnp.concatenate([_pair_diag(a), _pair_diag(x[0:CHUNK, :])], axis=1))
          for t, a, x in zip(tinv, ag_p, kv)]
    rwu = [_dot(a, jnp.concatenate([_pair_diag(x[:, 0:PAIR]), _pair_diag(x[:, PAIR:])], axis=1))
           for a, x in zip(a_rb, wu)]
    mn = [_dot(jnp.concatenate([a, b], axis=0),
               jnp.concatenate([x, jnp.concatenate([zero, c_], axis=1)], axis=0), TN)
          for a, b, x, c_ in zip(bdec_p, kdec_p, wu, v_p)]
    lhs = [jnp.concatenate([jnp.where(eye, e, 0.0) + pack(m[:, 0:PAIR]), a + x[:, 0:PAIR]], axis=0)
           for e, m, a, x in zip(eglast_p, mn, rg_p, rwu)]

    y_rows = [None] * nj
    for jj in range(TC // CHUNK):
        for s in range(NSEQ):
            j = s * (TC // CHUNK) + jj
            idx = [j * npair + p for p in range(npair)]
            res = [_dot(lhs[i], _pair_diag(s_ref[s, p])) for p, i in enumerate(idx)]
            for p, i in enumerate(idx):
                s_ref[s, p] = res[p][0:HS_B, :] + pack(mn[i][:, PAIR:])
            y_rows[j] = jnp.concatenate(
                [res[p][HS_B:, :] + rwu[i][:, PAIR:] + kv[i][CHUNK:, :] for p, i in enumerate(idx)],
                axis=-1)

    out = _rwkv_back(jnp.concatenate(y_rows, axis=0), r, kmod, v, gate,
                     rk_ref[...], lnw_ref[...], lnb_ref[...], ones_blk)
    o_ref[...] = out.reshape(NSEQ, TC, MIX_B).astype(o_ref.dtype)
    for s in range(NSEQ):
        for p in range(npair):
            st = s_ref[s, p].T
            s_out_ref[s, 2 * p] = st[0:HS_B, :]
            s_out_ref[s, 2 * p + 1] = st[HS_B:, :]


def _rwkv_prompt(pb, mu, w0, w2, a0, a2, k_k, k_a, rk, lnw, lnb, ones_blk):
    b, t, _ = pb.shape
    blk = lambda i, j: (i, j, 0)
    fixed = lambda i, j: (0, 0)
    vec = pl.BlockSpec((1, MIX_B), fixed)
    return pl.pallas_call(
        _rwkv_prompt_kernel,
        grid=(b // NSEQ, t // TC),
        in_specs=[pl.BlockSpec((NSEQ, TC, B_W), blk),
                  pl.BlockSpec((1, B_W), fixed),
                  vec, pl.BlockSpec((LORA, MIX_B), fixed),
                  vec, pl.BlockSpec((LORA, MIX_B), fixed),
                  vec, vec, vec, vec, vec,
                  pl.BlockSpec((MIX_B, MIX_B), fixed)],
        out_specs=[pl.BlockSpec((NSEQ, TC, MIX_B), blk),
                   pl.BlockSpec((NSEQ, HB, HS_B, HS_B), lambda i, j: (i, 0, 0, 0)),
                   pl.BlockSpec((NSEQ, 1, B_W), lambda i, j: (i, 0, 0))],
        out_shape=[jax.ShapeDtypeStruct((b, t, MIX_B), BF16),
                   jax.ShapeDtypeStruct((b, HB, HS_B, HS_B), F32),
                   jax.ShapeDtypeStruct((b, 1, B_W), F32)],
        scratch_shapes=[pltpu.VMEM((NSEQ, 8, B_W), F32),
                        pltpu.VMEM((NSEQ, HB // 2, HS_B, PAIR), F32)],
        compiler_params=pltpu.CompilerParams(dimension_semantics=("arbitrary", "arbitrary"),
                                             vmem_limit_bytes=VMEM_LIMIT),
        name="rwkv_prompt",
    )(pb, mu, w0, w2, a0, a2, k_k, k_a, rk, lnw, lnb, ones_blk)


def _pad_rows(rows, width):
    return jnp.concatenate(rows + [jnp.zeros((8 - len(rows), width), F32)], axis=0)


def _gdn_step_kernel(pa_ref, bd_ref, sc_ref, s_ref, cw_ref, hp_ref, anw_ref,
                     o_ref, s_out_ref, c_out_ref):
    u = pa_ref[:, 0:A_QKV]
    cw = cw_ref[...]
    conv = (sc_ref[0] * cw[0:1, :] + sc_ref[1] * cw[1:2, :] + sc_ref[2] * cw[2:3, :] + u * cw[3:4, :])
    qkv = _silu(conv)
    c_out_ref[0] = sc_ref[1]
    c_out_ref[1] = sc_ref[2]
    c_out_ref[2] = u

    qn, kn = _gdn_qk(qkv)
    beta_all, g_all = _gdn_gates(bd_ref[...], hp_ref[...])
    eg_all = jnp.exp(g_all)
    lanes_of = lambda h: slice(h * DK_A, (h + 1) * DK_A)
    q = [qn[:, lanes_of(h)] for h in range(HA)]
    k = [kn[:, lanes_of(h)] for h in range(HA)]
    v = [qkv[:, 2 * MIX_A + h * DK_A:2 * MIX_A + (h + 1) * DK_A] for h in range(HA)]
    beta = [beta_all[:, h:h + 1] for h in range(HA)]
    eg = [eg_all[:, HA + h:HA + h + 1] for h in range(HA)]

    bh = [(b, h) for b in range(SB) for h in range(HA)]
    s0 = [s_ref[b, h] for b, h in bh]
    ks_qs = [_dot(_pad_rows([k[h][b:b + 1, :], q[h][b:b + 1, :]], DK_A), s) for (b, h), s in zip(bh, s0)]
    o_heads = []
    for h in range(HA):
        ks = jnp.concatenate([ks_qs[b * HA + h][0:1, :] for b in range(SB)], axis=0)
        qs = jnp.concatenate([ks_qs[b * HA + h][1:2, :] for b in range(SB)], axis=0)
        u2 = beta[h] * (v[h] - eg[h] * ks)
        for b in range(SB):
            s_out_ref[b, h] = (eg[h][b:b + 1, :] * s0[b * HA + h]
                               + _dot(_pad_rows([k[h][b:b + 1, :]], DK_A),
                                      _pad_rows([u2[b:b + 1, :]], DK_A), TN))
        o_heads.append(eg[h] * qs + jnp.sum(q[h] * k[h], axis=-1, keepdims=True) * u2)

    o_ref[...] = _gdn_out(jnp.concatenate(o_heads, axis=-1), pa_ref[:, A_QKV:A_MAIN], anw_ref[...])


def _gdn_step(pa, bd, sconv_t, s_a, conv_w, hp, anw):
    n = pa.shape[0]
    row = lambda i: (i, 0)
    fixed = lambda i: (0, 0)
    return pl.pallas_call(
        _gdn_step_kernel,
        grid=(n // SB,),
        in_specs=[pl.BlockSpec((SB, A_MAIN), row),
                  pl.BlockSpec((SB, BD_W), row),
                  pl.BlockSpec((CONV_W - 1, SB, A_QKV), lambda i: (0, i, 0)),
                  pl.BlockSpec((SB, HA, DK_A, DK_A), lambda i: (i, 0, 0, 0)),
                  pl.BlockSpec((CONV_W, A_QKV), fixed),
                  pl.BlockSpec((8, BD_W), fixed),
                  pl.BlockSpec((1, MIX_A), fixed)],
        out_specs=[pl.BlockSpec((SB, MIX_A), row),
                   pl.BlockSpec((SB, HA, DK_A, DK_A), lambda i: (i, 0, 0, 0)),
                   pl.BlockSpec((CONV_W - 1, SB, A_QKV), lambda i: (0, i, 0))],
        out_shape=[jax.ShapeDtypeStruct((n, MIX_A), F32),
                   jax.ShapeDtypeStruct((n, HA, DK_A, DK_A), F32),
                   jax.ShapeDtypeStruct((CONV_W - 1, n, A_QKV), F32)],
        compiler_params=pltpu.CompilerParams(dimension_semantics=("arbitrary",),
                                             vmem_limit_bytes=VMEM_LIMIT),
        name="gdn_step",
    )(pa, bd, sconv_t, s_a, conv_w, hp, anw)


def _rwkv_step_lanes_kernel(pb_ref, sh_ref, s_ref, mu_ref, w0_ref, w2_ref, a0_ref, a2_ref, kk_ref, ka_ref,
                            rk_ref, lnw_ref, lnb_ref, ones_ref,
                            o_ref, s_out_ref, vec_scr, row_scr, y_scr):
    h = pl.program_id(0)
    ones_blk = ones_ref[...]

    @pl.when(h == 0)
    def _():
        pb = pb_ref[...]
        xb = pb + (sh_ref[...] - pb) * mu_ref[...]
        r, kmod, v, gate, logw, a, kk = _rwkv_front(
            xb, w0_ref[...], w2_ref[...], a0_ref[...], a2_ref[...], kk_ref[...], ka_ref[...], ones_blk)
        for slot, z in enumerate((-kk, kk * a, kmod, v, r, jnp.exp(logw))):
            vec_scr[slot] = z.T
        for slot, z in enumerate((r, kmod, v, gate)):
            row_scr[slot] = z

    base = pl.multiple_of(h * HS_B, HS_B)
    chan = pl.ds(base, HS_B)
    nkk = vec_scr[0, chan, :]
    kka = vec_scr[1, chan, :]
    kmod_t = vec_scr[2, chan, :]
    r_t = vec_scr[4, chan, :]
    wdec = vec_scr[5, chan, :]

    def body(vi, carry):
        s0 = s_ref[0, vi]
        sa = jnp.sum(s0 * nkk, axis=0, keepdims=True)
        s1 = s0 * wdec + sa * kka + vec_scr[3, pl.ds(base + vi, 1), :] * kmod_t
        s_out_ref[0, vi] = s1
        y_scr[pl.ds(base + vi, 1), :] = jnp.sum(s1 * r_t, axis=0, keepdims=True)
        return carry

    lax.fori_loop(0, HS_B, body, 0, unroll=4)

    @pl.when(h == HB - 1)
    def _():
        o_ref[...] = _rwkv_back(y_scr[...].T, row_scr[0], row_scr[1], row_scr[2], row_scr[3],
                                rk_ref[...], lnw_ref[...], lnb_ref[...], ones_blk)


def _rwkv_step_lanes(pb, shift, s_hvkb, mu, w0, w2, a0, a2, k_k, k_a, rk, lnw, lnb, ones_blk):
    n = pb.shape[0]
    fixed = lambda i: (0, 0)
    vec = pl.BlockSpec((1, MIX_B), fixed)
    state = pl.BlockSpec((1, HS_B, HS_B, n), lambda i: (i, 0, 0, 0))
    return pl.pallas_call(
        _rwkv_step_lanes_kernel,
        grid=(HB,),
        in_specs=[pl.BlockSpec((n, B_W), fixed),
                  pl.BlockSpec((n, B_W), fixed),
                  state,
                  pl.BlockSpec((1, B_W), fixed),
                  vec, pl.BlockSpec((LORA, MIX_B), fixed),
                  vec, pl.BlockSpec((LORA, MIX_B), fixed),
                  vec, vec, vec, vec, vec,
                  pl.BlockSpec((MIX_B, MIX_B), fixed)],
        out_specs=[pl.BlockSpec((n, MIX_B), fixed), state],
        out_shape=[jax.ShapeDtypeStruct((n, MIX_B), F32),
                   jax.ShapeDtypeStruct((HB, HS_B, HS_B, n), F32)],
        scratch_shapes=[pltpu.VMEM((6, MIX_B, n), F32),
                        pltpu.VMEM((4, n, MIX_B), F32),
                        pltpu.VMEM((MIX_B, n), F32)],
        compiler_params=pltpu.CompilerParams(dimension_semantics=("arbitrary",),
                                             vmem_limit_bytes=VMEM_LIMIT),
        name="rwkv_step",
    )(pb, shift, s_hvkb, mu, w0, w2, a0, a2, k_k, k_a, rk, lnw, lnb, ones_blk)


def _block_ones(width, group):
    idx = jnp.arange(width) // group
    return (idx[:, None] == idx[None, :]).astype(BF16)


def kernel(x_prompt, x_sample, state_a_mat, state_a_conv, state_b_mat, state_b_shift, norm_w, w_in,
           conv_w, a_log, dt_bias, a_norm_w, mu, w0, w2, a0, a2, k_k, k_a, r_k, ln_w, ln_b, w_out,
           final_norm_w):
    bsz, seq, _ = x_prompt.shape
    nsmp = x_sample.shape[0]
    assert bsz % NSEQ == 0 and bsz % NSEQ_A == 0 and seq % TC == 0 and seq % TC_A == 0
    assert nsmp % SB == 0 and x_sample.shape[1] == 1

    w_all = jnp.transpose(w_in[0])
    w_o = w_out[0].astype(BF16)
    hp = jnp.pad(jnp.concatenate([a_log, dt_bias], axis=0), ((0, 6), (HA, BD_W - 2 * HA)))
    nw = norm_w[0].reshape(1, D_MODEL)
    fw = final_norm_w.reshape(1, D_MODEL)
    anw = jnp.tile(a_norm_w[0].reshape(1, DK_A), (1, HA))
    ones_b = _block_ones(MIX_B, HS_B)
    row = lambda z: z[0].reshape(1, -1)
    b_params = (row(mu), row(w0), w2[0], row(a0), a2[0], row(k_k), row(k_a), row(r_k), row(ln_w), row(ln_b),
                ones_b)

    xp = x_prompt.reshape(bsz * seq, D_MODEL)
    xs = x_sample.reshape(nsmp * D_MODEL // LANE, LANE)
    pa, pb, bd, sa_, sb_, sbd = _inproj(xp, xs, nw, w_all)

    ob, p_bmat, p_bshift = _rwkv_prompt(pb.reshape(bsz, seq, B_W), *b_params)
    y_prompt, p_amat, p_aconv = _gdn_prompt(pa.reshape(bsz, seq, A_MAIN), bd.reshape(bsz, seq, BD_W),
                                            conv_w[0], hp, anw, x_prompt, ob, w_o, fw)

    sconv_t = jnp.swapaxes(state_a_conv[0], 0, 1)
    soa, s_amat, s_aconv_t = _gdn_step(sa_, sbd, sconv_t, state_a_mat[0], conv_w[0], hp, anw)
    sob, s_bmat_t = _rwkv_step_lanes(sb_, state_b_shift[0], jnp.transpose(state_b_mat[0], (1, 2, 3, 0)),
                                     *b_params)
    s_bmat = jnp.transpose(s_bmat_t, (3, 0, 1, 2))
    y_sample = _outproj(xs, soa, sob, w_o, fw)

    return (y_prompt, y_sample.reshape(nsmp, 1, D_MODEL),
            p_amat[None], p_aconv[None], p_bmat[None], p_bshift.reshape(1, bsz, B_W),
            s_amat[None], jnp.swapaxes(s_aconv_t, 0, 1)[None], s_bmat[None], sb_[None])
```

```python
import jax
import jax.numpy as jnp
from jax import lax
from jax.experimental import pallas as pl
from jax.experimental.pallas import tpu as pltpu

F32 = jnp.float32
BF16 = jnp.bfloat16

D_MODEL = 1024
MIX_A = 512
MIX_B = 512
HA = 4
DK_A = 128
HB = 8
HS_B = 64
LANE = 128
PAIR = 2 * HS_B
CONV_W = 4
LORA = 64
A_QKV = 3 * MIX_A
A_MAIN = A_QKV + MIX_A
B_W = 4 * MIX_B + 2 * LORA
BD_W = 128
RMS_EPS = 1e-6
GN_EPS = 64e-5
CHUNK = 64
TC = 256
TC_A = 256
NSEQ_A = 2
NSEQ = 4
SB = 16
VMEM_LIMIT = 56 * 1024 * 1024

EXP_NEG_HALF = 0.6065306597126334
NN = (((1,), (0,)), ((), ()))
NT = (((1,), (1,)), ((), ()))
TN = (((0,), (0,)), ((), ()))


def _dot(a, b, dn=NN):
    return lax.dot_general(a, b, dn, preferred_element_type=F32)


def _sigmoid(x):
    return 0.5 * jnp.tanh(0.5 * x) + 0.5


def _silu(x):
    h = 0.5 * x
    return h * jnp.tanh(h) + h


def _softplus(x):
    return jnp.maximum(x, 0.0) + jnp.log(1.0 + jnp.exp(-jnp.abs(x)))


def _l2norm(x):
    return x * lax.rsqrt(jnp.sum(x * x, axis=-1, keepdims=True) + 1e-12)


def _group_sum(x, ones_blk):
    return _dot(x.astype(BF16), ones_blk)


def _iota2(shape, dim):
    return lax.broadcasted_iota(jnp.int32, shape, dim)


def _chunk_cumsum(x):
    n = x.shape[0]
    r = _iota2((n, n), 0)
    c = _iota2((n, n), 1)
    tril = jnp.where(((r // CHUNK) == (c // CHUNK)) & (c <= r), 1.0, 0.0).astype(BF16)
    hi = x.astype(BF16)
    rest = x - hi.astype(F32)
    mid = rest.astype(BF16)
    lo = (rest - mid.astype(F32)).astype(BF16)
    return _dot(tril, hi) + _dot(tril, mid) + _dot(tril, lo)


def _tri_inv_many(lows):
    n = lows[0].shape[0]
    r = _iota2((n, n), 0)
    c = _iota2((n, n), 1)
    eye = jnp.where(r == c, 1.0, 0.0).astype(F32)
    base = (r // 8) == (c // 8)
    l0 = [jnp.where(base, low, 0.0) for low in lows]
    l2 = [_dot(a, a) for a in l0]
    l4 = [_dot(a, a) for a in l2]
    d = [_dot(eye - a, eye + b) for a, b in zip(l0, l2)]
    d = [_dot(a, eye + b) for a, b in zip(d, l4)]
    s = 8
    while s < n:
        sel = ((r // (2 * s)) == (c // (2 * s))) & ((r // s) != (c // s))
        t = [_dot(jnp.where(sel, low, 0.0), a) for low, a in zip(lows, d)]
        d = [a - _dot(a, b) for a, b in zip(d, t)]
        s *= 2
    return d


def _pair_diag(x):
    left = _iota2(x.shape, 1) < x.shape[1] // 2
    return jnp.concatenate([jnp.where(left, x, 0.0), jnp.where(left, 0.0, x)], axis=0)


def _tri_inv_pairs(lows):
    n = lows[0].shape[0]
    r = _iota2((n, 2 * n), 0)
    lane = _iota2((n, 2 * n), 1)
    c = jnp.where(lane >= n, lane - n, lane)
    mul = lambda a, b: _dot(a, _pair_diag(b))
    eye = jnp.where(r == c, 1.0, 0.0).astype(F32)
    base = (r // 8) == (c // 8)
    l0 = [jnp.where(base, low, 0.0) for low in lows]
    l2 = [mul(a, a) for a in l0]
    l4 = [mul(a, a) for a in l2]
    d = [mul(eye - a, eye + b) for a, b in zip(l0, l2)]
    d = [mul(a, eye + b) for a, b in zip(d, l4)]
    s = 8
    while s < n:
        sel = ((r // (2 * s)) == (c // (2 * s))) & ((r // s) != (c // s))
        t = [mul(jnp.where(sel, low, 0.0), a) for low, a in zip(lows, d)]
        d = [a - mul(a, b) for a, b in zip(d, t)]
        s *= 2
    return d


def _sample_rows(ref, n):
    pieces = D_MODEL // LANE
    return jnp.concatenate([ref[pl.ds(c, n, stride=pieces), :] for c in range(pieces)], axis=1)


def _inproj_kernel(x_ref, xs_ref, nw_ref, wt_ref, oa_ref, ob_ref, obd_ref, sa_ref, sb_ref, sbd_ref, w_scr):
    a_w = A_MAIN + 2 * HA

    def project(x, pa_ref, pb_ref, pbd_ref):
        xn = (x * lax.rsqrt(jnp.mean(x * x, axis=-1, keepdims=True) + RMS_EPS) * nw_ref[...]).astype(BF16)
        pa_ref[...] = _dot(xn, w_scr[0:A_MAIN, :], NT)
        pb = _dot(xn, w_scr[A_MAIN:, :], NT)
        pb_ref[...] = pb[:, :B_W]
        pbd_ref[...] = pb[:, B_W:]

    @pl.when(pl.program_id(0) == 0)
    def _():
        w_scr[0:A_MAIN, :] = wt_ref[0:A_MAIN, :].astype(BF16)
        w_scr[A_MAIN:A_MAIN + B_W, :] = wt_ref[a_w:, :].astype(BF16)
        w_scr[A_MAIN + B_W:, :] = jnp.concatenate(
            [wt_ref[A_MAIN:a_w, :], jnp.zeros((BD_W - 2 * HA, D_MODEL), F32)], axis=0).astype(BF16)
        project(_sample_rows(xs_ref, sa_ref.shape[0]), sa_ref, sb_ref, sbd_ref)

    project(x_ref[...], oa_ref, ob_ref, obd_ref)


def _inproj(x2d, xs2d, norm_w, w_all):
    n, ns = x2d.shape[0], xs2d.shape[0] * LANE // D_MODEL
    tm = min(512, n)
    row = lambda i: (i, 0)
    fixed = lambda i: (0, 0)
    widths = (A_MAIN, B_W, BD_W)
    return pl.pallas_call(
        _inproj_kernel,
        grid=(n // tm,),
        in_specs=[pl.BlockSpec((tm, D_MODEL), row),
                  pl.BlockSpec(xs2d.shape, fixed),
                  pl.BlockSpec((1, D_MODEL), fixed),
                  pl.BlockSpec(w_all.shape, fixed)],
        out_specs=[pl.BlockSpec((tm, w), row) for w in widths] + [pl.BlockSpec((ns, w), fixed) for w in widths],
        out_shape=[jax.ShapeDtypeStruct((n, w), F32) for w in widths]
                  + [jax.ShapeDtypeStruct((ns, w), F32) for w in widths],
        scratch_shapes=[pltpu.VMEM((A_MAIN + B_W + BD_W, D_MODEL), BF16)],
        compiler_params=pltpu.CompilerParams(dimension_semantics=("arbitrary",),
                                             vmem_limit_bytes=VMEM_LIMIT),
        name="inproj",
    )(x2d, xs2d, norm_w, w_all)


def _outproj_kernel(x_ref, ma_ref, mb_ref, w_ref, fw_ref, y_ref):
    n = ma_ref.shape[0]
    h = (_sample_rows(x_ref, n)
         + _dot(ma_ref[...].astype(BF16), w_ref[0:MIX_A, :])
         + _dot(mb_ref[...].astype(BF16), w_ref[MIX_A:, :]))
    y = h * lax.rsqrt(jnp.mean(h * h, axis=-1, keepdims=True) + RMS_EPS) * fw_ref[...]
    pieces = D_MODEL // LANE
    for c in range(pieces):
        y_ref[pl.ds(c, n, stride=pieces), :] = y[:, c * LANE:(c + 1) * LANE]


def _outproj(xs2d, mix_a, mix_b, w_out, final_w):
    n = mix_a.shape[0]
    fixed = lambda i: (0, 0)
    return pl.pallas_call(
        _outproj_kernel,
        grid=(1,),
        in_specs=[pl.BlockSpec(xs2d.shape, fixed),
                  pl.BlockSpec((n, MIX_A), fixed),
                  pl.BlockSpec((n, MIX_B), fixed),
                  pl.BlockSpec((D_MODEL, D_MODEL), fixed),
                  pl.BlockSpec((1, D_MODEL), fixed)],
        out_specs=pl.BlockSpec(xs2d.shape, fixed),
        out_shape=jax.ShapeDtypeStruct(xs2d.shape, F32),
        compiler_params=pltpu.CompilerParams(dimension_semantics=("arbitrary",),
                                             vmem_limit_bytes=VMEM_LIMIT),
        name="outproj",
    )(xs2d, mix_a, mix_b, w_out, final_w)


def _gdn_gates(bd, hp):
    beta = _sigmoid(bd)
    g = -jnp.exp(hp[0:1, :]) * _softplus(bd + hp[1:2, :])
    return beta, g


def _gdn_qk(qkv):
    heads = lambda z: [z[:, h * DK_A:(h + 1) * DK_A] for h in range(HA)]
    qn = jnp.concatenate([_l2norm(x) * (DK_A ** -0.5) for x in heads(qkv[:, 0:MIX_A])], axis=-1)
    kn = jnp.concatenate([_l2norm(x) for x in heads(qkv[:, MIX_A:2 * MIX_A])], axis=-1)
    return qn, kn


def _gdn_out(o, gate, anw):
    on = jnp.concatenate(
        [x * lax.rsqrt(jnp.mean(x * x, axis=-1, keepdims=True) + RMS_EPS)
         for x in [o[:, h * DK_A:(h + 1) * DK_A] for h in range(HA)]], axis=-1)
    return on * anw * _silu(gate)


def _gdn_prompt_kernel(pa_ref, bd_ref, cw_ref, hp_ref, anw_ref, x_ref, mb_ref, wo_ref, fw_ref,
                       y_ref, s_out_ref, c_out_ref, ext_ref, s_ref):
    c = pl.program_id(1)

    nseq, tc = pa_ref.shape[0], pa_ref.shape[1]

    @pl.when(c == 0)
    def _():
        ext_ref[...] = jnp.zeros_like(ext_ref)
        s_ref[...] = jnp.zeros_like(s_ref)

    cw = cw_ref[...]
    conv = []
    for s in range(nseq):
        u = pa_ref[s, :, 0:A_QKV]
        full = jnp.concatenate([ext_ref[s], u], axis=0)
        conv.append(u * cw[3:4, :] + sum(
            pltpu.roll(full, CONV_W - 1 - i, 0)[8:, :] * cw[i:i + 1, :] for i in range(CONV_W - 1)))
        ext_ref[s] = u[tc - 8:, :]
        c_out_ref[s] = ext_ref[s, 8 - (CONV_W - 1):, :]
    qkv = _silu(jnp.concatenate(conv, axis=0))

    qn, kn = _gdn_qk(qkv)
    beta_all, g_all = _gdn_gates(bd_ref[...].reshape(nseq * tc, BD_W), hp_ref[...])
    gcum = jnp.concatenate([_chunk_cumsum(g_all[s * tc:(s + 1) * tc, :]) for s in range(nseq)],
                           axis=0)
    gcum_t = gcum.T

    nrow = nseq * tc
    wide = lambda z, first: jnp.concatenate(
        [jnp.broadcast_to(z[:, first + h:first + h + 1], (nrow, DK_A)) for h in range(HA)], axis=1)
    bw, gw = wide(beta_all, 0), wide(gcum, HA)
    half = _iota2((nrow, PAIR), 1) < CHUNK
    narrow = lambda zw: jnp.concatenate(
        [jnp.where(half, zw[:, 2 * p * DK_A:(2 * p + 1) * DK_A], zw[:, (2 * p + 1) * DK_A:(2 * p + 2) * DK_A])
         for p in range(HA // 2)], axis=1)
    b64, g64 = narrow(bw), narrow(gw)
    v_all = qkv[:, 2 * MIX_A:]
    egw = jnp.exp(gw)
    bek = bw * egw * kn
    bv = bw * v_all
    qg = qn * egw

    rowp = _iota2((CHUNK, PAIR), 0)
    lane = _iota2((CHUNK, PAIR), 1)
    colp = jnp.where(lane >= CHUNK, lane - CHUNK, lane)
    incl = colp <= rowp
    strict = colp < rowp

    nj = nseq * tc // CHUNK
    npair = HA // 2
    jp = [(j, p) for j in range(nj) for p in range(npair)]
    jh = [(j, h) for j in range(nj) for h in range(HA)]
    rows_of = lambda j: slice(j * CHUNK, (j + 1) * CHUNK)
    lanes_of = lambda h: slice(h * DK_A, (h + 1) * DK_A)
    pair64 = lambda p: slice(p * PAIR, (p + 1) * PAIR)
    pairw = lambda p: slice(2 * p * DK_A, 2 * (p + 1) * DK_A)
    glast = [gw[(j + 1) * CHUNK - 1:(j + 1) * CHUNK, :] for j in range(nj)]
    kdec = [kn[rows_of(j), :] * jnp.exp(glast[j] - gw[rows_of(j), :]) for j in range(nj)]
    eglast = [jnp.exp(glast[j]) for j in range(nj)]
    gr = [jnp.concatenate([gcum_t[HA + 2 * p:HA + 2 * p + 1, rows_of(j)],
                           gcum_t[HA + 2 * p + 1:HA + 2 * p + 2, rows_of(j)]], axis=1) for j, p in jp]
    decay = [jnp.where(incl, jnp.exp(jnp.where(incl, g64[rows_of(j), pair64(p)] - b, 0.0)), 0.0)
             for (j, p), b in zip(jp, gr)]
    kq = [_dot(jnp.concatenate([kn[rows_of(j), pairw(p)], qn[rows_of(j), pairw(p)]], axis=0),
               _pair_diag(kn[rows_of(j), pairw(p)]), NT) for j, p in jp]
    low = [jnp.where(strict, b64[rows_of(j), pair64(p)] * d * m[0:CHUNK, :], 0.0)
           for (j, p), d, m in zip(jp, decay, kq)]
    tinv = _tri_inv_pairs(low)
    zw = jnp.zeros((CHUNK, 2 * DK_A), F32)
    wu = [_dot(t, jnp.concatenate(
        [jnp.concatenate([bek[rows_of(j), lanes_of(2 * p)], bv[rows_of(j), lanes_of(2 * p)], zw], axis=1),
         jnp.concatenate([zw, bek[rows_of(j), lanes_of(2 * p + 1)], bv[rows_of(j), lanes_of(2 * p + 1)]],
                         axis=1)], axis=0))
          for t, (j, p) in zip(tinv, jp)]
    qwu_p = [_dot(d * m[CHUNK:, :], _pair_diag(x)) for d, m, x in zip(decay, kq, wu)]
    head_of = lambda z, j, h: z[j * npair + h // 2][:, (h % 2) * 2 * DK_A:(h % 2 + 1) * 2 * DK_A]
    qwu = [head_of(qwu_p, j, h) for j, h in jh]
    kwu = [_dot(kdec[j][:, lanes_of(h)], head_of(wu, j, h), TN) for j, h in jh]
    rd = _iota2((DK_A, DK_A), 0)
    cd = _iota2((DK_A, DK_A), 1)
    lhs = [jnp.concatenate([jnp.where(rd == cd, eglast[j][:, lanes_of(h)], 0.0) - kx[:, 0:DK_A],
                            qg[rows_of(j), lanes_of(h)] - qx[:, 0:DK_A]], axis=0)
           for (j, h), kx, qx in zip(jh, kwu, qwu)]

    o_rows = [None] * nj
    for jj in range(tc // CHUNK):
        for s in range(nseq):
            j = s * (tc // CHUNK) + jj
            idx = [j * HA + h for h in range(HA)]
            res = [_dot(lhs[i], s_ref[s, h]) for h, i in enumerate(idx)]
            for h, i in enumerate(idx):
                s_ref[s, h] = res[h][0:DK_A, :] + kwu[i][:, DK_A:]
            o_rows[j] = jnp.concatenate(
                [res[h][DK_A:, :] + qwu[i][:, DK_A:] for h, i in enumerate(idx)], axis=-1)

    gate = pa_ref[:, :, A_QKV:A_MAIN].reshape(nseq * tc, MIX_A)
    mix_a = _gdn_out(jnp.concatenate(o_rows, axis=0), gate, anw_ref[...])
    h = (x_ref[...].reshape(nseq * tc, D_MODEL)
         + _dot(mb_ref[...].reshape(nseq * tc, MIX_B), wo_ref[MIX_A:, :])
         + _dot(mix_a.astype(BF16), wo_ref[0:MIX_A, :]))
    y = h * lax.rsqrt(jnp.mean(h * h, axis=-1, keepdims=True) + RMS_EPS) * fw_ref[...]
    y_ref[...] = y.reshape(nseq, tc, D_MODEL)
    s_out_ref[...] = s_ref[...]


def _gdn_prompt(pa, bd, conv_w, hp, anw, x, mix_b, w_out, final_w):
    b, t, _ = pa.shape
    blk = lambda i, j: (i, j, 0)
    fixed = lambda i, j: (0, 0)
    return pl.pallas_call(
        _gdn_prompt_kernel,
        grid=(b // NSEQ_A, t // TC_A),
        in_specs=[pl.BlockSpec((NSEQ_A, TC_A, A_MAIN), blk),
                  pl.BlockSpec((NSEQ_A, TC_A, BD_W), blk),
                  pl.BlockSpec((CONV_W, A_QKV), fixed),
                  pl.BlockSpec((8, BD_W), fixed),
                  pl.BlockSpec((1, MIX_A), fixed),
                  pl.BlockSpec((NSEQ_A, TC_A, D_MODEL), blk),
                  pl.BlockSpec((NSEQ_A, TC_A, MIX_B), blk),
                  pl.BlockSpec((D_MODEL, D_MODEL), fixed),
                  pl.BlockSpec((1, D_MODEL), fixed)],
        out_specs=[pl.BlockSpec((NSEQ_A, TC_A, D_MODEL), blk),
                   pl.BlockSpec((NSEQ_A, HA, DK_A, DK_A), lambda i, j: (i, 0, 0, 0)),
                   pl.BlockSpec((NSEQ_A, CONV_W - 1, A_QKV), lambda i, j: (i, 0, 0))],
        out_shape=[jax.ShapeDtypeStruct((b, t, D_MODEL), F32),
                   jax.ShapeDtypeStruct((b, HA, DK_A, DK_A), F32),
                   jax.ShapeDtypeStruct((b, CONV_W - 1, A_QKV), F32)],
        scratch_shapes=[pltpu.VMEM((NSEQ_A, 8, A_QKV), F32),
                        pltpu.VMEM((NSEQ_A, HA, DK_A, DK_A), F32)],
        compiler_params=pltpu.CompilerParams(dimension_semantics=("arbitrary", "arbitrary"),
                                             vmem_limit_bytes=VMEM_LIMIT),
        name="gdn_prompt",
    )(pa, bd, conv_w, hp, anw, x, mix_b, w_out, final_w)


def _rwkv_front(xb, w0, w2, a0, a2, k_k, k_a, ones_blk):
    r = xb[:, 0:MIX_B]
    kr = xb[:, MIX_B:2 * MIX_B]
    vr = xb[:, 2 * MIX_B:3 * MIX_B]
    gate = xb[:, 3 * MIX_B:4 * MIX_B]
    wd = xb[:, 4 * MIX_B:4 * MIX_B + LORA]
    ad = xb[:, 4 * MIX_B + LORA:]
    logw = -(EXP_NEG_HALF * _sigmoid(w0 + _dot(jnp.tanh(wd), w2)))
    a = _sigmoid(a0 + _dot(ad, a2))
    kraw = kr * k_k
    kk = kraw * lax.rsqrt(_group_sum(kraw * kraw, ones_blk) + 1e-12)
    kmod = kr * (1.0 + (a - 1.0) * k_a)
    return r, kmod, vr, gate, logw, a, kk


def _rwkv_back(y, r, kmod, v, gate, rk, lnw, lnb, ones_blk):
    inv = 1.0 / HS_B
    yc = y - _group_sum(y, ones_blk) * inv
    var = _group_sum(yc * yc, ones_blk) * inv
    gn = yc * lax.rsqrt(var + GN_EPS) * lnw + lnb
    bonus = _group_sum(r * kmod * rk, ones_blk) * v
    return (gn + bonus) * _silu(gate)


def _rwkv_prompt_kernel(pb_ref, mu_ref, w0_ref, w2_ref, a0_ref, a2_ref, kk_ref, ka_ref,
                        rk_ref, lnw_ref, lnb_ref, ones_ref,
                        o_ref, s_out_ref, sh_out_ref, carry_ref, s_ref):
    c = pl.program_id(1)

    @pl.when(c == 0)
    def _():
        carry_ref[...] = jnp.zeros_like(carry_ref)
        s_ref[...] = jnp.zeros_like(s_ref)

    ones_blk = ones_ref[...]
    pb = pb_ref[...].reshape(NSEQ * TC, B_W)
    prev = []
    for s in range(NSEQ):
        pb_s = pb_ref[s]
        prev.append(pltpu.roll(jnp.concatenate([carry_ref[s], pb_s], axis=0), 1, 0)[8:, :])
        carry_ref[s] = pb_s[TC - 8:, :]
        sh_out_ref[s] = pb_s[TC - 1:, :]
    prev = jnp.concatenate(prev, axis=0)
    xb = pb + (prev - pb) * mu_ref[...]
    r, kmod, v, gate, logw, a, kk = _rwkv_front(
        xb, w0_ref[...], w2_ref[...], a0_ref[...], a2_ref[...], kk_ref[...], ka_ref[...], ones_blk)

    g = jnp.concatenate([_chunk_cumsum(logw[s * TC:(s + 1) * TC, :]) for s in range(NSEQ)],
                        axis=0)
    eg = jnp.exp(g)
    egn = jnp.exp(-g)
    ag = -kk * jnp.exp(g - logw)
    kka = kk * a
    bg = kka * egn
    kg = kmod * egn
    rg = r * eg

    rowp = _iota2((CHUNK, PAIR), 0)
    lane = _iota2((CHUNK, PAIR), 1)
    colp = jnp.where(lane >= HS_B, lane - HS_B, lane)
    left = lane < HS_B
    incl = colp <= rowp
    strict = colp < rowp
    eye = colp == rowp
    zero = jnp.zeros((CHUNK, PAIR), F32)
    pack = lambda f: jnp.where(left, f[0:HS_B, :], f[HS_B:, :])

    nj = NSEQ * TC // CHUNK
    npair = HB // 2
    jp = [(j, p) for j in range(nj) for p in range(npair)]
    rows_of = lambda j: slice(j * CHUNK, (j + 1) * CHUNK)
    lanes_of = lambda p: slice(p * PAIR, (p + 1) * PAIR)
    glast = [g[(j + 1) * CHUNK - 1:(j + 1) * CHUNK, :] for j in range(nj)]
    eglast = [jnp.exp(glast[j]) for j in range(nj)]
    bdec = [bg[rows_of(j), :] * eglast[j] for j in range(nj)]
    kdec = [kg[rows_of(j), :] * eglast[j] for j in range(nj)]

    cut = lambda z: [z[rows_of(j), lanes_of(p)] for j, p in jp]
    ag_p, rg_p, bg_p, kg_p, v_p = cut(ag), cut(rg), cut(bg), cut(kg), cut(v)
    bdec_p = [bdec[j][:, lanes_of(p)] for j, p in jp]
    kdec_p = [kdec[j][:, lanes_of(p)] for j, p in jp]
    eglast_p = [eglast[j][:, lanes_of(p)] for j, p in jp]
    amat = [_dot(jnp.concatenate([a, b], axis=0),
                 jnp.concatenate([_pair_diag(c_), _pair_diag(d)], axis=0), NT)
            for a, b, c_, d in zip(ag_p, rg_p, bg_p, kg_p)]
    a_ab = [jnp.where(strict, m[0:CHUNK, 0:PAIR], 0.0) for m in amat]
    a_ak = [jnp.where(strict, m[0:CHUNK, PAIR:], 0.0) for m in amat]
    a_rb = [jnp.where(incl, m[CHUNK:, 0:PAIR], 0.0) for m in amat]
    a_rk = [jnp.where(incl, m[CHUNK:, PAIR:], 0.0) for m in amat]
    tinv = _tri_inv_pairs([-a for a in a_ab])
    kv = [_dot(jnp.concatenate([a, b], axis=0), _pair_diag(c_)) for a, b, c_ in zip(a_ak, a_rk, v_p)]
    wu = [_dot(t, jnp.concatenate([_pair_diag(a), _pair_diag(x[0:CHUNK, :])], axis=1))
          for t, a, x in zip(tinv, ag_p, kv)]
    rwu = [_dot(a, jnp.concatenate([_pair_diag(x[:, 0:PAIR]), _pair_diag(x[:, PAIR:])], axis=1))
           for a, x in zip(a_rb, wu)]
    mn = [_dot(jnp.concatenate([a, b], axis=0),
               jnp.concatenate([x, jnp.concatenate([zero, c_], axis=1)], axis=0), TN)
          for a, b, x, c_ in zip(bdec_p, kdec_p, wu, v_p)]
    lhs = [jnp.concatenate([jnp.where(eye, e, 0.0) + pack(m[:, 0:PAIR]), a + x[:, 0:PAIR]], axis=0)
           for e, m, a, x in zip(eglast_p, mn, rg_p, rwu)]

    y_rows = [None] * nj
    for jj in range(TC // CHUNK):
        for s in range(NSEQ):
            j = s * (TC // CHUNK) + jj
            idx = [j * npair + p for p in range(npair)]
            res = [_dot(lhs[i], _pair_diag(s_ref[s, p])) for p, i in enumerate(idx)]
            for p, i in enumerate(idx):
                s_ref[s, p] = res[p][0:HS_B, :] + pack(mn[i][:, PAIR:])
            y_rows[j] = jnp.concatenate(
                [res[p][HS_B:, :] + rwu[i][:, PAIR:] + kv[i][CHUNK:, :] for p, i in enumerate(idx)],
                axis=-1)

    out = _rwkv_back(jnp.concatenate(y_rows, axis=0), r, kmod, v, gate,
                     rk_ref[...], lnw_ref[...], lnb_ref[...], ones_blk)
    o_ref[...] = out.reshape(NSEQ, TC, MIX_B).astype(o_ref.dtype)
    for s in range(NSEQ):
        for p in range(npair):
            st = s_ref[s, p].T
            s_out_ref[s, 2 * p] = st[0:HS_B, :]
            s_out_ref[s, 2 * p + 1] = st[HS_B:, :]


def _rwkv_prompt(pb, mu, w0, w2, a0, a2, k_k, k_a, rk, lnw, lnb, ones_blk):
    b, t, _ = pb.shape
    blk = lambda i, j: (i, j, 0)
    fixed = lambda i, j: (0, 0)
    vec = pl.BlockSpec((1, MIX_B), fixed)
    return pl.pallas_call(
        _rwkv_prompt_kernel,
        grid=(b // NSEQ, t // TC),
        in_specs=[pl.BlockSpec((NSEQ, TC, B_W), blk),
                  pl.BlockSpec((1, B_W), fixed),
                  vec, pl.BlockSpec((LORA, MIX_B), fixed),
                  vec, pl.BlockSpec((LORA, MIX_B), fixed),
                  vec, vec, vec, vec, vec,
                  pl.BlockSpec((MIX_B, MIX_B), fixed)],
        out_specs=[pl.BlockSpec((NSEQ, TC, MIX_B), blk),
                   pl.BlockSpec((NSEQ, HB, HS_B, HS_B), lambda i, j: (i, 0, 0, 0)),
                   pl.BlockSpec((NSEQ, 1, B_W), lambda i, j: (i, 0, 0))],
        out_shape=[jax.ShapeDtypeStruct((b, t, MIX_B), BF16),
                   jax.ShapeDtypeStruct((b, HB, HS_B, HS_B), F32),
                   jax.ShapeDtypeStruct((b, 1, B_W), F32)],
        scratch_shapes=[pltpu.VMEM((NSEQ, 8, B_W), F32),
                        pltpu.VMEM((NSEQ, HB // 2, HS_B, PAIR), F32)],
        compiler_params=pltpu.CompilerParams(dimension_semantics=("arbitrary", "arbitrary"),
                                             vmem_limit_bytes=VMEM_LIMIT),
        name="rwkv_prompt",
    )(pb, mu, w0, w2, a0, a2, k_k, k_a, rk, lnw, lnb, ones_blk)


def _pad_rows(rows, width):
    return jnp.concatenate(rows + [jnp.zeros((8 - len(rows), width), F32)], axis=0)


def _gdn_step_kernel(pa_ref, bd_ref, sc_ref, s_ref, cw_ref, hp_ref, anw_ref,
                     o_ref, s_out_ref, c_out_ref):
    u = pa_ref[:, 0:A_QKV]
    cw = cw_ref[...]
    conv = (sc_ref[0] * cw[0:1, :] + sc_ref[1] * cw[1:2, :] + sc_ref[2] * cw[2:3, :] + u * cw[3:4, :])
    qkv = _silu(conv)
    c_out_ref[0] = sc_ref[1]
    c_out_ref[1] = sc_ref[2]
    c_out_ref[2] = u

    qn, kn = _gdn_qk(qkv)
    beta_all, g_all = _gdn_gates(bd_ref[...], hp_ref[...])
    eg_all = jnp.exp(g_all)
    lanes_of = lambda h: slice(h * DK_A, (h + 1) * DK_A)
    q = [qn[:, lanes_of(h)] for h in range(HA)]
    k = [kn[:, lanes_of(h)] for h in range(HA)]
    v = [qkv[:, 2 * MIX_A + h * DK_A:2 * MIX_A + (h + 1) * DK_A] for h in range(HA)]
    beta = [beta_all[:, h:h + 1] for h in range(HA)]
    eg = [eg_all[:, HA + h:HA + h + 1] for h in range(HA)]

    bh = [(b, h) for b in range(SB) for h in range(HA)]
    s0 = [s_ref[b, h] for b, h in bh]
    ks_qs = [_dot(_pad_rows([k[h][b:b + 1, :], q[h][b:b + 1, :]], DK_A), s) for (b, h), s in zip(bh, s0)]
    o_heads = []
    for h in range(HA):
        ks = jnp.concatenate([ks_qs[b * HA + h][0:1, :] for b in range(SB)], axis=0)
        qs = jnp.concatenate([ks_qs[b * HA + h][1:2, :] for b in range(SB)], axis=0)
        u2 = beta[h] * (v[h] - eg[h] * ks)
        for b in range(SB):
            s_out_ref[b, h] = (eg[h][b:b + 1, :] * s0[b * HA + h]
                               + _dot(_pad_rows([k[h][b:b + 1, :]], DK_A),
                                      _pad_rows([u2[b:b + 1, :]], DK_A), TN))
        o_heads.append(eg[h] * qs + jnp.sum(q[h] * k[h], axis=-1, keepdims=True) * u2)

    o_ref[...] = _gdn_out(jnp.concatenate(o_heads, axis=-1), pa_ref[:, A_QKV:A_MAIN], anw_ref[...])


def _gdn_step_specs(n):
    row = lambda i: (i, 0)
    fixed = lambda i: (0, 0)
    state = pl.BlockSpec((SB, HA, DK_A, DK_A), lambda i: (i, 0, 0, 0))
    conv = pl.BlockSpec((CONV_W - 1, SB, A_QKV), lambda i: (0, i, 0))
    in_specs = [pl.BlockSpec((SB, A_MAIN), row),
                pl.BlockSpec((SB, BD_W), row),
                conv, state,
                pl.BlockSpec((CONV_W, A_QKV), fixed),
                pl.BlockSpec((8, BD_W), fixed),
                pl.BlockSpec((1, MIX_A), fixed)]
    out_specs = [pl.BlockSpec((SB, MIX_A), row), state, conv]
    out_shape = [jax.ShapeDtypeStruct((n, MIX_A), F32),
                 jax.ShapeDtypeStruct((n, HA, DK_A, DK_A), F32),
                 jax.ShapeDtypeStruct((CONV_W - 1, n, A_QKV), F32)]
    return in_specs, out_specs, out_shape


def _rwkv_step_lanes_kernel(pb_ref, sh_ref, s_ref, mu_ref, w0_ref, w2_ref, a0_ref, a2_ref, kk_ref, ka_ref,
                            rk_ref, lnw_ref, lnb_ref, ones_ref,
                            o_ref, s_out_ref, vec_scr, row_scr, y_scr):
    h = pl.program_id(0)
    ones_blk = ones_ref[...]

    @pl.when(h == 0)
    def _():
        pb = pb_ref[...]
        xb = pb + (sh_ref[...] - pb) * mu_ref[...]
        r, kmod, v, gate, logw, a, kk = _rwkv_front(
            xb, w0_ref[...], w2_ref[...], a0_ref[...], a2_ref[...], kk_ref[...], ka_ref[...], ones_blk)
        for slot, z in enumerate((-kk, kk * a, kmod, v, r, jnp.exp(logw))):
            vec_scr[slot] = z.T
        for slot, z in enumerate((r, kmod, v, gate)):
            row_scr[slot] = z

    base = pl.multiple_of(h * HS_B, HS_B)
    chan = pl.ds(base, HS_B)
    nkk = vec_scr[0, chan, :]
    kka = vec_scr[1, chan, :]
    kmod_t = vec_scr[2, chan, :]
    r_t = vec_scr[4, chan, :]
    wdec = vec_scr[5, chan, :]

    def body(vi, carry):
        s0 = s_ref[0, vi]
        sa = jnp.sum(s0 * nkk, axis=0, keepdims=True)
        s1 = s0 * wdec + sa * kka + vec_scr[3, pl.ds(base + vi, 1), :] * kmod_t
        s_out_ref[0, vi] = s1
        y_scr[pl.ds(base + vi, 1), :] = jnp.sum(s1 * r_t, axis=0, keepdims=True)
        return carry

    lax.fori_loop(0, HS_B, body, 0, unroll=4)

    @pl.when(h == HB - 1)
    def _():
        o_ref[...] = _rwkv_back(y_scr[...].T, row_scr[0], row_scr[1], row_scr[2], row_scr[3],
                                rk_ref[...], lnw_ref[...], lnb_ref[...], ones_blk)


def _rwkv_step_specs(n):
    fixed = lambda i: (0, 0)
    vec = pl.BlockSpec((1, MIX_B), fixed)
    state = pl.BlockSpec((1, HS_B, HS_B, n), lambda i: (i, 0, 0, 0))
    in_specs = [pl.BlockSpec((n, B_W), fixed),
                pl.BlockSpec((n, B_W), fixed),
                state,
                pl.BlockSpec((1, B_W), fixed),
                vec, pl.BlockSpec((LORA, MIX_B), fixed),
                vec, pl.BlockSpec((LORA, MIX_B), fixed),
                vec, vec, vec, vec, vec,
                pl.BlockSpec((MIX_B, MIX_B), fixed)]
    out_specs = [pl.BlockSpec((n, MIX_B), fixed), state]
    out_shape = [jax.ShapeDtypeStruct((n, MIX_B), F32),
                 jax.ShapeDtypeStruct((HB, HS_B, HS_B, n), F32)]
    return in_specs, out_specs, out_shape


def _sample_step(a_args, b_args):
    n = a_args[0].shape[0]
    assert n // SB == HB
    a_in, a_out, a_shape = _gdn_step_specs(n)
    b_in, b_out, b_shape = _rwkv_step_specs(n)
    na, nb = len(a_in), len(b_in)

    def body(*refs):
        ins, outs, scratch = refs[:na + nb], refs[na + nb:na + nb + len(a_out) + len(b_out)], refs[-3:]
        _gdn_step_kernel(*ins[:na], *outs[:len(a_out)])
        _rwkv_step_lanes_kernel(*ins[na:], *outs[len(a_out):], *scratch)

    return pl.pallas_call(
        body,
        grid=(HB,),
        in_specs=a_in + b_in,
        out_specs=a_out + b_out,
        out_shape=a_shape + b_shape,
        scratch_shapes=[pltpu.VMEM((6, MIX_B, n), F32),
                        pltpu.VMEM((4, n, MIX_B), F32),
                        pltpu.VMEM((MIX_B, n), F32)],
        compiler_params=pltpu.CompilerParams(dimension_semantics=("arbitrary",),
                                             vmem_limit_bytes=VMEM_LIMIT),
        name="sample_step",
    )(*a_args, *b_args)


def _block_ones(width, group):
    idx = jnp.arange(width) // group
    return (idx[:, None] == idx[None, :]).astype(BF16)


def kernel(x_prompt, x_sample, state_a_mat, state_a_conv, state_b_mat, state_b_shift, norm_w, w_in,
           conv_w, a_log, dt_bias, a_norm_w, mu, w0, w2, a0, a2, k_k, k_a, r_k, ln_w, ln_b, w_out,
           final_norm_w):
    bsz, seq, _ = x_prompt.shape
    nsmp = x_sample.shape[0]
    assert bsz % NSEQ == 0 and bsz % NSEQ_A == 0 and seq % TC == 0 and seq % TC_A == 0
    assert nsmp % SB == 0 and x_sample.shape[1] == 1

    w_all = jnp.transpose(w_in[0])
    w_o = w_out[0].astype(BF16)
    hp = jnp.pad(jnp.concatenate([a_log, dt_bias], axis=0), ((0, 6), (HA, BD_W - 2 * HA)))
    nw = norm_w[0].reshape(1, D_MODEL)
    fw = final_norm_w.reshape(1, D_MODEL)
    anw = jnp.tile(a_norm_w[0].reshape(1, DK_A), (1, HA))
    ones_b = _block_ones(MIX_B, HS_B)
    row = lambda z: z[0].reshape(1, -1)
    b_params = (row(mu), row(w0), w2[0], row(a0), a2[0], row(k_k), row(k_a), row(r_k), row(ln_w), row(ln_b),
                ones_b)

    xp = x_prompt.reshape(bsz * seq, D_MODEL)
    xs = x_sample.reshape(nsmp * D_MODEL // LANE, LANE)
    pa, pb, bd, sa_, sb_, sbd = _inproj(xp, xs, nw, w_all)

    ob, p_bmat, p_bshift = _rwkv_prompt(pb.reshape(bsz, seq, B_W), *b_params)
    y_prompt, p_amat, p_aconv = _gdn_prompt(pa.reshape(bsz, seq, A_MAIN), bd.reshape(bsz, seq, BD_W),
                                            conv_w[0], hp, anw, x_prompt, ob, w_o, fw)

    sconv_t = jnp.swapaxes(state_a_conv[0], 0, 1)
    soa, s_amat, s_aconv_t, sob, s_bmat_t = _sample_step(
        (sa_, sbd, sconv_t, state_a_mat[0], conv_w[0], hp, anw),
        (sb_, state_b_shift[0], jnp.transpose(state_b_mat[0], (1, 2, 3, 0))) + b_params)
    s_bmat = jnp.transpose(s_bmat_t, (3, 0, 1, 2))
    y_sample = _outproj(xs, soa, sob, w_o, fw)

    return (y_prompt, y_sample.reshape(nsmp, 1, D_MODEL),
            p_amat[None], p_aconv[None], p_bmat[None], p_bshift.reshape(1, bsz, B_W),
            s_amat[None], jnp.swapaxes(s_aconv_t, 0, 1)[None], s_bmat[None], sb_[None])
```

```python
import jax
import jax.numpy as jnp
from jax import lax
from jax.experimental import pallas as pl
from jax.experimental.pallas import tpu as pltpu

F32 = jnp.float32
BF16 = jnp.bfloat16

D_MODEL = 1024
MIX_A = 512
MIX_B = 512
HA = 4
DK_A = 128
HB = 8
HS_B = 64
LANE = 128
PAIR = 2 * HS_B
CONV_W = 4
LORA = 64
A_QKV = 3 * MIX_A
A_MAIN = A_QKV + MIX_A
B_W = 4 * MIX_B + 2 * LORA
BD_W = 128
RMS_EPS = 1e-6
GN_EPS = 64e-5
CHUNK = 64
TC = 256
TC_A = 256
NSEQ_A = 2
NSEQ = 4
SB = 16
VMEM_LIMIT = 56 * 1024 * 1024

EXP_NEG_HALF = 0.6065306597126334
NN = (((1,), (0,)), ((), ()))
NT = (((1,), (1,)), ((), ()))
TN = (((0,), (0,)), ((), ()))


def _dot(a, b, dn=NN):
    return lax.dot_general(a, b, dn, preferred_element_type=F32)


def _sigmoid(x):
    return 0.5 * jnp.tanh(0.5 * x) + 0.5


def _silu(x):
    h = 0.5 * x
    return h * jnp.tanh(h) + h


def _softplus(x):
    return jnp.maximum(x, 0.0) + jnp.log(1.0 + jnp.exp(-jnp.abs(x)))


def _l2norm(x):
    return x * lax.rsqrt(jnp.sum(x * x, axis=-1, keepdims=True) + 1e-12)


def _group_sum(x, ones_blk):
    return _dot(x.astype(BF16), ones_blk)


def _iota2(shape, dim):
    return lax.broadcasted_iota(jnp.int32, shape, dim)


def _chunk_cumsum(x):
    n = x.shape[0]
    r = _iota2((n, n), 0)
    c = _iota2((n, n), 1)
    tril = jnp.where(((r // CHUNK) == (c // CHUNK)) & (c <= r), 1.0, 0.0).astype(BF16)
    hi = x.astype(BF16)
    rest = x - hi.astype(F32)
    mid = rest.astype(BF16)
    lo = (rest - mid.astype(F32)).astype(BF16)
    return _dot(tril, hi) + _dot(tril, mid) + _dot(tril, lo)


def _tri_inv_many(lows):
    n = lows[0].shape[0]
    r = _iota2((n, n), 0)
    c = _iota2((n, n), 1)
    eye = jnp.where(r == c, 1.0, 0.0).astype(F32)
    base = (r // 8) == (c // 8)
    l0 = [jnp.where(base, low, 0.0) for low in lows]
    l2 = [_dot(a, a) for a in l0]
    l4 = [_dot(a, a) for a in l2]
    d = [_dot(eye - a, eye + b) for a, b in zip(l0, l2)]
    d = [_dot(a, eye + b) for a, b in zip(d, l4)]
    s = 8
    while s < n:
        sel = ((r // (2 * s)) == (c // (2 * s))) & ((r // s) != (c // s))
        t = [_dot(jnp.where(sel, low, 0.0), a) for low, a in zip(lows, d)]
        d = [a - _dot(a, b) for a, b in zip(d, t)]
        s *= 2
    return d


def _pair_diag(x):
    left = _iota2(x.shape, 1) < x.shape[1] // 2
    return jnp.concatenate([jnp.where(left, x, 0.0), jnp.where(left, 0.0, x)], axis=0)


def _tri_inv_pairs(lows):
    n = lows[0].shape[0]
    r = _iota2((n, 2 * n), 0)
    lane = _iota2((n, 2 * n), 1)
    c = jnp.where(lane >= n, lane - n, lane)
    mul = lambda a, b: _dot(a, _pair_diag(b))
    eye = jnp.where(r == c, 1.0, 0.0).astype(F32)
    base = (r // 8) == (c // 8)
    l0 = [jnp.where(base, low, 0.0) for low in lows]
    l2 = [mul(a, a) for a in l0]
    l4 = [mul(a, a) for a in l2]
    d = [mul(eye - a, eye + b) for a, b in zip(l0, l2)]
    d = [mul(a, eye + b) for a, b in zip(d, l4)]
    s = 8
    while s < n:
        sel = ((r // (2 * s)) == (c // (2 * s))) & ((r // s) != (c // s))
        t = [mul(jnp.where(sel, low, 0.0), a) for low, a in zip(lows, d)]
        d = [a - mul(a, b) for a, b in zip(d, t)]
        s *= 2
    return d


def _sample_rows(ref, n):
    pieces = D_MODEL // LANE
    return jnp.concatenate([ref[pl.ds(c, n, stride=pieces), :] for c in range(pieces)], axis=1)


def _inproj_kernel(x_ref, xs_ref, nw_ref, wt_ref, oa_ref, ob_ref, obd_ref, sa_ref, sb_ref, sbd_ref, w_scr):
    a_w = A_MAIN + 2 * HA

    def project(x, pa_ref, pb_ref, pbd_ref):
        xn = (x * lax.rsqrt(jnp.mean(x * x, axis=-1, keepdims=True) + RMS_EPS) * nw_ref[...]).astype(BF16)
        pa_ref[...] = _dot(xn, w_scr[0:A_MAIN, :], NT)
        pb = _dot(xn, w_scr[A_MAIN:, :], NT)
        pb_ref[...] = pb[:, :B_W]
        pbd_ref[...] = pb[:, B_W:]

    @pl.when(pl.program_id(0) == 0)
    def _():
        w_scr[0:A_MAIN, :] = wt_ref[0:A_MAIN, :].astype(BF16)
        w_scr[A_MAIN:A_MAIN + B_W, :] = wt_ref[a_w:, :].astype(BF16)
        w_scr[A_MAIN + B_W:, :] = jnp.concatenate(
            [wt_ref[A_MAIN:a_w, :], jnp.zeros((BD_W - 2 * HA, D_MODEL), F32)], axis=0).astype(BF16)
        project(_sample_rows(xs_ref, sa_ref.shape[0]), sa_ref, sb_ref, sbd_ref)

    project(x_ref[...], oa_ref, ob_ref, obd_ref)


def _inproj(x2d, xs2d, norm_w, w_all):
    n, ns = x2d.shape[0], xs2d.shape[0] * LANE // D_MODEL
    tm = min(512, n)
    row = lambda i: (i, 0)
    fixed = lambda i: (0, 0)
    widths = (A_MAIN, B_W, BD_W)
    return pl.pallas_call(
        _inproj_kernel,
        grid=(n // tm,),
        in_specs=[pl.BlockSpec((tm, D_MODEL), row),
                  pl.BlockSpec(xs2d.shape, fixed),
                  pl.BlockSpec((1, D_MODEL), fixed),
                  pl.BlockSpec(w_all.shape, fixed)],
        out_specs=[pl.BlockSpec((tm, w), row) for w in widths] + [pl.BlockSpec((ns, w), fixed) for w in widths],
        out_shape=[jax.ShapeDtypeStruct((n, w), F32) for w in widths]
                  + [jax.ShapeDtypeStruct((ns, w), F32) for w in widths],
        scratch_shapes=[pltpu.VMEM((A_MAIN + B_W + BD_W, D_MODEL), BF16)],
        compiler_params=pltpu.CompilerParams(dimension_semantics=("arbitrary",),
                                             vmem_limit_bytes=VMEM_LIMIT),
        name="inproj",
    )(x2d, xs2d, norm_w, w_all)


def _outproj_kernel(x_ref, ma_ref, mb_ref, w_ref, fw_ref, y_ref):
    n = ma_ref.shape[0]
    h = (_sample_rows(x_ref, n)
         + _dot(ma_ref[...].astype(BF16), w_ref[0:MIX_A, :])
         + _dot(mb_ref[...].astype(BF16), w_ref[MIX_A:, :]))
    y = h * lax.rsqrt(jnp.mean(h * h, axis=-1, keepdims=True) + RMS_EPS) * fw_ref[...]
    pieces = D_MODEL // LANE
    for c in range(pieces):
        y_ref[pl.ds(c, n, stride=pieces), :] = y[:, c * LANE:(c + 1) * LANE]


def _gdn_gates(bd, hp):
    beta = _sigmoid(bd)
    g = -jnp.exp(hp[0:1, :]) * _softplus(bd + hp[1:2, :])
    return beta, g


def _gdn_qk(qkv):
    heads = lambda z: [z[:, h * DK_A:(h + 1) * DK_A] for h in range(HA)]
    qn = jnp.concatenate([_l2norm(x) * (DK_A ** -0.5) for x in heads(qkv[:, 0:MIX_A])], axis=-1)
    kn = jnp.concatenate([_l2norm(x) for x in heads(qkv[:, MIX_A:2 * MIX_A])], axis=-1)
    return qn, kn


def _gdn_out(o, gate, anw):
    on = jnp.concatenate(
        [x * lax.rsqrt(jnp.mean(x * x, axis=-1, keepdims=True) + RMS_EPS)
         for x in [o[:, h * DK_A:(h + 1) * DK_A] for h in range(HA)]], axis=-1)
    return on * anw * _silu(gate)


def _gdn_prompt_kernel(pa_ref, bd_ref, cw_ref, hp_ref, anw_ref, x_ref, mb_ref, wo_ref, fw_ref,
                       y_ref, s_out_ref, c_out_ref, ext_ref, s_ref):
    c = pl.program_id(1)

    nseq, tc = pa_ref.shape[0], pa_ref.shape[1]

    @pl.when(c == 0)
    def _():
        ext_ref[...] = jnp.zeros_like(ext_ref)
        s_ref[...] = jnp.zeros_like(s_ref)

    cw = cw_ref[...]
    conv = []
    for s in range(nseq):
        u = pa_ref[s, :, 0:A_QKV]
        full = jnp.concatenate([ext_ref[s], u], axis=0)
        conv.append(u * cw[3:4, :] + sum(
            pltpu.roll(full, CONV_W - 1 - i, 0)[8:, :] * cw[i:i + 1, :] for i in range(CONV_W - 1)))
        ext_ref[s] = u[tc - 8:, :]
        c_out_ref[s] = ext_ref[s, 8 - (CONV_W - 1):, :]
    qkv = _silu(jnp.concatenate(conv, axis=0))

    qn, kn = _gdn_qk(qkv)
    beta_all, g_all = _gdn_gates(bd_ref[...].reshape(nseq * tc, BD_W), hp_ref[...])
    gcum = jnp.concatenate([_chunk_cumsum(g_all[s * tc:(s + 1) * tc, :]) for s in range(nseq)],
                           axis=0)
    gcum_t = gcum.T

    nrow = nseq * tc
    wide = lambda z, first: jnp.concatenate(
        [jnp.broadcast_to(z[:, first + h:first + h + 1], (nrow, DK_A)) for h in range(HA)], axis=1)
    bw, gw = wide(beta_all, 0), wide(gcum, HA)
    half = _iota2((nrow, PAIR), 1) < CHUNK
    narrow = lambda zw: jnp.concatenate(
        [jnp.where(half, zw[:, 2 * p * DK_A:(2 * p + 1) * DK_A], zw[:, (2 * p + 1) * DK_A:(2 * p + 2) * DK_A])
         for p in range(HA // 2)], axis=1)
    b64, g64 = narrow(bw), narrow(gw)
    v_all = qkv[:, 2 * MIX_A:]
    egw = jnp.exp(gw)
    bek = bw * egw * kn
    bv = bw * v_all
    qg = qn * egw

    rowp = _iota2((CHUNK, PAIR), 0)
    lane = _iota2((CHUNK, PAIR), 1)
    colp = jnp.where(lane >= CHUNK, lane - CHUNK, lane)
    incl = colp <= rowp
    strict = colp < rowp

    nj = nseq * tc // CHUNK
    npair = HA // 2
    jp = [(j, p) for j in range(nj) for p in range(npair)]
    jh = [(j, h) for j in range(nj) for h in range(HA)]
    rows_of = lambda j: slice(j * CHUNK, (j + 1) * CHUNK)
    lanes_of = lambda h: slice(h * DK_A, (h + 1) * DK_A)
    pair64 = lambda p: slice(p * PAIR, (p + 1) * PAIR)
    pairw = lambda p: slice(2 * p * DK_A, 2 * (p + 1) * DK_A)
    glast = [gw[(j + 1) * CHUNK - 1:(j + 1) * CHUNK, :] for j in range(nj)]
    kdec = [kn[rows_of(j), :] * jnp.exp(glast[j] - gw[rows_of(j), :]) for j in range(nj)]
    eglast = [jnp.exp(glast[j]) for j in range(nj)]
    gr = [jnp.concatenate([gcum_t[HA + 2 * p:HA + 2 * p + 1, rows_of(j)],
                           gcum_t[HA + 2 * p + 1:HA + 2 * p + 2, rows_of(j)]], axis=1) for j, p in jp]
    decay = [jnp.where(incl, jnp.exp(jnp.where(incl, g64[rows_of(j), pair64(p)] - b, 0.0)), 0.0)
             for (j, p), b in zip(jp, gr)]
    kq = [_dot(jnp.concatenate([kn[rows_of(j), pairw(p)], qn[rows_of(j), pairw(p)]], axis=0),
               _pair_diag(kn[rows_of(j), pairw(p)]), NT) for j, p in jp]
    low = [jnp.where(strict, b64[rows_of(j), pair64(p)] * d * m[0:CHUNK, :], 0.0)
           for (j, p), d, m in zip(jp, decay, kq)]
    tinv = _tri_inv_pairs(low)
    zw = jnp.zeros((CHUNK, 2 * DK_A), F32)
    wu = [_dot(t, jnp.concatenate(
        [jnp.concatenate([bek[rows_of(j), lanes_of(2 * p)], bv[rows_of(j), lanes_of(2 * p)], zw], axis=1),
         jnp.concatenate([zw, bek[rows_of(j), lanes_of(2 * p + 1)], bv[rows_of(j), lanes_of(2 * p + 1)]],
                         axis=1)], axis=0))
          for t, (j, p) in zip(tinv, jp)]
    qwu_p = [_dot(d * m[CHUNK:, :], _pair_diag(x)) for d, m, x in zip(decay, kq, wu)]
    head_of = lambda z, j, h: z[j * npair + h // 2][:, (h % 2) * 2 * DK_A:(h % 2 + 1) * 2 * DK_A]
    qwu = [head_of(qwu_p, j, h) for j, h in jh]
    kwu = [_dot(kdec[j][:, lanes_of(h)], head_of(wu, j, h), TN) for j, h in jh]
    rd = _iota2((DK_A, DK_A), 0)
    cd = _iota2((DK_A, DK_A), 1)
    lhs = [jnp.concatenate([jnp.where(rd == cd, eglast[j][:, lanes_of(h)], 0.0) - kx[:, 0:DK_A],
                            qg[rows_of(j), lanes_of(h)] - qx[:, 0:DK_A]], axis=0)
           for (j, h), kx, qx in zip(jh, kwu, qwu)]

    o_rows = [None] * nj
    for jj in range(tc // CHUNK):
        for s in range(nseq):
            j = s * (tc // CHUNK) + jj
            idx = [j * HA + h for h in range(HA)]
            res = [_dot(lhs[i], s_ref[s, h]) for h, i in enumerate(idx)]
            for h, i in enumerate(idx):
                s_ref[s, h] = res[h][0:DK_A, :] + kwu[i][:, DK_A:]
            o_rows[j] = jnp.concatenate(
                [res[h][DK_A:, :] + qwu[i][:, DK_A:] for h, i in enumerate(idx)], axis=-1)

    gate = pa_ref[:, :, A_QKV:A_MAIN].reshape(nseq * tc, MIX_A)
    mix_a = _gdn_out(jnp.concatenate(o_rows, axis=0), gate, anw_ref[...])
    h = (x_ref[...].reshape(nseq * tc, D_MODEL)
         + _dot(mb_ref[...].reshape(nseq * tc, MIX_B), wo_ref[MIX_A:, :])
         + _dot(mix_a.astype(BF16), wo_ref[0:MIX_A, :]))
    y = h * lax.rsqrt(jnp.mean(h * h, axis=-1, keepdims=True) + RMS_EPS) * fw_ref[...]
    y_ref[...] = y.reshape(nseq, tc, D_MODEL)
    s_out_ref[...] = s_ref[...]


def _gdn_prompt(pa, bd, conv_w, hp, anw, x, mix_b, w_out, final_w):
    b, t, _ = pa.shape
    blk = lambda i, j: (i, j, 0)
    fixed = lambda i, j: (0, 0)
    return pl.pallas_call(
        _gdn_prompt_kernel,
        grid=(b // NSEQ_A, t // TC_A),
        in_specs=[pl.BlockSpec((NSEQ_A, TC_A, A_MAIN), blk),
                  pl.BlockSpec((NSEQ_A, TC_A, BD_W), blk),
                  pl.BlockSpec((CONV_W, A_QKV), fixed),
                  pl.BlockSpec((8, BD_W), fixed),
                  pl.BlockSpec((1, MIX_A), fixed),
                  pl.BlockSpec((NSEQ_A, TC_A, D_MODEL), blk),
                  pl.BlockSpec((NSEQ_A, TC_A, MIX_B), blk),
                  pl.BlockSpec((D_MODEL, D_MODEL), fixed),
                  pl.BlockSpec((1, D_MODEL), fixed)],
        out_specs=[pl.BlockSpec((NSEQ_A, TC_A, D_MODEL), blk),
                   pl.BlockSpec((NSEQ_A, HA, DK_A, DK_A), lambda i, j: (i, 0, 0, 0)),
                   pl.BlockSpec((NSEQ_A, CONV_W - 1, A_QKV), lambda i, j: (i, 0, 0))],
        out_shape=[jax.ShapeDtypeStruct((b, t, D_MODEL), F32),
                   jax.ShapeDtypeStruct((b, HA, DK_A, DK_A), F32),
                   jax.ShapeDtypeStruct((b, CONV_W - 1, A_QKV), F32)],
        scratch_shapes=[pltpu.VMEM((NSEQ_A, 8, A_QKV), F32),
                        pltpu.VMEM((NSEQ_A, HA, DK_A, DK_A), F32)],
        compiler_params=pltpu.CompilerParams(dimension_semantics=("arbitrary", "arbitrary"),
                                             vmem_limit_bytes=VMEM_LIMIT),
        name="gdn_prompt",
    )(pa, bd, conv_w, hp, anw, x, mix_b, w_out, final_w)


def _rwkv_front(xb, w0, w2, a0, a2, k_k, k_a, ones_blk):
    r = xb[:, 0:MIX_B]
    kr = xb[:, MIX_B:2 * MIX_B]
    vr = xb[:, 2 * MIX_B:3 * MIX_B]
    gate = xb[:, 3 * MIX_B:4 * MIX_B]
    wd = xb[:, 4 * MIX_B:4 * MIX_B + LORA]
    ad = xb[:, 4 * MIX_B + LORA:]
    logw = -(EXP_NEG_HALF * _sigmoid(w0 + _dot(jnp.tanh(wd), w2)))
    a = _sigmoid(a0 + _dot(ad, a2))
    kraw = kr * k_k
    kk = kraw * lax.rsqrt(_group_sum(kraw * kraw, ones_blk) + 1e-12)
    kmod = kr * (1.0 + (a - 1.0) * k_a)
    return r, kmod, vr, gate, logw, a, kk


def _rwkv_back(y, r, kmod, v, gate, rk, lnw, lnb, ones_blk):
    inv = 1.0 / HS_B
    yc = y - _group_sum(y, ones_blk) * inv
    var = _group_sum(yc * yc, ones_blk) * inv
    gn = yc * lax.rsqrt(var + GN_EPS) * lnw + lnb
    bonus = _group_sum(r * kmod * rk, ones_blk) * v
    return (gn + bonus) * _silu(gate)


def _rwkv_prompt_kernel(pb_ref, mu_ref, w0_ref, w2_ref, a0_ref, a2_ref, kk_ref, ka_ref,
                        rk_ref, lnw_ref, lnb_ref, ones_ref,
                        o_ref, s_out_ref, sh_out_ref, carry_ref, s_ref):
    c = pl.program_id(1)

    @pl.when(c == 0)
    def _():
        carry_ref[...] = jnp.zeros_like(carry_ref)
        s_ref[...] = jnp.zeros_like(s_ref)

    ones_blk = ones_ref[...]
    pb = pb_ref[...].reshape(NSEQ * TC, B_W)
    prev = []
    for s in range(NSEQ):
        pb_s = pb_ref[s]
        prev.append(pltpu.roll(jnp.concatenate([carry_ref[s], pb_s], axis=0), 1, 0)[8:, :])
        carry_ref[s] = pb_s[TC - 8:, :]
        sh_out_ref[s] = pb_s[TC - 1:, :]
    prev = jnp.concatenate(prev, axis=0)
    xb = pb + (prev - pb) * mu_ref[...]
    r, kmod, v, gate, logw, a, kk = _rwkv_front(
        xb, w0_ref[...], w2_ref[...], a0_ref[...], a2_ref[...], kk_ref[...], ka_ref[...], ones_blk)

    g = jnp.concatenate([_chunk_cumsum(logw[s * TC:(s + 1) * TC, :]) for s in range(NSEQ)],
                        axis=0)
    eg = jnp.exp(g)
    egn = jnp.exp(-g)
    ag = -kk * jnp.exp(g - logw)
    kka = kk * a
    bg = kka * egn
    kg = kmod * egn
    rg = r * eg

    rowp = _iota2((CHUNK, PAIR), 0)
    lane = _iota2((CHUNK, PAIR), 1)
    colp = jnp.where(lane >= HS_B, lane - HS_B, lane)
    left = lane < HS_B
    incl = colp <= rowp
    strict = colp < rowp
    eye = colp == rowp
    zero = jnp.zeros((CHUNK, PAIR), F32)
    pack = lambda f: jnp.where(left, f[0:HS_B, :], f[HS_B:, :])

    nj = NSEQ * TC // CHUNK
    npair = HB // 2
    jp = [(j, p) for j in range(nj) for p in range(npair)]
    rows_of = lambda j: slice(j * CHUNK, (j + 1) * CHUNK)
    lanes_of = lambda p: slice(p * PAIR, (p + 1) * PAIR)
    glast = [g[(j + 1) * CHUNK - 1:(j + 1) * CHUNK, :] for j in range(nj)]
    eglast = [jnp.exp(glast[j]) for j in range(nj)]
    bdec = [bg[rows_of(j), :] * eglast[j] for j in range(nj)]
    kdec = [kg[rows_of(j), :] * eglast[j] for j in range(nj)]

    cut = lambda z: [z[rows_of(j), lanes_of(p)] for j, p in jp]
    ag_p, rg_p, bg_p, kg_p, v_p = cut(ag), cut(rg), cut(bg), cut(kg), cut(v)
    bdec_p = [bdec[j][:, lanes_of(p)] for j, p in jp]
    kdec_p = [kdec[j][:, lanes_of(p)] for j, p in jp]
    eglast_p = [eglast[j][:, lanes_of(p)] for j, p in jp]
    amat = [_dot(jnp.concatenate([a, b], axis=0),
                 jnp.concatenate([_pair_diag(c_), _pair_diag(d)], axis=0), NT)
            for a, b, c_, d in zip(ag_p, rg_p, bg_p, kg_p)]
    a_ab = [jnp.where(strict, m[0:CHUNK, 0:PAIR], 0.0) for m in amat]
    a_ak = [jnp.where(strict, m[0:CHUNK, PAIR:], 0.0) for m in amat]
    a_rb = [jnp.where(incl, m[CHUNK:, 0:PAIR], 0.0) for m in amat]
    a_rk = [jnp.where(incl, m[CHUNK:, PAIR:], 0.0) for m in amat]
    tinv = _tri_inv_pairs([-a for a in a_ab])
    kv = [_dot(jnp.concatenate([a, b], axis=0), _pair_diag(c_)) for a, b, c_ in zip(a_ak, a_rk, v_p)]
    wu = [_dot(t, jnp.concatenate([_pair_diag(a), _pair_diag(x[0:CHUNK, :])], axis=1))
          for t, a, x in zip(tinv, ag_p, kv)]
    rwu = [_dot(a, jnp.concatenate([_pair_diag(x[:, 0:PAIR]), _pair_diag(x[:, PAIR:])], axis=1))
           for a, x in zip(a_rb, wu)]
    mn = [_dot(jnp.concatenate([a, b], axis=0),
               jnp.concatenate([x, jnp.concatenate([zero, c_], axis=1)], axis=0), TN)
          for a, b, x, c_ in zip(bdec_p, kdec_p, wu, v_p)]
    lhs = [jnp.concatenate([jnp.where(eye, e, 0.0) + pack(m[:, 0:PAIR]), a + x[:, 0:PAIR]], axis=0)
           for e, m, a, x in zip(eglast_p, mn, rg_p, rwu)]

    y_rows = [None] * nj
    for jj in range(TC // CHUNK):
        for s in range(NSEQ):
            j = s * (TC // CHUNK) + jj
            idx = [j * npair + p for p in range(npair)]
            res = [_dot(lhs[i], _pair_diag(s_ref[s, p])) for p, i in enumerate(idx)]
            for p, i in enumerate(idx):
                s_ref[s, p] = res[p][0:HS_B, :] + pack(mn[i][:, PAIR:])
            y_rows[j] = jnp.concatenate(
                [res[p][HS_B:, :] + rwu[i][:, PAIR:] + kv[i][CHUNK:, :] for p, i in enumerate(idx)],
                axis=-1)

    out = _rwkv_back(jnp.concatenate(y_rows, axis=0), r, kmod, v, gate,
                     rk_ref[...], lnw_ref[...], lnb_ref[...], ones_blk)
    o_ref[...] = out.reshape(NSEQ, TC, MIX_B).astype(o_ref.dtype)
    for s in range(NSEQ):
        for p in range(npair):
            st = s_ref[s, p].T
            s_out_ref[s, 2 * p] = st[0:HS_B, :]
            s_out_ref[s, 2 * p + 1] = st[HS_B:, :]


def _rwkv_prompt(pb, mu, w0, w2, a0, a2, k_k, k_a, rk, lnw, lnb, ones_blk):
    b, t, _ = pb.shape
    blk = lambda i, j: (i, j, 0)
    fixed = lambda i, j: (0, 0)
    vec = pl.BlockSpec((1, MIX_B), fixed)
    return pl.pallas_call(
        _rwkv_prompt_kernel,
        grid=(b // NSEQ, t // TC),
        in_specs=[pl.BlockSpec((NSEQ, TC, B_W), blk),
                  pl.BlockSpec((1, B_W), fixed),
                  vec, pl.BlockSpec((LORA, MIX_B), fixed),
                  vec, pl.BlockSpec((LORA, MIX_B), fixed),
                  vec, vec, vec, vec, vec,
                  pl.BlockSpec((MIX_B, MIX_B), fixed)],
        out_specs=[pl.BlockSpec((NSEQ, TC, MIX_B), blk),
                   pl.BlockSpec((NSEQ, HB, HS_B, HS_B), lambda i, j: (i, 0, 0, 0)),
                   pl.BlockSpec((NSEQ, 1, B_W), lambda i, j: (i, 0, 0))],
        out_shape=[jax.ShapeDtypeStruct((b, t, MIX_B), BF16),
                   jax.ShapeDtypeStruct((b, HB, HS_B, HS_B), F32),
                   jax.ShapeDtypeStruct((b, 1, B_W), F32)],
        scratch_shapes=[pltpu.VMEM((NSEQ, 8, B_W), F32),
                        pltpu.VMEM((NSEQ, HB // 2, HS_B, PAIR), F32)],
        compiler_params=pltpu.CompilerParams(dimension_semantics=("arbitrary", "arbitrary"),
                                             vmem_limit_bytes=VMEM_LIMIT),
        name="rwkv_prompt",
    )(pb, mu, w0, w2, a0, a2, k_k, k_a, rk, lnw, lnb, ones_blk)


def _pad_rows(rows, width):
    return jnp.concatenate(rows + [jnp.zeros((8 - len(rows), width), F32)], axis=0)


def _gdn_step_kernel(pa_ref, bd_ref, sc_ref, s_ref, cw_ref, hp_ref, anw_ref,
                     o_ref, s_out_ref, c_out_ref):
    u = pa_ref[:, 0:A_QKV]
    cw = cw_ref[...]
    conv = (sc_ref[0] * cw[0:1, :] + sc_ref[1] * cw[1:2, :] + sc_ref[2] * cw[2:3, :] + u * cw[3:4, :])
    qkv = _silu(conv)
    c_out_ref[0] = sc_ref[1]
    c_out_ref[1] = sc_ref[2]
    c_out_ref[2] = u

    qn, kn = _gdn_qk(qkv)
    beta_all, g_all = _gdn_gates(bd_ref[...], hp_ref[...])
    eg_all = jnp.exp(g_all)
    lanes_of = lambda h: slice(h * DK_A, (h + 1) * DK_A)
    q = [qn[:, lanes_of(h)] for h in range(HA)]
    k = [kn[:, lanes_of(h)] for h in range(HA)]
    v = [qkv[:, 2 * MIX_A + h * DK_A:2 * MIX_A + (h + 1) * DK_A] for h in range(HA)]
    beta = [beta_all[:, h:h + 1] for h in range(HA)]
    eg = [eg_all[:, HA + h:HA + h + 1] for h in range(HA)]

    bh = [(b, h) for b in range(SB) for h in range(HA)]
    s0 = [s_ref[b, h] for b, h in bh]
    ks_qs = [_dot(_pad_rows([k[h][b:b + 1, :], q[h][b:b + 1, :]], DK_A), s) for (b, h), s in zip(bh, s0)]
    o_heads = []
    for h in range(HA):
        ks = jnp.concatenate([ks_qs[b * HA + h][0:1, :] for b in range(SB)], axis=0)
        qs = jnp.concatenate([ks_qs[b * HA + h][1:2, :] for b in range(SB)], axis=0)
        u2 = beta[h] * (v[h] - eg[h] * ks)
        for b in range(SB):
            s_out_ref[b, h] = (eg[h][b:b + 1, :] * s0[b * HA + h]
                               + _dot(_pad_rows([k[h][b:b + 1, :]], DK_A),
                                      _pad_rows([u2[b:b + 1, :]], DK_A), TN))
        o_heads.append(eg[h] * qs + jnp.sum(q[h] * k[h], axis=-1, keepdims=True) * u2)

    o_ref[...] = _gdn_out(jnp.concatenate(o_heads, axis=-1), pa_ref[:, A_QKV:A_MAIN], anw_ref[...])


def _gdn_step_specs(n):
    row = lambda i: (i, 0)
    fixed = lambda i: (0, 0)
    state = pl.BlockSpec((SB, HA, DK_A, DK_A), lambda i: (i, 0, 0, 0))
    conv = pl.BlockSpec((CONV_W - 1, SB, A_QKV), lambda i: (0, i, 0))
    in_specs = [pl.BlockSpec((SB, A_MAIN), row),
                pl.BlockSpec((SB, BD_W), row),
                conv, state,
                pl.BlockSpec((CONV_W, A_QKV), fixed),
                pl.BlockSpec((8, BD_W), fixed),
                pl.BlockSpec((1, MIX_A), fixed)]
    out_specs = [pl.BlockSpec((SB, MIX_A), row), state, conv]
    out_shape = [jax.ShapeDtypeStruct((n, MIX_A), F32),
                 jax.ShapeDtypeStruct((n, HA, DK_A, DK_A), F32),
                 jax.ShapeDtypeStruct((CONV_W - 1, n, A_QKV), F32)]
    return in_specs, out_specs, out_shape


def _rwkv_step_lanes_kernel(pb_ref, sh_ref, s_ref, mu_ref, w0_ref, w2_ref, a0_ref, a2_ref, kk_ref, ka_ref,
                            rk_ref, lnw_ref, lnb_ref, ones_ref,
                            o_ref, s_out_ref, vec_scr, row_scr, y_scr):
    h = pl.program_id(0)
    ones_blk = ones_ref[...]

    @pl.when(h == 0)
    def _():
        pb = pb_ref[...]
        xb = pb + (sh_ref[...] - pb) * mu_ref[...]
        r, kmod, v, gate, logw, a, kk = _rwkv_front(
            xb, w0_ref[...], w2_ref[...], a0_ref[...], a2_ref[...], kk_ref[...], ka_ref[...], ones_blk)
        for slot, z in enumerate((-kk, kk * a, kmod, v, r, jnp.exp(logw))):
            vec_scr[slot] = z.T
        for slot, z in enumerate((r, kmod, v, gate)):
            row_scr[slot] = z

    base = pl.multiple_of(h * HS_B, HS_B)
    chan = pl.ds(base, HS_B)
    nkk = vec_scr[0, chan, :]
    kka = vec_scr[1, chan, :]
    kmod_t = vec_scr[2, chan, :]
    r_t = vec_scr[4, chan, :]
    wdec = vec_scr[5, chan, :]

    def body(vi, carry):
        s0 = s_ref[0, vi]
        sa = jnp.sum(s0 * nkk, axis=0, keepdims=True)
        s1 = s0 * wdec + sa * kka + vec_scr[3, pl.ds(base + vi, 1), :] * kmod_t
        s_out_ref[0, vi] = s1
        y_scr[pl.ds(base + vi, 1), :] = jnp.sum(s1 * r_t, axis=0, keepdims=True)
        return carry

    lax.fori_loop(0, HS_B, body, 0, unroll=4)

    @pl.when(h == HB - 1)
    def _():
        o_ref[...] = _rwkv_back(y_scr[...].T, row_scr[0], row_scr[1], row_scr[2], row_scr[3],
                                rk_ref[...], lnw_ref[...], lnb_ref[...], ones_blk)


def _rwkv_step_specs(n):
    fixed = lambda i: (0, 0)
    vec = pl.BlockSpec((1, MIX_B), fixed)
    state = pl.BlockSpec((1, HS_B, HS_B, n), lambda i: (i, 0, 0, 0))
    in_specs = [pl.BlockSpec((n, B_W), fixed),
                pl.BlockSpec((n, B_W), fixed),
                state,
                pl.BlockSpec((1, B_W), fixed),
                vec, pl.BlockSpec((LORA, MIX_B), fixed),
                vec, pl.BlockSpec((LORA, MIX_B), fixed),
                vec, vec, vec, vec, vec,
                pl.BlockSpec((MIX_B, MIX_B), fixed)]
    out_specs = [pl.BlockSpec((n, MIX_B), fixed), state]
    out_shape = [jax.ShapeDtypeStruct((n, MIX_B), F32),
                 jax.ShapeDtypeStruct((HB, HS_B, HS_B, n), F32)]
    return in_specs, out_specs, out_shape


def _sample_step(a_args, b_args, xs2d, w_out, final_w):
    n = a_args[0].shape[0]
    assert n // SB == HB
    a_in, a_out, a_shape = _gdn_step_specs(n)
    b_in, b_out, b_shape = _rwkv_step_specs(n)
    na, nb = len(a_in), len(b_in)
    fixed = lambda i: (0, 0)
    o_in = [pl.BlockSpec(xs2d.shape, fixed), pl.BlockSpec((D_MODEL, D_MODEL), fixed), pl.BlockSpec((1, D_MODEL), fixed)]

    def body(*refs):
        a_refs, b_refs, o_refs = refs[:na], refs[na:na + nb], refs[na + nb:na + nb + 3]
        sa_out, ca_out, sb_out, y_ref = refs[na + nb + 3:na + nb + 7]
        vec_scr, row_scr, y_scr, mixa_scr, mixb_scr = refs[na + nb + 7:]
        i = pl.program_id(0)
        _gdn_step_kernel(*a_refs, mixa_scr.at[pl.ds(pl.multiple_of(i * SB, SB), SB), :], sa_out, ca_out)
        _rwkv_step_lanes_kernel(*b_refs, mixb_scr, sb_out, vec_scr, row_scr, y_scr)

        @pl.when(i == HB - 1)
        def _():
            _outproj_kernel(o_refs[0], mixa_scr, mixb_scr, o_refs[1], o_refs[2], y_ref)

    return pl.pallas_call(
        body,
        grid=(HB,),
        in_specs=a_in + b_in + o_in,
        out_specs=a_out[1:] + b_out[1:] + [pl.BlockSpec(xs2d.shape, fixed)],
        out_shape=a_shape[1:] + b_shape[1:] + [jax.ShapeDtypeStruct(xs2d.shape, F32)],
        scratch_shapes=[pltpu.VMEM((6, MIX_B, n), F32),
                        pltpu.VMEM((4, n, MIX_B), F32),
                        pltpu.VMEM((MIX_B, n), F32),
                        pltpu.VMEM((n, MIX_A), F32),
                        pltpu.VMEM((n, MIX_B), F32)],
        compiler_params=pltpu.CompilerParams(dimension_semantics=("arbitrary",),
                                             vmem_limit_bytes=VMEM_LIMIT),
        name="sample_step",
    )(*a_args, *b_args, xs2d, w_out, final_w)


def _block_ones(width, group):
    idx = jnp.arange(width) // group
    return (idx[:, None] == idx[None, :]).astype(BF16)


def kernel(x_prompt, x_sample, state_a_mat, state_a_conv, state_b_mat, state_b_shift, norm_w, w_in,
           conv_w, a_log, dt_bias, a_norm_w, mu, w0, w2, a0, a2, k_k, k_a, r_k, ln_w, ln_b, w_out,
           final_norm_w):
    bsz, seq, _ = x_prompt.shape
    nsmp = x_sample.shape[0]
    assert bsz % NSEQ == 0 and bsz % NSEQ_A == 0 and seq % TC == 0 and seq % TC_A == 0
    assert nsmp % SB == 0 and x_sample.shape[1] == 1

    w_all = jnp.transpose(w_in[0])
    w_o = w_out[0].astype(BF16)
    hp = jnp.pad(jnp.concatenate([a_log, dt_bias], axis=0), ((0, 6), (HA, BD_W - 2 * HA)))
    nw = norm_w[0].reshape(1, D_MODEL)
    fw = final_norm_w.reshape(1, D_MODEL)
    anw = jnp.tile(a_norm_w[0].reshape(1, DK_A), (1, HA))
    ones_b = _block_ones(MIX_B, HS_B)
    row = lambda z: z[0].reshape(1, -1)
    b_params = (row(mu), row(w0), w2[0], row(a0), a2[0], row(k_k), row(k_a), row(r_k), row(ln_w), row(ln_b),
                ones_b)

    xp = x_prompt.reshape(bsz * seq, D_MODEL)
    xs = x_sample.reshape(nsmp * D_MODEL // LANE, LANE)
    pa, pb, bd, sa_, sb_, sbd = _inproj(xp, xs, nw, w_all)

    ob, p_bmat, p_bshift = _rwkv_prompt(pb.reshape(bsz, seq, B_W), *b_params)
    y_prompt, p_amat, p_aconv = _gdn_prompt(pa.reshape(bsz, seq, A_MAIN), bd.reshape(bsz, seq, BD_W),
                                            conv_w[0], hp, anw, x_prompt, ob, w_o, fw)

    sconv_t = jnp.swapaxes(state_a_conv[0], 0, 1)
    s_amat, s_aconv_t, s_bmat_t, y_sample = _sample_step(
        (sa_, sbd, sconv_t, state_a_mat[0], conv_w[0], hp, anw),
        (sb_, state_b_shift[0], jnp.transpose(state_b_mat[0], (1, 2, 3, 0))) + b_params,
        xs, w_o, fw)
    s_bmat = jnp.transpose(s_bmat_t, (3, 0, 1, 2))

    return (y_prompt, y_sample.reshape(nsmp, 1, D_MODEL),
            p_amat[None], p_aconv[None], p_bmat[None], p_bshift.reshape(1, bsz, B_W),
            s_amat[None], jnp.swapaxes(s_aconv_t, 0, 1)[None], s_bmat[None], sb_[None])
```

```python
import jax
import jax.numpy as jnp
from jax import lax
from jax.experimental import pallas as pl
from jax.experimental.pallas import tpu as pltpu

F32 = jnp.float32
BF16 = jnp.bfloat16

D_MODEL = 1024
MIX_A = 512
MIX_B = 512
HA = 4
DK_A = 128
HB = 8
HS_B = 64
LANE = 128
PAIR = 2 * HS_B
CONV_W = 4
LORA = 64
A_QKV = 3 * MIX_A
A_MAIN = A_QKV + MIX_A
B_W = 4 * MIX_B + 2 * LORA
BD_W = 128
RMS_EPS = 1e-6
GN_EPS = 64e-5
CHUNK = 64
TC = 256
TC_A = 256
NSEQ_A = 2
NSEQ = 4
SB = 16
RING = 3
A_STATE_ARG = 3
B_STATE_ARG = 2
VMEM_LIMIT = 56 * 1024 * 1024

EXP_NEG_HALF = 0.6065306597126334
NN = (((1,), (0,)), ((), ()))
NT = (((1,), (1,)), ((), ()))
TN = (((0,), (0,)), ((), ()))


def _dot(a, b, dn=NN):
    return lax.dot_general(a, b, dn, preferred_element_type=F32)


def _sigmoid(x):
    return 0.5 * jnp.tanh(0.5 * x) + 0.5


def _silu(x):
    h = 0.5 * x
    return h * jnp.tanh(h) + h


def _softplus(x):
    return jnp.maximum(x, 0.0) + jnp.log(1.0 + jnp.exp(-jnp.abs(x)))


def _l2norm(x):
    return x * lax.rsqrt(jnp.sum(x * x, axis=-1, keepdims=True) + 1e-12)


def _group_sum(x, ones_blk):
    return _dot(x.astype(BF16), ones_blk)


def _iota2(shape, dim):
    return lax.broadcasted_iota(jnp.int32, shape, dim)


def _chunk_cumsum(x):
    n = x.shape[0]
    r = _iota2((n, n), 0)
    c = _iota2((n, n), 1)
    tril = jnp.where(((r // CHUNK) == (c // CHUNK)) & (c <= r), 1.0, 0.0).astype(BF16)
    hi = x.astype(BF16)
    rest = x - hi.astype(F32)
    mid = rest.astype(BF16)
    lo = (rest - mid.astype(F32)).astype(BF16)
    return _dot(tril, hi) + _dot(tril, mid) + _dot(tril, lo)


def _tri_inv_many(lows):
    n = lows[0].shape[0]
    r = _iota2((n, n), 0)
    c = _iota2((n, n), 1)
    eye = jnp.where(r == c, 1.0, 0.0).astype(F32)
    base = (r // 8) == (c // 8)
    l0 = [jnp.where(base, low, 0.0) for low in lows]
    l2 = [_dot(a, a) for a in l0]
    l4 = [_dot(a, a) for a in l2]
    d = [_dot(eye - a, eye + b) for a, b in zip(l0, l2)]
    d = [_dot(a, eye + b) for a, b in zip(d, l4)]
    s = 8
    while s < n:
        sel = ((r // (2 * s)) == (c // (2 * s))) & ((r // s) != (c // s))
        t = [_dot(jnp.where(sel, low, 0.0), a) for low, a in zip(lows, d)]
        d = [a - _dot(a, b) for a, b in zip(d, t)]
        s *= 2
    return d


def _pair_diag(x):
    left = _iota2(x.shape, 1) < x.shape[1] // 2
    return jnp.concatenate([jnp.where(left, x, 0.0), jnp.where(left, 0.0, x)], axis=0)


def _tri_inv_pairs(lows):
    n = lows[0].shape[0]
    r = _iota2((n, 2 * n), 0)
    lane = _iota2((n, 2 * n), 1)
    c = jnp.where(lane >= n, lane - n, lane)
    mul = lambda a, b: _dot(a, _pair_diag(b))
    eye = jnp.where(r == c, 1.0, 0.0).astype(F32)
    base = (r // 8) == (c // 8)
    l0 = [jnp.where(base, low, 0.0) for low in lows]
    l2 = [mul(a, a) for a in l0]
    l4 = [mul(a, a) for a in l2]
    d = [mul(eye - a, eye + b) for a, b in zip(l0, l2)]
    d = [mul(a, eye + b) for a, b in zip(d, l4)]
    s = 8
    while s < n:
        sel = ((r // (2 * s)) == (c // (2 * s))) & ((r // s) != (c // s))
        t = [mul(jnp.where(sel, low, 0.0), a) for low, a in zip(lows, d)]
        d = [a - mul(a, b) for a, b in zip(d, t)]
        s *= 2
    return d


def _sample_rows(ref, n):
    pieces = D_MODEL // LANE
    return jnp.concatenate([ref[pl.ds(c, n, stride=pieces), :] for c in range(pieces)], axis=1)


def _inproj_kernel(x_ref, xs_ref, nw_ref, wt_ref, oa_ref, ob_ref, obd_ref, sa_ref, sb_ref, sbd_ref, w_scr):
    a_w = A_MAIN + 2 * HA

    def project(x, pa_ref, pb_ref, pbd_ref):
        xn = (x * lax.rsqrt(jnp.mean(x * x, axis=-1, keepdims=True) + RMS_EPS) * nw_ref[...]).astype(BF16)
        pa_ref[...] = _dot(xn, w_scr[0:A_MAIN, :], NT)
        pb = _dot(xn, w_scr[A_MAIN:, :], NT)
        pb_ref[...] = pb[:, :B_W]
        pbd_ref[...] = pb[:, B_W:]

    @pl.when(pl.program_id(0) == 0)
    def _():
        w_scr[0:A_MAIN, :] = wt_ref[0:A_MAIN, :].astype(BF16)
        w_scr[A_MAIN:A_MAIN + B_W, :] = wt_ref[a_w:, :].astype(BF16)
        w_scr[A_MAIN + B_W:, :] = jnp.concatenate(
            [wt_ref[A_MAIN:a_w, :], jnp.zeros((BD_W - 2 * HA, D_MODEL), F32)], axis=0).astype(BF16)
        project(_sample_rows(xs_ref, sa_ref.shape[0]), sa_ref, sb_ref, sbd_ref)

    project(x_ref[...], oa_ref, ob_ref, obd_ref)


def _inproj(x2d, xs2d, norm_w, w_all):
    n, ns = x2d.shape[0], xs2d.shape[0] * LANE // D_MODEL
    tm = min(512, n)
    row = lambda i: (i, 0)
    fixed = lambda i: (0, 0)
    widths = (A_MAIN, B_W, BD_W)
    return pl.pallas_call(
        _inproj_kernel,
        grid=(n // tm,),
        in_specs=[pl.BlockSpec((tm, D_MODEL), row),
                  pl.BlockSpec(xs2d.shape, fixed),
                  pl.BlockSpec((1, D_MODEL), fixed),
                  pl.BlockSpec(w_all.shape, fixed)],
        out_specs=[pl.BlockSpec((tm, w), row) for w in widths] + [pl.BlockSpec((ns, w), fixed) for w in widths],
        out_shape=[jax.ShapeDtypeStruct((n, w), F32) for w in widths]
                  + [jax.ShapeDtypeStruct((ns, w), F32) for w in widths],
        scratch_shapes=[pltpu.VMEM((A_MAIN + B_W + BD_W, D_MODEL), BF16)],
        compiler_params=pltpu.CompilerParams(dimension_semantics=("arbitrary",),
                                             vmem_limit_bytes=VMEM_LIMIT),
        name="inproj",
    )(x2d, xs2d, norm_w, w_all)


def _outproj_kernel(x_ref, ma_ref, mb_ref, w_ref, fw_ref, y_ref):
    n = ma_ref.shape[0]
    h = (_sample_rows(x_ref, n)
         + _dot(ma_ref[...].astype(BF16), w_ref[0:MIX_A, :])
         + _dot(mb_ref[...].astype(BF16), w_ref[MIX_A:, :]))
    y = h * lax.rsqrt(jnp.mean(h * h, axis=-1, keepdims=True) + RMS_EPS) * fw_ref[...]
    pieces = D_MODEL // LANE
    for c in range(pieces):
        y_ref[pl.ds(c, n, stride=pieces), :] = y[:, c * LANE:(c + 1) * LANE]


def _gdn_gates(bd, hp):
    beta = _sigmoid(bd)
    g = -jnp.exp(hp[0:1, :]) * _softplus(bd + hp[1:2, :])
    return beta, g


def _gdn_qk(qkv):
    heads = lambda z: [z[:, h * DK_A:(h + 1) * DK_A] for h in range(HA)]
    qn = jnp.concatenate([_l2norm(x) * (DK_A ** -0.5) for x in heads(qkv[:, 0:MIX_A])], axis=-1)
    kn = jnp.concatenate([_l2norm(x) for x in heads(qkv[:, MIX_A:2 * MIX_A])], axis=-1)
    return qn, kn


def _gdn_out(o, gate, anw):
    on = jnp.concatenate(
        [x * lax.rsqrt(jnp.mean(x * x, axis=-1, keepdims=True) + RMS_EPS)
         for x in [o[:, h * DK_A:(h + 1) * DK_A] for h in range(HA)]], axis=-1)
    return on * anw * _silu(gate)


def _gdn_prompt_kernel(pa_ref, bd_ref, cw_ref, hp_ref, anw_ref, x_ref, mb_ref, wo_ref, fw_ref,
                       y_ref, s_out_ref, c_out_ref, ext_ref, s_ref):
    c = pl.program_id(1)

    nseq, tc = pa_ref.shape[0], pa_ref.shape[1]

    @pl.when(c == 0)
    def _():
        ext_ref[...] = jnp.zeros_like(ext_ref)
        s_ref[...] = jnp.zeros_like(s_ref)

    cw = cw_ref[...]
    conv = []
    for s in range(nseq):
        u = pa_ref[s, :, 0:A_QKV]
        full = jnp.concatenate([ext_ref[s], u], axis=0)
        conv.append(u * cw[3:4, :] + sum(
            pltpu.roll(full, CONV_W - 1 - i, 0)[8:, :] * cw[i:i + 1, :] for i in range(CONV_W - 1)))
        ext_ref[s] = u[tc - 8:, :]
        c_out_ref[s] = ext_ref[s, 8 - (CONV_W - 1):, :]
    qkv = _silu(jnp.concatenate(conv, axis=0))

    qn, kn = _gdn_qk(qkv)
    beta_all, g_all = _gdn_gates(bd_ref[...].reshape(nseq * tc, BD_W), hp_ref[...])
    gcum = jnp.concatenate([_chunk_cumsum(g_all[s * tc:(s + 1) * tc, :]) for s in range(nseq)],
                           axis=0)
    gcum_t = gcum.T

    nrow = nseq * tc
    wide = lambda z, first: jnp.concatenate(
        [jnp.broadcast_to(z[:, first + h:first + h + 1], (nrow, DK_A)) for h in range(HA)], axis=1)
    bw, gw = wide(beta_all, 0), wide(gcum, HA)
    half = _iota2((nrow, PAIR), 1) < CHUNK
    narrow = lambda zw: jnp.concatenate(
        [jnp.where(half, zw[:, 2 * p * DK_A:(2 * p + 1) * DK_A], zw[:, (2 * p + 1) * DK_A:(2 * p + 2) * DK_A])
         for p in range(HA // 2)], axis=1)
    b64, g64 = narrow(bw), narrow(gw)
    v_all = qkv[:, 2 * MIX_A:]
    egw = jnp.exp(gw)
    bek = bw * egw * kn
    bv = bw * v_all
    qg = qn * egw

    rowp = _iota2((CHUNK, PAIR), 0)
    lane = _iota2((CHUNK, PAIR), 1)
    colp = jnp.where(lane >= CHUNK, lane - CHUNK, lane)
    incl = colp <= rowp
    strict = colp < rowp

    nj = nseq * tc // CHUNK
    npair = HA // 2
    jp = [(j, p) for j in range(nj) for p in range(npair)]
    jh = [(j, h) for j in range(nj) for h in range(HA)]
    rows_of = lambda j: slice(j * CHUNK, (j + 1) * CHUNK)
    lanes_of = lambda h: slice(h * DK_A, (h + 1) * DK_A)
    pair64 = lambda p: slice(p * PAIR, (p + 1) * PAIR)
    pairw = lambda p: slice(2 * p * DK_A, 2 * (p + 1) * DK_A)
    glast = [gw[(j + 1) * CHUNK - 1:(j + 1) * CHUNK, :] for j in range(nj)]
    kdec = [kn[rows_of(j), :] * jnp.exp(glast[j] - gw[rows_of(j), :]) for j in range(nj)]
    eglast = [jnp.exp(glast[j]) for j in range(nj)]
    gr = [jnp.concatenate([gcum_t[HA + 2 * p:HA + 2 * p + 1, rows_of(j)],
                           gcum_t[HA + 2 * p + 1:HA + 2 * p + 2, rows_of(j)]], axis=1) for j, p in jp]
    decay = [jnp.where(incl, jnp.exp(jnp.where(incl, g64[rows_of(j), pair64(p)] - b, 0.0)), 0.0)
             for (j, p), b in zip(jp, gr)]
    kq = [_dot(jnp.concatenate([kn[rows_of(j), pairw(p)], qn[rows_of(j), pairw(p)]], axis=0),
               _pair_diag(kn[rows_of(j), pairw(p)]), NT) for j, p in jp]
    low = [jnp.where(strict, b64[rows_of(j), pair64(p)] * d * m[0:CHUNK, :], 0.0)
           for (j, p), d, m in zip(jp, decay, kq)]
    tinv = _tri_inv_pairs(low)
    zw = jnp.zeros((CHUNK, 2 * DK_A), F32)
    wu = [_dot(t, jnp.concatenate(
        [jnp.concatenate([bek[rows_of(j), lanes_of(2 * p)], bv[rows_of(j), lanes_of(2 * p)], zw], axis=1),
         jnp.concatenate([zw, bek[rows_of(j), lanes_of(2 * p + 1)], bv[rows_of(j), lanes_of(2 * p + 1)]],
                         axis=1)], axis=0))
          for t, (j, p) in zip(tinv, jp)]
    qwu_p = [_dot(d * m[CHUNK:, :], _pair_diag(x)) for d, m, x in zip(decay, kq, wu)]
    head_of = lambda z, j, h: z[j * npair + h // 2][:, (h % 2) * 2 * DK_A:(h % 2 + 1) * 2 * DK_A]
    qwu = [head_of(qwu_p, j, h) for j, h in jh]
    kwu = [_dot(kdec[j][:, lanes_of(h)], head_of(wu, j, h), TN) for j, h in jh]
    rd = _iota2((DK_A, DK_A), 0)
    cd = _iota2((DK_A, DK_A), 1)
    lhs = [jnp.concatenate([jnp.where(rd == cd, eglast[j][:, lanes_of(h)], 0.0) - kx[:, 0:DK_A],
                            qg[rows_of(j), lanes_of(h)] - qx[:, 0:DK_A]], axis=0)
           for (j, h), kx, qx in zip(jh, kwu, qwu)]

    o_rows = [None] * nj
    for jj in range(tc // CHUNK):
        for s in range(nseq):
            j = s * (tc // CHUNK) + jj
            idx = [j * HA + h for h in range(HA)]
            res = [_dot(lhs[i], s_ref[s, h]) for h, i in enumerate(idx)]
            for h, i in enumerate(idx):
                s_ref[s, h] = res[h][0:DK_A, :] + kwu[i][:, DK_A:]
            o_rows[j] = jnp.concatenate(
                [res[h][DK_A:, :] + qwu[i][:, DK_A:] for h, i in enumerate(idx)], axis=-1)

    gate = pa_ref[:, :, A_QKV:A_MAIN].reshape(nseq * tc, MIX_A)
    mix_a = _gdn_out(jnp.concatenate(o_rows, axis=0), gate, anw_ref[...])
    h = (x_ref[...].reshape(nseq * tc, D_MODEL)
         + _dot(mb_ref[...].reshape(nseq * tc, MIX_B), wo_ref[MIX_A:, :])
         + _dot(mix_a.astype(BF16), wo_ref[0:MIX_A, :]))
    y = h * lax.rsqrt(jnp.mean(h * h, axis=-1, keepdims=True) + RMS_EPS) * fw_ref[...]
    y_ref[...] = y.reshape(nseq, tc, D_MODEL)
    s_out_ref[...] = s_ref[...]


def _gdn_prompt(pa, bd, conv_w, hp, anw, x, mix_b, w_out, final_w):
    b, t, _ = pa.shape
    blk = lambda i, j: (i, j, 0)
    fixed = lambda i, j: (0, 0)
    return pl.pallas_call(
        _gdn_prompt_kernel,
        grid=(b // NSEQ_A, t // TC_A),
        in_specs=[pl.BlockSpec((NSEQ_A, TC_A, A_MAIN), blk),
                  pl.BlockSpec((NSEQ_A, TC_A, BD_W), blk),
                  pl.BlockSpec((CONV_W, A_QKV), fixed),
                  pl.BlockSpec((8, BD_W), fixed),
                  pl.BlockSpec((1, MIX_A), fixed),
                  pl.BlockSpec((NSEQ_A, TC_A, D_MODEL), blk),
                  pl.BlockSpec((NSEQ_A, TC_A, MIX_B), blk),
                  pl.BlockSpec((D_MODEL, D_MODEL), fixed),
                  pl.BlockSpec((1, D_MODEL), fixed)],
        out_specs=[pl.BlockSpec((NSEQ_A, TC_A, D_MODEL), blk),
                   pl.BlockSpec((NSEQ_A, HA, DK_A, DK_A), lambda i, j: (i, 0, 0, 0)),
                   pl.BlockSpec((NSEQ_A, CONV_W - 1, A_QKV), lambda i, j: (i, 0, 0))],
        out_shape=[jax.ShapeDtypeStruct((b, t, D_MODEL), F32),
                   jax.ShapeDtypeStruct((b, HA, DK_A, DK_A), F32),
                   jax.ShapeDtypeStruct((b, CONV_W - 1, A_QKV), F32)],
        scratch_shapes=[pltpu.VMEM((NSEQ_A, 8, A_QKV), F32),
                        pltpu.VMEM((NSEQ_A, HA, DK_A, DK_A), F32)],
        compiler_params=pltpu.CompilerParams(dimension_semantics=("arbitrary", "arbitrary"),
                                             vmem_limit_bytes=VMEM_LIMIT),
        name="gdn_prompt",
    )(pa, bd, conv_w, hp, anw, x, mix_b, w_out, final_w)


def _rwkv_front(xb, w0, w2, a0, a2, k_k, k_a, ones_blk):
    r = xb[:, 0:MIX_B]
    kr = xb[:, MIX_B:2 * MIX_B]
    vr = xb[:, 2 * MIX_B:3 * MIX_B]
    gate = xb[:, 3 * MIX_B:4 * MIX_B]
    wd = xb[:, 4 * MIX_B:4 * MIX_B + LORA]
    ad = xb[:, 4 * MIX_B + LORA:]
    logw = -(EXP_NEG_HALF * _sigmoid(w0 + _dot(jnp.tanh(wd), w2)))
    a = _sigmoid(a0 + _dot(ad, a2))
    kraw = kr * k_k
    kk = kraw * lax.rsqrt(_group_sum(kraw * kraw, ones_blk) + 1e-12)
    kmod = kr * (1.0 + (a - 1.0) * k_a)
    return r, kmod, vr, gate, logw, a, kk


def _rwkv_back(y, r, kmod, v, gate, rk, lnw, lnb, ones_blk):
    inv = 1.0 / HS_B
    yc = y - _group_sum(y, ones_blk) * inv
    var = _group_sum(yc * yc, ones_blk) * inv
    gn = yc * lax.rsqrt(var + GN_EPS) * lnw + lnb
    bonus = _group_sum(r * kmod * rk, ones_blk) * v
    return (gn + bonus) * _silu(gate)


def _rwkv_prompt_kernel(pb_ref, mu_ref, w0_ref, w2_ref, a0_ref, a2_ref, kk_ref, ka_ref,
                        rk_ref, lnw_ref, lnb_ref, ones_ref,
                        o_ref, s_out_ref, sh_out_ref, carry_ref, s_ref):
    c = pl.program_id(1)

    @pl.when(c == 0)
    def _():
        carry_ref[...] = jnp.zeros_like(carry_ref)
        s_ref[...] = jnp.zeros_like(s_ref)

    ones_blk = ones_ref[...]
    pb = pb_ref[...].reshape(NSEQ * TC, B_W)
    prev = []
    for s in range(NSEQ):
        pb_s = pb_ref[s]
        prev.append(pltpu.roll(jnp.concatenate([carry_ref[s], pb_s], axis=0), 1, 0)[8:, :])
        carry_ref[s] = pb_s[TC - 8:, :]
        sh_out_ref[s] = pb_s[TC - 1:, :]
    prev = jnp.concatenate(prev, axis=0)
    xb = pb + (prev - pb) * mu_ref[...]
    r, kmod, v, gate, logw, a, kk = _rwkv_front(
        xb, w0_ref[...], w2_ref[...], a0_ref[...], a2_ref[...], kk_ref[...], ka_ref[...], ones_blk)

    g = jnp.concatenate([_chunk_cumsum(logw[s * TC:(s + 1) * TC, :]) for s in range(NSEQ)],
                        axis=0)
    eg = jnp.exp(g)
    egn = jnp.exp(-g)
    ag = -kk * jnp.exp(g - logw)
    kka = kk * a
    bg = kka * egn
    kg = kmod * egn
    rg = r * eg

    rowp = _iota2((CHUNK, PAIR), 0)
    lane = _iota2((CHUNK, PAIR), 1)
    colp = jnp.where(lane >= HS_B, lane - HS_B, lane)
    left = lane < HS_B
    incl = colp <= rowp
    strict = colp < rowp
    eye = colp == rowp
    zero = jnp.zeros((CHUNK, PAIR), F32)
    pack = lambda f: jnp.where(left, f[0:HS_B, :], f[HS_B:, :])

    nj = NSEQ * TC // CHUNK
    npair = HB // 2
    jp = [(j, p) for j in range(nj) for p in range(npair)]
    rows_of = lambda j: slice(j * CHUNK, (j + 1) * CHUNK)
    lanes_of = lambda p: slice(p * PAIR, (p + 1) * PAIR)
    glast = [g[(j + 1) * CHUNK - 1:(j + 1) * CHUNK, :] for j in range(nj)]
    eglast = [jnp.exp(glast[j]) for j in range(nj)]
    bdec = [bg[rows_of(j), :] * eglast[j] for j in range(nj)]
    kdec = [kg[rows_of(j), :] * eglast[j] for j in range(nj)]

    cut = lambda z: [z[rows_of(j), lanes_of(p)] for j, p in jp]
    ag_p, rg_p, bg_p, kg_p, v_p = cut(ag), cut(rg), cut(bg), cut(kg), cut(v)
    bdec_p = [bdec[j][:, lanes_of(p)] for j, p in jp]
    kdec_p = [kdec[j][:, lanes_of(p)] for j, p in jp]
    eglast_p = [eglast[j][:, lanes_of(p)] for j, p in jp]
    amat = [_dot(jnp.concatenate([a, b], axis=0),
                 jnp.concatenate([_pair_diag(c_), _pair_diag(d)], axis=0), NT)
            for a, b, c_, d in zip(ag_p, rg_p, bg_p, kg_p)]
    a_ab = [jnp.where(strict, m[0:CHUNK, 0:PAIR], 0.0) for m in amat]
    a_ak = [jnp.where(strict, m[0:CHUNK, PAIR:], 0.0) for m in amat]
    a_rb = [jnp.where(incl, m[CHUNK:, 0:PAIR], 0.0) for m in amat]
    a_rk = [jnp.where(incl, m[CHUNK:, PAIR:], 0.0) for m in amat]
    tinv = _tri_inv_pairs([-a for a in a_ab])
    kv = [_dot(jnp.concatenate([a, b], axis=0), _pair_diag(c_)) for a, b, c_ in zip(a_ak, a_rk, v_p)]
    wu = [_dot(t, jnp.concatenate([_pair_diag(a), _pair_diag(x[0:CHUNK, :])], axis=1))
          for t, a, x in zip(tinv, ag_p, kv)]
    rwu = [_dot(a, jnp.concatenate([_pair_diag(x[:, 0:PAIR]), _pair_diag(x[:, PAIR:])], axis=1))
           for a, x in zip(a_rb, wu)]
    mn = [_dot(jnp.concatenate([a, b], axis=0),
               jnp.concatenate([x, jnp.concatenate([zero, c_], axis=1)], axis=0), TN)
          for a, b, x, c_ in zip(bdec_p, kdec_p, wu, v_p)]
    lhs = [jnp.concatenate([jnp.where(eye, e, 0.0) + pack(m[:, 0:PAIR]), a + x[:, 0:PAIR]], axis=0)
           for e, m, a, x in zip(eglast_p, mn, rg_p, rwu)]

    y_rows = [None] * nj
    for jj in range(TC // CHUNK):
        for s in range(NSEQ):
            j = s * (TC // CHUNK) + jj
            idx = [j * npair + p for p in range(npair)]
            res = [_dot(lhs[i], _pair_diag(s_ref[s, p])) for p, i in enumerate(idx)]
            for p, i in enumerate(idx):
                s_ref[s, p] = res[p][0:HS_B, :] + pack(mn[i][:, PAIR:])
            y_rows[j] = jnp.concatenate(
                [res[p][HS_B:, :] + rwu[i][:, PAIR:] + kv[i][CHUNK:, :] for p, i in enumerate(idx)],
                axis=-1)

    out = _rwkv_back(jnp.concatenate(y_rows, axis=0), r, kmod, v, gate,
                     rk_ref[...], lnw_ref[...], lnb_ref[...], ones_blk)
    o_ref[...] = out.reshape(NSEQ, TC, MIX_B).astype(o_ref.dtype)
    for s in range(NSEQ):
        for p in range(npair):
            st = s_ref[s, p].T
            s_out_ref[s, 2 * p] = st[0:HS_B, :]
            s_out_ref[s, 2 * p + 1] = st[HS_B:, :]


def _rwkv_prompt(pb, mu, w0, w2, a0, a2, k_k, k_a, rk, lnw, lnb, ones_blk):
    b, t, _ = pb.shape
    blk = lambda i, j: (i, j, 0)
    fixed = lambda i, j: (0, 0)
    vec = pl.BlockSpec((1, MIX_B), fixed)
    return pl.pallas_call(
        _rwkv_prompt_kernel,
        grid=(b // NSEQ, t // TC),
        in_specs=[pl.BlockSpec((NSEQ, TC, B_W), blk),
                  pl.BlockSpec((1, B_W), fixed),
                  vec, pl.BlockSpec((LORA, MIX_B), fixed),
                  vec, pl.BlockSpec((LORA, MIX_B), fixed),
                  vec, vec, vec, vec, vec,
                  pl.BlockSpec((MIX_B, MIX_B), fixed)],
        out_specs=[pl.BlockSpec((NSEQ, TC, MIX_B), blk),
                   pl.BlockSpec((NSEQ, HB, HS_B, HS_B), lambda i, j: (i, 0, 0, 0)),
                   pl.BlockSpec((NSEQ, 1, B_W), lambda i, j: (i, 0, 0))],
        out_shape=[jax.ShapeDtypeStruct((b, t, MIX_B), BF16),
                   jax.ShapeDtypeStruct((b, HB, HS_B, HS_B), F32),
                   jax.ShapeDtypeStruct((b, 1, B_W), F32)],
        scratch_shapes=[pltpu.VMEM((NSEQ, 8, B_W), F32),
                        pltpu.VMEM((NSEQ, HB // 2, HS_B, PAIR), F32)],
        compiler_params=pltpu.CompilerParams(dimension_semantics=("arbitrary", "arbitrary"),
                                             vmem_limit_bytes=VMEM_LIMIT),
        name="rwkv_prompt",
    )(pb, mu, w0, w2, a0, a2, k_k, k_a, rk, lnw, lnb, ones_blk)


def _pad_rows(rows, width):
    return jnp.concatenate(rows + [jnp.zeros((8 - len(rows), width), F32)], axis=0)


def _gdn_step_kernel(pa_ref, bd_ref, sc_ref, s_ref, cw_ref, hp_ref, anw_ref,
                     o_ref, s_out_ref, c_out_ref):
    u = pa_ref[:, 0:A_QKV]
    cw = cw_ref[...]
    conv = (sc_ref[0] * cw[0:1, :] + sc_ref[1] * cw[1:2, :] + sc_ref[2] * cw[2:3, :] + u * cw[3:4, :])
    qkv = _silu(conv)
    c_out_ref[0] = sc_ref[1]
    c_out_ref[1] = sc_ref[2]
    c_out_ref[2] = u

    qn, kn = _gdn_qk(qkv)
    beta_all, g_all = _gdn_gates(bd_ref[...], hp_ref[...])
    eg_all = jnp.exp(g_all)
    lanes_of = lambda h: slice(h * DK_A, (h + 1) * DK_A)
    q = [qn[:, lanes_of(h)] for h in range(HA)]
    k = [kn[:, lanes_of(h)] for h in range(HA)]
    v = [qkv[:, 2 * MIX_A + h * DK_A:2 * MIX_A + (h + 1) * DK_A] for h in range(HA)]
    beta = [beta_all[:, h:h + 1] for h in range(HA)]
    eg = [eg_all[:, HA + h:HA + h + 1] for h in range(HA)]

    bh = [(b, h) for b in range(SB) for h in range(HA)]
    s0 = [s_ref[b, h] for b, h in bh]
    ks_qs = [_dot(_pad_rows([k[h][b:b + 1, :], q[h][b:b + 1, :]], DK_A), s) for (b, h), s in zip(bh, s0)]
    o_heads = []
    for h in range(HA):
        ks = jnp.concatenate([ks_qs[b * HA + h][0:1, :] for b in range(SB)], axis=0)
        qs = jnp.concatenate([ks_qs[b * HA + h][1:2, :] for b in range(SB)], axis=0)
        u2 = beta[h] * (v[h] - eg[h] * ks)
        for b in range(SB):
            s_out_ref[b, h] = (eg[h][b:b + 1, :] * s0[b * HA + h]
                               + _dot(_pad_rows([k[h][b:b + 1, :]], DK_A),
                                      _pad_rows([u2[b:b + 1, :]], DK_A), TN))
        o_heads.append(eg[h] * qs + jnp.sum(q[h] * k[h], axis=-1, keepdims=True) * u2)

    o_ref[...] = _gdn_out(jnp.concatenate(o_heads, axis=-1), pa_ref[:, A_QKV:A_MAIN], anw_ref[...])


def _gdn_step_specs(n):
    row = lambda i: (i, 0)
    fixed = lambda i: (0, 0)
    state = pl.BlockSpec((SB, HA, DK_A, DK_A), lambda i: (i, 0, 0, 0))
    conv = pl.BlockSpec((CONV_W - 1, SB, A_QKV), lambda i: (0, i, 0))
    in_specs = [pl.BlockSpec((SB, A_MAIN), row),
                pl.BlockSpec((SB, BD_W), row),
                conv, state,
                pl.BlockSpec((CONV_W, A_QKV), fixed),
                pl.BlockSpec((8, BD_W), fixed),
                pl.BlockSpec((1, MIX_A), fixed)]
    out_specs = [pl.BlockSpec((SB, MIX_A), row), state, conv]
    out_shape = [jax.ShapeDtypeStruct((n, MIX_A), F32),
                 jax.ShapeDtypeStruct((n, HA, DK_A, DK_A), F32),
                 jax.ShapeDtypeStruct((CONV_W - 1, n, A_QKV), F32)]
    return in_specs, out_specs, out_shape


def _rwkv_step_lanes_kernel(pb_ref, sh_ref, s_ref, mu_ref, w0_ref, w2_ref, a0_ref, a2_ref, kk_ref, ka_ref,
                            rk_ref, lnw_ref, lnb_ref, ones_ref,
                            o_ref, s_out_ref, vec_scr, row_scr, y_scr):
    h = pl.program_id(0)
    ones_blk = ones_ref[...]

    @pl.when(h == 0)
    def _():
        pb = pb_ref[...]
        xb = pb + (sh_ref[...] - pb) * mu_ref[...]
        r, kmod, v, gate, logw, a, kk = _rwkv_front(
            xb, w0_ref[...], w2_ref[...], a0_ref[...], a2_ref[...], kk_ref[...], ka_ref[...], ones_blk)
        for slot, z in enumerate((-kk, kk * a, kmod, v, r, jnp.exp(logw))):
            vec_scr[slot] = z.T
        for slot, z in enumerate((r, kmod, v, gate)):
            row_scr[slot] = z

    base = pl.multiple_of(h * HS_B, HS_B)
    chan = pl.ds(base, HS_B)
    nkk = vec_scr[0, chan, :]
    kka = vec_scr[1, chan, :]
    kmod_t = vec_scr[2, chan, :]
    r_t = vec_scr[4, chan, :]
    wdec = vec_scr[5, chan, :]

    def body(vi, carry):
        s0 = s_ref[0, vi]
        sa = jnp.sum(s0 * nkk, axis=0, keepdims=True)
        s1 = s0 * wdec + sa * kka + vec_scr[3, pl.ds(base + vi, 1), :] * kmod_t
        s_out_ref[0, vi] = s1
        y_scr[pl.ds(base + vi, 1), :] = jnp.sum(s1 * r_t, axis=0, keepdims=True)
        return carry

    lax.fori_loop(0, HS_B, body, 0, unroll=4)

    @pl.when(h == HB - 1)
    def _():
        o_ref[...] = _rwkv_back(y_scr[...].T, row_scr[0], row_scr[1], row_scr[2], row_scr[3],
                                rk_ref[...], lnw_ref[...], lnb_ref[...], ones_blk)


def _rwkv_step_specs(n):
    fixed = lambda i: (0, 0)
    vec = pl.BlockSpec((1, MIX_B), fixed)
    state = pl.BlockSpec((1, HS_B, HS_B, n), lambda i: (i, 0, 0, 0))
    in_specs = [pl.BlockSpec((n, B_W), fixed),
                pl.BlockSpec((n, B_W), fixed),
                state,
                pl.BlockSpec((1, B_W), fixed),
                vec, pl.BlockSpec((LORA, MIX_B), fixed),
                vec, pl.BlockSpec((LORA, MIX_B), fixed),
                vec, vec, vec, vec, vec,
                pl.BlockSpec((MIX_B, MIX_B), fixed)]
    out_specs = [pl.BlockSpec((n, MIX_B), fixed), state]
    out_shape = [jax.ShapeDtypeStruct((n, MIX_B), F32),
                 jax.ShapeDtypeStruct((HB, HS_B, HS_B, n), F32)]
    return in_specs, out_specs, out_shape


def _sample_step(a_args, b_args, xs2d, w_out, final_w):
    n = a_args[0].shape[0]
    assert n // SB == HB
    a_in, a_out, a_shape = _gdn_step_specs(n)
    b_in, b_out, b_shape = _rwkv_step_specs(n)
    na, nb = len(a_in), len(b_in)
    fixed = lambda i: (0, 0)
    o_in = [pl.BlockSpec(xs2d.shape, fixed), pl.BlockSpec((D_MODEL, D_MODEL), fixed), pl.BlockSpec((1, D_MODEL), fixed)]
    a_in[A_STATE_ARG] = pl.BlockSpec(memory_space=pl.ANY)
    b_in[B_STATE_ARG] = pl.BlockSpec(memory_space=pl.ANY)

    def body(*refs):
        a_refs, b_refs, o_refs = list(refs[:na]), list(refs[na:na + nb]), refs[na + nb:na + nb + 3]
        sa_out, ca_out, sb_out, y_ref = refs[na + nb + 3:na + nb + 7]
        vec_scr, row_scr, y_scr, mixa_scr, mixb_scr, ring_a, ring_b, sem_a, sem_b = refs[na + nb + 7:]
        i = pl.program_id(0)
        sa_hbm, sb_hbm = a_refs[A_STATE_ARG], b_refs[B_STATE_ARG]

        def fetch(step):
            slot = step % RING
            return (pltpu.make_async_copy(sa_hbm.at[pl.ds(step * SB, SB)], ring_a.at[slot], sem_a.at[slot]),
                    pltpu.make_async_copy(sb_hbm.at[pl.ds(step, 1)], ring_b.at[slot], sem_b.at[slot]))

        @pl.when(i == 0)
        def _():
            for step in range(RING - 1):
                for cp in fetch(step):
                    cp.start()

        @pl.when(i + RING - 1 < HB)
        def _():
            for cp in fetch(i + RING - 1):
                cp.start()

        for cp in fetch(i):
            cp.wait()
        a_refs[A_STATE_ARG] = ring_a.at[i % RING]
        b_refs[B_STATE_ARG] = ring_b.at[i % RING]
        _gdn_step_kernel(*a_refs, mixa_scr.at[pl.ds(pl.multiple_of(i * SB, SB), SB), :], sa_out, ca_out)
        _rwkv_step_lanes_kernel(*b_refs, mixb_scr, sb_out, vec_scr, row_scr, y_scr)

        @pl.when(i == HB - 1)
        def _():
            _outproj_kernel(o_refs[0], mixa_scr, mixb_scr, o_refs[1], o_refs[2], y_ref)

    return pl.pallas_call(
        body,
        grid=(HB,),
        in_specs=a_in + b_in + o_in,
        out_specs=a_out[1:] + b_out[1:] + [pl.BlockSpec(xs2d.shape, fixed)],
        out_shape=a_shape[1:] + b_shape[1:] + [jax.ShapeDtypeStruct(xs2d.shape, F32)],
        scratch_shapes=[pltpu.VMEM((6, MIX_B, n), F32),
                        pltpu.VMEM((4, n, MIX_B), F32),
                        pltpu.VMEM((MIX_B, n), F32),
                        pltpu.VMEM((n, MIX_A), F32),
                        pltpu.VMEM((n, MIX_B), F32),
                        pltpu.VMEM((RING, SB, HA, DK_A, DK_A), F32),
                        pltpu.VMEM((RING, 1, HS_B, HS_B, n), F32),
                        pltpu.SemaphoreType.DMA((RING,)),
                        pltpu.SemaphoreType.DMA((RING,))],
        compiler_params=pltpu.CompilerParams(dimension_semantics=("arbitrary",),
                                             vmem_limit_bytes=VMEM_LIMIT),
        name="sample_step",
    )(*a_args, *b_args, xs2d, w_out, final_w)


def _block_ones(width, group):
    idx = jnp.arange(width) // group
    return (idx[:, None] == idx[None, :]).astype(BF16)


def kernel(x_prompt, x_sample, state_a_mat, state_a_conv, state_b_mat, state_b_shift, norm_w, w_in,
           conv_w, a_log, dt_bias, a_norm_w, mu, w0, w2, a0, a2, k_k, k_a, r_k, ln_w, ln_b, w_out,
           final_norm_w):
    bsz, seq, _ = x_prompt.shape
    nsmp = x_sample.shape[0]
    assert bsz % NSEQ == 0 and bsz % NSEQ_A == 0 and seq % TC == 0 and seq % TC_A == 0
    assert nsmp % SB == 0 and x_sample.shape[1] == 1

    w_all = jnp.transpose(w_in[0])
    w_o = w_out[0].astype(BF16)
    hp = jnp.pad(jnp.concatenate([a_log, dt_bias], axis=0), ((0, 6), (HA, BD_W - 2 * HA)))
    nw = norm_w[0].reshape(1, D_MODEL)
    fw = final_norm_w.reshape(1, D_MODEL)
    anw = jnp.tile(a_norm_w[0].reshape(1, DK_A), (1, HA))
    ones_b = _block_ones(MIX_B, HS_B)
    row = lambda z: z[0].reshape(1, -1)
    b_params = (row(mu), row(w0), w2[0], row(a0), a2[0], row(k_k), row(k_a), row(r_k), row(ln_w), row(ln_b),
                ones_b)

    xp = x_prompt.reshape(bsz * seq, D_MODEL)
    xs = x_sample.reshape(nsmp * D_MODEL // LANE, LANE)
    pa, pb, bd, sa_, sb_, sbd = _inproj(xp, xs, nw, w_all)

    ob, p_bmat, p_bshift = _rwkv_prompt(pb.reshape(bsz, seq, B_W), *b_params)
    y_prompt, p_amat, p_aconv = _gdn_prompt(pa.reshape(bsz, seq, A_MAIN), bd.reshape(bsz, seq, BD_W),
                                            conv_w[0], hp, anw, x_prompt, ob, w_o, fw)

    sconv_t = jnp.swapaxes(state_a_conv[0], 0, 1)
    s_amat, s_aconv_t, s_bmat_t, y_sample = _sample_step(
        (sa_, sbd, sconv_t, state_a_mat[0], conv_w[0], hp, anw),
        (sb_, state_b_shift[0], jnp.transpose(state_b_mat[0], (1, 2, 3, 0))) + b_params,
        xs, w_o, fw)
    s_bmat = jnp.transpose(s_bmat_t, (3, 0, 1, 2))

    return (y_prompt, y_sample.reshape(nsmp, 1, D_MODEL),
            p_amat[None], p_aconv[None], p_bmat[None], p_bshift.reshape(1, bsz, B_W),
            s_amat[None], jnp.swapaxes(s_aconv_t, 0, 1)[None], s_bmat[None], sb_[None])
```

```python
import jax
import jax.numpy as jnp
from jax import lax
from jax.experimental import pallas as pl
from jax.experimental.pallas import tpu as pltpu

F32 = jnp.float32
BF16 = jnp.bfloat16

D_MODEL = 1024
MIX_A = 512
MIX_B = 512
HA = 4
DK_A = 128
HB = 8
HS_B = 64
LANE = 128
PAIR = 2 * HS_B
CONV_W = 4
LORA = 64
A_QKV = 3 * MIX_A
A_MAIN = A_QKV + MIX_A
B_W = 4 * MIX_B + 2 * LORA
BD_W = 128
RMS_EPS = 1e-6
GN_EPS = 64e-5
CHUNK = 64
TC = 256
TC_A = 256
NSEQ_A = 2
NSEQ = 4
SB = 16
RING = 3
A_STATE_ARG = 3
B_STATE_ARG = 2
VMEM_LIMIT = 56 * 1024 * 1024

EXP_NEG_HALF = 0.6065306597126334
NN = (((1,), (0,)), ((), ()))
NT = (((1,), (1,)), ((), ()))
TN = (((0,), (0,)), ((), ()))


def _dot(a, b, dn=NN):
    return lax.dot_general(a, b, dn, preferred_element_type=F32)


def _sigmoid(x):
    return 0.5 * jnp.tanh(0.5 * x) + 0.5


def _silu(x):
    h = 0.5 * x
    return h * jnp.tanh(h) + h


def _softplus(x):
    return jnp.maximum(x, 0.0) + jnp.log(1.0 + jnp.exp(-jnp.abs(x)))


def _l2norm(x):
    return x * lax.rsqrt(jnp.sum(x * x, axis=-1, keepdims=True) + 1e-12)


def _group_sum(x, ones_blk):
    return _dot(x.astype(BF16), ones_blk)


def _iota2(shape, dim):
    return lax.broadcasted_iota(jnp.int32, shape, dim)


def _chunk_cumsum(x):
    n = x.shape[0]
    r = _iota2((n, n), 0)
    c = _iota2((n, n), 1)
    tril = jnp.where(((r // CHUNK) == (c // CHUNK)) & (c <= r), 1.0, 0.0).astype(BF16)
    hi = x.astype(BF16)
    rest = x - hi.astype(F32)
    mid = rest.astype(BF16)
    lo = (rest - mid.astype(F32)).astype(BF16)
    return _dot(tril, hi) + _dot(tril, mid) + _dot(tril, lo)


def _tri_inv_many(lows):
    n = lows[0].shape[0]
    r = _iota2((n, n), 0)
    c = _iota2((n, n), 1)
    eye = jnp.where(r == c, 1.0, 0.0).astype(F32)
    base = (r // 8) == (c // 8)
    l0 = [jnp.where(base, low, 0.0) for low in lows]
    l2 = [_dot(a, a) for a in l0]
    l4 = [_dot(a, a) for a in l2]
    d = [_dot(eye - a, eye + b) for a, b in zip(l0, l2)]
    d = [_dot(a, eye + b) for a, b in zip(d, l4)]
    s = 8
    while s < n:
        sel = ((r // (2 * s)) == (c // (2 * s))) & ((r // s) != (c // s))
        t = [_dot(jnp.where(sel, low, 0.0), a) for low, a in zip(lows, d)]
        d = [a - _dot(a, b) for a, b in zip(d, t)]
        s *= 2
    return d


def _pair_diag(x):
    left = _iota2(x.shape, 1) < x.shape[1] // 2
    return jnp.concatenate([jnp.where(left, x, 0.0), jnp.where(left, 0.0, x)], axis=0)


def _tri_inv_pairs(lows):
    n = lows[0].shape[0]
    r = _iota2((n, 2 * n), 0)
    lane = _iota2((n, 2 * n), 1)
    c = jnp.where(lane >= n, lane - n, lane)
    mul = lambda a, b: _dot(a, _pair_diag(b))
    eye = jnp.where(r == c, 1.0, 0.0).astype(F32)
    base = (r // 8) == (c // 8)
    l0 = [jnp.where(base, low, 0.0) for low in lows]
    l2 = [mul(a, a) for a in l0]
    l4 = [mul(a, a) for a in l2]
    d = [mul(eye - a, eye + b) for a, b in zip(l0, l2)]
    d = [mul(a, eye + b) for a, b in zip(d, l4)]
    s = 8
    while s < n:
        sel = ((r // (2 * s)) == (c // (2 * s))) & ((r // s) != (c // s))
        t = [mul(jnp.where(sel, low, 0.0), a) for low, a in zip(lows, d)]
        d = [a - mul(a, b) for a, b in zip(d, t)]
        s *= 2
    return d


def _sample_rows(ref, n):
    pieces = D_MODEL // LANE
    return jnp.concatenate([ref[pl.ds(c, n, stride=pieces), :] for c in range(pieces)], axis=1)


def _inproj_kernel(x_hbm, xs_ref, nw_ref, wt_ref, oa_ref, ob_ref, obd_ref, sa_ref, sb_ref, sbd_ref,
                   w_scr, x_ring, x_sem):
    a_w = A_MAIN + 2 * HA
    i = pl.program_id(0)
    tm = x_ring.shape[1]

    def fetch(step):
        slot = step % RING
        return pltpu.make_async_copy(x_hbm.at[pl.ds(step * tm, tm)], x_ring.at[slot], x_sem.at[slot])

    @pl.when(i == 0)
    def _():
        for step in range(RING - 1):
            fetch(step).start()

    @pl.when(i + RING - 1 < pl.num_programs(0))
    def _():
        fetch(i + RING - 1).start()

    def project(x, pa_ref, pb_ref, pbd_ref):
        xn = (x * lax.rsqrt(jnp.mean(x * x, axis=-1, keepdims=True) + RMS_EPS) * nw_ref[...]).astype(BF16)
        pa_ref[...] = _dot(xn, w_scr[0:A_MAIN, :], NT)
        pb = _dot(xn, w_scr[A_MAIN:, :], NT)
        pb_ref[...] = pb[:, :B_W]
        pbd_ref[...] = pb[:, B_W:]

    @pl.when(pl.program_id(0) == 0)
    def _():
        w_scr[0:A_MAIN, :] = wt_ref[0:A_MAIN, :].astype(BF16)
        w_scr[A_MAIN:A_MAIN + B_W, :] = wt_ref[a_w:, :].astype(BF16)
        w_scr[A_MAIN + B_W:, :] = jnp.concatenate(
            [wt_ref[A_MAIN:a_w, :], jnp.zeros((BD_W - 2 * HA, D_MODEL), F32)], axis=0).astype(BF16)
        project(_sample_rows(xs_ref, sa_ref.shape[0]), sa_ref, sb_ref, sbd_ref)

    fetch(i).wait()
    project(x_ring[i % RING], oa_ref, ob_ref, obd_ref)


def _inproj(x2d, xs2d, norm_w, w_all):
    n, ns = x2d.shape[0], xs2d.shape[0] * LANE // D_MODEL
    tm = min(512, n)
    row = lambda i: (i, 0)
    fixed = lambda i: (0, 0)
    widths = (A_MAIN, B_W, BD_W)
    return pl.pallas_call(
        _inproj_kernel,
        grid=(n // tm,),
        in_specs=[pl.BlockSpec(memory_space=pl.ANY),
                  pl.BlockSpec(xs2d.shape, fixed),
                  pl.BlockSpec((1, D_MODEL), fixed),
                  pl.BlockSpec(w_all.shape, fixed)],
        out_specs=[pl.BlockSpec((tm, w), row) for w in widths] + [pl.BlockSpec((ns, w), fixed) for w in widths],
        out_shape=[jax.ShapeDtypeStruct((n, w), F32) for w in widths]
                  + [jax.ShapeDtypeStruct((ns, w), F32) for w in widths],
        scratch_shapes=[pltpu.VMEM((A_MAIN + B_W + BD_W, D_MODEL), BF16),
                        pltpu.VMEM((RING, tm, D_MODEL), F32),
                        pltpu.SemaphoreType.DMA((RING,))],
        compiler_params=pltpu.CompilerParams(dimension_semantics=("arbitrary",),
                                             vmem_limit_bytes=VMEM_LIMIT),
        name="inproj",
    )(x2d, xs2d, norm_w, w_all)


def _outproj_kernel(x_ref, ma_ref, mb_ref, w_ref, fw_ref, y_ref):
    n = ma_ref.shape[0]
    h = (_sample_rows(x_ref, n)
         + _dot(ma_ref[...].astype(BF16), w_ref[0:MIX_A, :])
         + _dot(mb_ref[...].astype(BF16), w_ref[MIX_A:, :]))
    y = h * lax.rsqrt(jnp.mean(h * h, axis=-1, keepdims=True) + RMS_EPS) * fw_ref[...]
    pieces = D_MODEL // LANE
    for c in range(pieces):
        y_ref[pl.ds(c, n, stride=pieces), :] = y[:, c * LANE:(c + 1) * LANE]


def _gdn_gates(bd, hp):
    beta = _sigmoid(bd)
    g = -jnp.exp(hp[0:1, :]) * _softplus(bd + hp[1:2, :])
    return beta, g


def _gdn_qk(qkv):
    heads = lambda z: [z[:, h * DK_A:(h + 1) * DK_A] for h in range(HA)]
    qn = jnp.concatenate([_l2norm(x) * (DK_A ** -0.5) for x in heads(qkv[:, 0:MIX_A])], axis=-1)
    kn = jnp.concatenate([_l2norm(x) for x in heads(qkv[:, MIX_A:2 * MIX_A])], axis=-1)
    return qn, kn


def _gdn_out(o, gate, anw):
    on = jnp.concatenate(
        [x * lax.rsqrt(jnp.mean(x * x, axis=-1, keepdims=True) + RMS_EPS)
         for x in [o[:, h * DK_A:(h + 1) * DK_A] for h in range(HA)]], axis=-1)
    return on * anw * _silu(gate)


def _gdn_prompt_kernel(pa_ref, bd_ref, cw_ref, hp_ref, anw_ref, x_ref, mb_ref, wo_ref, fw_ref,
                       y_ref, s_out_ref, c_out_ref, ext_ref, s_ref):
    c = pl.program_id(1)

    nseq, tc = pa_ref.shape[0], pa_ref.shape[1]

    @pl.when(c == 0)
    def _():
        ext_ref[...] = jnp.zeros_like(ext_ref)
        s_ref[...] = jnp.zeros_like(s_ref)

    cw = cw_ref[...]
    conv = []
    for s in range(nseq):
        u = pa_ref[s, :, 0:A_QKV]
        full = jnp.concatenate([ext_ref[s], u], axis=0)
        conv.append(u * cw[3:4, :] + sum(
            pltpu.roll(full, CONV_W - 1 - i, 0)[8:, :] * cw[i:i + 1, :] for i in range(CONV_W - 1)))
        ext_ref[s] = u[tc - 8:, :]
        c_out_ref[s] = ext_ref[s, 8 - (CONV_W - 1):, :]
    qkv = _silu(jnp.concatenate(conv, axis=0))

    qn, kn = _gdn_qk(qkv)
    beta_all, g_all = _gdn_gates(bd_ref[...].reshape(nseq * tc, BD_W), hp_ref[...])
    gcum = jnp.concatenate([_chunk_cumsum(g_all[s * tc:(s + 1) * tc, :]) for s in range(nseq)],
                           axis=0)
    gcum_t = gcum.T

    nrow = nseq * tc
    wide = lambda z, first: jnp.concatenate(
        [jnp.broadcast_to(z[:, first + h:first + h + 1], (nrow, DK_A)) for h in range(HA)], axis=1)
    bw, gw = wide(beta_all, 0), wide(gcum, HA)
    half = _iota2((nrow, PAIR), 1) < CHUNK
    narrow = lambda zw: jnp.concatenate(
        [jnp.where(half, zw[:, 2 * p * DK_A:(2 * p + 1) * DK_A], zw[:, (2 * p + 1) * DK_A:(2 * p + 2) * DK_A])
         for p in range(HA // 2)], axis=1)
    b64, g64 = narrow(bw), narrow(gw)
    v_all = qkv[:, 2 * MIX_A:]
    egw = jnp.exp(gw)
    bek = bw * egw * kn
    bv = bw * v_all
    qg = qn * egw

    rowp = _iota2((CHUNK, PAIR), 0)
    lane = _iota2((CHUNK, PAIR), 1)
    colp = jnp.where(lane >= CHUNK, lane - CHUNK, lane)
    incl = colp <= rowp
    strict = colp < rowp

    nj = nseq * tc // CHUNK
    npair = HA // 2
    jp = [(j, p) for j in range(nj) for p in range(npair)]
    jh = [(j, h) for j in range(nj) for h in range(HA)]
    rows_of = lambda j: slice(j * CHUNK, (j + 1) * CHUNK)
    lanes_of = lambda h: slice(h * DK_A, (h + 1) * DK_A)
    pair64 = lambda p: slice(p * PAIR, (p + 1) * PAIR)
    pairw = lambda p: slice(2 * p * DK_A, 2 * (p + 1) * DK_A)
    glast = [gw[(j + 1) * CHUNK - 1:(j + 1) * CHUNK, :] for j in range(nj)]
    kdec = [kn[rows_of(j), :] * jnp.exp(glast[j] - gw[rows_of(j), :]) for j in range(nj)]
    eglast = [jnp.exp(glast[j]) for j in range(nj)]
    gr = [jnp.concatenate([gcum_t[HA + 2 * p:HA + 2 * p + 1, rows_of(j)],
                           gcum_t[HA + 2 * p + 1:HA + 2 * p + 2, rows_of(j)]], axis=1) for j, p in jp]
    decay = [jnp.where(incl, jnp.exp(jnp.where(incl, g64[rows_of(j), pair64(p)] - b, 0.0)), 0.0)
             for (j, p), b in zip(jp, gr)]
    kq = [_dot(jnp.concatenate([kn[rows_of(j), pairw(p)], qn[rows_of(j), pairw(p)]], axis=0),
               _pair_diag(kn[rows_of(j), pairw(p)]), NT) for j, p in jp]
    low = [jnp.where(strict, b64[rows_of(j), pair64(p)] * d * m[0:CHUNK, :], 0.0)
           for (j, p), d, m in zip(jp, decay, kq)]
    tinv = _tri_inv_pairs(low)
    zw = jnp.zeros((CHUNK, 2 * DK_A), F32)
    wu = [_dot(t, jnp.concatenate(
        [jnp.concatenate([bek[rows_of(j), lanes_of(2 * p)], bv[rows_of(j), lanes_of(2 * p)], zw], axis=1),
         jnp.concatenate([zw, bek[rows_of(j), lanes_of(2 * p + 1)], bv[rows_of(j), lanes_of(2 * p + 1)]],
                         axis=1)], axis=0))
          for t, (j, p) in zip(tinv, jp)]
    qwu_p = [_dot(d * m[CHUNK:, :], _pair_diag(x)) for d, m, x in zip(decay, kq, wu)]
    head_of = lambda z, j, h: z[j * npair + h // 2][:, (h % 2) * 2 * DK_A:(h % 2 + 1) * 2 * DK_A]
    qwu = [head_of(qwu_p, j, h) for j, h in jh]
    kwu = [_dot(kdec[j][:, lanes_of(h)], head_of(wu, j, h), TN) for j, h in jh]
    rd = _iota2((DK_A, DK_A), 0)
    cd = _iota2((DK_A, DK_A), 1)
    lhs = [jnp.concatenate([jnp.where(rd == cd, eglast[j][:, lanes_of(h)], 0.0) - kx[:, 0:DK_A],
                            qg[rows_of(j), lanes_of(h)] - qx[:, 0:DK_A]], axis=0)
           for (j, h), kx, qx in zip(jh, kwu, qwu)]

    o_rows = [None] * nj
    for jj in range(tc // CHUNK):
        for s in range(nseq):
            j = s * (tc // CHUNK) + jj
            idx = [j * HA + h for h in range(HA)]
            res = [_dot(lhs[i], s_ref[s, h]) for h, i in enumerate(idx)]
            for h, i in enumerate(idx):
                s_ref[s, h] = res[h][0:DK_A, :] + kwu[i][:, DK_A:]
            o_rows[j] = jnp.concatenate(
                [res[h][DK_A:, :] + qwu[i][:, DK_A:] for h, i in enumerate(idx)], axis=-1)

    gate = pa_ref[:, :, A_QKV:A_MAIN].reshape(nseq * tc, MIX_A)
    mix_a = _gdn_out(jnp.concatenate(o_rows, axis=0), gate, anw_ref[...])
    h = (x_ref[...].reshape(nseq * tc, D_MODEL)
         + _dot(mb_ref[...].reshape(nseq * tc, MIX_B), wo_ref[MIX_A:, :])
         + _dot(mix_a.astype(BF16), wo_ref[0:MIX_A, :]))
    y = h * lax.rsqrt(jnp.mean(h * h, axis=-1, keepdims=True) + RMS_EPS) * fw_ref[...]
    y_ref[...] = y.reshape(nseq, tc, D_MODEL)
    s_out_ref[...] = s_ref[...]


def _gdn_prompt(pa, bd, conv_w, hp, anw, x, mix_b, w_out, final_w):
    b, t, _ = pa.shape
    blk = lambda i, j: (i, j, 0)
    fixed = lambda i, j: (0, 0)
    return pl.pallas_call(
        _gdn_prompt_kernel,
        grid=(b // NSEQ_A, t // TC_A),
        in_specs=[pl.BlockSpec((NSEQ_A, TC_A, A_MAIN), blk),
                  pl.BlockSpec((NSEQ_A, TC_A, BD_W), blk),
                  pl.BlockSpec((CONV_W, A_QKV), fixed),
                  pl.BlockSpec((8, BD_W), fixed),
                  pl.BlockSpec((1, MIX_A), fixed),
                  pl.BlockSpec((NSEQ_A, TC_A, D_MODEL), blk),
                  pl.BlockSpec((NSEQ_A, TC_A, MIX_B), blk),
                  pl.BlockSpec((D_MODEL, D_MODEL), fixed),
                  pl.BlockSpec((1, D_MODEL), fixed)],
        out_specs=[pl.BlockSpec((NSEQ_A, TC_A, D_MODEL), blk),
                   pl.BlockSpec((NSEQ_A, HA, DK_A, DK_A), lambda i, j: (i, 0, 0, 0)),
                   pl.BlockSpec((NSEQ_A, CONV_W - 1, A_QKV), lambda i, j: (i, 0, 0))],
        out_shape=[jax.ShapeDtypeStruct((b, t, D_MODEL), F32),
                   jax.ShapeDtypeStruct((b, HA, DK_A, DK_A), F32),
                   jax.ShapeDtypeStruct((b, CONV_W - 1, A_QKV), F32)],
        scratch_shapes=[pltpu.VMEM((NSEQ_A, 8, A_QKV), F32),
                        pltpu.VMEM((NSEQ_A, HA, DK_A, DK_A), F32)],
        compiler_params=pltpu.CompilerParams(dimension_semantics=("arbitrary", "arbitrary"),
                                             vmem_limit_bytes=VMEM_LIMIT),
        name="gdn_prompt",
    )(pa, bd, conv_w, hp, anw, x, mix_b, w_out, final_w)


def _rwkv_front(xb, w0, w2, a0, a2, k_k, k_a, ones_blk):
    r = xb[:, 0:MIX_B]
    kr = xb[:, MIX_B:2 * MIX_B]
    vr = xb[:, 2 * MIX_B:3 * MIX_B]
    gate = xb[:, 3 * MIX_B:4 * MIX_B]
    wd = xb[:, 4 * MIX_B:4 * MIX_B + LORA]
    ad = xb[:, 4 * MIX_B + LORA:]
    logw = -(EXP_NEG_HALF * _sigmoid(w0 + _dot(jnp.tanh(wd), w2)))
    a = _sigmoid(a0 + _dot(ad, a2))
    kraw = kr * k_k
    kk = kraw * lax.rsqrt(_group_sum(kraw * kraw, ones_blk) + 1e-12)
    kmod = kr * (1.0 + (a - 1.0) * k_a)
    return r, kmod, vr, gate, logw, a, kk


def _rwkv_back(y, r, kmod, v, gate, rk, lnw, lnb, ones_blk):
    inv = 1.0 / HS_B
    yc = y - _group_sum(y, ones_blk) * inv
    var = _group_sum(yc * yc, ones_blk) * inv
    gn = yc * lax.rsqrt(var + GN_EPS) * lnw + lnb
    bonus = _group_sum(r * kmod * rk, ones_blk) * v
    return (gn + bonus) * _silu(gate)


def _rwkv_prompt_kernel(pb_ref, mu_ref, w0_ref, w2_ref, a0_ref, a2_ref, kk_ref, ka_ref,
                        rk_ref, lnw_ref, lnb_ref, ones_ref,
                        o_ref, s_out_ref, sh_out_ref, carry_ref, s_ref):
    c = pl.program_id(1)

    @pl.when(c == 0)
    def _():
        carry_ref[...] = jnp.zeros_like(carry_ref)
        s_ref[...] = jnp.zeros_like(s_ref)

    ones_blk = ones_ref[...]
    pb = pb_ref[...].reshape(NSEQ * TC, B_W)
    prev = []
    for s in range(NSEQ):
        pb_s = pb_ref[s]
        prev.append(pltpu.roll(jnp.concatenate([carry_ref[s], pb_s], axis=0), 1, 0)[8:, :])
        carry_ref[s] = pb_s[TC - 8:, :]
        sh_out_ref[s] = pb_s[TC - 1:, :]
    prev = jnp.concatenate(prev, axis=0)
    xb = pb + (prev - pb) * mu_ref[...]
    r, kmod, v, gate, logw, a, kk = _rwkv_front(
        xb, w0_ref[...], w2_ref[...], a0_ref[...], a2_ref[...], kk_ref[...], ka_ref[...], ones_blk)

    g = jnp.concatenate([_chunk_cumsum(logw[s * TC:(s + 1) * TC, :]) for s in range(NSEQ)],
                        axis=0)
    eg = jnp.exp(g)
    egn = jnp.exp(-g)
    ag = -kk * jnp.exp(g - logw)
    kka = kk * a
    bg = kka * egn
    kg = kmod * egn
    rg = r * eg

    rowp = _iota2((CHUNK, PAIR), 0)
    lane = _iota2((CHUNK, PAIR), 1)
    colp = jnp.where(lane >= HS_B, lane - HS_B, lane)
    left = lane < HS_B
    incl = colp <= rowp
    strict = colp < rowp
    eye = colp == rowp
    zero = jnp.zeros((CHUNK, PAIR), F32)
    pack = lambda f: jnp.where(left, f[0:HS_B, :], f[HS_B:, :])

    nj = NSEQ * TC // CHUNK
    npair = HB // 2
    jp = [(j, p) for j in range(nj) for p in range(npair)]
    rows_of = lambda j: slice(j * CHUNK, (j + 1) * CHUNK)
    lanes_of = lambda p: slice(p * PAIR, (p + 1) * PAIR)
    glast = [g[(j + 1) * CHUNK - 1:(j + 1) * CHUNK, :] for j in range(nj)]
    eglast = [jnp.exp(glast[j]) for j in range(nj)]
    bdec = [bg[rows_of(j), :] * eglast[j] for j in range(nj)]
    kdec = [kg[rows_of(j), :] * eglast[j] for j in range(nj)]

    cut = lambda z: [z[rows_of(j), lanes_of(p)] for j, p in jp]
    ag_p, rg_p, bg_p, kg_p, v_p = cut(ag), cut(rg), cut(bg), cut(kg), cut(v)
    bdec_p = [bdec[j][:, lanes_of(p)] for j, p in jp]
    kdec_p = [kdec[j][:, lanes_of(p)] for j, p in jp]
    eglast_p = [eglast[j][:, lanes_of(p)] for j, p in jp]
    amat = [_dot(jnp.concatenate([a, b], axis=0),
                 jnp.concatenate([_pair_diag(c_), _pair_diag(d)], axis=0), NT)
            for a, b, c_, d in zip(ag_p, rg_p, bg_p, kg_p)]
    a_ab = [jnp.where(strict, m[0:CHUNK, 0:PAIR], 0.0) for m in amat]
    a_ak = [jnp.where(strict, m[0:CHUNK, PAIR:], 0.0) for m in amat]
    a_rb = [jnp.where(incl, m[CHUNK:, 0:PAIR], 0.0) for m in amat]
    a_rk = [jnp.where(incl, m[CHUNK:, PAIR:], 0.0) for m in amat]
    tinv = _tri_inv_pairs([-a for a in a_ab])
    kv = [_dot(jnp.concatenate([a, b], axis=0), _pair_diag(c_)) for a, b, c_ in zip(a_ak, a_rk, v_p)]
    wu = [_dot(t, jnp.concatenate([_pair_diag(a), _pair_diag(x[0:CHUNK, :])], axis=1))
          for t, a, x in zip(tinv, ag_p, kv)]
    rwu = [_dot(a, jnp.concatenate([_pair_diag(x[:, 0:PAIR]), _pair_diag(x[:, PAIR:])], axis=1))
           for a, x in zip(a_rb, wu)]
    mn = [_dot(jnp.concatenate([a, b], axis=0),
               jnp.concatenate([x, jnp.concatenate([zero, c_], axis=1)], axis=0), TN)
          for a, b, x, c_ in zip(bdec_p, kdec_p, wu, v_p)]
    lhs = [jnp.concatenate([jnp.where(eye, e, 0.0) + pack(m[:, 0:PAIR]), a + x[:, 0:PAIR]], axis=0)
           for e, m, a, x in zip(eglast_p, mn, rg_p, rwu)]

    y_rows = [None] * nj
    for jj in range(TC // CHUNK):
        for s in range(NSEQ):
            j = s * (TC // CHUNK) + jj
            idx = [j * npair + p for p in range(npair)]
            res = [_dot(lhs[i], _pair_diag(s_ref[s, p])) for p, i in enumerate(idx)]
            for p, i in enumerate(idx):
                s_ref[s, p] = res[p][0:HS_B, :] + pack(mn[i][:, PAIR:])
            y_rows[j] = jnp.concatenate(
                [res[p][HS_B:, :] + rwu[i][:, PAIR:] + kv[i][CHUNK:, :] for p, i in enumerate(idx)],
                axis=-1)

    out = _rwkv_back(jnp.concatenate(y_rows, axis=0), r, kmod, v, gate,
                     rk_ref[...], lnw_ref[...], lnb_ref[...], ones_blk)
    o_ref[...] = out.reshape(NSEQ, TC, MIX_B).astype(o_ref.dtype)
    for s in range(NSEQ):
        for p in range(npair):
            st = s_ref[s, p].T
            s_out_ref[s, 2 * p] = st[0:HS_B, :]
            s_out_ref[s, 2 * p + 1] = st[HS_B:, :]


def _rwkv_prompt(pb, mu, w0, w2, a0, a2, k_k, k_a, rk, lnw, lnb, ones_blk):
    b, t, _ = pb.shape
    blk = lambda i, j: (i, j, 0)
    fixed = lambda i, j: (0, 0)
    vec = pl.BlockSpec((1, MIX_B), fixed)
    return pl.pallas_call(
        _rwkv_prompt_kernel,
        grid=(b // NSEQ, t // TC),
        in_specs=[pl.BlockSpec((NSEQ, TC, B_W), blk),
                  pl.BlockSpec((1, B_W), fixed),
                  vec, pl.BlockSpec((LORA, MIX_B), fixed),
                  vec, pl.BlockSpec((LORA, MIX_B), fixed),
                  vec, vec, vec, vec, vec,
                  pl.BlockSpec((MIX_B, MIX_B), fixed)],
        out_specs=[pl.BlockSpec((NSEQ, TC, MIX_B), blk),
                   pl.BlockSpec((NSEQ, HB, HS_B, HS_B), lambda i, j: (i, 0, 0, 0)),
                   pl.BlockSpec((NSEQ, 1, B_W), lambda i, j: (i, 0, 0))],
        out_shape=[jax.ShapeDtypeStruct((b, t, MIX_B), BF16),
                   jax.ShapeDtypeStruct((b, HB, HS_B, HS_B), F32),
                   jax.ShapeDtypeStruct((b, 1, B_W), F32)],
        scratch_shapes=[pltpu.VMEM((NSEQ, 8, B_W), F32),
                        pltpu.VMEM((NSEQ, HB // 2, HS_B, PAIR), F32)],
        compiler_params=pltpu.CompilerParams(dimension_semantics=("arbitrary", "arbitrary"),
                                             vmem_limit_bytes=VMEM_LIMIT),
        name="rwkv_prompt",
    )(pb, mu, w0, w2, a0, a2, k_k, k_a, rk, lnw, lnb, ones_blk)


def _pad_rows(rows, width):
    return jnp.concatenate(rows + [jnp.zeros((8 - len(rows), width), F32)], axis=0)


def _gdn_step_kernel(pa_ref, bd_ref, sc_ref, s_ref, cw_ref, hp_ref, anw_ref,
                     o_ref, s_out_ref, c_out_ref):
    u = pa_ref[:, 0:A_QKV]
    cw = cw_ref[...]
    conv = (sc_ref[0] * cw[0:1, :] + sc_ref[1] * cw[1:2, :] + sc_ref[2] * cw[2:3, :] + u * cw[3:4, :])
    qkv = _silu(conv)
    c_out_ref[0] = sc_ref[1]
    c_out_ref[1] = sc_ref[2]
    c_out_ref[2] = u

    qn, kn = _gdn_qk(qkv)
    beta_all, g_all = _gdn_gates(bd_ref[...], hp_ref[...])
    eg_all = jnp.exp(g_all)
    lanes_of = lambda h: slice(h * DK_A, (h + 1) * DK_A)
    q = [qn[:, lanes_of(h)] for h in range(HA)]
    k = [kn[:, lanes_of(h)] for h in range(HA)]
    v = [qkv[:, 2 * MIX_A + h * DK_A:2 * MIX_A + (h + 1) * DK_A] for h in range(HA)]
    beta = [beta_all[:, h:h + 1] for h in range(HA)]
    eg = [eg_all[:, HA + h:HA + h + 1] for h in range(HA)]

    bh = [(b, h) for b in range(SB) for h in range(HA)]
    s0 = [s_ref[b, h] for b, h in bh]
    ks_qs = [_dot(_pad_rows([k[h][b:b + 1, :], q[h][b:b + 1, :]], DK_A), s) for (b, h), s in zip(bh, s0)]
    o_heads = []
    for h in range(HA):
        ks = jnp.concatenate([ks_qs[b * HA + h][0:1, :] for b in range(SB)], axis=0)
        qs = jnp.concatenate([ks_qs[b * HA + h][1:2, :] for b in range(SB)], axis=0)
        u2 = beta[h] * (v[h] - eg[h] * ks)
        for b in range(SB):
            s_out_ref[b, h] = (eg[h][b:b + 1, :] * s0[b * HA + h]
                               + _dot(_pad_rows([k[h][b:b + 1, :]], DK_A),
                                      _pad_rows([u2[b:b + 1, :]], DK_A), TN))
        o_heads.append(eg[h] * qs + jnp.sum(q[h] * k[h], axis=-1, keepdims=True) * u2)

    o_ref[...] = _gdn_out(jnp.concatenate(o_heads, axis=-1), pa_ref[:, A_QKV:A_MAIN], anw_ref[...])


def _gdn_step_specs(n):
    row = lambda i: (i, 0)
    fixed = lambda i: (0, 0)
    state = pl.BlockSpec((SB, HA, DK_A, DK_A), lambda i: (i, 0, 0, 0))
    conv = pl.BlockSpec((CONV_W - 1, SB, A_QKV), lambda i: (0, i, 0))
    in_specs = [pl.BlockSpec((SB, A_MAIN), row),
                pl.BlockSpec((SB, BD_W), row),
                conv, state,
                pl.BlockSpec((CONV_W, A_QKV), fixed),
                pl.BlockSpec((8, BD_W), fixed),
                pl.BlockSpec((1, MIX_A), fixed)]
    out_specs = [pl.BlockSpec((SB, MIX_A), row), state, conv]
    out_shape = [jax.ShapeDtypeStruct((n, MIX_A), F32),
                 jax.ShapeDtypeStruct((n, HA, DK_A, DK_A), F32),
                 jax.ShapeDtypeStruct((CONV_W - 1, n, A_QKV), F32)]
    return in_specs, out_specs, out_shape


def _rwkv_step_lanes_kernel(pb_ref, sh_ref, s_ref, mu_ref, w0_ref, w2_ref, a0_ref, a2_ref, kk_ref, ka_ref,
                            rk_ref, lnw_ref, lnb_ref, ones_ref,
                            o_ref, s_out_ref, vec_scr, row_scr, y_scr):
    h = pl.program_id(0)
    ones_blk = ones_ref[...]

    @pl.when(h == 0)
    def _():
        pb = pb_ref[...]
        xb = pb + (sh_ref[...] - pb) * mu_ref[...]
        r, kmod, v, gate, logw, a, kk = _rwkv_front(
            xb, w0_ref[...], w2_ref[...], a0_ref[...], a2_ref[...], kk_ref[...], ka_ref[...], ones_blk)
        for slot, z in enumerate((-kk, kk * a, kmod, v, r, jnp.exp(logw))):
            vec_scr[slot] = z.T
        for slot, z in enumerate((r, kmod, v, gate)):
            row_scr[slot] = z

    base = pl.multiple_of(h * HS_B, HS_B)
    chan = pl.ds(base, HS_B)
    nkk = vec_scr[0, chan, :]
    kka = vec_scr[1, chan, :]
    kmod_t = vec_scr[2, chan, :]
    r_t = vec_scr[4, chan, :]
    wdec = vec_scr[5, chan, :]

    def body(vi, carry):
        s0 = s_ref[0, vi]
        sa = jnp.sum(s0 * nkk, axis=0, keepdims=True)
        s1 = s0 * wdec + sa * kka + vec_scr[3, pl.ds(base + vi, 1), :] * kmod_t
        s_out_ref[0, vi] = s1
        y_scr[pl.ds(base + vi, 1), :] = jnp.sum(s1 * r_t, axis=0, keepdims=True)
        return carry

    lax.fori_loop(0, HS_B, body, 0, unroll=4)

    @pl.when(h == HB - 1)
    def _():
        o_ref[...] = _rwkv_back(y_scr[...].T, row_scr[0], row_scr[1], row_scr[2], row_scr[3],
                                rk_ref[...], lnw_ref[...], lnb_ref[...], ones_blk)


def _rwkv_step_specs(n):
    fixed = lambda i: (0, 0)
    vec = pl.BlockSpec((1, MIX_B), fixed)
    state = pl.BlockSpec((1, HS_B, HS_B, n), lambda i: (i, 0, 0, 0))
    in_specs = [pl.BlockSpec((n, B_W), fixed),
                pl.BlockSpec((n, B_W), fixed),
                state,
                pl.BlockSpec((1, B_W), fixed),
                vec, pl.BlockSpec((LORA, MIX_B), fixed),
                vec, pl.BlockSpec((LORA, MIX_B), fixed),
                vec, vec, vec, vec, vec,
                pl.BlockSpec((MIX_B, MIX_B), fixed)]
    out_specs = [pl.BlockSpec((n, MIX_B), fixed), state]
    out_shape = [jax.ShapeDtypeStruct((n, MIX_B), F32),
                 jax.ShapeDtypeStruct((HB, HS_B, HS_B, n), F32)]
    return in_specs, out_specs, out_shape


def _sample_step(a_args, b_args, xs2d, w_out, final_w):
    n = a_args[0].shape[0]
    assert n // SB == HB
    a_in, a_out, a_shape = _gdn_step_specs(n)
    b_in, b_out, b_shape = _rwkv_step_specs(n)
    na, nb = len(a_in), len(b_in)
    fixed = lambda i: (0, 0)
    o_in = [pl.BlockSpec(xs2d.shape, fixed), pl.BlockSpec((D_MODEL, D_MODEL), fixed), pl.BlockSpec((1, D_MODEL), fixed)]
    a_in[A_STATE_ARG] = pl.BlockSpec(memory_space=pl.ANY)
    b_in[B_STATE_ARG] = pl.BlockSpec(memory_space=pl.ANY)

    def body(*refs):
        a_refs, b_refs, o_refs = list(refs[:na]), list(refs[na:na + nb]), refs[na + nb:na + nb + 3]
        sa_out, ca_out, sb_out, y_ref = refs[na + nb + 3:na + nb + 7]
        vec_scr, row_scr, y_scr, mixa_scr, mixb_scr, ring_a, ring_b, sem_a, sem_b = refs[na + nb + 7:]
        i = pl.program_id(0)
        sa_hbm, sb_hbm = a_refs[A_STATE_ARG], b_refs[B_STATE_ARG]

        def fetch(step):
            slot = step % RING
            return (pltpu.make_async_copy(sa_hbm.at[pl.ds(step * SB, SB)], ring_a.at[slot], sem_a.at[slot]),
                    pltpu.make_async_copy(sb_hbm.at[pl.ds(step, 1)], ring_b.at[slot], sem_b.at[slot]))

        @pl.when(i == 0)
        def _():
            for step in range(RING - 1):
                for cp in fetch(step):
                    cp.start()

        @pl.when(i + RING - 1 < HB)
        def _():
            for cp in fetch(i + RING - 1):
                cp.start()

        for cp in fetch(i):
            cp.wait()
        a_refs[A_STATE_ARG] = ring_a.at[i % RING]
        b_refs[B_STATE_ARG] = ring_b.at[i % RING]
        _gdn_step_kernel(*a_refs, mixa_scr.at[pl.ds(pl.multiple_of(i * SB, SB), SB), :], sa_out, ca_out)
        _rwkv_step_lanes_kernel(*b_refs, mixb_scr, sb_out, vec_scr, row_scr, y_scr)

        @pl.when(i == HB - 1)
        def _():
            _outproj_kernel(o_refs[0], mixa_scr, mixb_scr, o_refs[1], o_refs[2], y_ref)

    return pl.pallas_call(
        body,
        grid=(HB,),
        in_specs=a_in + b_in + o_in,
        out_specs=a_out[1:] + b_out[1:] + [pl.BlockSpec(xs2d.shape, fixed)],
        out_shape=a_shape[1:] + b_shape[1:] + [jax.ShapeDtypeStruct(xs2d.shape, F32)],
        scratch_shapes=[pltpu.VMEM((6, MIX_B, n), F32),
                        pltpu.VMEM((4, n, MIX_B), F32),
                        pltpu.VMEM((MIX_B, n), F32),
                        pltpu.VMEM((n, MIX_A), F32),
                        pltpu.VMEM((n, MIX_B), F32),
                        pltpu.VMEM((RING, SB, HA, DK_A, DK_A), F32),
                        pltpu.VMEM((RING, 1, HS_B, HS_B, n), F32),
                        pltpu.SemaphoreType.DMA((RING,)),
                        pltpu.SemaphoreType.DMA((RING,))],
        compiler_params=pltpu.CompilerParams(dimension_semantics=("arbitrary",),
                                             vmem_limit_bytes=VMEM_LIMIT),
        name="sample_step",
    )(*a_args, *b_args, xs2d, w_out, final_w)


def _block_ones(width, group):
    idx = jnp.arange(width) // group
    return (idx[:, None] == idx[None, :]).astype(BF16)


def kernel(x_prompt, x_sample, state_a_mat, state_a_conv, state_b_mat, state_b_shift, norm_w, w_in,
           conv_w, a_log, dt_bias, a_norm_w, mu, w0, w2, a0, a2, k_k, k_a, r_k, ln_w, ln_b, w_out,
           final_norm_w):
    bsz, seq, _ = x_prompt.shape
    nsmp = x_sample.shape[0]
    assert bsz % NSEQ == 0 and bsz % NSEQ_A == 0 and seq % TC == 0 and seq % TC_A == 0
    assert nsmp % SB == 0 and x_sample.shape[1] == 1

    w_all = jnp.transpose(w_in[0])
    w_o = w_out[0].astype(BF16)
    hp = jnp.pad(jnp.concatenate([a_log, dt_bias], axis=0), ((0, 6), (HA, BD_W - 2 * HA)))
    nw = norm_w[0].reshape(1, D_MODEL)
    fw = final_norm_w.reshape(1, D_MODEL)
    anw = jnp.tile(a_norm_w[0].reshape(1, DK_A), (1, HA))
    ones_b = _block_ones(MIX_B, HS_B)
    row = lambda z: z[0].reshape(1, -1)
    b_params = (row(mu), row(w0), w2[0], row(a0), a2[0], row(k_k), row(k_a), row(r_k), row(ln_w), row(ln_b),
                ones_b)

    xp = x_prompt.reshape(bsz * seq, D_MODEL)
    xs = x_sample.reshape(nsmp * D_MODEL // LANE, LANE)
    pa, pb, bd, sa_, sb_, sbd = _inproj(xp, xs, nw, w_all)

    ob, p_bmat, p_bshift = _rwkv_prompt(pb.reshape(bsz, seq, B_W), *b_params)
    y_prompt, p_amat, p_aconv = _gdn_prompt(pa.reshape(bsz, seq, A_MAIN), bd.reshape(bsz, seq, BD_W),
                                            conv_w[0], hp, anw, x_prompt, ob, w_o, fw)

    sconv_t = jnp.swapaxes(state_a_conv[0], 0, 1)
    s_amat, s_aconv_t, s_bmat_t, y_sample = _sample_step(
        (sa_, sbd, sconv_t, state_a_mat[0], conv_w[0], hp, anw),
        (sb_, state_b_shift[0], jnp.transpose(state_b_mat[0], (1, 2, 3, 0))) + b_params,
        xs, w_o, fw)
    s_bmat = jnp.transpose(s_bmat_t, (3, 0, 1, 2))

    return (y_prompt, y_sample.reshape(nsmp, 1, D_MODEL),
            p_amat[None], p_aconv[None], p_bmat[None], p_bshift.reshape(1, bsz, B_W),
            s_amat[None], jnp.swapaxes(s_aconv_t, 0, 1)[None], s_bmat[None], sb_[None])
```
